```python
import math
import jax
import jax.numpy as jnp
from jax import lax
import numpy as np

D_MODEL = 2048
BATCH = 2
SEQ = 4096
DEPTH = 2

SSM_GROUP = 16
SSM_STATE = 64
SSM_WIDTH = 1024
SSM_GROUPS = SSM_WIDTH // SSM_GROUP
HEAD_DIM = 128
DILATION_PATTERN = ((128, 1), (512, 4), (2048, 16))
HEADS_PER_GROUP = 4
N_ATT_GROUPS = len(DILATION_PATTERN)
N_HEADS = N_ATT_GROUPS * HEADS_PER_GROUP
ATT_WIDTH = N_HEADS * HEAD_DIM
ATT_OUT_WIDTH = HEADS_PER_GROUP * HEAD_DIM
Q_BLOCK = 128
NEG_INF = -1e30
N_BUCKETS = 32
MAX_DISTANCE = 2048
SPLIT_POINTS = (SSM_WIDTH,
                SSM_WIDTH + ATT_WIDTH,
                SSM_WIDTH + 2 * ATT_WIDTH,
                SSM_WIDTH + 3 * ATT_WIDTH,
                SSM_WIDTH + 3 * ATT_WIDTH + D_MODEL)
IN_WIDTH = SSM_WIDTH + 3 * ATT_WIDTH + 2 * D_MODEL
D_FF = 5632
N_EXPERTS = 8
TOP_K = 2
D_FF_EXPERT = 7168
EXPERT_BLOCK = 128
N_DENSE = (DEPTH + 1) // 2
N_MOE = DEPTH // 2
N_MOD = 6
EPS = 1e-6

kernel_name = "hybrid_s5_dilated_attn_moe_block"


def rms_norm(x, g):
    x32 = x.astype(jnp.float32)
    y = x32 * lax.rsqrt(jnp.mean(x32 * x32, axis=-1, keepdims=True) + EPS)
    return y.astype(x.dtype) * g


def modulate(h, shift, scale):
    return h * (1 + scale[:, None, :]) + shift[:, None, :]


def swiglu(h, w_gate, w_up, w_down):
    return (jax.nn.silu(h @ w_gate) * (h @ w_up)) @ w_down


def s5_mixer(u, a_re, a_im, log_dt, b_re, b_im, c_re, c_im, d_skip, w_glu, b_glu):
    bsz, seqlen, _ = u.shape
    f32 = jnp.float32
    lam = lax.complex(a_re.astype(f32), a_im.astype(f32))
    dt = jnp.exp(log_dt.astype(f32))[:, None]
    a_bar = jnp.exp(lam * dt)
    b_bar = ((a_bar - 1.0) / lam)[:, :, None] * lax.complex(b_re.astype(f32), b_im.astype(f32))
    c_mat = lax.complex(c_re.astype(f32), c_im.astype(f32))
    u32 = u.astype(f32)
    u_g = u32.reshape(bsz, seqlen, SSM_GROUPS, SSM_GROUP).astype(jnp.complex64)
    bu = jnp.einsum('blgh,gph->blgp', u_g, b_bar)
    a_seq = jnp.broadcast_to(a_bar, bu.shape)

    def combine(left, right):
        a_l, s_l = left
        a_r, s_r = right
        return a_r * a_l, a_r * s_l + s_r

    _, states = lax.associative_scan(combine, (a_seq, bu), axis=1)
    y = jnp.einsum('blgp,ghp->blgh', states, c_mat).real.reshape(bsz, seqlen, SSM_WIDTH)
    y = y + d_skip.astype(f32) * u32
    y = jax.nn.gelu(y).astype(u.dtype)
    return y * jax.nn.sigmoid(y @ w_glu + b_glu)


def t5_causal_bucket(dist):
    max_exact = N_BUCKETS // 2
    d32 = jnp.maximum(dist, 1).astype(jnp.float32)
    large = max_exact + (jnp.log(d32 / max_exact) / math.log(MAX_DISTANCE / max_exact)
                         * (N_BUCKETS - max_exact)).astype(jnp.int32)
    return jnp.where(dist < max_exact, dist, jnp.minimum(large, N_BUCKETS - 1))


def dilated_attention(q, k, v, rel_bias):
    bsz, seqlen = q.shape[0], q.shape[1]
    n_blocks = seqlen // Q_BLOCK
    scale = HEAD_DIM ** -0.5
    f32 = jnp.float32
    offsets, biases, qs, ks, vs = [], [], [], [], []
    for g, (window, dilation) in enumerate(DILATION_PATTERN):
        hs = slice(g * HEADS_PER_GROUP, (g + 1) * HEADS_PER_GROUP)
        offs = jnp.arange(window // dilation + 1, dtype=jnp.int32) * dilation
        offsets.append(offs)
        biases.append(rel_bias[t5_causal_bucket(offs)][:, hs].T.astype(f32))
        qs.append(q[:, :, hs])
        ks.append(k[:, :, hs])
        vs.append(v[:, :, hs])

    def block(blk):
        start = blk * Q_BLOCK
        t = start + jnp.arange(Q_BLOCK, dtype=jnp.int32)
        outs, lses = [], []
        for g in range(N_ATT_GROUPS):
            kpos = t[:, None] - offsets[g][None, :]
            valid = kpos >= 0
            kidx = jnp.maximum(kpos, 0)
            qb = lax.dynamic_slice_in_dim(qs[g], start, Q_BLOCK, axis=1)
            kb = ks[g][:, kidx]
            vb = vs[g][:, kidx].astype(f32)
            logits = jnp.einsum('bqhd,bqkhd->bhqk', qb, kb).astype(f32) * scale
            logits = logits + biases[g][None, :, None, :]
            logits = jnp.where(valid[None, None], logits, NEG_INF)
            m = jnp.max(logits, axis=-1, keepdims=True)
            p = jnp.exp(logits - m)
            s = jnp.sum(p, axis=-1)
            o = jnp.einsum('bhqk,bqkhd->bqhd', p, vb) / jnp.transpose(s, (0, 2, 1))[..., None]
            outs.append(o)
            lses.append(m[..., 0] + jnp.log(s))
        w = jax.nn.softmax(jnp.stack(lses), axis=0)
        o = jnp.einsum('gbhq,gbqhd->bqhd', w, jnp.stack(outs))
        return o.astype(q.dtype)

    out = lax.map(block, jnp.arange(n_blocks))
    return jnp.transpose(out, (1, 0, 2, 3, 4)).reshape(bsz, seqlen, ATT_OUT_WIDTH)


def moe_swiglu(h, w_router, w_gate, w_up, w_down):
    n_tok = h.shape[0]
    n_assign = n_tok * TOP_K
    logits = (h @ w_router).astype(jnp.float32)
    top_logit, top_expert = lax.top_k(logits, TOP_K)
    top_gate = jax.nn.softmax(top_logit, axis=-1)
    flat_expert = top_expert.reshape(-1)
    flat_token = jnp.repeat(jnp.arange(n_tok, dtype=jnp.int32), TOP_K)
    order = jnp.argsort(flat_expert)
    sorted_expert = flat_expert[order]
    sorted_token = flat_token[order]
    sorted_gate = top_gate.reshape(-1)[order]
    counts = jnp.bincount(flat_expert, length=N_EXPERTS)
    padded = (counts + EXPERT_BLOCK - 1) // EXPERT_BLOCK * EXPERT_BLOCK
    padded_end = jnp.cumsum(padded)
    start = jnp.cumsum(counts) - counts
    dest = (padded_end - padded)[sorted_expert] + jnp.arange(n_assign, dtype=jnp.int32) - start[sorted_expert]
    n_rows = n_assign + N_EXPERTS * EXPERT_BLOCK
    n_blocks = n_rows // EXPERT_BLOCK
    row_token = jnp.zeros((n_rows,), jnp.int32).at[dest].set(sorted_token)
    block_expert = jnp.minimum(
        jnp.searchsorted(padded_end, jnp.arange(n_blocks, dtype=jnp.int32) * EXPERT_BLOCK, side='right'),
        N_EXPERTS - 1)
    x_rows = h[row_token].reshape(n_blocks, EXPERT_BLOCK, h.shape[-1])

    def expert_block(args):
        xb, e = args
        return swiglu(xb, w_gate[e], w_up[e], w_down[e])

    y_rows = lax.map(expert_block, (x_rows, block_expert)).reshape(n_rows, h.shape[-1])
    contrib = y_rows[dest] * sorted_gate[:, None].astype(h.dtype)
    return jax.ops.segment_sum(contrib, sorted_token, num_segments=n_tok)


def setup_inputs(seed: int = 0) -> dict:
    key = jax.random.key(seed)
    ks = iter(jax.random.split(key, 32))
    f32 = jnp.float32

    def normal(shape, std):
        return jax.random.normal(next(ks), shape, f32) * std

    x = normal((BATCH, SEQ, D_MODEL), 1.0)
    c = normal((BATCH, D_MODEL), 1.0)
    w_mod = normal((DEPTH, D_MODEL, N_MOD * D_MODEL), 0.5 * D_MODEL ** -0.5)
    b_mod = normal((DEPTH, N_MOD * D_MODEL), 0.02)
    norm_mix_g = 1.0 + normal((DEPTH, D_MODEL), 0.05)
    norm_ffn_g = 1.0 + normal((DEPTH, D_MODEL), 0.05)
    w_in = normal((DEPTH, D_MODEL, IN_WIDTH), D_MODEL ** -0.5)
    ssm_a_re = -0.5 + normal((DEPTH, SSM_GROUPS, SSM_STATE), 0.01)
    ssm_a_im = math.pi * jnp.arange(SSM_STATE, dtype=f32) + normal((DEPTH, SSM_GROUPS, SSM_STATE), 0.01)
    ssm_log_dt = jax.random.uniform(next(ks), (DEPTH, SSM_GROUPS), f32, math.log(1e-3), math.log(1e-1))
    ssm_b_re = normal((DEPTH, SSM_GROUPS, SSM_STATE, SSM_GROUP), (2 * SSM_GROUP) ** -0.5)
    ssm_b_im = normal((DEPTH, SSM_GROUPS, SSM_STATE, SSM_GROUP), (2 * SSM_GROUP) ** -0.5)
    ssm_c_re = normal((DEPTH, SSM_GROUPS, SSM_GROUP, SSM_STATE), SSM_STATE ** -0.5)
    ssm_c_im = normal((DEPTH, SSM_GROUPS, SSM_GROUP, SSM_STATE), SSM_STATE ** -0.5)
    ssm_d = normal((DEPTH, SSM_WIDTH), 1.0)
    w_glu = normal((DEPTH, SSM_WIDTH, SSM_WIDTH), SSM_WIDTH ** -0.5)
    b_glu = normal((DEPTH, SSM_WIDTH), 0.02)
    rel_bias = normal((N_BUCKETS, N_HEADS), 0.5)
    w_branch_ssm = normal((DEPTH, SSM_WIDTH, D_MODEL), SSM_WIDTH ** -0.5)
    w_branch_att = normal((DEPTH, ATT_OUT_WIDTH, D_MODEL), ATT_OUT_WIDTH ** -0.5)
    w_out = normal((DEPTH, D_MODEL, D_MODEL), D_MODEL ** -0.5)
    ffn_w_gate = normal((N_DENSE, D_MODEL, D_FF), D_MODEL ** -0.5)
    ffn_w_up = normal((N_DENSE, D_MODEL, D_FF), D_MODEL ** -0.5)
    ffn_w_down = normal((N_DENSE, D_FF, D_MODEL), D_FF ** -0.5)
    moe_router = normal((N_MOE, D_MODEL, N_EXPERTS), D_MODEL ** -0.5)
    moe_w_gate = normal((N_MOE, N_EXPERTS, D_MODEL, D_FF_EXPERT), D_MODEL ** -0.5)
    moe_w_up = normal((N_MOE, N_EXPERTS, D_MODEL, D_FF_EXPERT), D_MODEL ** -0.5)
    moe_w_down = normal((N_MOE, N_EXPERTS, D_FF_EXPERT, D_MODEL), D_FF_EXPERT ** -0.5)
    final_norm_g = 1.0 + normal((D_MODEL,), 0.05)
    return {"x": x, "c": c, "w_mod": w_mod, "b_mod": b_mod,
            "norm_mix_g": norm_mix_g, "norm_ffn_g": norm_ffn_g, "w_in": w_in,
            "ssm_a_re": ssm_a_re, "ssm_a_im": ssm_a_im, "ssm_log_dt": ssm_log_dt,
            "ssm_b_re": ssm_b_re, "ssm_b_im": ssm_b_im, "ssm_c_re": ssm_c_re, "ssm_c_im": ssm_c_im,
            "ssm_d": ssm_d, "w_glu": w_glu, "b_glu": b_glu, "rel_bias": rel_bias,
            "w_branch_ssm": w_branch_ssm, "w_branch_att": w_branch_att, "w_out": w_out,
            "ffn_w_gate": ffn_w_gate, "ffn_w_up": ffn_w_up, "ffn_w_down": ffn_w_down,
            "moe_router": moe_router, "moe_w_gate": moe_w_gate, "moe_w_up": moe_w_up,
            "moe_w_down": moe_w_down, "final_norm_g": final_norm_g}


def reference(x, c, w_mod, b_mod, norm_mix_g, norm_ffn_g, w_in,
              ssm_a_re, ssm_a_im, ssm_log_dt, ssm_b_re, ssm_b_im, ssm_c_re, ssm_c_im,
              ssm_d, w_glu, b_glu, rel_bias, w_branch_ssm, w_branch_att, w_out,
              ffn_w_gate, ffn_w_up, ffn_w_down, moe_router, moe_w_gate, moe_w_up,
              moe_w_down, final_norm_g):
    bsz, seqlen, _ = x.shape
    cond = jax.nn.silu(c)
    for i in range(DEPTH):
        mod = cond @ w_mod[i] + b_mod[i]
        shift_m, scale_m, gate_m, shift_f, scale_f, gate_f = jnp.split(mod, N_MOD, axis=-1)
        h = modulate(rms_norm(x, norm_mix_g[i]), shift_m, scale_m)
        proj = h @ w_in[i]
        u, q, k, v, g_ssm, g_att = jnp.split(proj, SPLIT_POINTS, axis=-1)
        y_ssm = s5_mixer(u, ssm_a_re[i], ssm_a_im[i], ssm_log_dt[i], ssm_b_re[i], ssm_b_im[i],
                         ssm_c_re[i], ssm_c_im[i], ssm_d[i], w_glu[i], b_glu[i])
        y_att = dilated_attention(q.reshape(bsz, seqlen, N_HEADS, HEAD_DIM),
                                  k.reshape(bsz, seqlen, N_HEADS, HEAD_DIM),
                                  v.reshape(bsz, seqlen, N_HEADS, HEAD_DIM), rel_bias)
        merged = (jax.nn.sigmoid(g_ssm) * (y_ssm @ w_branch_ssm[i])
                  + jax.nn.sigmoid(g_att) * (y_att @ w_branch_att[i]))
        x = x + gate_m[:, None, :] * (merged @ w_out[i])
        h = modulate(rms_norm(x, norm_ffn_g[i]), shift_f, scale_f).reshape(bsz * seqlen, D_MODEL)
        j = i // 2
        if i % 2 == 0:
            f = swiglu(h, ffn_w_gate[j], ffn_w_up[j], ffn_w_down[j])
        else:
            f = moe_swiglu(h, moe_router[j], moe_w_gate[j], moe_w_up[j], moe_w_down[j])
        x = x + gate_f[:, None, :] * f.reshape(bsz, seqlen, D_MODEL)
    return rms_norm(x, final_norm_g)
```

```python
import functools
import math

import jax
import jax.numpy as jnp
import numpy as np
from jax import lax
from jax.experimental import pallas as pl
from jax.experimental.pallas import tpu as pltpu

F32 = jnp.float32
BF16 = jnp.bfloat16

LANES = 128
SUBLANES = 8
VMEM_BYTES = 64 * 1024 * 1024

SSM_GROUP = 16
SSM_STATE = 64
SSM_WIDTH = 1024
HEAD_DIM = 128
DILATION_PATTERN = ((128, 1), (512, 4), (2048, 16))
HEADS_PER_GROUP = 4
N_ATT_GROUPS = len(DILATION_PATTERN)
ATT_OUT_WIDTH = HEADS_PER_GROUP * HEAD_DIM
Q_BLOCK = 128
NEG_INF = -1e30
N_BUCKETS = 32
MAX_DISTANCE = 2048
N_EXPERTS = 8
TOP_K = 2
N_MOD = 6
EPS = 1e-6

PROJ_TILE = 512
U_TILE0 = 0
Q_TILE0 = SSM_WIDTH // PROJ_TILE
K_TILE0 = Q_TILE0 + N_ATT_GROUPS
V_TILE0 = K_TILE0 + N_ATT_GROUPS
GS_TILE0 = V_TILE0 + N_ATT_GROUPS
MAIN_GATE0 = Q_TILE0


def _params(dims, vmem_mb):
    return pltpu.CompilerParams(dimension_semantics=dims,
                                vmem_limit_bytes=vmem_mb * 1024 * 1024)


def _norm_modulate(x, g, shift, scale):
    ms = jnp.mean(x * x, axis=-1, keepdims=True)
    y = x * lax.rsqrt(ms + EPS)
    return (y * g) * (1.0 + scale) + shift


def _mod_kernel(c_ref, w_ref, b_ref, o_ref):
    c = c_ref[...]
    cond = (c * jax.nn.sigmoid(c)).astype(BF16)
    o_ref[...] = jnp.dot(cond, w_ref[...].astype(BF16),
                         preferred_element_type=F32) + b_ref[...]


def _modulation(c, w_mod, b_mod):
    depth, d, n = w_mod.shape
    bsz = c.shape[0]
    rows = SUBLANES
    c_pad = jnp.zeros((rows, d), F32).at[:bsz].set(c)
    tn = 1536
    out = pl.pallas_call(
        _mod_kernel,
        grid=(depth, n // tn),
        in_specs=[pl.BlockSpec((rows, d), lambda l, j: (0, 0)),
                  pl.BlockSpec((None, d, tn), lambda l, j: (l, 0, j)),
                  pl.BlockSpec((None, 1, tn), lambda l, j: (l, 0, j))],
        out_specs=pl.BlockSpec((None, rows, tn), lambda l, j: (l, 0, j)),
        out_shape=jax.ShapeDtypeStruct((depth, rows, n), F32),
        compiler_params=_params(("arbitrary", "arbitrary"), 40),
        name="modulation",
    )(c_pad, w_mod, b_mod.reshape(depth, 1, n))
    return out[:, :bsz]


def _inproj_kernel(x_ref, g_ref, shift_ref, scale_ref, w_ref, main_ref, *rest, tm):
    qkv_refs, (h_ref, res_ref) = rest[:N_ATT_GROUPS], rest[N_ATT_GROUPS:]
    j = pl.program_id(2)

    @pl.when(j == 0)
    def _():
        h = _norm_modulate(x_ref[...], g_ref[...], shift_ref[...], scale_ref[...])
        h_ref[...] = h.astype(BF16)

    res = jnp.dot(h_ref[...], w_ref[...].astype(BF16), preferred_element_type=F32)

    @pl.when(jnp.logical_or(j < Q_TILE0, j >= GS_TILE0))
    def _():
        main_ref[...] = res.astype(BF16)

    for g, (_, dil) in enumerate(DILATION_PATTERN):
        is_g = functools.reduce(jnp.logical_or, [j == t0 + g for t0 in (Q_TILE0, K_TILE0, V_TILE0)])

        @pl.when(is_g)
        def _(g=g, dil=dil):
            if dil == 1:
                qkv_refs[g][0] = res.astype(BF16)
            else:
                n_chunks = PROJ_TILE // LANES
                for ch in range(n_chunks):
                    res_ref[ch] = res[:, ch * LANES:(ch + 1) * LANES]
                for r in range(dil):
                    rows = [res_ref[ch, pl.ds(r, tm // dil, stride=dil), :] for ch in range(n_chunks)]
                    qkv_refs[g][r] = jnp.concatenate(rows, axis=-1).astype(BF16)


def _in_projection(x, g, shift, scale, w_in, tm=1024):
    bsz, seqlen, d = x.shape
    n_tiles = w_in.shape[1] // PROJ_TILE
    n_main = n_tiles - 3 * N_ATT_GROUPS

    def main_map(b, i, j):
        return (jnp.where(j < Q_TILE0, j, jnp.where(j < GS_TILE0, Q_TILE0 - 1, j - 3 * N_ATT_GROUPS)), b, i, 0)

    def qkv_map(g):
        return lambda b, i, j: ((j > Q_TILE0 + g).astype(jnp.int32) + (j > K_TILE0 + g).astype(jnp.int32),
                                b, 0, i, 0)

    qkv_specs = [pl.BlockSpec((None, None, dil, tm // dil, PROJ_TILE), qkv_map(g))
                 for g, (_, dil) in enumerate(DILATION_PATTERN)]
    qkv_shapes = [jax.ShapeDtypeStruct((3, bsz, dil, seqlen // dil, PROJ_TILE), BF16)
                  for _, dil in DILATION_PATTERN]
    outs = pl.pallas_call(
        functools.partial(_inproj_kernel, tm=tm),
        grid=(bsz, seqlen // tm, n_tiles),
        in_specs=[pl.BlockSpec((None, tm, d), lambda b, i, j: (b, i, 0)),
                  pl.BlockSpec((1, d), lambda b, i, j: (0, 0)),
                  pl.BlockSpec((None, 1, d), lambda b, i, j: (b, 0, 0)),
                  pl.BlockSpec((None, 1, d), lambda b, i, j: (b, 0, 0)),
                  pl.BlockSpec((d, PROJ_TILE), lambda b, i, j: (0, j))],
        out_specs=[pl.BlockSpec((None, None, tm, PROJ_TILE), main_map)] + qkv_specs,
        out_shape=[jax.ShapeDtypeStruct((n_main, bsz, seqlen, PROJ_TILE), BF16)] + qkv_shapes,
        scratch_shapes=[pltpu.VMEM((tm, d), BF16), pltpu.VMEM((PROJ_TILE // LANES, tm, LANES), F32)],
        compiler_params=_params(("arbitrary", "arbitrary", "arbitrary"), 52),
        name="in_projection",
    )(x, g.reshape(1, d), shift[:, None, :], scale[:, None, :], w_in)
    return outs[0], outs[1:]


SSM_GB = 16
SSM_NGB = (SSM_WIDTH // SSM_GROUP) // SSM_GB
SSM_GB_IN = SSM_GB * SSM_GROUP
SSM_GB_RE = SSM_GB * SSM_STATE
SSM_CB = 2 * SSM_GB_RE // LANES
SSM_SLOTS = SSM_NGB * SSM_CB + SUBLANES


def _s5_kernel(u0_ref, u1_ref, bm_ref, cm_ref, ar_ref, ai_ref, d_ref, wglu_ref, bglu_ref,
               o_ref, bu_ref, xs_ref, st_ref, *, bsz, tt):
    stride = tt + SUBLANES
    rows = bsz * tt

    @pl.when(pl.program_id(0) == 0)
    def _():
        st_ref[...] = jnp.zeros_like(st_ref)

    u_halves = (u0_ref[...].reshape(rows, PROJ_TILE), u1_ref[...].reshape(rows, PROJ_TILE))

    for gb in range(SSM_NGB):
        half, off = divmod(gb * SSM_GB_IN, PROJ_TILE)
        bu = jnp.dot(u_halves[half][:, off:off + SSM_GB_IN], bm_ref[gb],
                     preferred_element_type=F32)
        for b in range(bsz):
            for cb in range(SSM_CB):
                blk = (b * SSM_NGB + gb) * SSM_CB + cb
                bu_ref[blk * stride: blk * stride + tt, :] = (
                    bu[b * tt:(b + 1) * tt, cb * LANES:(cb + 1) * LANES])

    a_re = [ar_ref[gb] for gb in range(SSM_NGB)]
    a_im = [ai_ref[gb] for gb in range(SSM_NGB)]
    half_cb = SSM_CB // 2

    def step(t, carry):
        new = []
        for b in range(bsz):
            for gb in range(SSM_NGB):
                k = (b * SSM_NGB + gb) * 2
                s_re, s_im = carry[k], carry[k + 1]
                base = ((b * SSM_NGB + gb) * SSM_CB) * stride
                bu_re = bu_ref[pl.ds(base + t, half_cb, stride=stride), :]
                bu_im = bu_ref[pl.ds(base + half_cb * stride + t, half_cb, stride=stride), :]
                n_re = a_re[gb] * s_re - a_im[gb] * s_im + bu_re
                n_im = a_re[gb] * s_im + a_im[gb] * s_re + bu_im
                row = pl.multiple_of((b * tt + t) * SSM_SLOTS + gb * SSM_CB, SUBLANES)
                xs_ref[pl.ds(row, half_cb), :] = n_re
                xs_ref[pl.ds(row + half_cb, half_cb), :] = n_im
                new += [n_re, n_im]
        return tuple(new)

    n_carry = bsz * SSM_NGB * 2
    carry = lax.fori_loop(0, tt, step, tuple(st_ref[k] for k in range(n_carry)), unroll=4)
    for k in range(n_carry):
        st_ref[k] = carry[k]

    ys = []
    for gb in range(SSM_NGB):
        cols = [xs_ref[pl.ds(gb * SSM_CB + cb, rows, stride=SSM_SLOTS), :] for cb in range(SSM_CB)]
        xg = jnp.concatenate(cols, axis=-1).astype(BF16)
        ys.append(jnp.dot(xg, cm_ref[gb], preferred_element_type=F32))
    y = jnp.concatenate(ys, axis=-1)
    u32 = jnp.concatenate(u_halves, axis=-1).astype(F32)
    y = jax.nn.gelu(y + d_ref[...] * u32)
    z = jnp.dot(y.astype(BF16), wglu_ref[...], preferred_element_type=F32) + bglu_ref[...]
    o_ref[...] = (y * jax.nn.sigmoid(z)).astype(o_ref.dtype).reshape(bsz, tt, SSM_WIDTH)


def _block_diag(m):
    ngb, gb, r, c = m.shape
    eye = jnp.eye(gb, dtype=m.dtype)
    return jnp.einsum('ngrc,gk->ngrkc', m, eye).reshape(ngb, gb * r, gb * c)


def _s5_tables(a_re, a_im, log_dt, b_re, b_im, c_re, c_im):
    lam = lax.complex(a_re.astype(F32), a_im.astype(F32))
    dt = jnp.exp(log_dt.astype(F32))[:, None]
    a_bar = jnp.exp(lam * dt)
    b_bar = ((a_bar - 1.0) / lam)[:, :, None] * lax.complex(b_re.astype(F32), b_im.astype(F32))
    g, p, h = b_bar.shape
    bt = jnp.transpose(b_bar, (0, 2, 1)).reshape(SSM_NGB, SSM_GB, h, p)
    bm = jnp.concatenate([_block_diag(bt.real), _block_diag(bt.imag)], axis=-1)
    ct_re = jnp.transpose(c_re.astype(F32), (0, 2, 1)).reshape(SSM_NGB, SSM_GB, p, h)
    ct_im = jnp.transpose(c_im.astype(F32), (0, 2, 1)).reshape(SSM_NGB, SSM_GB, p, h)
    cm = jnp.concatenate([_block_diag(ct_re), -_block_diag(ct_im)], axis=1)
    ar = a_bar.real.reshape(SSM_NGB, SSM_CB // 2, LANES)
    ai = a_bar.imag.reshape(SSM_NGB, SSM_CB // 2, LANES)
    return bm.astype(BF16), cm.astype(BF16), ar, ai


def _s5_branch(proj, tables, d_skip, w_glu, b_glu, tt=128):
    _, bsz, seqlen, _ = proj.shape
    bm, cm, ar, ai = tables
    stride = tt + SUBLANES
    n_blocks = bsz * SSM_NGB * SSM_CB
    d_skip = d_skip.reshape(1, SSM_WIDTH).astype(F32)
    w_glu = w_glu.astype(BF16)
    b_glu = b_glu.reshape(1, SSM_WIDTH).astype(F32)
    return pl.pallas_call(
        functools.partial(_s5_kernel, bsz=bsz, tt=tt),
        grid=(seqlen // tt,),
        in_specs=[pl.BlockSpec((None, bsz, tt, PROJ_TILE), lambda t: (U_TILE0, 0, t, 0)),
                  pl.BlockSpec((None, bsz, tt, PROJ_TILE), lambda t: (U_TILE0 + 1, 0, t, 0)),
                  _resident(bm), _resident(cm), _resident(ar), _resident(ai),
                  _resident(d_skip), _resident(w_glu), _resident(b_glu)],
        out_specs=pl.BlockSpec((bsz, tt, SSM_WIDTH), lambda t: (0, t, 0)),
        out_shape=jax.ShapeDtypeStruct((bsz, seqlen, SSM_WIDTH), BF16),
        scratch_shapes=[pltpu.VMEM((n_blocks * stride, LANES), F32),
                        pltpu.VMEM((bsz * tt * SSM_SLOTS, LANES), F32),
                        pltpu.VMEM((bsz * SSM_NGB * 2, SSM_CB // 2, LANES), F32)],
        compiler_params=_params(("arbitrary",), 48),
        name="s5_branch",
    )(proj, proj, bm, cm, ar, ai, d_skip, w_glu, b_glu)


def _t5_causal_bucket(dist):
    max_exact = N_BUCKETS // 2
    d32 = jnp.maximum(dist, 1).astype(F32)
    large = max_exact + (jnp.log(d32 / max_exact) / math.log(MAX_DISTANCE / max_exact)
                         * (N_BUCKETS - max_exact)).astype(jnp.int32)
    return jnp.where(dist < max_exact, dist, jnp.minimum(large, N_BUCKETS - 1))


def _bias_tile(rel_bias, group):
    window, dilation = DILATION_PATTERN[group]
    steps = window // dilation
    a = np.arange(Q_BLOCK)[:, None]
    c = np.arange(2 * Q_BLOCK)[None, :]
    delta = Q_BLOCK + a - c
    valid = (delta >= 0) & (delta <= steps)
    dist = jnp.asarray(np.where(valid, delta, 0) * dilation, jnp.int32)
    heads = slice(group * HEADS_PER_GROUP, (group + 1) * HEADS_PER_GROUP)
    bias = rel_bias[_t5_causal_bucket(dist)][:, :, heads].astype(F32)
    bias = jnp.where(jnp.asarray(valid)[:, :, None], bias, NEG_INF)
    return jnp.transpose(bias, (2, 0, 1))


def _attn_kernel(q_ref, kc_ref, kp_ref, vc_ref, vp_ref, bias_ref, o_ref, lse_ref, kf_ref, vf_ref, *, tq):
    kf_ref[0:Q_BLOCK, :] = kp_ref[...]
    kf_ref[Q_BLOCK:, :] = kc_ref[...]
    vf_ref[0:Q_BLOCK, :] = vp_ref[...]
    vf_ref[Q_BLOCK:, :] = vc_ref[...]
    scale = HEAD_DIM ** -0.5
    first_tile = pl.program_id(2) == 0
    col = lax.broadcasted_iota(jnp.int32, (Q_BLOCK, 2 * Q_BLOCK), 1)

    def block(jb, _):
        r0 = pl.multiple_of(jb * Q_BLOCK, Q_BLOCK)
        no_prev = jnp.logical_and(first_tile, jb == 0)
        for h in range(HEADS_PER_GROUP):
            hs = slice(h * HEAD_DIM, (h + 1) * HEAD_DIM)
            q = q_ref[pl.ds(r0, Q_BLOCK), hs]
            k2 = kf_ref[pl.ds(r0, 2 * Q_BLOCK), hs]
            v2 = vf_ref[pl.ds(r0, 2 * Q_BLOCK), hs]
            s = lax.dot_general(q, k2, (((1,), (1,)), ((), ())), preferred_element_type=F32)
            s = s * scale + bias_ref[h]
            s = jnp.where(jnp.logical_and(no_prev, col < Q_BLOCK), NEG_INF, s)
            m = jnp.max(s, axis=-1, keepdims=True)
            p = jnp.exp(s - m)
            l = jnp.sum(p, axis=-1, keepdims=True)
            o = jnp.dot(p.astype(BF16), v2, preferred_element_type=F32) / l
            o_ref[pl.ds(r0, Q_BLOCK), hs] = o
            lse_ref[pl.ds(r0, Q_BLOCK), hs] = jnp.broadcast_to(m + jnp.log(l), (Q_BLOCK, HEAD_DIM))
        return 0

    lax.fori_loop(0, tq // Q_BLOCK, block, 0)


def _attention_group(qkv, bias, group):
    _, bsz, d, lc, _ = qkv.shape
    tq = min(lc, 1024)
    per_tq = tq // Q_BLOCK
    cur = lambda which: pl.BlockSpec((None, None, None, tq, PROJ_TILE), lambda b, r, i: (which, b, r, i, 0))
    prev = lambda which: pl.BlockSpec((None, None, None, Q_BLOCK, PROJ_TILE),
                                      lambda b, r, i: (which, b, r, jnp.maximum(i * per_tq - 1, 0), 0))
    out_spec = pl.BlockSpec((None, None, tq, ATT_OUT_WIDTH), lambda b, r, i: (b, r, i, 0))
    out_sds = jax.ShapeDtypeStruct((bsz, d, lc, ATT_OUT_WIDTH), F32)
    return pl.pallas_call(
        functools.partial(_attn_kernel, tq=tq),
        grid=(bsz, d, lc // tq),
        in_specs=[cur(0), cur(1), prev(1), cur(2), prev(2),
                  pl.BlockSpec(bias.shape, lambda b, r, i: (0, 0, 0))],
        out_specs=[out_spec, out_spec],
        out_shape=[out_sds, out_sds],
        scratch_shapes=[pltpu.VMEM((Q_BLOCK + tq, PROJ_TILE), BF16),
                        pltpu.VMEM((Q_BLOCK + tq, PROJ_TILE), BF16)],
        compiler_params=_params(("arbitrary", "arbitrary", "arbitrary"), 40),
        name=f"attention_group{group}",
    )(qkv, qkv, qkv, qkv, qkv, bias)


def _merge_kernel(*refs, tm):
    ys_ref = refs[0]
    o_refs = refs[1:1 + N_ATT_GROUPS]
    l_refs = refs[1 + N_ATT_GROUPS:1 + 2 * N_ATT_GROUPS]
    k = 1 + 2 * N_ATT_GROUPS
    n_gate = (len(refs) - k - 7) // 2
    gs_refs = refs[k:k + n_gate]
    ga_refs = refs[k + n_gate:k + 2 * n_gate]
    x_ref, gate_ref, wbs_ref, wba_ref, wout_ref, out_ref, tok_ref = refs[k + 2 * n_gate:]

    def token_order(ref, g):
        dil = DILATION_PATTERN[g][1]
        if dil == 1:
            return ref[0]
        n_chunks = ATT_OUT_WIDTH // LANES
        for r in range(dil):
            for ch in range(n_chunks):
                tok_ref[ch, pl.ds(r, tm // dil, stride=dil), :] = ref[r, :, ch * LANES:(ch + 1) * LANES]
        return jnp.concatenate([tok_ref[ch] for ch in range(n_chunks)], axis=-1)

    lses = [token_order(r, g) for g, r in enumerate(l_refs)]
    m = functools.reduce(jnp.maximum, lses)
    es = [jnp.exp(l - m) for l in lses]
    den = functools.reduce(lambda a, b: a + b, es)
    y_att = functools.reduce(lambda a, b: a + b,
                             [(e / den) * token_order(r, g) for g, (e, r) in enumerate(zip(es, o_refs))])

    m_ssm = jnp.dot(ys_ref[...], wbs_ref[...], preferred_element_type=F32)
    m_att = jnp.dot(y_att.astype(BF16), wba_ref[...], preferred_element_type=F32)
    g_ssm = jnp.concatenate([r[...] for r in gs_refs], axis=-1).astype(F32)
    g_att = jnp.concatenate([r[...] for r in ga_refs], axis=-1).astype(F32)
    merged = jax.nn.sigmoid(g_ssm) * m_ssm + jax.nn.sigmoid(g_att) * m_att
    mixed = jnp.dot(merged.astype(BF16), wout_ref[...], preferred_element_type=F32)
    out_ref[...] = x_ref[...] + gate_ref[...] * mixed


def _resident(a):
    nd = a.ndim
    return pl.BlockSpec(a.shape, lambda *_: (0,) * nd, pipeline_mode=pl.Buffered(1))


def _merge(x, gate, y_ssm, att, main, w_branch_ssm, w_branch_att, w_out, tm=256):
    bsz, seqlen, d = x.shape
    n_gate = d // PROJ_TILE
    row = lambda width: pl.BlockSpec((None, tm, width), lambda b, i: (b, i, 0))
    tile = lambda t: pl.BlockSpec((None, None, tm, PROJ_TILE), lambda b, i: (t, b, i, 0))
    res_major = lambda dil: pl.BlockSpec((None, dil, tm // dil, ATT_OUT_WIDTH), lambda b, i: (b, 0, i, 0))
    wbs, wba, wout = (w.astype(BF16) for w in (w_branch_ssm, w_branch_att, w_out))
    os_, ls_ = zip(*att)
    att_specs = [res_major(dil) for _, dil in DILATION_PATTERN]
    in_specs = ([row(SSM_WIDTH)] + att_specs + att_specs
                + [tile(MAIN_GATE0 + t) for t in range(2 * n_gate)]
                + [row(d), pl.BlockSpec((None, 1, d), lambda b, i: (b, 0, 0)),
                   _resident(wbs), _resident(wba), _resident(wout)])
    return pl.pallas_call(
        functools.partial(_merge_kernel, tm=tm),
        grid=(bsz, seqlen // tm),
        in_specs=in_specs,
        out_specs=row(d),
        out_shape=jax.ShapeDtypeStruct(x.shape, F32),
        scratch_shapes=[pltpu.VMEM((ATT_OUT_WIDTH // LANES, tm, LANES), F32)],
        compiler_params=_params(("arbitrary", "arbitrary"), 48),
        name="merge",
    )(y_ssm, *os_, *ls_, *([main] * (2 * n_gate)), x, gate[:, None, :], wbs, wba, wout)


def _ffn_kernel(x_ref, g_ref, shift_ref, scale_ref, gate_ref, wg_ref, wu_ref, wd_ref,
                o_ref, h_ref, acc_ref):
    f = pl.program_id(2)

    @pl.when(f == 0)
    def _():
        h = _norm_modulate(x_ref[...], g_ref[...], shift_ref[...], scale_ref[...])
        h_ref[...] = h.astype(BF16)
        acc_ref[...] = jnp.zeros_like(acc_ref)

    h = h_ref[...]
    a = jnp.dot(h, wg_ref[...].astype(BF16), preferred_element_type=F32)
    b = jnp.dot(h, wu_ref[...].astype(BF16), preferred_element_type=F32)
    act = (a * jax.nn.sigmoid(a)) * b
    acc_ref[...] += jnp.dot(act.astype(BF16), wd_ref[...].astype(BF16), preferred_element_type=F32)

    @pl.when(f == pl.num_programs(2) - 1)
    def _():
        o_ref[...] = x_ref[...] + gate_ref[...] * acc_ref[...]


def _dense_ffn(x, g, shift, scale, gate, w_gate, w_up, w_down, tm=1024, tf=512):
    bsz, seqlen, d = x.shape
    dff = w_gate.shape[1]
    vec = pl.BlockSpec((None, 1, d), lambda b, i, f: (b, 0, 0))
    row = pl.BlockSpec((None, tm, d), lambda b, i, f: (b, i, 0), pipeline_mode=pl.Buffered(1))
    return pl.pallas_call(
        _ffn_kernel,
        grid=(bsz, seqlen // tm, dff // tf),
        in_specs=[row,
                  pl.BlockSpec((1, d), lambda b, i, f: (0, 0)),
                  vec, vec, vec,
                  pl.BlockSpec((d, tf), lambda b, i, f: (0, f)),
                  pl.BlockSpec((d, tf), lambda b, i, f: (0, f)),
                  pl.BlockSpec((tf, d), lambda b, i, f: (f, 0))],
        out_specs=row,
        out_shape=jax.ShapeDtypeStruct(x.shape, F32),
        scratch_shapes=[pltpu.VMEM((tm, d), BF16), pltpu.VMEM((tm, d), F32)],
        compiler_params=_params(("arbitrary", "arbitrary", "arbitrary"), 56),
        name="dense_ffn",
    )(x, g.reshape(1, d), shift[:, None, :], scale[:, None, :], gate[:, None, :],
      w_gate.astype(BF16), w_up.astype(BF16), w_down.astype(BF16))


MOE_TM = 512


def _router_kernel(x_ref, g_ref, shift_ref, scale_ref, wr_ref, h_ref, idx_ref, gate_ref):
    h = _norm_modulate(x_ref[...], g_ref[...], shift_ref[...], scale_ref[...])
    h_ref[...] = h
    logits = jnp.dot(h, wr_ref[...], preferred_element_type=F32, precision=lax.Precision.HIGHEST)
    lane_i = lax.broadcasted_iota(jnp.int32, logits.shape, 1)
    lane = lane_i.astype(F32)
    logits = jnp.where(lane_i < N_EXPERTS, logits, -jnp.inf)
    m1 = jnp.max(logits, axis=-1, keepdims=True)
    i1 = jnp.min(jnp.where(logits == m1, lane, float(LANES)), axis=-1, keepdims=True)
    rest = jnp.where(lane == i1, -jnp.inf, logits)
    m2 = jnp.max(rest, axis=-1, keepdims=True)
    i2 = jnp.min(jnp.where(rest == m2, lane, float(LANES)), axis=-1, keepdims=True)
    e2 = jnp.exp(m2 - m1)
    den = 1.0 + e2
    idx_ref[...] = jnp.where(lane_i == 0, i1, jnp.where(lane_i == 1, i2, 0.0)).astype(jnp.int32)
    gate_ref[...] = jnp.where(lane_i == 0, 1.0 / den, jnp.where(lane_i == 1, e2 / den, 0.0))


def _router(x, g, shift, scale, w_router, tm=512):
    bsz, seqlen, d = x.shape
    wr = jnp.zeros((d, LANES), F32).at[:, :N_EXPERTS].set(w_router)
    row = lambda width: pl.BlockSpec((None, tm, width), lambda b, i: (b, i, 0))
    vec = pl.BlockSpec((None, 1, d), lambda b, i: (b, 0, 0))
    return pl.pallas_call(
        _router_kernel,
        grid=(bsz, seqlen // tm),
        in_specs=[row(d), pl.BlockSpec((1, d), lambda b, i: (0, 0)), vec, vec,
                  pl.BlockSpec((d, LANES), lambda b, i: (0, 0))],
        out_specs=[row(d), row(LANES), row(LANES)],
        out_shape=[jax.ShapeDtypeStruct(x.shape, F32),
                   jax.ShapeDtypeStruct((bsz, seqlen, LANES), jnp.int32),
                   jax.ShapeDtypeStruct((bsz, seqlen, LANES), F32)],
        compiler_params=_params(("arbitrary", "arbitrary"), 40),
        name="moe_router",
    )(x, g.reshape(1, d), shift[:, None, :], scale[:, None, :], wr)


def _routing_tables(top_expert, n_tok):
    flat_e = top_expert.reshape(-1)
    onehot = (flat_e[:, None] == jnp.arange(N_EXPERTS, dtype=jnp.int32)[None, :]).astype(jnp.int32)
    csum = jnp.cumsum(onehot, axis=0)
    rank = jnp.sum(csum * onehot, axis=1) - 1
    counts = csum[-1]
    padded = (counts + MOE_TM - 1) // MOE_TM * MOE_TM
    pend = jnp.cumsum(padded)
    pstart = pend - padded
    dest = (pstart[flat_e] + rank).astype(jnp.int32)
    n_rows = n_tok * TOP_K + N_EXPERTS * MOE_TM
    n_tiles = n_rows // MOE_TM
    tile_start = jnp.arange(n_tiles, dtype=jnp.int32) * MOE_TM
    tile_expert = jnp.minimum(jnp.searchsorted(pend, tile_start, side='right'), N_EXPERTS - 1).astype(jnp.int32)
    n_used = (pend[-1] // MOE_TM).astype(jnp.int32)
    return dest, counts.astype(jnp.int32), pstart.astype(jnp.int32), pend.astype(jnp.int32), tile_expert, n_used


ROW_SEMS = 64


def _gather_kernel(dest_ref, meta_ref, h_ref, xr_ref, zero_ref, sems, fill_sems, *, n_assign):
    zero_ref[...] = jnp.zeros_like(zero_ref)

    def row_copy(a):
        return pltpu.make_async_copy(h_ref.at[pl.ds(a // TOP_K, 1)], xr_ref.at[pl.ds(dest_ref[a], 1)],
                                     sems.at[a % ROW_SEMS])

    def issue(a, _):
        @pl.when(a >= ROW_SEMS)
        def _():
            row_copy(a - ROW_SEMS).wait()
        row_copy(a).start()
        return 0

    lax.fori_loop(0, n_assign, issue, 0)

    def drain(a, _):
        row_copy(a).wait()
        return 0

    lax.fori_loop(n_assign - ROW_SEMS, n_assign, drain, 0)

    fills = []
    for e in range(N_EXPERTS):
        count, pstart, pend = meta_ref[0, e], meta_ref[1, e], meta_ref[2, e]
        pos = pstart + count
        n_single = (-pos) & (SUBLANES - 1)
        for k in range(SUBLANES - 1):
            fills.append((k < n_single, pos + k, 1))
        pos = pos + n_single
        n_pad = pend - pos
        for bit in PAD_BITS:
            fills.append((n_pad & bit != 0, pl.multiple_of(pos, SUBLANES), bit))
            pos = pos + (n_pad & bit)
    n_tiles = xr_ref.shape[0] // MOE_TM
    n_used = meta_ref[2, N_EXPERTS - 1] // MOE_TM
    for k in range(N_EXPERTS):
        fills.append((n_used + k < n_tiles, pl.multiple_of((n_used + k) * MOE_TM, MOE_TM), MOE_TM))
    copies = [pltpu.make_async_copy(zero_ref.at[pl.ds(0, size)], xr_ref.at[pl.ds(pos, size)], fill_sems.at[n])
              for n, (_, pos, size) in enumerate(fills)]
    for (cond, _, _), cp in zip(fills, copies):
        pl.when(cond)(cp.start)
    for (cond, _, _), cp in zip(fills, copies):
        pl.when(cond)(cp.wait)


PAD_BITS = tuple(1 << k for k in reversed(range(3, (MOE_TM - 1).bit_length())))
N_FILLS = N_EXPERTS * (SUBLANES - 1 + len(PAD_BITS) + 1)


def _gather_rows(h2d, dest, meta, n_rows):
    n_tok, d = h2d.shape
    return pl.pallas_call(
        functools.partial(_gather_kernel, n_assign=n_tok * TOP_K),
        grid_spec=pltpu.PrefetchScalarGridSpec(
            num_scalar_prefetch=2,
            grid=(1,),
            in_specs=[pl.BlockSpec(memory_space=pl.ANY)],
            out_specs=pl.BlockSpec(memory_space=pl.ANY),
            scratch_shapes=[pltpu.VMEM((MOE_TM, d), F32), pltpu.SemaphoreType.DMA((ROW_SEMS,)),
                            pltpu.SemaphoreType.DMA((N_FILLS,))]),
        out_shape=jax.ShapeDtypeStruct((n_rows, d), F32),
        compiler_params=_params(("arbitrary",), 16),
        name="moe_gather",
    )(dest, meta, h2d)


def _moe_kernel(te_ref, nu_ref, x_ref, wg_ref, wu_ref, wd_ref, o_ref, xb_ref, acc_ref):
    i, f = pl.program_id(0), pl.program_id(1)
    used = i < nu_ref[0]

    @pl.when(jnp.logical_and(used, f == 0))
    def _():
        xb_ref[...] = x_ref[...].astype(BF16)
        acc_ref[...] = jnp.zeros_like(acc_ref)

    @pl.when(used)
    def _():
        h = xb_ref[...]
        a = jnp.dot(h, wg_ref[...].astype(BF16), preferred_element_type=F32)
        b = jnp.dot(h, wu_ref[...].astype(BF16), preferred_element_type=F32)
        act = (a * jax.nn.sigmoid(a)) * b
        acc_ref[...] += jnp.dot(act.astype(BF16), wd_ref[...].astype(BF16), preferred_element_type=F32)

    last = f == pl.num_programs(1) - 1

    @pl.when(jnp.logical_and(used, last))
    def _():
        o_ref[...] = acc_ref[...]

    @pl.when(jnp.logical_and(jnp.logical_not(used), last))
    def _():
        o_ref[...] = jnp.zeros_like(o_ref)


def _moe_experts(x_rows, tile_expert, n_used, w_gate, w_up, w_down, tf=512):
    n_rows, d = x_rows.shape
    dff = w_gate.shape[2]
    n_tiles = n_rows // MOE_TM
    row_map = lambda i, f, te, nu: (jnp.minimum(i, nu[0] - 1), 0)
    tile_f = lambda i, f, te, nu: jnp.where(i < nu[0], f, dff // tf - 1)
    return pl.pallas_call(
        _moe_kernel,
        grid_spec=pltpu.PrefetchScalarGridSpec(
            num_scalar_prefetch=2,
            grid=(n_tiles, dff // tf),
            in_specs=[pl.BlockSpec((MOE_TM, d), row_map),
                      pl.BlockSpec((None, d, tf), lambda i, f, te, nu: (te[jnp.minimum(i, nu[0] - 1)], 0, tile_f(i, f, te, nu))),
                      pl.BlockSpec((None, d, tf), lambda i, f, te, nu: (te[jnp.minimum(i, nu[0] - 1)], 0, tile_f(i, f, te, nu))),
                      pl.BlockSpec((None, tf, d), lambda i, f, te, nu: (te[jnp.minimum(i, nu[0] - 1)], tile_f(i, f, te, nu), 0))],
            out_specs=pl.BlockSpec((MOE_TM, d), lambda i, f, te, nu: (i, 0)),
            scratch_shapes=[pltpu.VMEM((MOE_TM, d), BF16), pltpu.VMEM((MOE_TM, d), F32)]),
        out_shape=jax.ShapeDtypeStruct((n_rows, d), F32),
        compiler_params=_params(("arbitrary", "arbitrary"), 58),
        name="moe_experts",
    )(tile_expert, n_used.reshape(1), x_rows, w_gate, w_up, w_down)


def _combine_kernel(dest_ref, y_ref, x_ref, gates_ref, gate_f_ref, gn_ref, o_ref, rows_ref, sems,
                    *, tm, tiles_per_batch, final_norm):
    t0 = (pl.program_id(0) * tiles_per_batch + pl.program_id(1)) * tm
    n = tm * TOP_K

    def row_copy(j):
        return pltpu.make_async_copy(y_ref.at[pl.ds(dest_ref[t0 * TOP_K + j], 1)],
                                     rows_ref.at[j % TOP_K, pl.ds(j // TOP_K, 1)], sems.at[j % ROW_SEMS])

    def issue(j, _):
        @pl.when(j >= ROW_SEMS)
        def _():
            row_copy(j - ROW_SEMS).wait()
        row_copy(j).start()
        return 0

    lax.fori_loop(0, n, issue, 0)

    def drain(j, _):
        row_copy(j).wait()
        return 0

    lax.fori_loop(n - ROW_SEMS, n, drain, 0)

    gates = gates_ref[...]
    f = gates[:, 0:1] * rows_ref[0] + gates[:, 1:2] * rows_ref[1]
    out = x_ref[...] + gate_f_ref[...] * f
    if final_norm:
        ms = jnp.mean(out * out, axis=-1, keepdims=True)
        out = (out * lax.rsqrt(ms + EPS)) * gn_ref[...]
    o_ref[...] = out


def _combine(x, y_rows, dest, top_gate, gate_f, final_g, tm=256):
    bsz, seqlen, d = x.shape
    final_norm = final_g is not None
    gn = (final_g if final_norm else jnp.ones((d,), F32)).reshape(1, d)
    tiles_per_batch = seqlen // tm
    return pl.pallas_call(
        functools.partial(_combine_kernel, tm=tm, tiles_per_batch=tiles_per_batch, final_norm=final_norm),
        grid_spec=pltpu.PrefetchScalarGridSpec(
            num_scalar_prefetch=1,
            grid=(bsz, tiles_per_batch),
            in_specs=[pl.BlockSpec(memory_space=pl.ANY),
                      pl.BlockSpec((None, tm, d), lambda b, i, dst: (b, i, 0)),
                      pl.BlockSpec((None, tm, LANES), lambda b, i, dst: (b, i, 0)),
                      pl.BlockSpec((None, 1, d), lambda b, i, dst: (b, 0, 0)),
                      pl.BlockSpec((1, d), lambda b, i, dst: (0, 0))],
            out_specs=pl.BlockSpec((None, tm, d), lambda b, i, dst: (b, i, 0)),
            scratch_shapes=[pltpu.VMEM((TOP_K, tm, d), F32), pltpu.SemaphoreType.DMA((ROW_SEMS,))]),
        out_shape=jax.ShapeDtypeStruct(x.shape, F32),
        compiler_params=_params(("arbitrary", "arbitrary"), 32),
        name="moe_combine",
    )(dest, y_rows, x, top_gate, gate_f[:, None, :], gn)


def _moe_ffn(x, g, shift, scale, gate_f, w_router, w_gate, w_up, w_down, final_g):
    bsz, seqlen, d = x.shape
    n_tok = bsz * seqlen
    h, top_idx, top_gate = _router(x, g, shift, scale, w_router)
    dest, counts, pstart, pend, tile_expert, n_used = _routing_tables(top_idx[..., :TOP_K], n_tok)
    n_rows = n_tok * TOP_K + N_EXPERTS * MOE_TM
    meta = jnp.stack([counts, pstart, pend])
    x_rows = _gather_rows(h.reshape(n_tok, d), dest, meta, n_rows)
    y_rows = _moe_experts(x_rows, tile_expert, n_used, w_gate, w_up, w_down)
    return _combine(x, y_rows, dest, top_gate, gate_f, final_g)


def _final_norm_kernel(x_ref, g_ref, o_ref):
    x = x_ref[...]
    ms = jnp.mean(x * x, axis=-1, keepdims=True)
    o_ref[...] = (x * lax.rsqrt(ms + EPS)) * g_ref[...]


def _final_norm(x, g, tm=512):
    bsz, seqlen, d = x.shape
    return pl.pallas_call(
        _final_norm_kernel,
        grid=(bsz, seqlen // tm),
        in_specs=[pl.BlockSpec((None, tm, d), lambda b, i: (b, i, 0)),
                  pl.BlockSpec((1, d), lambda b, i: (0, 0))],
        out_specs=pl.BlockSpec((None, tm, d), lambda b, i: (b, i, 0)),
        out_shape=jax.ShapeDtypeStruct(x.shape, F32),
        compiler_params=_params(("arbitrary", "arbitrary"), 32),
        name="final_norm",
    )(x, g.reshape(1, d))


def kernel(x, c, w_mod, b_mod, norm_mix_g, norm_ffn_g, w_in, ssm_a_re, ssm_a_im, ssm_log_dt, ssm_b_re, ssm_b_im, ssm_c_re, ssm_c_im, ssm_d, w_glu, b_glu, rel_bias, w_branch_ssm, w_branch_att, w_out, ffn_w_gate, ffn_w_up, ffn_w_down, moe_router, moe_w_gate, moe_w_up, moe_w_down, final_norm_g):
    depth = w_mod.shape[0]
    mod = _modulation(c, w_mod, b_mod)
    biases = [_bias_tile(rel_bias, g) for g in range(N_ATT_GROUPS)]
    for i in range(depth):
        shift_m, scale_m, gate_m, shift_f, scale_f, gate_f = jnp.split(mod[i], N_MOD, axis=-1)
        main, qkv = _in_projection(x, norm_mix_g[i], shift_m, scale_m, w_in[i])
        tables = _s5_tables(ssm_a_re[i], ssm_a_im[i], ssm_log_dt[i], ssm_b_re[i], ssm_b_im[i],
                            ssm_c_re[i], ssm_c_im[i])
        y_ssm = _s5_branch(main, tables, ssm_d[i], w_glu[i], b_glu[i])
        att = [_attention_group(qkv[g], biases[g], g) for g in range(N_ATT_GROUPS)]
        x = _merge(x, gate_m, y_ssm, att, main, w_branch_ssm[i], w_branch_att[i], w_out[i])
        j = i // 2
        last = i == depth - 1
        if i % 2 == 0:
            x = _dense_ffn(x, norm_ffn_g[i], shift_f, scale_f, gate_f,
                           ffn_w_gate[j], ffn_w_up[j], ffn_w_down[j])
            if last:
                x = _final_norm(x, final_norm_g)
        else:
            x = _moe_ffn(x, norm_ffn_g[i], shift_f, scale_f, gate_f, moe_router[j],
                         moe_w_gate[j], moe_w_up[j], moe_w_down[j], final_norm_g if last else None)
    return x
```

```python
import functools
import math

import jax
import jax.numpy as jnp
from jax import lax
from jax.experimental import pallas as pl
from jax.experimental.pallas import tpu as pltpu

F32 = jnp.float32
BF16 = jnp.bfloat16

LANES = 128
SUBLANES = 8
VMEM_BYTES = 64 * 1024 * 1024

SSM_GROUP = 16
SSM_STATE = 64
SSM_WIDTH = 1024
HEAD_DIM = 128
DILATION_PATTERN = ((128, 1), (512, 4), (2048, 16))
HEADS_PER_GROUP = 4
N_ATT_GROUPS = len(DILATION_PATTERN)
ATT_OUT_WIDTH = HEADS_PER_GROUP * HEAD_DIM
Q_BLOCK = 128
NEG_INF = -1e30
N_BUCKETS = 32
MAX_DISTANCE = 2048
N_EXPERTS = 8
TOP_K = 2
N_MOD = 6
EPS = 1e-6

PROJ_TILE = 512
U_TILE0 = 0
Q_TILE0 = SSM_WIDTH // PROJ_TILE
K_TILE0 = Q_TILE0 + N_ATT_GROUPS
V_TILE0 = K_TILE0 + N_ATT_GROUPS
GS_TILE0 = V_TILE0 + N_ATT_GROUPS
MAIN_GATE0 = Q_TILE0


def _params(dims, vmem_mb):
    return pltpu.CompilerParams(dimension_semantics=dims,
                                vmem_limit_bytes=vmem_mb * 1024 * 1024)


def _norm_modulate(x, g, shift, scale):
    ms = jnp.mean(x * x, axis=-1, keepdims=True)
    y = x * lax.rsqrt(ms + EPS)
    return (y * g) * (1.0 + scale) + shift


def _mod_kernel(c_ref, w_ref, b_ref, o_ref):
    c = c_ref[...]
    cond = (c * jax.nn.sigmoid(c)).astype(BF16)
    o_ref[...] = jnp.dot(cond, w_ref[...].astype(BF16),
                         preferred_element_type=F32) + b_ref[...]


def _modulation(c, w_mod, b_mod):
    depth, d, n = w_mod.shape
    bsz = c.shape[0]
    rows = SUBLANES
    c_pad = jnp.zeros((rows, d), F32).at[:bsz].set(c)
    tn = 1536
    out = pl.pallas_call(
        _mod_kernel,
        grid=(depth, n // tn),
        in_specs=[pl.BlockSpec((rows, d), lambda l, j: (0, 0)),
                  pl.BlockSpec((None, d, tn), lambda l, j: (l, 0, j)),
                  pl.BlockSpec((None, 1, tn), lambda l, j: (l, 0, j))],
        out_specs=pl.BlockSpec((None, rows, tn), lambda l, j: (l, 0, j)),
        out_shape=jax.ShapeDtypeStruct((depth, rows, n), F32),
        compiler_params=_params(("arbitrary", "arbitrary"), 40),
        name="modulation",
    )(c_pad, w_mod, b_mod.reshape(depth, 1, n))
    return out[:, :bsz]


def _inproj_kernel(x_ref, g_ref, shift_ref, scale_ref, w_ref, main_ref, *rest, tm):
    qkv_refs, (h_ref, res_ref) = rest[:N_ATT_GROUPS], rest[N_ATT_GROUPS:]
    j = pl.program_id(2)

    @pl.when(j == 0)
    def _():
        h = _norm_modulate(x_ref[...], g_ref[...], shift_ref[...], scale_ref[...])
        h_ref[...] = h.astype(BF16)

    res = jnp.dot(h_ref[...], w_ref[...].astype(BF16), preferred_element_type=F32)

    @pl.when(jnp.logical_or(j < Q_TILE0, j >= GS_TILE0))
    def _():
        main_ref[...] = res.astype(BF16)

    for g, (_, dil) in enumerate(DILATION_PATTERN):
        is_g = functools.reduce(jnp.logical_or, [j == t0 + g for t0 in (Q_TILE0, K_TILE0, V_TILE0)])

        @pl.when(is_g)
        def _(g=g, dil=dil):
            if dil == 1:
                qkv_refs[g][0] = res.astype(BF16)
            else:
                n_chunks = PROJ_TILE // LANES
                for ch in range(n_chunks):
                    res_ref[ch] = res[:, ch * LANES:(ch + 1) * LANES]
                for r in range(dil):
                    rows = [res_ref[ch, pl.ds(r, tm // dil, stride=dil), :] for ch in range(n_chunks)]
                    qkv_refs[g][r] = jnp.concatenate(rows, axis=-1).astype(BF16)


def _in_projection(x, g, shift, scale, w_in, layer, tm=1024):
    bsz, seqlen, d = x.shape
    n_tiles = w_in.shape[2] // PROJ_TILE
    n_main = n_tiles - 3 * N_ATT_GROUPS

    def main_map(b, i, j):
        return (jnp.where(j < Q_TILE0, j, jnp.where(j < GS_TILE0, Q_TILE0 - 1, j - 3 * N_ATT_GROUPS)), b, i, 0)

    def qkv_map(g):
        return lambda b, i, j: ((j > Q_TILE0 + g).astype(jnp.int32) + (j > K_TILE0 + g).astype(jnp.int32),
                                b, 0, i, 0)

    qkv_specs = [pl.BlockSpec((None, None, dil, tm // dil, PROJ_TILE), qkv_map(g))
                 for g, (_, dil) in enumerate(DILATION_PATTERN)]
    qkv_shapes = [jax.ShapeDtypeStruct((3, bsz, dil, seqlen // dil, PROJ_TILE), BF16)
                  for _, dil in DILATION_PATTERN]
    outs = pl.pallas_call(
        functools.partial(_inproj_kernel, tm=tm),
        grid=(bsz, seqlen // tm, n_tiles),
        in_specs=[pl.BlockSpec((None, tm, d), lambda b, i, j: (b, i, 0)),
                  pl.BlockSpec((1, d), lambda b, i, j: (0, 0)),
                  pl.BlockSpec((None, 1, d), lambda b, i, j: (b, 0, 0)),
                  pl.BlockSpec((None, 1, d), lambda b, i, j: (b, 0, 0)),
                  pl.BlockSpec((None, d, PROJ_TILE), lambda b, i, j: (layer, 0, j))],
        out_specs=[pl.BlockSpec((None, None, tm, PROJ_TILE), main_map)] + qkv_specs,
        out_shape=[jax.ShapeDtypeStruct((n_main, bsz, seqlen, PROJ_TILE), BF16)] + qkv_shapes,
        scratch_shapes=[pltpu.VMEM((tm, d), BF16), pltpu.VMEM((PROJ_TILE // LANES, tm, LANES), F32)],
        compiler_params=_params(("arbitrary", "arbitrary", "arbitrary"), 52),
        name="in_projection",
    )(x, g.reshape(1, d), shift[:, None, :], scale[:, None, :], w_in)
    return outs[0], outs[1:]


SSM_GB = 16
SSM_NGB = (SSM_WIDTH // SSM_GROUP) // SSM_GB
SSM_GB_IN = SSM_GB * SSM_GROUP
SSM_GB_RE = SSM_GB * SSM_STATE
SSM_CB = 2 * SSM_GB_RE // LANES
SSM_SLOTS = SSM_NGB * SSM_CB + SUBLANES


def _s5_kernel(u0_ref, u1_ref, bm_ref, cm_ref, ar_ref, ai_ref, d_ref, wglu_ref, bglu_ref,
               o_ref, bu_ref, xs_ref, st_ref, *, bsz, tt):
    stride = tt + SUBLANES
    rows = bsz * tt

    @pl.when(pl.program_id(0) == 0)
    def _():
        st_ref[...] = jnp.zeros_like(st_ref)

    u_halves = (u0_ref[...].reshape(rows, PROJ_TILE), u1_ref[...].reshape(rows, PROJ_TILE))

    for gb in range(SSM_NGB):
        half, off = divmod(gb * SSM_GB_IN, PROJ_TILE)
        bu = jnp.dot(u_halves[half][:, off:off + SSM_GB_IN], bm_ref[gb],
                     preferred_element_type=F32)
        for b in range(bsz):
            for cb in range(SSM_CB):
                blk = (b * SSM_NGB + gb) * SSM_CB + cb
                bu_ref[blk * stride: blk * stride + tt, :] = (
                    bu[b * tt:(b + 1) * tt, cb * LANES:(cb + 1) * LANES])

    a_re = [ar_ref[gb] for gb in range(SSM_NGB)]
    a_im = [ai_ref[gb] for gb in range(SSM_NGB)]
    half_cb = SSM_CB // 2

    def step(t, carry):
        new = []
        for b in range(bsz):
            for gb in range(SSM_NGB):
                k = (b * SSM_NGB + gb) * 2
                s_re, s_im = carry[k], carry[k + 1]
                base = ((b * SSM_NGB + gb) * SSM_CB) * stride
                bu_re = bu_ref[pl.ds(base + t, half_cb, stride=stride), :]
                bu_im = bu_ref[pl.ds(base + half_cb * stride + t, half_cb, stride=stride), :]
                n_re = a_re[gb] * s_re - a_im[gb] * s_im + bu_re
                n_im = a_re[gb] * s_im + a_im[gb] * s_re + bu_im
                row = pl.multiple_of((b * tt + t) * SSM_SLOTS + gb * SSM_CB, SUBLANES)
                xs_ref[pl.ds(row, half_cb), :] = n_re
                xs_ref[pl.ds(row + half_cb, half_cb), :] = n_im
                new += [n_re, n_im]
        return tuple(new)

    n_carry = bsz * SSM_NGB * 2
    carry = lax.fori_loop(0, tt, step, tuple(st_ref[k] for k in range(n_carry)), unroll=4)
    for k in range(n_carry):
        st_ref[k] = carry[k]

    ys = []
    for gb in range(SSM_NGB):
        cols = [xs_ref[pl.ds(gb * SSM_CB + cb, rows, stride=SSM_SLOTS), :] for cb in range(SSM_CB)]
        xg = jnp.concatenate(cols, axis=-1).astype(BF16)
        ys.append(jnp.dot(xg, cm_ref[gb], preferred_element_type=F32))
    y = jnp.concatenate(ys, axis=-1)
    u32 = jnp.concatenate(u_halves, axis=-1).astype(F32)
    y = jax.nn.gelu(y + d_ref[...] * u32)
    z = jnp.dot(y.astype(BF16), wglu_ref[...], preferred_element_type=F32) + bglu_ref[...]
    o_ref[...] = (y * jax.nn.sigmoid(z)).astype(o_ref.dtype).reshape(bsz, tt, SSM_WIDTH)


def _block_diag(m):
    ngb, gb, r, c = m.shape
    eye = jnp.eye(gb, dtype=m.dtype)
    return jnp.einsum('ngrc,gk->ngrkc', m, eye).reshape(ngb, gb * r, gb * c)


def _s5_tables(a_re, a_im, log_dt, b_re, b_im, c_re, c_im):
    lam = lax.complex(a_re.astype(F32), a_im.astype(F32))
    dt = jnp.exp(log_dt.astype(F32))[:, None]
    a_bar = jnp.exp(lam * dt)
    b_bar = ((a_bar - 1.0) / lam)[:, :, None] * lax.complex(b_re.astype(F32), b_im.astype(F32))
    g, p, h = b_bar.shape
    bt = jnp.transpose(b_bar, (0, 2, 1)).reshape(SSM_NGB, SSM_GB, h, p)
    bm = jnp.concatenate([_block_diag(bt.real), _block_diag(bt.imag)], axis=-1)
    ct_re = jnp.transpose(c_re.astype(F32), (0, 2, 1)).reshape(SSM_NGB, SSM_GB, p, h)
    ct_im = jnp.transpose(c_im.astype(F32), (0, 2, 1)).reshape(SSM_NGB, SSM_GB, p, h)
    cm = jnp.concatenate([_block_diag(ct_re), -_block_diag(ct_im)], axis=1)
    ar = a_bar.real.reshape(SSM_NGB, SSM_CB // 2, LANES)
    ai = a_bar.imag.reshape(SSM_NGB, SSM_CB // 2, LANES)
    return bm.astype(BF16), cm.astype(BF16), ar, ai


def _s5_branch(proj, tables, d_skip, w_glu, b_glu, tt=128):
    _, bsz, seqlen, _ = proj.shape
    bm, cm, ar, ai = tables
    stride = tt + SUBLANES
    n_blocks = bsz * SSM_NGB * SSM_CB
    d_skip = d_skip.reshape(1, SSM_WIDTH).astype(F32)
    w_glu = w_glu.astype(BF16)
    b_glu = b_glu.reshape(1, SSM_WIDTH).astype(F32)
    return pl.pallas_call(
        functools.partial(_s5_kernel, bsz=bsz, tt=tt),
        grid=(seqlen // tt,),
        in_specs=[pl.BlockSpec((None, bsz, tt, PROJ_TILE), lambda t: (U_TILE0, 0, t, 0)),
                  pl.BlockSpec((None, bsz, tt, PROJ_TILE), lambda t: (U_TILE0 + 1, 0, t, 0)),
                  _resident(bm), _resident(cm), _resident(ar), _resident(ai),
                  _resident(d_skip), _resident(w_glu), _resident(b_glu)],
        out_specs=pl.BlockSpec((bsz, tt, SSM_WIDTH), lambda t: (0, t, 0)),
        out_shape=jax.ShapeDtypeStruct((bsz, seqlen, SSM_WIDTH), BF16),
        scratch_shapes=[pltpu.VMEM((n_blocks * stride, LANES), F32),
                        pltpu.VMEM((bsz * tt * SSM_SLOTS, LANES), F32),
                        pltpu.VMEM((bsz * SSM_NGB * 2, SSM_CB // 2, LANES), F32)],
        compiler_params=_params(("arbitrary",), 48),
        name="s5_branch",
    )(proj, proj, bm, cm, ar, ai, d_skip, w_glu, b_glu)


def _t5_causal_bucket(dist):
    max_exact = N_BUCKETS // 2
    d32 = jnp.maximum(dist, 1).astype(F32)
    large = max_exact + (jnp.log(d32 / max_exact) / math.log(MAX_DISTANCE / max_exact)
                         * (N_BUCKETS - max_exact)).astype(jnp.int32)
    return jnp.where(dist < max_exact, dist, jnp.minimum(large, N_BUCKETS - 1))


def _bias_tile(rel_bias, group):
    window, dilation = DILATION_PATTERN[group]
    steps = window // dilation
    assert steps == Q_BLOCK
    heads = slice(group * HEADS_PER_GROUP, (group + 1) * HEADS_PER_GROUP)
    back = jnp.arange(steps, -1, -1, dtype=jnp.int32)
    vals = rel_bias[_t5_causal_bucket(back * dilation)][:, heads].astype(F32).T
    period = 3 * Q_BLOCK
    v = jnp.concatenate([vals, jnp.full((HEADS_PER_GROUP, period - steps - 1), NEG_INF, F32)], axis=1)
    flat = jnp.tile(v, (1, Q_BLOCK))[:, :Q_BLOCK * (period - 1)]
    return flat.reshape(HEADS_PER_GROUP, Q_BLOCK, period - 1)[:, :, :2 * Q_BLOCK]


def _attn_kernel(q_ref, kc_ref, kp_ref, vc_ref, vp_ref, bias_ref, o_ref, lse_ref, kf_ref, vf_ref, *, tq):
    kf_ref[0:Q_BLOCK, :] = kp_ref[...]
    kf_ref[Q_BLOCK:, :] = kc_ref[...]
    vf_ref[0:Q_BLOCK, :] = vp_ref[...]
    vf_ref[Q_BLOCK:, :] = vc_ref[...]
    scale = HEAD_DIM ** -0.5
    first_tile = pl.program_id(2) == 0
    col = lax.broadcasted_iota(jnp.int32, (Q_BLOCK, 2 * Q_BLOCK), 1)

    def block(jb, _):
        r0 = pl.multiple_of(jb * Q_BLOCK, Q_BLOCK)
        no_prev = jnp.logical_and(first_tile, jb == 0)
        for h in range(HEADS_PER_GROUP):
            hs = slice(h * HEAD_DIM, (h + 1) * HEAD_DIM)
            q = q_ref[pl.ds(r0, Q_BLOCK), hs]
            k2 = kf_ref[pl.ds(r0, 2 * Q_BLOCK), hs]
            v2 = vf_ref[pl.ds(r0, 2 * Q_BLOCK), hs]
            s = lax.dot_general(q, k2, (((1,), (1,)), ((), ())), preferred_element_type=F32)
            s = s * scale + bias_ref[h]
            s = jnp.where(jnp.logical_and(no_prev, col < Q_BLOCK), NEG_INF, s)
            m = jnp.max(s, axis=-1, keepdims=True)
            p = jnp.exp(s - m)
            l = jnp.sum(p, axis=-1, keepdims=True)
            o = jnp.dot(p.astype(BF16), v2, preferred_element_type=F32) / l
            o_ref[pl.ds(r0, Q_BLOCK), hs] = o
            lse_ref[pl.ds(r0, Q_BLOCK), hs] = jnp.broadcast_to(m + jnp.log(l), (Q_BLOCK, HEAD_DIM))
        return 0

    lax.fori_loop(0, tq // Q_BLOCK, block, 0)


def _attention_group(qkv, bias, group):
    _, bsz, d, lc, _ = qkv.shape
    tq = min(lc, 1024)
    per_tq = tq // Q_BLOCK
    cur = lambda which: pl.BlockSpec((None, None, None, tq, PROJ_TILE), lambda b, r, i: (which, b, r, i, 0))
    prev = lambda which: pl.BlockSpec((None, None, None, Q_BLOCK, PROJ_TILE),
                                      lambda b, r, i: (which, b, r, jnp.maximum(i * per_tq - 1, 0), 0))
    out_spec = pl.BlockSpec((None, None, tq, ATT_OUT_WIDTH), lambda b, r, i: (b, r, i, 0))
    out_sds = jax.ShapeDtypeStruct((bsz, d, lc, ATT_OUT_WIDTH), F32)
    return pl.pallas_call(
        functools.partial(_attn_kernel, tq=tq),
        grid=(bsz, d, lc // tq),
        in_specs=[cur(0), cur(1), prev(1), cur(2), prev(2),
                  pl.BlockSpec(bias.shape, lambda b, r, i: (0, 0, 0))],
        out_specs=[out_spec, out_spec],
        out_shape=[out_sds, out_sds],
        scratch_shapes=[pltpu.VMEM((Q_BLOCK + tq, PROJ_TILE), BF16),
                        pltpu.VMEM((Q_BLOCK + tq, PROJ_TILE), BF16)],
        compiler_params=_params(("arbitrary", "arbitrary", "arbitrary"), 40),
        name=f"attention_group{group}",
    )(qkv, qkv, qkv, qkv, qkv, bias)


def _merge_kernel(*refs, tm):
    ys_ref = refs[0]
    o_refs = refs[1:1 + N_ATT_GROUPS]
    l_refs = refs[1 + N_ATT_GROUPS:1 + 2 * N_ATT_GROUPS]
    k = 1 + 2 * N_ATT_GROUPS
    n_gate = (len(refs) - k - 7) // 2
    gs_refs = refs[k:k + n_gate]
    ga_refs = refs[k + n_gate:k + 2 * n_gate]
    x_ref, gate_ref, wbs_ref, wba_ref, wout_ref, out_ref, tok_ref = refs[k + 2 * n_gate:]

    def token_order(ref, g):
        dil = DILATION_PATTERN[g][1]
        if dil == 1:
            return ref[0]
        n_chunks = ATT_OUT_WIDTH // LANES
        for r in range(dil):
            for ch in range(n_chunks):
                tok_ref[ch, pl.ds(r, tm // dil, stride=dil), :] = ref[r, :, ch * LANES:(ch + 1) * LANES]
        return jnp.concatenate([tok_ref[ch] for ch in range(n_chunks)], axis=-1)

    lses = [token_order(r, g) for g, r in enumerate(l_refs)]
    m = functools.reduce(jnp.maximum, lses)
    es = [jnp.exp(l - m) for l in lses]
    den = functools.reduce(lambda a, b: a + b, es)
    y_att = functools.reduce(lambda a, b: a + b,
                             [(e / den) * token_order(r, g) for g, (e, r) in enumerate(zip(es, o_refs))])

    m_ssm = jnp.dot(ys_ref[...], wbs_ref[...], preferred_element_type=F32)
    m_att = jnp.dot(y_att.astype(BF16), wba_ref[...], preferred_element_type=F32)
    g_ssm = jnp.concatenate([r[...] for r in gs_refs], axis=-1).astype(F32)
    g_att = jnp.concatenate([r[...] for r in ga_refs], axis=-1).astype(F32)
    merged = jax.nn.sigmoid(g_ssm) * m_ssm + jax.nn.sigmoid(g_att) * m_att
    mixed = jnp.dot(merged.astype(BF16), wout_ref[...], preferred_element_type=F32)
    out_ref[...] = x_ref[...] + gate_ref[...] * mixed


def _resident(a):
    nd = a.ndim
    return pl.BlockSpec(a.shape, lambda *_: (0,) * nd, pipeline_mode=pl.Buffered(1))


def _merge(x, gate, y_ssm, att, main, w_branch_ssm, w_branch_att, w_out, tm=256):
    bsz, seqlen, d = x.shape
    n_gate = d // PROJ_TILE
    row = lambda width: pl.BlockSpec((None, tm, width), lambda b, i: (b, i, 0))
    tile = lambda t: pl.BlockSpec((None, None, tm, PROJ_TILE), lambda b, i: (t, b, i, 0))
    res_major = lambda dil: pl.BlockSpec((None, dil, tm // dil, ATT_OUT_WIDTH), lambda b, i: (b, 0, i, 0))
    wbs, wba, wout = (w.astype(BF16) for w in (w_branch_ssm, w_branch_att, w_out))
    os_, ls_ = zip(*att)
    att_specs = [res_major(dil) for _, dil in DILATION_PATTERN]
    in_specs = ([row(SSM_WIDTH)] + att_specs + att_specs
                + [tile(MAIN_GATE0 + t) for t in range(2 * n_gate)]
                + [row(d), pl.BlockSpec((None, 1, d), lambda b, i: (b, 0, 0)),
                   _resident(wbs), _resident(wba), _resident(wout)])
    return pl.pallas_call(
        functools.partial(_merge_kernel, tm=tm),
        grid=(bsz, seqlen // tm),
        in_specs=in_specs,
        out_specs=row(d),
        out_shape=jax.ShapeDtypeStruct(x.shape, F32),
        scratch_shapes=[pltpu.VMEM((ATT_OUT_WIDTH // LANES, tm, LANES), F32)],
        compiler_params=_params(("arbitrary", "arbitrary"), 48),
        name="merge",
    )(y_ssm, *os_, *ls_, *([main] * (2 * n_gate)), x, gate[:, None, :], wbs, wba, wout)


def _ffn_kernel(x_ref, g_ref, shift_ref, scale_ref, gate_ref, wg_ref, wu_ref, wd_ref,
                o_ref, h_ref, acc_ref):
    f = pl.program_id(2)

    @pl.when(f == 0)
    def _():
        h = _norm_modulate(x_ref[...], g_ref[...], shift_ref[...], scale_ref[...])
        h_ref[...] = h.astype(BF16)
        acc_ref[...] = jnp.zeros_like(acc_ref)

    h = h_ref[...]
    a = jnp.dot(h, wg_ref[...].astype(BF16), preferred_element_type=F32)
    b = jnp.dot(h, wu_ref[...].astype(BF16), preferred_element_type=F32)
    act = (a * jax.nn.sigmoid(a)) * b
    acc_ref[...] += jnp.dot(act.astype(BF16), wd_ref[...].astype(BF16), preferred_element_type=F32)

    @pl.when(f == pl.num_programs(2) - 1)
    def _():
        o_ref[...] = x_ref[...] + gate_ref[...] * acc_ref[...]


def _dense_ffn(x, g, shift, scale, gate, w_gate, w_up, w_down, tm=1024, tf=512):
    bsz, seqlen, d = x.shape
    dff = w_gate.shape[1]
    vec = pl.BlockSpec((None, 1, d), lambda b, i, f: (b, 0, 0))
    row = pl.BlockSpec((None, tm, d), lambda b, i, f: (b, i, 0), pipeline_mode=pl.Buffered(1))
    return pl.pallas_call(
        _ffn_kernel,
        grid=(bsz, seqlen // tm, dff // tf),
        in_specs=[row,
                  pl.BlockSpec((1, d), lambda b, i, f: (0, 0)),
                  vec, vec, vec,
                  pl.BlockSpec((d, tf), lambda b, i, f: (0, f)),
                  pl.BlockSpec((d, tf), lambda b, i, f: (0, f)),
                  pl.BlockSpec((tf, d), lambda b, i, f: (f, 0))],
        out_specs=row,
        out_shape=jax.ShapeDtypeStruct(x.shape, F32),
        scratch_shapes=[pltpu.VMEM((tm, d), BF16), pltpu.VMEM((tm, d), F32)],
        compiler_params=_params(("arbitrary", "arbitrary", "arbitrary"), 56),
        name="dense_ffn",
    )(x, g.reshape(1, d), shift[:, None, :], scale[:, None, :], gate[:, None, :],
      w_gate.astype(BF16), w_up.astype(BF16), w_down.astype(BF16))


MOE_SUB = 512
MOE_SUPER = 2 * MOE_SUB


def _pack_halves(x):
    half = x.shape[-1] // 2
    bits = lambda v: lax.bitcast_convert_type(v.astype(BF16).astype(F32), jnp.uint32)
    return bits(x[:, half:]) | (bits(x[:, :half]) >> 16)


def _unpack_halves(w):
    lo = lax.bitcast_convert_type(w << 16, F32)
    hi = lax.bitcast_convert_type(w & jnp.uint32(0xFFFF0000), F32)
    return lo, hi


def _router_kernel(x_ref, g_ref, shift_ref, scale_ref, wr_ref, h_ref, idx_ref, gate_ref):
    h = _norm_modulate(x_ref[...], g_ref[...], shift_ref[...], scale_ref[...])
    h_ref[...] = _pack_halves(h)
    logits = jnp.dot(h, wr_ref[...], preferred_element_type=F32, precision=lax.Precision.HIGHEST)
    lane_i = lax.broadcasted_iota(jnp.int32, logits.shape, 1)
    lane = lane_i.astype(F32)
    logits = jnp.where(lane_i < N_EXPERTS, logits, -jnp.inf)
    m1 = jnp.max(logits, axis=-1, keepdims=True)
    i1 = jnp.min(jnp.where(logits == m1, lane, float(LANES)), axis=-1, keepdims=True)
    rest = jnp.where(lane == i1, -jnp.inf, logits)
    m2 = jnp.max(rest, axis=-1, keepdims=True)
    i2 = jnp.min(jnp.where(rest == m2, lane, float(LANES)), axis=-1, keepdims=True)
    e2 = jnp.exp(m2 - m1)
    den = 1.0 + e2
    idx_ref[...] = jnp.where(lane_i == 0, i1, jnp.where(lane_i == 1, i2, 0.0)).astype(jnp.int32)
    gate_ref[...] = jnp.where(lane_i == 0, 1.0 / den, jnp.where(lane_i == 1, e2 / den, 0.0))


def _router(x, g, shift, scale, w_router, tm=512):
    bsz, seqlen, d = x.shape
    wr = jnp.zeros((d, LANES), F32).at[:, :N_EXPERTS].set(w_router)
    row = lambda width: pl.BlockSpec((None, tm, width), lambda b, i: (b, i, 0))
    vec = pl.BlockSpec((None, 1, d), lambda b, i: (b, 0, 0))
    return pl.pallas_call(
        _router_kernel,
        grid=(bsz, seqlen // tm),
        in_specs=[row(d), pl.BlockSpec((1, d), lambda b, i: (0, 0)), vec, vec,
                  pl.BlockSpec((d, LANES), lambda b, i: (0, 0))],
        out_specs=[row(d // 2), row(LANES), row(LANES)],
        out_shape=[jax.ShapeDtypeStruct((bsz, seqlen, d // 2), jnp.uint32),
                   jax.ShapeDtypeStruct((bsz, seqlen, LANES), jnp.int32),
                   jax.ShapeDtypeStruct((bsz, seqlen, LANES), F32)],
        compiler_params=_params(("arbitrary", "arbitrary"), 40),
        name="moe_router",
    )(x, g.reshape(1, d), shift[:, None, :], scale[:, None, :], wr)


def _moe_dims(n_tok, nf):
    n_sub = n_tok * TOP_K // MOE_SUB + N_EXPERTS
    n_super = (n_sub + N_EXPERTS) // 2 + 1
    rows_per_step = -(-MOE_SUPER // nf)
    while (rows_per_step * nf) % SUBLANES:
        rows_per_step += 1
    n_fetch = rows_per_step * nf
    return n_sub, n_super, rows_per_step, n_fetch


def _routing_tables(top_expert, n_tok, nf):
    n_sub, n_super, _, n_fetch = _moe_dims(n_tok, nf)
    i32 = jnp.int32
    flat_e = top_expert.reshape(-1)
    onehot = (flat_e[:, None] == jnp.arange(N_EXPERTS, dtype=i32)[None, :]).astype(i32)
    csum = jnp.cumsum(onehot, axis=0)
    rank = jnp.sum(csum * onehot, axis=1) - 1
    counts = csum[-1]
    subs = (counts + MOE_SUB - 1) // MOE_SUB
    pend = jnp.cumsum(subs) * MOE_SUB
    pstart = pend - subs * MOE_SUB
    dest = (pstart[flat_e] + rank).astype(i32)
    flat_token = jnp.arange(n_tok * TOP_K, dtype=i32) // TOP_K
    row_token = jnp.zeros((n_sub * MOE_SUB + n_fetch,), i32).at[dest].set(flat_token)
    supers = (subs + 1) // 2
    send = jnp.cumsum(supers)
    sstart = send - supers
    s = jnp.arange(n_super + 1, dtype=i32)
    e = jnp.minimum(jnp.searchsorted(send, s, side='right'), N_EXPERTS - 1).astype(i32)
    local = s - sstart[e]
    used = s < send[-1]
    tile_expert = jnp.where(used, e, e[jnp.maximum(send[-1] - 1, 0)]).astype(i32)
    tile_row0 = jnp.where(used, pstart[e] + local * MOE_SUPER, 0).astype(i32)
    tile_nsub = jnp.where(used, jnp.minimum(subs[e] - 2 * local, 2), 0).astype(i32)
    n_used = send[-1].astype(i32).reshape(1)
    n_sub_used = (pend[-1] // MOE_SUB).astype(i32).reshape(1)
    return dest, row_token, tile_expert, tile_row0, tile_nsub, n_used, n_sub_used


def _moe_kernel(te_ref, row0_ref, nsub_ref, nu_ref, nsu_ref, tok_ref, hp_ref, wg_ref, wu_ref, wd_ref, y_ref,
                xg_ref, xb_ref, acc_ref, yb_ref, gsem, osem, fsem, *, nf, rows_per_step, n_sub_alloc, n_fill):
    s, f = pl.program_id(0), pl.program_id(1)
    n_used = nu_ref[0]
    used = s < n_used
    slot = s % 2
    n_fetch = rows_per_step * nf
    half = xg_ref.shape[-1]

    def row_copy(tile, slot_, r):
        tok = tok_ref[row0_ref[tile] + r]
        return pltpu.make_async_copy(hp_ref.at[pl.ds(tok, 1)], xg_ref.at[slot_, pl.ds(r, 1)], gsem.at[slot_])

    def wait_rows(slot_):
        pltpu.make_async_copy(hp_ref.at[pl.ds(0, n_fetch)], xg_ref.at[slot_], gsem.at[slot_]).wait()

    def out_copies(tile):
        r0 = pl.multiple_of(row0_ref[tile], MOE_SUB)
        return [pltpu.make_async_copy(yb_ref.at[pl.ds(k * MOE_SUB, MOE_SUB)],
                                      y_ref.at[pl.ds(r0 + k * MOE_SUB, MOE_SUB)], osem.at[k]) for k in range(2)]

    def wait_out(tile):
        first, second = out_copies(tile)
        first.wait()
        pl.when(nsub_ref[tile] == 2)(second.wait)

    @pl.when(jnp.logical_and(s == 0, f == 0))
    def _():
        def body(r, _):
            row_copy(0, 0, r).start()
            return 0
        lax.fori_loop(0, n_fetch, body, 0)

    @pl.when(jnp.logical_and(f == 0, s <= n_used))
    def _():
        wait_rows(slot)

    @pl.when(jnp.logical_and(used, f == 0))
    def _():
        lo, hi = _unpack_halves(xg_ref[slot, 0:MOE_SUPER, :])
        xb_ref[:, :half] = lo.astype(BF16)
        xb_ref[:, half:] = hi.astype(BF16)
        acc_ref[...] = jnp.zeros_like(acc_ref)

    @pl.when(used)
    def _():
        for k in range(rows_per_step):
            row_copy(s + 1, 1 - slot, f * rows_per_step + k).start()
        wg = wg_ref[...].astype(BF16)
        wu = wu_ref[...].astype(BF16)
        wd = wd_ref[...].astype(BF16)

        def sub_tile(r0):
            h = xb_ref[r0:r0 + MOE_SUB, :]
            a = jnp.dot(h, wg, preferred_element_type=F32)
            b = jnp.dot(h, wu, preferred_element_type=F32)
            act = (a * jax.nn.sigmoid(a)) * b
            acc_ref[r0:r0 + MOE_SUB, :] += jnp.dot(act.astype(BF16), wd, preferred_element_type=F32)

        sub_tile(0)
        pl.when(nsub_ref[s] == 2)(functools.partial(sub_tile, MOE_SUB))

    @pl.when(jnp.logical_and(used, f == nf - 1))
    def _():
        pl.when(s > 0)(functools.partial(wait_out, s - 1))
        yb_ref[...] = _pack_halves(acc_ref[...])
        first, second = out_copies(s)
        first.start()
        pl.when(nsub_ref[s] == 2)(second.start)

    @pl.when(jnp.logical_and(s == n_used, f == 0))
    def _():
        wait_out(s - 1)
        yb_ref[0:MOE_SUB, :] = jnp.zeros((MOE_SUB, half), yb_ref.dtype)
        fills = []
        for k in range(n_fill):
            sub = nsu_ref[0] + k
            cp = pltpu.make_async_copy(yb_ref.at[pl.ds(0, MOE_SUB)],
                                       y_ref.at[pl.ds(pl.multiple_of(sub * MOE_SUB, MOE_SUB), MOE_SUB)], fsem.at[k])
            fills.append((sub < n_sub_alloc, cp))
        for cond, cp in fills:
            pl.when(cond)(cp.start)
        for cond, cp in fills:
            pl.when(cond)(cp.wait)


def _moe_experts(hp, tables, w_gate, w_up, w_down, tf=512):
    n_tok, half = hp.shape
    d = 2 * half
    dff = w_gate.shape[2]
    nf = dff // tf
    n_sub, n_super, rows_per_step, n_fetch = _moe_dims(n_tok, nf)
    _, row_token, tile_expert, tile_row0, tile_nsub, n_used, n_sub_used = tables
    n_unused_max = n_sub - n_tok * TOP_K // MOE_SUB
    tile_f = lambda s, f, nu: jnp.where(s < nu[0], f, nf - 1)
    w_in_spec = pl.BlockSpec((None, d, tf), lambda s, f, te, r0, ns, nu, nsu, tok: (te[s], 0, tile_f(s, f, nu)))
    w_out_spec = pl.BlockSpec((None, tf, d), lambda s, f, te, r0, ns, nu, nsu, tok: (te[s], tile_f(s, f, nu), 0))
    return pl.pallas_call(
        functools.partial(_moe_kernel, nf=nf, rows_per_step=rows_per_step, n_sub_alloc=n_sub,
                          n_fill=n_unused_max),
        grid_spec=pltpu.PrefetchScalarGridSpec(
            num_scalar_prefetch=6,
            grid=(n_super, nf),
            in_specs=[pl.BlockSpec(memory_space=pl.ANY), w_in_spec, w_in_spec, w_out_spec],
            out_specs=pl.BlockSpec(memory_space=pl.ANY),
            scratch_shapes=[pltpu.VMEM((2, n_fetch, half), jnp.uint32),
                            pltpu.VMEM((MOE_SUPER, d), BF16),
                            pltpu.VMEM((MOE_SUPER, d), F32),
                            pltpu.VMEM((MOE_SUPER, half), jnp.uint32),
                            pltpu.SemaphoreType.DMA((2,)), pltpu.SemaphoreType.DMA((2,)),
                            pltpu.SemaphoreType.DMA((n_unused_max,))]),
        out_shape=jax.ShapeDtypeStruct((n_sub * MOE_SUB, half), jnp.uint32),
        compiler_params=_params(("arbitrary", "arbitrary"), 58),
        name="moe_experts",
    )(tile_expert, tile_row0, tile_nsub, n_used, n_sub_used, row_token, hp, w_gate, w_up, w_down)


COMBINE_UNROLL = 8


def _combine_kernel(dest_ref, y_ref, x_ref, gates_ref, gate_f_ref, gn_ref, o_ref, rows_ref, sems,
                    *, tm, n_steps, final_norm):
    t = pl.program_id(0)
    slot = t % 2
    n = tm * TOP_K
    half = rows_ref.shape[-1]

    def issue(tile, slot_):
        base = tile * n

        def body(i, _):
            for u in range(COMBINE_UNROLL):
                j = i * COMBINE_UNROLL + u
                row = (u % TOP_K) * tm + i * (COMBINE_UNROLL // TOP_K) + u // TOP_K
                pltpu.make_async_copy(y_ref.at[pl.ds(dest_ref[base + j], 1)],
                                      rows_ref.at[slot_, pl.ds(row, 1)], sems.at[slot_]).start()
            return 0

        lax.fori_loop(0, n // COMBINE_UNROLL, body, 0)

    pl.when(t == 0)(functools.partial(issue, 0, 0))
    pl.when(t + 1 < n_steps)(functools.partial(issue, t + 1, 1 - slot))
    pltpu.make_async_copy(y_ref.at[pl.ds(0, n)], rows_ref.at[slot], sems.at[slot]).wait()

    gates = gates_ref[...]
    g0, g1 = gates[:, 0:1], gates[:, 1:2]
    lo0, hi0 = _unpack_halves(rows_ref[slot, 0:tm, :])
    lo1, hi1 = _unpack_halves(rows_ref[slot, tm:2 * tm, :])
    out_lo = x_ref[:, :half] + gate_f_ref[:, :half] * (g0 * lo0 + g1 * lo1)
    out_hi = x_ref[:, half:] + gate_f_ref[:, half:] * (g0 * hi0 + g1 * hi1)
    if final_norm:
        ssq = jnp.sum(out_lo * out_lo, axis=-1, keepdims=True) + jnp.sum(out_hi * out_hi, axis=-1, keepdims=True)
        inv = lax.rsqrt(ssq / (2 * half) + EPS)
        out_lo = (out_lo * inv) * gn_ref[:, :half]
        out_hi = (out_hi * inv) * gn_ref[:, half:]
    o_ref[:, :half] = out_lo
    o_ref[:, half:] = out_hi


def _combine(x, y_rows, dest, top_gate, gate_f, final_g, tm=256):
    bsz, seqlen, d = x.shape
    final_norm = final_g is not None
    gn = (final_g if final_norm else jnp.ones((d,), F32)).reshape(1, d)
    per_batch = seqlen // tm
    n_steps = bsz * per_batch
    row = lambda width: pl.BlockSpec((None, tm, width), lambda t, dst: (t // per_batch, t % per_batch, 0))
    return pl.pallas_call(
        functools.partial(_combine_kernel, tm=tm, n_steps=n_steps, final_norm=final_norm),
        grid_spec=pltpu.PrefetchScalarGridSpec(
            num_scalar_prefetch=1,
            grid=(n_steps,),
            in_specs=[pl.BlockSpec(memory_space=pl.ANY), row(d), row(LANES),
                      pl.BlockSpec((None, 1, d), lambda t, dst: (t // per_batch, 0, 0)),
                      pl.BlockSpec((1, d), lambda t, dst: (0, 0))],
            out_specs=row(d),
            scratch_shapes=[pltpu.VMEM((2, TOP_K * tm, d // 2), jnp.uint32), pltpu.SemaphoreType.DMA((2,))]),
        out_shape=jax.ShapeDtypeStruct(x.shape, F32),
        compiler_params=_params(("arbitrary",), 32),
        name="moe_combine",
    )(dest, y_rows, x, top_gate, gate_f[:, None, :], gn)


def _moe_ffn(x, g, shift, scale, gate_f, w_router, w_gate, w_up, w_down, final_g, tf=512):
    bsz, seqlen, d = x.shape
    n_tok = bsz * seqlen
    hp, top_idx, top_gate = _router(x, g, shift, scale, w_router)
    tables = _routing_tables(top_idx[..., :TOP_K], n_tok, w_gate.shape[2] // tf)
    y_rows = _moe_experts(hp.reshape(n_tok, d // 2), tables, w_gate, w_up, w_down, tf=tf)
    return _combine(x, y_rows, tables[0], top_gate, gate_f, final_g)


def _final_norm_kernel(x_ref, g_ref, o_ref):
    x = x_ref[...]
    ms = jnp.mean(x * x, axis=-1, keepdims=True)
    o_ref[...] = (x * lax.rsqrt(ms + EPS)) * g_ref[...]


def _final_norm(x, g, tm=512):
    bsz, seqlen, d = x.shape
    return pl.pallas_call(
        _final_norm_kernel,
        grid=(bsz, seqlen // tm),
        in_specs=[pl.BlockSpec((None, tm, d), lambda b, i: (b, i, 0)),
                  pl.BlockSpec((1, d), lambda b, i: (0, 0))],
        out_specs=pl.BlockSpec((None, tm, d), lambda b, i: (b, i, 0)),
        out_shape=jax.ShapeDtypeStruct(x.shape, F32),
        compiler_params=_params(("arbitrary", "arbitrary"), 32),
        name="final_norm",
    )(x, g.reshape(1, d))


def kernel(x, c, w_mod, b_mod, norm_mix_g, norm_ffn_g, w_in, ssm_a_re, ssm_a_im, ssm_log_dt, ssm_b_re, ssm_b_im, ssm_c_re, ssm_c_im, ssm_d, w_glu, b_glu, rel_bias, w_branch_ssm, w_branch_att, w_out, ffn_w_gate, ffn_w_up, ffn_w_down, moe_router, moe_w_gate, moe_w_up, moe_w_down, final_norm_g):
    depth = w_mod.shape[0]
    mod = _modulation(c, w_mod, b_mod)
    biases = [_bias_tile(rel_bias, g) for g in range(N_ATT_GROUPS)]
    for i in range(depth):
        shift_m, scale_m, gate_m, shift_f, scale_f, gate_f = jnp.split(mod[i], N_MOD, axis=-1)
        main, qkv = _in_projection(x, norm_mix_g[i], shift_m, scale_m, w_in, i)
        tables = _s5_tables(ssm_a_re[i], ssm_a_im[i], ssm_log_dt[i], ssm_b_re[i], ssm_b_im[i],
                            ssm_c_re[i], ssm_c_im[i])
        y_ssm = _s5_branch(main, tables, ssm_d[i], w_glu[i], b_glu[i])
        att = [_attention_group(qkv[g], biases[g], g) for g in range(N_ATT_GROUPS)]
        x = _merge(x, gate_m, y_ssm, att, main, w_branch_ssm[i], w_branch_att[i], w_out[i])
        j = i // 2
        last = i == depth - 1
        if i % 2 == 0:
            x = _dense_ffn(x, norm_ffn_g[i], shift_f, scale_f, gate_f,
                           ffn_w_gate[j], ffn_w_up[j], ffn_w_down[j])
            if last:
                x = _final_norm(x, final_norm_g)
        else:
            x = _moe_ffn(x, norm_ffn_g[i], shift_f, scale_f, gate_f, moe_router[j],
                         moe_w_gate[j], moe_w_up[j], moe_w_down[j], final_norm_g if last else None)
    return x
```

```python
import functools
import math

import jax
import jax.numpy as jnp
from jax import lax
from jax.experimental import pallas as pl
from jax.experimental.pallas import tpu as pltpu

F32 = jnp.float32
BF16 = jnp.bfloat16

LANES = 128
SUBLANES = 8
VMEM_BYTES = 64 * 1024 * 1024

SSM_GROUP = 16
SSM_STATE = 64
SSM_WIDTH = 1024
HEAD_DIM = 128
DILATION_PATTERN = ((128, 1), (512, 4), (2048, 16))
HEADS_PER_GROUP = 4
N_ATT_GROUPS = len(DILATION_PATTERN)
ATT_OUT_WIDTH = HEADS_PER_GROUP * HEAD_DIM
Q_BLOCK = 128
NEG_INF = -1e30
N_BUCKETS = 32
MAX_DISTANCE = 2048
N_EXPERTS = 8
TOP_K = 2
N_MOD = 6
EPS = 1e-6

PROJ_TILE = 512
U_TILE0 = 0
Q_TILE0 = SSM_WIDTH // PROJ_TILE
K_TILE0 = Q_TILE0 + N_ATT_GROUPS
V_TILE0 = K_TILE0 + N_ATT_GROUPS
GS_TILE0 = V_TILE0 + N_ATT_GROUPS
MAIN_GATE0 = Q_TILE0


def _params(dims, vmem_mb):
    return pltpu.CompilerParams(dimension_semantics=dims,
                                vmem_limit_bytes=vmem_mb * 1024 * 1024)


def _norm_modulate(x, g, shift, scale):
    ms = jnp.mean(x * x, axis=-1, keepdims=True)
    y = x * lax.rsqrt(ms + EPS)
    return (y * g) * (1.0 + scale) + shift


def _mod_kernel(c_ref, w_ref, b_ref, o_ref):
    c = c_ref[...]
    cond = (c * jax.nn.sigmoid(c)).astype(BF16)
    o_ref[...] = jnp.dot(cond, w_ref[...].astype(BF16),
                         preferred_element_type=F32) + b_ref[...]


def _modulation(c, w_mod, b_mod):
    depth, d, n = w_mod.shape
    bsz = c.shape[0]
    rows = SUBLANES
    c_pad = jnp.zeros((rows, d), F32).at[:bsz].set(c)
    tn = 1536
    out = pl.pallas_call(
        _mod_kernel,
        grid=(depth, n // tn),
        in_specs=[pl.BlockSpec((rows, d), lambda l, j: (0, 0)),
                  pl.BlockSpec((None, d, tn), lambda l, j: (l, 0, j)),
                  pl.BlockSpec((None, 1, tn), lambda l, j: (l, 0, j))],
        out_specs=pl.BlockSpec((None, rows, tn), lambda l, j: (l, 0, j)),
        out_shape=jax.ShapeDtypeStruct((depth, rows, n), F32),
        compiler_params=_params(("arbitrary", "arbitrary"), 40),
        name="modulation",
    )(c_pad, w_mod, b_mod.reshape(depth, 1, n))
    return out[:, :bsz]


def _inproj_kernel(x_ref, g_ref, shift_ref, scale_ref, w_ref, main_ref, *rest, tm):
    qkv_refs, (h_ref, res_ref) = rest[:N_ATT_GROUPS], rest[N_ATT_GROUPS:]
    j = pl.program_id(2)

    @pl.when(j == 0)
    def _():
        h = _norm_modulate(x_ref[...], g_ref[...], shift_ref[...], scale_ref[...])
        h_ref[...] = h.astype(BF16)

    res = jnp.dot(h_ref[...], w_ref[...].astype(BF16), preferred_element_type=F32)

    @pl.when(jnp.logical_or(j < Q_TILE0, j >= GS_TILE0))
    def _():
        main_ref[...] = res.astype(BF16)

    for g, (_, dil) in enumerate(DILATION_PATTERN):
        is_g = functools.reduce(jnp.logical_or, [j == t0 + g for t0 in (Q_TILE0, K_TILE0, V_TILE0)])

        @pl.when(is_g)
        def _(g=g, dil=dil):
            if dil == 1:
                qkv_refs[g][0] = res.astype(BF16)
            else:
                n_chunks = PROJ_TILE // LANES
                for ch in range(n_chunks):
                    res_ref[ch] = res[:, ch * LANES:(ch + 1) * LANES]
                for r in range(dil):
                    rows = [res_ref[ch, pl.ds(r, tm // dil, stride=dil), :] for ch in range(n_chunks)]
                    qkv_refs[g][r] = jnp.concatenate(rows, axis=-1).astype(BF16)


def _in_projection(x, g, shift, scale, w_in, layer, tm=1024):
    bsz, seqlen, d = x.shape
    n_tiles = w_in.shape[2] // PROJ_TILE
    n_main = n_tiles - 3 * N_ATT_GROUPS

    def main_map(b, i, j):
        return (jnp.where(j < Q_TILE0, j, jnp.where(j < GS_TILE0, Q_TILE0 - 1, j - 3 * N_ATT_GROUPS)), b, i, 0)

    def qkv_map(g):
        return lambda b, i, j: ((j > Q_TILE0 + g).astype(jnp.int32) + (j > K_TILE0 + g).astype(jnp.int32),
                                b, 0, i, 0)

    qkv_specs = [pl.BlockSpec((None, None, dil, tm // dil, PROJ_TILE), qkv_map(g))
                 for g, (_, dil) in enumerate(DILATION_PATTERN)]
    qkv_shapes = [jax.ShapeDtypeStruct((3, bsz, dil, seqlen // dil, PROJ_TILE), BF16)
                  for _, dil in DILATION_PATTERN]
    outs = pl.pallas_call(
        functools.partial(_inproj_kernel, tm=tm),
        grid=(bsz, seqlen // tm, n_tiles),
        in_specs=[pl.BlockSpec((None, tm, d), lambda b, i, j: (b, i, 0)),
                  pl.BlockSpec((1, d), lambda b, i, j: (0, 0)),
                  pl.BlockSpec((None, 1, d), lambda b, i, j: (b, 0, 0)),
                  pl.BlockSpec((None, 1, d), lambda b, i, j: (b, 0, 0)),
                  pl.BlockSpec((None, d, PROJ_TILE), lambda b, i, j: (layer, 0, j))],
        out_specs=[pl.BlockSpec((None, None, tm, PROJ_TILE), main_map)] + qkv_specs,
        out_shape=[jax.ShapeDtypeStruct((n_main, bsz, seqlen, PROJ_TILE), BF16)] + qkv_shapes,
        scratch_shapes=[pltpu.VMEM((tm, d), BF16), pltpu.VMEM((PROJ_TILE // LANES, tm, LANES), F32)],
        compiler_params=_params(("arbitrary", "arbitrary", "arbitrary"), 52),
        name="in_projection",
    )(x, g.reshape(1, d), shift[:, None, :], scale[:, None, :], w_in)
    return outs[0], outs[1:]


SSM_GB = 16
SSM_NGB = (SSM_WIDTH // SSM_GROUP) // SSM_GB
SSM_GB_IN = SSM_GB * SSM_GROUP
SSM_GB_RE = SSM_GB * SSM_STATE
SSM_CB = 2 * SSM_GB_RE // LANES
SSM_SLOTS = SSM_NGB * SSM_CB + SUBLANES


def _s5_kernel(u0_ref, u1_ref, bm_ref, cm_ref, ar_ref, ai_ref, d_ref, wglu_ref, bglu_ref,
               o_ref, bu_ref, xs_ref, st_ref, *, bsz, tt):
    stride = tt + SUBLANES
    rows = bsz * tt

    @pl.when(pl.program_id(0) == 0)
    def _():
        st_ref[...] = jnp.zeros_like(st_ref)

    u_halves = (u0_ref[...].reshape(rows, PROJ_TILE), u1_ref[...].reshape(rows, PROJ_TILE))

    for gb in range(SSM_NGB):
        half, off = divmod(gb * SSM_GB_IN, PROJ_TILE)
        bu = jnp.dot(u_halves[half][:, off:off + SSM_GB_IN], bm_ref[gb],
                     preferred_element_type=F32)
        for b in range(bsz):
            for cb in range(SSM_CB):
                blk = (b * SSM_NGB + gb) * SSM_CB + cb
                bu_ref[blk * stride: blk * stride + tt, :] = (
                    bu[b * tt:(b + 1) * tt, cb * LANES:(cb + 1) * LANES])

    a_re = [ar_ref[gb] for gb in range(SSM_NGB)]
    a_im = [ai_ref[gb] for gb in range(SSM_NGB)]
    half_cb = SSM_CB // 2

    def step(t, carry):
        new = []
        for b in range(bsz):
            for gb in range(SSM_NGB):
                k = (b * SSM_NGB + gb) * 2
                s_re, s_im = carry[k], carry[k + 1]
                base = ((b * SSM_NGB + gb) * SSM_CB) * stride
                bu_re = bu_ref[pl.ds(base + t, half_cb, stride=stride), :]
                bu_im = bu_ref[pl.ds(base + half_cb * stride + t, half_cb, stride=stride), :]
                n_re = a_re[gb] * s_re - a_im[gb] * s_im + bu_re
                n_im = a_re[gb] * s_im + a_im[gb] * s_re + bu_im
                row = pl.multiple_of((b * tt + t) * SSM_SLOTS + gb * SSM_CB, SUBLANES)
                xs_ref[pl.ds(row, half_cb), :] = n_re
                xs_ref[pl.ds(row + half_cb, half_cb), :] = n_im
                new += [n_re, n_im]
        return tuple(new)

    n_carry = bsz * SSM_NGB * 2
    carry = lax.fori_loop(0, tt, step, tuple(st_ref[k] for k in range(n_carry)), unroll=4)
    for k in range(n_carry):
        st_ref[k] = carry[k]

    ys = []
    for gb in range(SSM_NGB):
        cols = [xs_ref[pl.ds(gb * SSM_CB + cb, rows, stride=SSM_SLOTS), :] for cb in range(SSM_CB)]
        xg = jnp.concatenate(cols, axis=-1).astype(BF16)
        ys.append(jnp.dot(xg, cm_ref[gb], preferred_element_type=F32))
    y = jnp.concatenate(ys, axis=-1)
    u32 = jnp.concatenate(u_halves, axis=-1).astype(F32)
    y = jax.nn.gelu(y + d_ref[...] * u32)
    z = jnp.dot(y.astype(BF16), wglu_ref[...], preferred_element_type=F32) + bglu_ref[...]
    o_ref[...] = (y * jax.nn.sigmoid(z)).astype(o_ref.dtype).reshape(bsz, tt, SSM_WIDTH)


def _block_diag(m):
    ngb, gb, r, c = m.shape
    eye = jnp.eye(gb, dtype=m.dtype)
    return jnp.einsum('ngrc,gk->ngrkc', m, eye).reshape(ngb, gb * r, gb * c)


def _s5_tables(a_re, a_im, log_dt, b_re, b_im, c_re, c_im):
    lam = lax.complex(a_re.astype(F32), a_im.astype(F32))
    dt = jnp.exp(log_dt.astype(F32))[:, None]
    a_bar = jnp.exp(lam * dt)
    b_bar = ((a_bar - 1.0) / lam)[:, :, None] * lax.complex(b_re.astype(F32), b_im.astype(F32))
    g, p, h = b_bar.shape
    bt = jnp.transpose(b_bar, (0, 2, 1)).reshape(SSM_NGB, SSM_GB, h, p)
    bm = jnp.concatenate([_block_diag(bt.real), _block_diag(bt.imag)], axis=-1)
    ct_re = jnp.transpose(c_re.astype(F32), (0, 2, 1)).reshape(SSM_NGB, SSM_GB, p, h)
    ct_im = jnp.transpose(c_im.astype(F32), (0, 2, 1)).reshape(SSM_NGB, SSM_GB, p, h)
    cm = jnp.concatenate([_block_diag(ct_re), -_block_diag(ct_im)], axis=1)
    ar = a_bar.real.reshape(SSM_NGB, SSM_CB // 2, LANES)
    ai = a_bar.imag.reshape(SSM_NGB, SSM_CB // 2, LANES)
    return bm.astype(BF16), cm.astype(BF16), ar, ai


def _s5_branch(proj, tables, d_skip, w_glu, b_glu, tt=256):
    _, bsz, seqlen, _ = proj.shape
    bm, cm, ar, ai = tables
    stride = tt + SUBLANES
    n_blocks = bsz * SSM_NGB * SSM_CB
    d_skip = d_skip.reshape(1, SSM_WIDTH).astype(F32)
    w_glu = w_glu.astype(BF16)
    b_glu = b_glu.reshape(1, SSM_WIDTH).astype(F32)
    return pl.pallas_call(
        functools.partial(_s5_kernel, bsz=bsz, tt=tt),
        grid=(seqlen // tt,),
        in_specs=[pl.BlockSpec((None, bsz, tt, PROJ_TILE), lambda t: (U_TILE0, 0, t, 0)),
                  pl.BlockSpec((None, bsz, tt, PROJ_TILE), lambda t: (U_TILE0 + 1, 0, t, 0)),
                  _resident(bm), _resident(cm), _resident(ar), _resident(ai),
                  _resident(d_skip), _resident(w_glu), _resident(b_glu)],
        out_specs=pl.BlockSpec((bsz, tt, SSM_WIDTH), lambda t: (0, t, 0)),
        out_shape=jax.ShapeDtypeStruct((bsz, seqlen, SSM_WIDTH), BF16),
        scratch_shapes=[pltpu.VMEM((n_blocks * stride, LANES), F32),
                        pltpu.VMEM((bsz * tt * SSM_SLOTS, LANES), F32),
                        pltpu.VMEM((bsz * SSM_NGB * 2, SSM_CB // 2, LANES), F32)],
        compiler_params=_params(("arbitrary",), 60),
        name="s5_branch",
    )(proj, proj, bm, cm, ar, ai, d_skip, w_glu, b_glu)


def _t5_causal_bucket(dist):
    max_exact = N_BUCKETS // 2
    d32 = jnp.maximum(dist, 1).astype(F32)
    large = max_exact + (jnp.log(d32 / max_exact) / math.log(MAX_DISTANCE / max_exact)
                         * (N_BUCKETS - max_exact)).astype(jnp.int32)
    return jnp.where(dist < max_exact, dist, jnp.minimum(large, N_BUCKETS - 1))


def _bias_tile(rel_bias, group):
    window, dilation = DILATION_PATTERN[group]
    steps = window // dilation
    assert steps == Q_BLOCK
    heads = slice(group * HEADS_PER_GROUP, (group + 1) * HEADS_PER_GROUP)
    back = jnp.arange(steps, -1, -1, dtype=jnp.int32)
    vals = rel_bias[_t5_causal_bucket(back * dilation)][:, heads].astype(F32).T
    period = 3 * Q_BLOCK
    v = jnp.concatenate([vals, jnp.full((HEADS_PER_GROUP, period - steps - 1), NEG_INF, F32)], axis=1)
    flat = jnp.tile(v, (1, Q_BLOCK))[:, :Q_BLOCK * (period - 1)]
    tile = flat.reshape(HEADS_PER_GROUP, Q_BLOCK, period - 1)[:, :, :2 * Q_BLOCK]
    col = lax.broadcasted_iota(jnp.int32, tile.shape, 2)
    return jnp.concatenate([tile, jnp.where(col < Q_BLOCK, NEG_INF, tile)], axis=0)


LSE_LANES = LANES // HEADS_PER_GROUP


def _attn_kernel(q_ref, kc_ref, kp_ref, vc_ref, vp_ref, bias_ref, o_ref, lse_ref, kf_ref, vf_ref, *, tq):
    kf_ref[0:Q_BLOCK, :] = kp_ref[...]
    kf_ref[Q_BLOCK:, :] = kc_ref[...]
    vf_ref[0:Q_BLOCK, :] = vp_ref[...]
    vf_ref[Q_BLOCK:, :] = vc_ref[...]
    scale = HEAD_DIM ** -0.5
    first_tile = pl.program_id(2) == 0
    lane_head = lax.broadcasted_iota(jnp.int32, (Q_BLOCK, LANES), 1) // LSE_LANES

    def block(jb, _):
        r0 = pl.multiple_of(jb * Q_BLOCK, Q_BLOCK)
        bias_set = jnp.logical_and(first_tile, jb == 0).astype(jnp.int32) * HEADS_PER_GROUP
        lse = jnp.zeros((Q_BLOCK, LANES), F32)
        for h in range(HEADS_PER_GROUP):
            hs = slice(h * HEAD_DIM, (h + 1) * HEAD_DIM)
            q = q_ref[pl.ds(r0, Q_BLOCK), hs]
            k2 = kf_ref[pl.ds(r0, 2 * Q_BLOCK), hs]
            v2 = vf_ref[pl.ds(r0, 2 * Q_BLOCK), hs]
            s = lax.dot_general(q, k2, (((1,), (1,)), ((), ())), preferred_element_type=F32)
            s = s * scale + bias_ref[bias_set + h]
            m = jnp.max(s, axis=-1, keepdims=True)
            p = jnp.exp(s - m)
            l = jnp.sum(p, axis=-1, keepdims=True)
            o = jnp.dot(p.astype(BF16), v2, preferred_element_type=F32) / l
            o_ref[pl.ds(r0, Q_BLOCK), hs] = o.astype(o_ref.dtype)
            lse = jnp.where(lane_head == h, m + jnp.log(l), lse)
        lse_ref[pl.ds(r0, Q_BLOCK), :] = lse
        return 0

    lax.fori_loop(0, tq // Q_BLOCK, block, 0, unroll=2)


def _attention_group(qkv, bias, group, tile0=0):
    _, bsz, d, lc, _ = qkv.shape
    tq = min(lc, 1024)
    per_tq = tq // Q_BLOCK
    cur = lambda which: pl.BlockSpec((None, None, None, tq, PROJ_TILE),
                                     lambda b, r, i: (tile0 + which, b, r, i, 0))
    prev = lambda which: pl.BlockSpec((None, None, None, Q_BLOCK, PROJ_TILE),
                                      lambda b, r, i: (tile0 + which, b, r, jnp.maximum(i * per_tq - 1, 0), 0))
    out_spec = lambda width: pl.BlockSpec((None, None, tq, width), lambda b, r, i: (b, r, i, 0))
    return pl.pallas_call(
        functools.partial(_attn_kernel, tq=tq),
        grid=(bsz, d, lc // tq),
        in_specs=[cur(0), cur(1), prev(1), cur(2), prev(2),
                  pl.BlockSpec(bias.shape, lambda b, r, i: (0, 0, 0))],
        out_specs=[out_spec(ATT_OUT_WIDTH), out_spec(LANES)],
        out_shape=[jax.ShapeDtypeStruct((bsz, d, lc, ATT_OUT_WIDTH), BF16),
                   jax.ShapeDtypeStruct((bsz, d, lc, LANES), F32)],
        scratch_shapes=[pltpu.VMEM((Q_BLOCK + tq, PROJ_TILE), BF16),
                        pltpu.VMEM((Q_BLOCK + tq, PROJ_TILE), BF16)],
        compiler_params=_params(("arbitrary", "arbitrary", "arbitrary"), 40),
        name=f"attention_group{group}",
    )(qkv, qkv, qkv, qkv, qkv, bias)


def _merge_kernel(*refs, tm):
    ys_ref = refs[0]
    o_refs = refs[1:1 + N_ATT_GROUPS]
    l_refs = refs[1 + N_ATT_GROUPS:1 + 2 * N_ATT_GROUPS]
    k = 1 + 2 * N_ATT_GROUPS
    n_gate = (len(refs) - k - 7) // 2
    gs_refs = refs[k:k + n_gate]
    ga_refs = refs[k + n_gate:k + 2 * n_gate]
    x_ref, gate_ref, wbs_ref, wba_ref, wout_ref, out_ref, tok_ref = refs[k + 2 * n_gate:]

    def token_order(ref, g):
        dil = DILATION_PATTERN[g][1]
        if dil == 1:
            return ref[0].astype(F32)
        n_chunks = ref.shape[-1] // LANES
        for r in range(dil):
            for ch in range(n_chunks):
                tok_ref[ch, pl.ds(r, tm // dil, stride=dil), :] = (
                    ref[r, :, ch * LANES:(ch + 1) * LANES].astype(F32))
        return jnp.concatenate([tok_ref[ch] for ch in range(n_chunks)], axis=-1)

    lses = [token_order(r, g) for g, r in enumerate(l_refs)]
    m = functools.reduce(jnp.maximum, lses)
    es = [jnp.exp(l - m) for l in lses]
    den = functools.reduce(lambda a, b: a + b, es)

    def per_head(w):
        return jnp.concatenate([jnp.broadcast_to(w[:, h * LSE_LANES:h * LSE_LANES + 1], (tm, HEAD_DIM))
                                for h in range(HEADS_PER_GROUP)], axis=-1)

    y_att = functools.reduce(lambda a, b: a + b,
                             [per_head(e / den) * token_order(r, g) for g, (e, r) in enumerate(zip(es, o_refs))])

    m_ssm = jnp.dot(ys_ref[...], wbs_ref[...], preferred_element_type=F32)
    m_att = jnp.dot(y_att.astype(BF16), wba_ref[...], preferred_element_type=F32)
    g_ssm = jnp.concatenate([r[...] for r in gs_refs], axis=-1).astype(F32)
    g_att = jnp.concatenate([r[...] for r in ga_refs], axis=-1).astype(F32)
    merged = jax.nn.sigmoid(g_ssm) * m_ssm + jax.nn.sigmoid(g_att) * m_att
    mixed = jnp.dot(merged.astype(BF16), wout_ref[...], preferred_element_type=F32)
    out_ref[...] = x_ref[...] + gate_ref[...] * mixed


def _resident(a):
    nd = a.ndim
    return pl.BlockSpec(a.shape, lambda *_: (0,) * nd, pipeline_mode=pl.Buffered(1))


def _merge(x, gate, y_ssm, att, main, w_branch_ssm, w_branch_att, w_out, tm=512):
    bsz, seqlen, d = x.shape
    n_gate = d // PROJ_TILE
    row = lambda width: pl.BlockSpec((None, tm, width), lambda b, i: (b, i, 0))
    tile = lambda t: pl.BlockSpec((None, None, tm, PROJ_TILE), lambda b, i: (t, b, i, 0))
    res_major = lambda dil, width: pl.BlockSpec((None, dil, tm // dil, width), lambda b, i: (b, 0, i, 0))
    wbs, wba, wout = (w.astype(BF16) for w in (w_branch_ssm, w_branch_att, w_out))
    os_, ls_ = zip(*att)
    in_specs = ([row(SSM_WIDTH)]
                + [res_major(dil, ATT_OUT_WIDTH) for _, dil in DILATION_PATTERN]
                + [res_major(dil, LANES) for _, dil in DILATION_PATTERN]
                + [tile(MAIN_GATE0 + t) for t in range(2 * n_gate)]
                + [row(d), pl.BlockSpec((None, 1, d), lambda b, i: (b, 0, 0)),
                   _resident(wbs), _resident(wba), _resident(wout)])
    return pl.pallas_call(
        functools.partial(_merge_kernel, tm=tm),
        grid=(bsz, seqlen // tm),
        in_specs=in_specs,
        out_specs=row(d),
        out_shape=jax.ShapeDtypeStruct(x.shape, F32),
        scratch_shapes=[pltpu.VMEM((ATT_OUT_WIDTH // LANES, tm, LANES), F32)],
        compiler_params=_params(("arbitrary", "arbitrary"), 56),
        name="merge",
    )(y_ssm, *os_, *ls_, *([main] * (2 * n_gate)), x, gate[:, None, :], wbs, wba, wout)


def _ffn_kernel(x_ref, g_ref, shift_ref, scale_ref, gate_ref, wg_ref, wu_ref, wd_ref,
                o_ref, h_ref, acc_ref):
    f = pl.program_id(2)

    @pl.when(f == 0)
    def _():
        h = _norm_modulate(x_ref[...], g_ref[...], shift_ref[...], scale_ref[...])
        h_ref[...] = h.astype(BF16)
        acc_ref[...] = jnp.zeros_like(acc_ref)

    h = h_ref[...]
    a = jnp.dot(h, wg_ref[...].astype(BF16), preferred_element_type=F32)
    b = jnp.dot(h, wu_ref[...].astype(BF16), preferred_element_type=F32)
    act = (a * jax.nn.sigmoid(a)) * b
    acc_ref[...] += jnp.dot(act.astype(BF16), wd_ref[...].astype(BF16), preferred_element_type=F32)

    @pl.when(f == pl.num_programs(2) - 1)
    def _():
        o_ref[...] = x_ref[...] + gate_ref[...] * acc_ref[...]


def _dense_ffn(x, g, shift, scale, gate, w_gate, w_up, w_down, tm=1024, tf=512):
    bsz, seqlen, d = x.shape
    dff = w_gate.shape[1]
    vec = pl.BlockSpec((None, 1, d), lambda b, i, f: (b, 0, 0))
    row = pl.BlockSpec((None, tm, d), lambda b, i, f: (b, i, 0), pipeline_mode=pl.Buffered(1))
    return pl.pallas_call(
        _ffn_kernel,
        grid=(bsz, seqlen // tm, dff // tf),
        in_specs=[row,
                  pl.BlockSpec((1, d), lambda b, i, f: (0, 0)),
                  vec, vec, vec,
                  pl.BlockSpec((d, tf), lambda b, i, f: (0, f)),
                  pl.BlockSpec((d, tf), lambda b, i, f: (0, f)),
                  pl.BlockSpec((tf, d), lambda b, i, f: (f, 0))],
        out_specs=row,
        out_shape=jax.ShapeDtypeStruct(x.shape, F32),
        scratch_shapes=[pltpu.VMEM((tm, d), BF16), pltpu.VMEM((tm, d), F32)],
        compiler_params=_params(("arbitrary", "arbitrary", "arbitrary"), 56),
        name="dense_ffn",
    )(x, g.reshape(1, d), shift[:, None, :], scale[:, None, :], gate[:, None, :],
      w_gate.astype(BF16), w_up.astype(BF16), w_down.astype(BF16))


MOE_SUB = 256
SUBS_PER_TILE = 4
MOE_SUPER = SUBS_PER_TILE * MOE_SUB


def _pack_halves(x):
    half = x.shape[-1] // 2
    bits = lambda v: lax.bitcast_convert_type(v.astype(BF16).astype(F32), jnp.uint32)
    return bits(x[:, half:]) | (bits(x[:, :half]) >> 16)


def _unpack_halves(w):
    lo = lax.bitcast_convert_type(w << 16, F32)
    hi = lax.bitcast_convert_type(w & jnp.uint32(0xFFFF0000), F32)
    return lo, hi


def _router_kernel(x_ref, g_ref, shift_ref, scale_ref, wr_ref, h_ref, idx_ref, gate_ref):
    h = _norm_modulate(x_ref[...], g_ref[...], shift_ref[...], scale_ref[...])
    h_ref[...] = _pack_halves(h)
    logits = jnp.dot(h, wr_ref[...], preferred_element_type=F32, precision=lax.Precision.HIGHEST)
    lane_i = lax.broadcasted_iota(jnp.int32, logits.shape, 1)
    lane = lane_i.astype(F32)
    logits = jnp.where(lane_i < N_EXPERTS, logits, -jnp.inf)
    m1 = jnp.max(logits, axis=-1, keepdims=True)
    i1 = jnp.min(jnp.where(logits == m1, lane, float(LANES)), axis=-1, keepdims=True)
    rest = jnp.where(lane == i1, -jnp.inf, logits)
    m2 = jnp.max(rest, axis=-1, keepdims=True)
    i2 = jnp.min(jnp.where(rest == m2, lane, float(LANES)), axis=-1, keepdims=True)
    e2 = jnp.exp(m2 - m1)
    den = 1.0 + e2
    idx_ref[...] = jnp.where(lane_i == 0, i1, jnp.where(lane_i == 1, i2, 0.0)).astype(jnp.int32)
    gate_ref[...] = jnp.where(lane_i == 0, 1.0 / den, jnp.where(lane_i == 1, e2 / den, 0.0))


def _router(x, g, shift, scale, w_router, tm=512):
    bsz, seqlen, d = x.shape
    wr = jnp.zeros((d, LANES), F32).at[:, :N_EXPERTS].set(w_router)
    row = lambda width: pl.BlockSpec((None, tm, width), lambda b, i: (b, i, 0))
    vec = pl.BlockSpec((None, 1, d), lambda b, i: (b, 0, 0))
    return pl.pallas_call(
        _router_kernel,
        grid=(bsz, seqlen // tm),
        in_specs=[row(d), pl.BlockSpec((1, d), lambda b, i: (0, 0)), vec, vec,
                  pl.BlockSpec((d, LANES), lambda b, i: (0, 0))],
        out_specs=[row(d // 2), row(LANES), row(LANES)],
        out_shape=[jax.ShapeDtypeStruct((bsz, seqlen, d // 2), jnp.uint32),
                   jax.ShapeDtypeStruct((bsz, seqlen, LANES), jnp.int32),
                   jax.ShapeDtypeStruct((bsz, seqlen, LANES), F32)],
        compiler_params=_params(("arbitrary", "arbitrary"), 40),
        name="moe_router",
    )(x, g.reshape(1, d), shift[:, None, :], scale[:, None, :], wr)


def _moe_dims(n_tok, nf):
    n_sub = n_tok * TOP_K // MOE_SUB + N_EXPERTS
    n_super = (n_sub + (SUBS_PER_TILE - 1) * N_EXPERTS) // SUBS_PER_TILE + 1
    rows_per_step = -(-MOE_SUPER // nf)
    while (rows_per_step * nf) % SUBLANES:
        rows_per_step += 1
    n_fetch = rows_per_step * nf
    return n_sub, n_super, rows_per_step, n_fetch


def _routing_tables(top_expert, n_tok, nf):
    n_sub, n_super, _, n_fetch = _moe_dims(n_tok, nf)
    i32 = jnp.int32
    flat_e = top_expert.reshape(-1)
    onehot = (flat_e[:, None] == jnp.arange(N_EXPERTS, dtype=i32)[None, :]).astype(i32)
    csum = jnp.cumsum(onehot, axis=0)
    rank = jnp.sum(csum * onehot, axis=1) - 1
    counts = csum[-1]
    subs = (counts + MOE_SUB - 1) // MOE_SUB
    pend = jnp.cumsum(subs) * MOE_SUB
    pstart = pend - subs * MOE_SUB
    dest = (pstart[flat_e] + rank).astype(i32)
    flat_token = jnp.arange(n_tok * TOP_K, dtype=i32) // TOP_K
    row_token = jnp.zeros((n_sub * MOE_SUB + n_fetch,), i32).at[dest].set(flat_token)
    supers = (subs + SUBS_PER_TILE - 1) // SUBS_PER_TILE
    send = jnp.cumsum(supers)
    sstart = send - supers
    s = jnp.arange(n_super + 1, dtype=i32)
    e = jnp.minimum(jnp.searchsorted(send, s, side='right'), N_EXPERTS - 1).astype(i32)
    local = s - sstart[e]
    used = s < send[-1]
    tile_expert = jnp.where(used, e, e[jnp.maximum(send[-1] - 1, 0)]).astype(i32)
    tile_row0 = jnp.where(used, pstart[e] + local * MOE_SUPER, 0).astype(i32)
    tile_nsub = jnp.where(used, jnp.minimum(subs[e] - SUBS_PER_TILE * local, SUBS_PER_TILE), 0).astype(i32)
    n_used = send[-1].astype(i32).reshape(1)
    n_sub_used = (pend[-1] // MOE_SUB).astype(i32).reshape(1)
    return dest, row_token, tile_expert, tile_row0, tile_nsub, n_used, n_sub_used


def _moe_kernel(te_ref, row0_ref, nsub_ref, nu_ref, nsu_ref, tok_ref, hp_ref, wg_ref, wu_ref, wd_ref, y_ref,
                xg_ref, xb_ref, acc_ref, yb_ref, gsem, osem, fsem, *, nf, rows_per_step, n_sub_alloc, n_fill):
    s, f = pl.program_id(0), pl.program_id(1)
    n_used = nu_ref[0]
    used = s < n_used
    slot = s % 2
    n_fetch = rows_per_step * nf
    half = xg_ref.shape[-1]

    def row_copy(tile, slot_, r):
        tok = tok_ref[row0_ref[tile] + r]
        return pltpu.make_async_copy(hp_ref.at[pl.ds(tok, 1)], xg_ref.at[slot_, pl.ds(r, 1)], gsem.at[slot_])

    def wait_rows(slot_):
        pltpu.make_async_copy(hp_ref.at[pl.ds(0, n_fetch)], xg_ref.at[slot_], gsem.at[slot_]).wait()

    def out_copies(tile):
        r0 = pl.multiple_of(row0_ref[tile], MOE_SUB)
        return [pltpu.make_async_copy(yb_ref.at[pl.ds(k * MOE_SUB, MOE_SUB)],
                                      y_ref.at[pl.ds(r0 + k * MOE_SUB, MOE_SUB)], osem.at[k])
                for k in range(SUBS_PER_TILE)]

    def start_out(tile):
        for k, cp in enumerate(out_copies(tile)):
            pl.when(k < nsub_ref[tile])(cp.start)

    def wait_out(tile):
        for k, cp in enumerate(out_copies(tile)):
            pl.when(k < nsub_ref[tile])(cp.wait)

    @pl.when(jnp.logical_and(s == 0, f == 0))
    def _():
        def body(r, _):
            row_copy(0, 0, r).start()
            return 0
        lax.fori_loop(0, n_fetch, body, 0)

    @pl.when(jnp.logical_and(f == 0, s <= n_used))
    def _():
        wait_rows(slot)

    @pl.when(jnp.logical_and(used, f == 0))
    def _():
        lo, hi = _unpack_halves(xg_ref[slot, 0:MOE_SUPER, :])
        xb_ref[:, :half] = lo.astype(BF16)
        xb_ref[:, half:] = hi.astype(BF16)
        acc_ref[...] = jnp.zeros_like(acc_ref)

    def step(n_rows):
        for k in range(rows_per_step):
            row_copy(s + 1, 1 - slot, f * rows_per_step + k).start()
        h = xb_ref[0:n_rows, :]
        a = jnp.dot(h, wg_ref[...].astype(BF16), preferred_element_type=F32)
        b = jnp.dot(h, wu_ref[...].astype(BF16), preferred_element_type=F32)
        act = (a * jax.nn.sigmoid(a)) * b
        acc_ref[0:n_rows, :] += jnp.dot(act.astype(BF16), wd_ref[...].astype(BF16), preferred_element_type=F32)

    for n in range(1, SUBS_PER_TILE + 1):
        pl.when(jnp.logical_and(used, nsub_ref[s] == n))(functools.partial(step, n * MOE_SUB))

    @pl.when(jnp.logical_and(used, f == nf - 1))
    def _():
        pl.when(s > 0)(functools.partial(wait_out, s - 1))
        yb_ref[...] = _pack_halves(acc_ref[...])
        start_out(s)

    @pl.when(jnp.logical_and(s == n_used, f == 0))
    def _():
        wait_out(s - 1)
        yb_ref[0:MOE_SUB, :] = jnp.zeros((MOE_SUB, half), yb_ref.dtype)
        fills = []
        for k in range(n_fill):
            sub = nsu_ref[0] + k
            cp = pltpu.make_async_copy(yb_ref.at[pl.ds(0, MOE_SUB)],
                                       y_ref.at[pl.ds(pl.multiple_of(sub * MOE_SUB, MOE_SUB), MOE_SUB)], fsem.at[k])
            fills.append((sub < n_sub_alloc, cp))
        for cond, cp in fills:
            pl.when(cond)(cp.start)
        for cond, cp in fills:
            pl.when(cond)(cp.wait)


def _moe_experts(hp, tables, w_gate, w_up, w_down, tf=512):
    n_tok, half = hp.shape
    d = 2 * half
    dff = w_gate.shape[2]
    nf = dff // tf
    n_sub, n_super, rows_per_step, n_fetch = _moe_dims(n_tok, nf)
    _, row_token, tile_expert, tile_row0, tile_nsub, n_used, n_sub_used = tables
    n_unused_max = n_sub - n_tok * TOP_K // MOE_SUB
    tile_f = lambda s, f, nu: jnp.where(s < nu[0], f, nf - 1)
    w_in_spec = pl.BlockSpec((None, d, tf), lambda s, f, te, r0, ns, nu, nsu, tok: (te[s], 0, tile_f(s, f, nu)))
    w_out_spec = pl.BlockSpec((None, tf, d), lambda s, f, te, r0, ns, nu, nsu, tok: (te[s], tile_f(s, f, nu), 0))
    return pl.pallas_call(
        functools.partial(_moe_kernel, nf=nf, rows_per_step=rows_per_step, n_sub_alloc=n_sub,
                          n_fill=n_unused_max),
        grid_spec=pltpu.PrefetchScalarGridSpec(
            num_scalar_prefetch=6,
            grid=(n_super, nf),
            in_specs=[pl.BlockSpec(memory_space=pl.ANY), w_in_spec, w_in_spec, w_out_spec],
            out_specs=pl.BlockSpec(memory_space=pl.ANY),
            scratch_shapes=[pltpu.VMEM((2, n_fetch, half), jnp.uint32),
                            pltpu.VMEM((MOE_SUPER, d), BF16),
                            pltpu.VMEM((MOE_SUPER, d), F32),
                            pltpu.VMEM((MOE_SUPER, half), jnp.uint32),
                            pltpu.SemaphoreType.DMA((2,)), pltpu.SemaphoreType.DMA((SUBS_PER_TILE,)),
                            pltpu.SemaphoreType.DMA((n_unused_max,))]),
        out_shape=jax.ShapeDtypeStruct((n_sub * MOE_SUB, half), jnp.uint32),
        compiler_params=_params(("arbitrary", "arbitrary"), 58),
        name="moe_experts",
    )(tile_expert, tile_row0, tile_nsub, n_used, n_sub_used, row_token, hp, w_gate, w_up, w_down)


COMBINE_UNROLL = 8


def _combine_kernel(dest_ref, y_ref, x_ref, gates_ref, gate_f_ref, gn_ref, o_ref, rows_ref, sems,
                    *, tm, n_steps, final_norm):
    t = pl.program_id(0)
    slot = t % 2
    n = tm * TOP_K
    half = rows_ref.shape[-1]

    def issue(tile, slot_):
        base = tile * n

        def body(i, _):
            for u in range(COMBINE_UNROLL):
                j = i * COMBINE_UNROLL + u
                row = (u % TOP_K) * tm + i * (COMBINE_UNROLL // TOP_K) + u // TOP_K
                pltpu.make_async_copy(y_ref.at[pl.ds(dest_ref[base + j], 1)],
                                      rows_ref.at[slot_, pl.ds(row, 1)], sems.at[slot_]).start()
            return 0

        lax.fori_loop(0, n // COMBINE_UNROLL, body, 0)

    pl.when(t == 0)(functools.partial(issue, 0, 0))
    pl.when(t + 1 < n_steps)(functools.partial(issue, t + 1, 1 - slot))
    pltpu.make_async_copy(y_ref.at[pl.ds(0, n)], rows_ref.at[slot], sems.at[slot]).wait()

    gates = gates_ref[...]
    g0, g1 = gates[:, 0:1], gates[:, 1:2]
    lo0, hi0 = _unpack_halves(rows_ref[slot, 0:tm, :])
    lo1, hi1 = _unpack_halves(rows_ref[slot, tm:2 * tm, :])
    out_lo = x_ref[:, :half] + gate_f_ref[:, :half] * (g0 * lo0 + g1 * lo1)
    out_hi = x_ref[:, half:] + gate_f_ref[:, half:] * (g0 * hi0 + g1 * hi1)
    if final_norm:
        ssq = jnp.sum(out_lo * out_lo, axis=-1, keepdims=True) + jnp.sum(out_hi * out_hi, axis=-1, keepdims=True)
        inv = lax.rsqrt(ssq / (2 * half) + EPS)
        out_lo = (out_lo * inv) * gn_ref[:, :half]
        out_hi = (out_hi * inv) * gn_ref[:, half:]
    o_ref[:, :half] = out_lo
    o_ref[:, half:] = out_hi


def _combine(x, y_rows, dest, top_gate, gate_f, final_g, tm=256):
    bsz, seqlen, d = x.shape
    final_norm = final_g is not None
    gn = (final_g if final_norm else jnp.ones((d,), F32)).reshape(1, d)
    per_batch = seqlen // tm
    n_steps = bsz * per_batch
    row = lambda width: pl.BlockSpec((None, tm, width), lambda t, dst: (t // per_batch, t % per_batch, 0))
    return pl.pallas_call(
        functools.partial(_combine_kernel, tm=tm, n_steps=n_steps, final_norm=final_norm),
        grid_spec=pltpu.PrefetchScalarGridSpec(
            num_scalar_prefetch=1,
            grid=(n_steps,),
            in_specs=[pl.BlockSpec(memory_space=pl.ANY), row(d), row(LANES),
                      pl.BlockSpec((None, 1, d), lambda t, dst: (t // per_batch, 0, 0)),
                      pl.BlockSpec((1, d), lambda t, dst: (0, 0))],
            out_specs=row(d),
            scratch_shapes=[pltpu.VMEM((2, TOP_K * tm, d // 2), jnp.uint32), pltpu.SemaphoreType.DMA((2,))]),
        out_shape=jax.ShapeDtypeStruct(x.shape, F32),
        compiler_params=_params(("arbitrary",), 32),
        name="moe_combine",
    )(dest, y_rows, x, top_gate, gate_f[:, None, :], gn)


def _moe_ffn(x, g, shift, scale, gate_f, w_router, w_gate, w_up, w_down, final_g, tf=512):
    bsz, seqlen, d = x.shape
    n_tok = bsz * seqlen
    hp, top_idx, top_gate = _router(x, g, shift, scale, w_router)
    tables = _routing_tables(top_idx[..., :TOP_K], n_tok, w_gate.shape[2] // tf)
    y_rows = _moe_experts(hp.reshape(n_tok, d // 2), tables, w_gate, w_up, w_down, tf=tf)
    return _combine(x, y_rows, tables[0], top_gate, gate_f, final_g)


def _final_norm_kernel(x_ref, g_ref, o_ref):
    x = x_ref[...]
    ms = jnp.mean(x * x, axis=-1, keepdims=True)
    o_ref[...] = (x * lax.rsqrt(ms + EPS)) * g_ref[...]


def _final_norm(x, g, tm=512):
    bsz, seqlen, d = x.shape
    return pl.pallas_call(
        _final_norm_kernel,
        grid=(bsz, seqlen // tm),
        in_specs=[pl.BlockSpec((None, tm, d), lambda b, i: (b, i, 0)),
                  pl.BlockSpec((1, d), lambda b, i: (0, 0))],
        out_specs=pl.BlockSpec((None, tm, d), lambda b, i: (b, i, 0)),
        out_shape=jax.ShapeDtypeStruct(x.shape, F32),
        compiler_params=_params(("arbitrary", "arbitrary"), 32),
        name="final_norm",
    )(x, g.reshape(1, d))


def kernel(x, c, w_mod, b_mod, norm_mix_g, norm_ffn_g, w_in, ssm_a_re, ssm_a_im, ssm_log_dt, ssm_b_re, ssm_b_im, ssm_c_re, ssm_c_im, ssm_d, w_glu, b_glu, rel_bias, w_branch_ssm, w_branch_att, w_out, ffn_w_gate, ffn_w_up, ffn_w_down, moe_router, moe_w_gate, moe_w_up, moe_w_down, final_norm_g):
    depth = w_mod.shape[0]
    mod = _modulation(c, w_mod, b_mod)
    biases = [_bias_tile(rel_bias, g) for g in range(N_ATT_GROUPS)]
    for i in range(depth):
        shift_m, scale_m, gate_m, shift_f, scale_f, gate_f = jnp.split(mod[i], N_MOD, axis=-1)
        main, qkv = _in_projection(x, norm_mix_g[i], shift_m, scale_m, w_in, i)
        tables = _s5_tables(ssm_a_re[i], ssm_a_im[i], ssm_log_dt[i], ssm_b_re[i], ssm_b_im[i],
                            ssm_c_re[i], ssm_c_im[i])
        y_ssm = _s5_branch(main, tables, ssm_d[i], w_glu[i], b_glu[i])
        att = [_attention_group(qkv[g], biases[g], g) for g in range(N_ATT_GROUPS)]
        x = _merge(x, gate_m, y_ssm, att, main, w_branch_ssm[i], w_branch_att[i], w_out[i])
        j = i // 2
        last = i == depth - 1
        if i % 2 == 0:
            x = _dense_ffn(x, norm_ffn_g[i], shift_f, scale_f, gate_f,
                           ffn_w_gate[j], ffn_w_up[j], ffn_w_down[j])
            if last:
                x = _final_norm(x, final_norm_g)
        else:
            x = _moe_ffn(x, norm_ffn_g[i], shift_f, scale_f, gate_f, moe_router[j],
                         moe_w_gate[j], moe_w_up[j], moe_w_down[j], final_norm_g if last else None)
    return x
```

```python
import functools
import math

import jax
import jax.numpy as jnp
from jax import lax
from jax.experimental import pallas as pl
from jax.experimental.pallas import tpu as pltpu

F32 = jnp.float32
BF16 = jnp.bfloat16

LANES = 128
SUBLANES = 8
VMEM_BYTES = 64 * 1024 * 1024

SSM_GROUP = 16
SSM_STATE = 64
SSM_WIDTH = 1024
HEAD_DIM = 128
DILATION_PATTERN = ((128, 1), (512, 4), (2048, 16))
HEADS_PER_GROUP = 4
N_ATT_GROUPS = len(DILATION_PATTERN)
ATT_OUT_WIDTH = HEADS_PER_GROUP * HEAD_DIM
Q_BLOCK = 128
NEG_INF = -1e30
N_BUCKETS = 32
MAX_DISTANCE = 2048
N_EXPERTS = 8
TOP_K = 2
N_MOD = 6
EPS = 1e-6

PROJ_TILE = 512
U_TILE0 = 0
Q_TILE0 = SSM_WIDTH // PROJ_TILE
K_TILE0 = Q_TILE0 + N_ATT_GROUPS
V_TILE0 = K_TILE0 + N_ATT_GROUPS
GS_TILE0 = V_TILE0 + N_ATT_GROUPS
MAIN_GATE0 = Q_TILE0


def _params(dims, vmem_mb):
    return pltpu.CompilerParams(dimension_semantics=dims,
                                vmem_limit_bytes=vmem_mb * 1024 * 1024)


def _norm_modulate(x, g, shift, scale):
    ms = jnp.mean(x * x, axis=-1, keepdims=True)
    y = x * lax.rsqrt(ms + EPS)
    return (y * g) * (1.0 + scale) + shift


def _mod_kernel(c_ref, w_ref, b_ref, o_ref):
    c = c_ref[...]
    cond = (c * jax.nn.sigmoid(c)).astype(BF16)
    o_ref[...] = jnp.dot(cond, w_ref[...].astype(BF16),
                         preferred_element_type=F32) + b_ref[...]


def _modulation(c, w_mod, b_mod):
    depth, d, n = w_mod.shape
    bsz = c.shape[0]
    rows = SUBLANES
    c_pad = jnp.zeros((rows, d), F32).at[:bsz].set(c)
    tn = 1536
    out = pl.pallas_call(
        _mod_kernel,
        grid=(depth, n // tn),
        in_specs=[pl.BlockSpec((rows, d), lambda l, j: (0, 0)),
                  pl.BlockSpec((None, d, tn), lambda l, j: (l, 0, j)),
                  pl.BlockSpec((None, 1, tn), lambda l, j: (l, 0, j))],
        out_specs=pl.BlockSpec((None, rows, tn), lambda l, j: (l, 0, j)),
        out_shape=jax.ShapeDtypeStruct((depth, rows, n), F32),
        compiler_params=_params(("arbitrary", "arbitrary"), 40),
        name="modulation",
    )(c_pad, w_mod, b_mod.reshape(depth, 1, n))
    return out[:, :bsz]


def _inproj_kernel(x_ref, g_ref, shift_ref, scale_ref, w_ref, main_ref, *rest, tm):
    qkv_refs, (h_ref, res_ref) = rest[:N_ATT_GROUPS], rest[N_ATT_GROUPS:]
    j = pl.program_id(2)

    @pl.when(j == 0)
    def _():
        h = _norm_modulate(x_ref[...], g_ref[...], shift_ref[...], scale_ref[...])
        h_ref[...] = h.astype(BF16)

    def project():
        return jnp.dot(h_ref[...], w_ref[...].astype(BF16), preferred_element_type=F32)

    @pl.when(jnp.logical_or(j < Q_TILE0, j >= GS_TILE0))
    def _():
        main_ref[...] = project().astype(BF16)

    for g, (_, dil) in enumerate(DILATION_PATTERN):
        is_g = functools.reduce(jnp.logical_or, [j == t0 + g for t0 in (Q_TILE0, K_TILE0, V_TILE0)])

        @pl.when(is_g)
        def _(g=g, dil=dil):
            res = project()
            if dil == 1:
                qkv_refs[g][0] = res.astype(BF16)
            else:
                n_chunks = PROJ_TILE // LANES
                for ch in range(n_chunks):
                    res_ref[ch] = res[:, ch * LANES:(ch + 1) * LANES]
                for r in range(dil):
                    rows = [res_ref[ch, pl.ds(r, tm // dil, stride=dil), :] for ch in range(n_chunks)]
                    qkv_refs[g][r] = jnp.concatenate(rows, axis=-1).astype(BF16)


def _in_projection(x, g, shift, scale, w_in, layer, tm=1024):
    bsz, seqlen, d = x.shape
    n_tiles = w_in.shape[2] // PROJ_TILE
    n_main = n_tiles - 3 * N_ATT_GROUPS

    def main_map(b, i, j):
        return (jnp.where(j < Q_TILE0, j, jnp.where(j < GS_TILE0, Q_TILE0 - 1, j - 3 * N_ATT_GROUPS)), b, i, 0)

    def qkv_map(g):
        return lambda b, i, j: ((j > Q_TILE0 + g).astype(jnp.int32) + (j > K_TILE0 + g).astype(jnp.int32),
                                b, 0, i, 0)

    qkv_specs = [pl.BlockSpec((None, None, dil, tm // dil, PROJ_TILE), qkv_map(g))
                 for g, (_, dil) in enumerate(DILATION_PATTERN)]
    qkv_shapes = [jax.ShapeDtypeStruct((3, bsz, dil, seqlen // dil, PROJ_TILE), BF16)
                  for _, dil in DILATION_PATTERN]
    outs = pl.pallas_call(
        functools.partial(_inproj_kernel, tm=tm),
        grid=(bsz, seqlen // tm, n_tiles),
        in_specs=[pl.BlockSpec((None, tm, d), lambda b, i, j: (b, i, 0)),
                  pl.BlockSpec((1, d), lambda b, i, j: (0, 0)),
                  pl.BlockSpec((None, 1, d), lambda b, i, j: (b, 0, 0)),
                  pl.BlockSpec((None, 1, d), lambda b, i, j: (b, 0, 0)),
                  pl.BlockSpec((None, d, PROJ_TILE), lambda b, i, j: (layer, 0, j))],
        out_specs=[pl.BlockSpec((None, None, tm, PROJ_TILE), main_map)] + qkv_specs,
        out_shape=[jax.ShapeDtypeStruct((n_main, bsz, seqlen, PROJ_TILE), BF16)] + qkv_shapes,
        scratch_shapes=[pltpu.VMEM((tm, d), BF16), pltpu.VMEM((PROJ_TILE // LANES, tm, LANES), F32)],
        compiler_params=_params(("arbitrary", "arbitrary", "arbitrary"), 52),
        name="in_projection",
    )(x, g.reshape(1, d), shift[:, None, :], scale[:, None, :], w_in)
    return outs[0], outs[1:]


SSM_GB = 16
SSM_NGB = (SSM_WIDTH // SSM_GROUP) // SSM_GB
SSM_GB_IN = SSM_GB * SSM_GROUP
SSM_GB_RE = SSM_GB * SSM_STATE
SSM_CB = 2 * SSM_GB_RE // LANES
SSM_SLOTS = SSM_NGB * SSM_CB + SUBLANES


def _s5_kernel(u0_ref, u1_ref, bm_ref, cm_ref, ar_ref, ai_ref, d_ref, wglu_ref, bglu_ref,
               o_ref, bu_ref, xs_ref, st_ref, *, bsz, tt):
    stride = tt + SUBLANES
    rows = bsz * tt

    @pl.when(pl.program_id(0) == 0)
    def _():
        st_ref[...] = jnp.zeros_like(st_ref)

    u_halves = (u0_ref[...].reshape(rows, PROJ_TILE), u1_ref[...].reshape(rows, PROJ_TILE))

    for gb in range(SSM_NGB):
        half, off = divmod(gb * SSM_GB_IN, PROJ_TILE)
        bu = jnp.dot(u_halves[half][:, off:off + SSM_GB_IN], bm_ref[gb],
                     preferred_element_type=F32)
        for b in range(bsz):
            for cb in range(SSM_CB):
                blk = (b * SSM_NGB + gb) * SSM_CB + cb
                bu_ref[blk * stride: blk * stride + tt, :] = (
                    bu[b * tt:(b + 1) * tt, cb * LANES:(cb + 1) * LANES])

    a_re = [ar_ref[gb] for gb in range(SSM_NGB)]
    a_im = [ai_ref[gb] for gb in range(SSM_NGB)]
    half_cb = SSM_CB // 2

    def step(t, carry):
        new = []
        for b in range(bsz):
            for gb in range(SSM_NGB):
                k = (b * SSM_NGB + gb) * 2
                s_re, s_im = carry[k], carry[k + 1]
                base = ((b * SSM_NGB + gb) * SSM_CB) * stride
                bu_re = bu_ref[pl.ds(base + t, half_cb, stride=stride), :]
                bu_im = bu_ref[pl.ds(base + half_cb * stride + t, half_cb, stride=stride), :]
                n_re = a_re[gb] * s_re - a_im[gb] * s_im + bu_re
                n_im = a_re[gb] * s_im + a_im[gb] * s_re + bu_im
                row = pl.multiple_of((b * tt + t) * SSM_SLOTS + gb * SSM_CB, SUBLANES)
                xs_ref[pl.ds(row, half_cb), :] = n_re
                xs_ref[pl.ds(row + half_cb, half_cb), :] = n_im
                new += [n_re, n_im]
        return tuple(new)

    n_carry = bsz * SSM_NGB * 2
    carry = lax.fori_loop(0, tt, step, tuple(st_ref[k] for k in range(n_carry)), unroll=4)
    for k in range(n_carry):
        st_ref[k] = carry[k]

    ys = []
    for gb in range(SSM_NGB):
        cols = [xs_ref[pl.ds(gb * SSM_CB + cb, rows, stride=SSM_SLOTS), :] for cb in range(SSM_CB)]
        xg = jnp.concatenate(cols, axis=-1).astype(BF16)
        ys.append(jnp.dot(xg, cm_ref[gb], preferred_element_type=F32))
    y = jnp.concatenate(ys, axis=-1)
    u32 = jnp.concatenate(u_halves, axis=-1).astype(F32)
    y = jax.nn.gelu(y + d_ref[...] * u32)
    z = jnp.dot(y.astype(BF16), wglu_ref[...], preferred_element_type=F32) + bglu_ref[...]
    o_ref[...] = (y * jax.nn.sigmoid(z)).astype(o_ref.dtype).reshape(bsz, tt, SSM_WIDTH)


def _block_diag(m):
    ngb, gb, r, c = m.shape
    eye = jnp.eye(gb, dtype=m.dtype)
    return jnp.einsum('ngrc,gk->ngrkc', m, eye).reshape(ngb, gb * r, gb * c)


def _s5_tables(a_re, a_im, log_dt, b_re, b_im, c_re, c_im):
    lam = lax.complex(a_re.astype(F32), a_im.astype(F32))
    dt = jnp.exp(log_dt.astype(F32))[:, None]
    a_bar = jnp.exp(lam * dt)
    b_bar = ((a_bar - 1.0) / lam)[:, :, None] * lax.complex(b_re.astype(F32), b_im.astype(F32))
    g, p, h = b_bar.shape
    bt = jnp.transpose(b_bar, (0, 2, 1)).reshape(SSM_NGB, SSM_GB, h, p)
    bm = jnp.concatenate([_block_diag(bt.real), _block_diag(bt.imag)], axis=-1)
    ct_re = jnp.transpose(c_re.astype(F32), (0, 2, 1)).reshape(SSM_NGB, SSM_GB, p, h)
    ct_im = jnp.transpose(c_im.astype(F32), (0, 2, 1)).reshape(SSM_NGB, SSM_GB, p, h)
    cm = jnp.concatenate([_block_diag(ct_re), -_block_diag(ct_im)], axis=1)
    ar = a_bar.real.reshape(SSM_NGB, SSM_CB // 2, LANES)
    ai = a_bar.imag.reshape(SSM_NGB, SSM_CB // 2, LANES)
    return bm.astype(BF16), cm.astype(BF16), ar, ai


def _s5_branch(proj, tables, d_skip, w_glu, b_glu, tt=256):
    _, bsz, seqlen, _ = proj.shape
    bm, cm, ar, ai = tables
    stride = tt + SUBLANES
    n_blocks = bsz * SSM_NGB * SSM_CB
    d_skip = d_skip.reshape(1, SSM_WIDTH).astype(F32)
    w_glu = w_glu.astype(BF16)
    b_glu = b_glu.reshape(1, SSM_WIDTH).astype(F32)
    return pl.pallas_call(
        functools.partial(_s5_kernel, bsz=bsz, tt=tt),
        grid=(seqlen // tt,),
        in_specs=[pl.BlockSpec((None, bsz, tt, PROJ_TILE), lambda t: (U_TILE0, 0, t, 0)),
                  pl.BlockSpec((None, bsz, tt, PROJ_TILE), lambda t: (U_TILE0 + 1, 0, t, 0)),
                  _resident(bm), _resident(cm), _resident(ar), _resident(ai),
                  _resident(d_skip), _resident(w_glu), _resident(b_glu)],
        out_specs=pl.BlockSpec((bsz, tt, SSM_WIDTH), lambda t: (0, t, 0)),
        out_shape=jax.ShapeDtypeStruct((bsz, seqlen, SSM_WIDTH), BF16),
        scratch_shapes=[pltpu.VMEM((n_blocks * stride, LANES), F32),
                        pltpu.VMEM((bsz * tt * SSM_SLOTS, LANES), F32),
                        pltpu.VMEM((bsz * SSM_NGB * 2, SSM_CB // 2, LANES), F32)],
        compiler_params=_params(("arbitrary",), 60),
        name="s5_branch",
    )(proj, proj, bm, cm, ar, ai, d_skip, w_glu, b_glu)


def _t5_causal_bucket(dist):
    max_exact = N_BUCKETS // 2
    d32 = jnp.maximum(dist, 1).astype(F32)
    large = max_exact + (jnp.log(d32 / max_exact) / math.log(MAX_DISTANCE / max_exact)
                         * (N_BUCKETS - max_exact)).astype(jnp.int32)
    return jnp.where(dist < max_exact, dist, jnp.minimum(large, N_BUCKETS - 1))


def _bias_tile(rel_bias, group):
    window, dilation = DILATION_PATTERN[group]
    steps = window // dilation
    assert steps == Q_BLOCK
    heads = slice(group * HEADS_PER_GROUP, (group + 1) * HEADS_PER_GROUP)
    back = jnp.arange(steps, -1, -1, dtype=jnp.int32)
    vals = rel_bias[_t5_causal_bucket(back * dilation)][:, heads].astype(F32).T
    period = 3 * Q_BLOCK
    v = jnp.concatenate([vals, jnp.full((HEADS_PER_GROUP, period - steps - 1), NEG_INF, F32)], axis=1)
    flat = jnp.tile(v, (1, Q_BLOCK))[:, :Q_BLOCK * (period - 1)]
    tile = flat.reshape(HEADS_PER_GROUP, Q_BLOCK, period - 1)[:, :, :2 * Q_BLOCK]
    col = lax.broadcasted_iota(jnp.int32, tile.shape, 2)
    return jnp.concatenate([tile, jnp.where(col < Q_BLOCK, NEG_INF, tile)], axis=0)


LSE_LANES = LANES // HEADS_PER_GROUP


def _attn_kernel(q_ref, kc_ref, kp_ref, vc_ref, vp_ref, bias_ref, o_ref, lse_ref, kf_ref, vf_ref, *, tq):
    kf_ref[0:Q_BLOCK, :] = kp_ref[...]
    kf_ref[Q_BLOCK:, :] = kc_ref[...]
    vf_ref[0:Q_BLOCK, :] = vp_ref[...]
    vf_ref[Q_BLOCK:, :] = vc_ref[...]
    scale = HEAD_DIM ** -0.5
    first_tile = pl.program_id(2) == 0
    lane_head = lax.broadcasted_iota(jnp.int32, (Q_BLOCK, LANES), 1) // LSE_LANES

    def block(jb, _):
        r0 = pl.multiple_of(jb * Q_BLOCK, Q_BLOCK)
        bias_set = jnp.logical_and(first_tile, jb == 0).astype(jnp.int32) * HEADS_PER_GROUP
        lse = jnp.zeros((Q_BLOCK, LANES), F32)
        for h in range(HEADS_PER_GROUP):
            hs = slice(h * HEAD_DIM, (h + 1) * HEAD_DIM)
            q = q_ref[pl.ds(r0, Q_BLOCK), hs]
            k2 = kf_ref[pl.ds(r0, 2 * Q_BLOCK), hs]
            v2 = vf_ref[pl.ds(r0, 2 * Q_BLOCK), hs]
            s = lax.dot_general(q, k2, (((1,), (1,)), ((), ())), preferred_element_type=F32)
            s = s * scale + bias_ref[bias_set + h]
            m = jnp.max(s, axis=-1, keepdims=True)
            p = jnp.exp(s - m)
            l = jnp.sum(p, axis=-1, keepdims=True)
            o = jnp.dot(p.astype(BF16), v2, preferred_element_type=F32) / l
            o_ref[pl.ds(r0, Q_BLOCK), hs] = o.astype(o_ref.dtype)
            lse = jnp.where(lane_head == h, m + jnp.log(l), lse)
        lse_ref[pl.ds(r0, Q_BLOCK), :] = lse
        return 0

    lax.fori_loop(0, tq // Q_BLOCK, block, 0, unroll=min(4, tq // Q_BLOCK))


def _attention_group(qkv, bias, group, tile0=0):
    _, bsz, d, lc, _ = qkv.shape
    tq = min(lc, 1024)
    per_tq = tq // Q_BLOCK
    cur = lambda which: pl.BlockSpec((None, None, None, tq, PROJ_TILE),
                                     lambda b, r, i: (tile0 + which, b, r, i, 0))
    prev = lambda which: pl.BlockSpec((None, None, None, Q_BLOCK, PROJ_TILE),
                                      lambda b, r, i: (tile0 + which, b, r, jnp.maximum(i * per_tq - 1, 0), 0))
    out_spec = lambda width: pl.BlockSpec((None, None, tq, width), lambda b, r, i: (b, r, i, 0))
    return pl.pallas_call(
        functools.partial(_attn_kernel, tq=tq),
        grid=(bsz, d, lc // tq),
        in_specs=[cur(0), cur(1), prev(1), cur(2), prev(2),
                  pl.BlockSpec(bias.shape, lambda b, r, i: (0, 0, 0))],
        out_specs=[out_spec(ATT_OUT_WIDTH), out_spec(LANES)],
        out_shape=[jax.ShapeDtypeStruct((bsz, d, lc, ATT_OUT_WIDTH), BF16),
                   jax.ShapeDtypeStruct((bsz, d, lc, LANES), F32)],
        scratch_shapes=[pltpu.VMEM((Q_BLOCK + tq, PROJ_TILE), BF16),
                        pltpu.VMEM((Q_BLOCK + tq, PROJ_TILE), BF16)],
        compiler_params=_params(("arbitrary", "arbitrary", "arbitrary"), 40),
        name=f"attention_group{group}",
    )(qkv, qkv, qkv, qkv, qkv, bias)


def _merge_kernel(*refs, tm):
    ys_ref = refs[0]
    o_refs = refs[1:1 + N_ATT_GROUPS]
    l_refs = refs[1 + N_ATT_GROUPS:1 + 2 * N_ATT_GROUPS]
    k = 1 + 2 * N_ATT_GROUPS
    n_gate = (len(refs) - k - 7) // 2
    gs_refs = refs[k:k + n_gate]
    ga_refs = refs[k + n_gate:k + 2 * n_gate]
    x_ref, gate_ref, wbs_ref, wba_ref, wout_ref, out_ref, tok_ref = refs[k + 2 * n_gate:]

    def token_order(ref, g):
        dil = DILATION_PATTERN[g][1]
        if dil == 1:
            return ref[0].astype(F32)
        n_chunks = ref.shape[-1] // LANES
        for r in range(dil):
            for ch in range(n_chunks):
                tok_ref[ch, pl.ds(r, tm // dil, stride=dil), :] = (
                    ref[r, :, ch * LANES:(ch + 1) * LANES].astype(F32))
        return jnp.concatenate([tok_ref[ch] for ch in range(n_chunks)], axis=-1)

    lses = [token_order(r, g) for g, r in enumerate(l_refs)]
    m = functools.reduce(jnp.maximum, lses)
    es = [jnp.exp(l - m) for l in lses]
    den = functools.reduce(lambda a, b: a + b, es)

    def per_head(w):
        return jnp.concatenate([jnp.broadcast_to(w[:, h * LSE_LANES:h * LSE_LANES + 1], (tm, HEAD_DIM))
                                for h in range(HEADS_PER_GROUP)], axis=-1)

    y_att = functools.reduce(lambda a, b: a + b,
                             [per_head(e / den) * token_order(r, g) for g, (e, r) in enumerate(zip(es, o_refs))])

    m_ssm = jnp.dot(ys_ref[...], wbs_ref[...], preferred_element_type=F32)
    m_att = jnp.dot(y_att.astype(BF16), wba_ref[...], preferred_element_type=F32)
    g_ssm = jnp.concatenate([r[...] for r in gs_refs], axis=-1).astype(F32)
    g_att = jnp.concatenate([r[...] for r in ga_refs], axis=-1).astype(F32)
    merged = jax.nn.sigmoid(g_ssm) * m_ssm + jax.nn.sigmoid(g_att) * m_att
    mixed = jnp.dot(merged.astype(BF16), wout_ref[...], preferred_element_type=F32)
    out_ref[...] = x_ref[...] + gate_ref[...] * mixed


def _resident(a):
    nd = a.ndim
    return pl.BlockSpec(a.shape, lambda *_: (0,) * nd, pipeline_mode=pl.Buffered(1))


def _merge(x, gate, y_ssm, att, main, w_branch_ssm, w_branch_att, w_out, tm=512):
    bsz, seqlen, d = x.shape
    n_gate = d // PROJ_TILE
    row = lambda width: pl.BlockSpec((None, tm, width), lambda b, i: (b, i, 0))
    tile = lambda t: pl.BlockSpec((None, None, tm, PROJ_TILE), lambda b, i: (t, b, i, 0))
    res_major = lambda dil, width: pl.BlockSpec((None, dil, tm // dil, width), lambda b, i: (b, 0, i, 0))
    wbs, wba, wout = (w.astype(BF16) for w in (w_branch_ssm, w_branch_att, w_out))
    os_, ls_ = zip(*att)
    in_specs = ([row(SSM_WIDTH)]
                + [res_major(dil, ATT_OUT_WIDTH) for _, dil in DILATION_PATTERN]
                + [res_major(dil, LANES) for _, dil in DILATION_PATTERN]
                + [tile(MAIN_GATE0 + t) for t in range(2 * n_gate)]
                + [row(d), pl.BlockSpec((None, 1, d), lambda b, i: (b, 0, 0)),
                   _resident(wbs), _resident(wba), _resident(wout)])
    return pl.pallas_call(
        functools.partial(_merge_kernel, tm=tm),
        grid=(bsz, seqlen // tm),
        in_specs=in_specs,
        out_specs=row(d),
        out_shape=jax.ShapeDtypeStruct(x.shape, F32),
        scratch_shapes=[pltpu.VMEM((ATT_OUT_WIDTH // LANES, tm, LANES), F32)],
        compiler_params=_params(("arbitrary", "arbitrary"), 56),
        name="merge",
    )(y_ssm, *os_, *ls_, *([main] * (2 * n_gate)), x, gate[:, None, :], wbs, wba, wout)


def _ffn_kernel(x_ref, g_ref, shift_ref, scale_ref, gate_ref, wg_ref, wu_ref, wd_ref,
                o_ref, h_ref, acc_ref):
    f = pl.program_id(2)

    @pl.when(f == 0)
    def _():
        h = _norm_modulate(x_ref[...], g_ref[...], shift_ref[...], scale_ref[...])
        h_ref[...] = h.astype(BF16)
        acc_ref[...] = jnp.zeros_like(acc_ref)

    h = h_ref[...]
    a = jnp.dot(h, wg_ref[...].astype(BF16), preferred_element_type=F32)
    b = jnp.dot(h, wu_ref[...].astype(BF16), preferred_element_type=F32)
    act = (a * jax.nn.sigmoid(a)) * b
    acc_ref[...] += jnp.dot(act.astype(BF16), wd_ref[...].astype(BF16), preferred_element_type=F32)

    @pl.when(f == pl.num_programs(2) - 1)
    def _():
        o_ref[...] = x_ref[...] + gate_ref[...] * acc_ref[...]


def _dense_ffn(x, g, shift, scale, gate, w_gate, w_up, w_down, tm=1024, tf=512):
    bsz, seqlen, d = x.shape
    dff = w_gate.shape[1]
    vec = pl.BlockSpec((None, 1, d), lambda b, i, f: (b, 0, 0))
    row = pl.BlockSpec((None, tm, d), lambda b, i, f: (b, i, 0), pipeline_mode=pl.Buffered(1))
    return pl.pallas_call(
        _ffn_kernel,
        grid=(bsz, seqlen // tm, dff // tf),
        in_specs=[row,
                  pl.BlockSpec((1, d), lambda b, i, f: (0, 0)),
                  vec, vec, vec,
                  pl.BlockSpec((d, tf), lambda b, i, f: (0, f)),
                  pl.BlockSpec((d, tf), lambda b, i, f: (0, f)),
                  pl.BlockSpec((tf, d), lambda b, i, f: (f, 0))],
        out_specs=row,
        out_shape=jax.ShapeDtypeStruct(x.shape, F32),
        scratch_shapes=[pltpu.VMEM((tm, d), BF16), pltpu.VMEM((tm, d), F32)],
        compiler_params=_params(("arbitrary", "arbitrary", "arbitrary"), 56),
        name="dense_ffn",
    )(x, g.reshape(1, d), shift[:, None, :], scale[:, None, :], gate[:, None, :],
      w_gate.astype(BF16), w_up.astype(BF16), w_down.astype(BF16))


MOE_SUB = 256
SUBS_PER_TILE = 4
MOE_SUPER = SUBS_PER_TILE * MOE_SUB


def _pack_halves(x):
    half = x.shape[-1] // 2
    bits = lambda v: lax.bitcast_convert_type(v.astype(BF16).astype(F32), jnp.uint32)
    return bits(x[:, half:]) | (bits(x[:, :half]) >> 16)


def _unpack_halves(w):
    lo = lax.bitcast_convert_type(w << 16, F32)
    hi = lax.bitcast_convert_type(w & jnp.uint32(0xFFFF0000), F32)
    return lo, hi


def _router_kernel(x_ref, g_ref, shift_ref, scale_ref, wr_ref, h_ref, idx_ref, gate_ref):
    h = _norm_modulate(x_ref[...], g_ref[...], shift_ref[...], scale_ref[...])
    h_ref[...] = _pack_halves(h)
    logits = jnp.dot(h, wr_ref[...], preferred_element_type=F32, precision=lax.Precision.HIGHEST)
    lane_i = lax.broadcasted_iota(jnp.int32, logits.shape, 1)
    lane = lane_i.astype(F32)
    logits = jnp.where(lane_i < N_EXPERTS, logits, -jnp.inf)
    m1 = jnp.max(logits, axis=-1, keepdims=True)
    i1 = jnp.min(jnp.where(logits == m1, lane, float(LANES)), axis=-1, keepdims=True)
    rest = jnp.where(lane == i1, -jnp.inf, logits)
    m2 = jnp.max(rest, axis=-1, keepdims=True)
    i2 = jnp.min(jnp.where(rest == m2, lane, float(LANES)), axis=-1, keepdims=True)
    e2 = jnp.exp(m2 - m1)
    den = 1.0 + e2
    idx_ref[...] = jnp.where(lane_i == 0, i1, jnp.where(lane_i == 1, i2, 0.0)).astype(jnp.int32)
    gate_ref[...] = jnp.where(lane_i == 0, 1.0 / den, jnp.where(lane_i == 1, e2 / den, 0.0))


def _router(x, g, shift, scale, w_router, tm=512):
    bsz, seqlen, d = x.shape
    wr = jnp.zeros((d, LANES), F32).at[:, :N_EXPERTS].set(w_router)
    row = lambda width: pl.BlockSpec((None, tm, width), lambda b, i: (b, i, 0))
    vec = pl.BlockSpec((None, 1, d), lambda b, i: (b, 0, 0))
    return pl.pallas_call(
        _router_kernel,
        grid=(bsz, seqlen // tm),
        in_specs=[row(d), pl.BlockSpec((1, d), lambda b, i: (0, 0)), vec, vec,
                  pl.BlockSpec((d, LANES), lambda b, i: (0, 0))],
        out_specs=[row(d // 2), row(LANES), row(LANES)],
        out_shape=[jax.ShapeDtypeStruct((bsz, seqlen, d // 2), jnp.uint32),
                   jax.ShapeDtypeStruct((bsz, seqlen, LANES), jnp.int32),
                   jax.ShapeDtypeStruct((bsz, seqlen, LANES), F32)],
        compiler_params=_params(("arbitrary", "arbitrary"), 40),
        name="moe_router",
    )(x, g.reshape(1, d), shift[:, None, :], scale[:, None, :], wr)


def _moe_dims(n_tok, nf):
    n_sub = n_tok * TOP_K // MOE_SUB + N_EXPERTS
    n_super = (n_sub + (SUBS_PER_TILE - 1) * N_EXPERTS) // SUBS_PER_TILE + 1
    rows_per_step = -(-MOE_SUPER // nf)
    while (rows_per_step * nf) % SUBLANES:
        rows_per_step += 1
    n_fetch = rows_per_step * nf
    return n_sub, n_super, rows_per_step, n_fetch


def _routing_tables(top_expert, n_tok, nf):
    n_sub, n_super, _, n_fetch = _moe_dims(n_tok, nf)
    i32 = jnp.int32
    flat_e = top_expert.reshape(-1)
    onehot = (flat_e[:, None] == jnp.arange(N_EXPERTS, dtype=i32)[None, :]).astype(i32)
    csum = jnp.cumsum(onehot, axis=0)
    rank = jnp.sum(csum * onehot, axis=1) - 1
    counts = csum[-1]
    subs = (counts + MOE_SUB - 1) // MOE_SUB
    pend = jnp.cumsum(subs) * MOE_SUB
    pstart = pend - subs * MOE_SUB
    dest = (pstart[flat_e] + rank).astype(i32)
    flat_token = jnp.arange(n_tok * TOP_K, dtype=i32) // TOP_K
    row_token = jnp.zeros((n_sub * MOE_SUB + n_fetch,), i32).at[dest].set(flat_token)
    supers = (subs + SUBS_PER_TILE - 1) // SUBS_PER_TILE
    send = jnp.cumsum(supers)
    sstart = send - supers
    s = jnp.arange(n_super + 1, dtype=i32)
    e = jnp.minimum(jnp.searchsorted(send, s, side='right'), N_EXPERTS - 1).astype(i32)
    local = s - sstart[e]
    used = s < send[-1]
    tile_expert = jnp.where(used, e, e[jnp.maximum(send[-1] - 1, 0)]).astype(i32)
    tile_row0 = jnp.where(used, pstart[e] + local * MOE_SUPER, 0).astype(i32)
    tile_nsub = jnp.where(used, jnp.minimum(subs[e] - SUBS_PER_TILE * local, SUBS_PER_TILE), 0).astype(i32)
    n_used = send[-1].astype(i32).reshape(1)
    n_sub_used = (pend[-1] // MOE_SUB).astype(i32).reshape(1)
    return dest, row_token, tile_expert, tile_row0, tile_nsub, n_used, n_sub_used


def _moe_kernel(te_ref, row0_ref, nsub_ref, nu_ref, nsu_ref, tok_ref, hp_ref, wg_ref, wu_ref, wd_ref, y_ref,
                xg_ref, xb_ref, acc_ref, yb_ref, gsem, osem, fsem, *, nf, rows_per_step, n_sub_alloc, n_fill):
    s, f = pl.program_id(0), pl.program_id(1)
    n_used = nu_ref[0]
    used = s < n_used
    slot = s % 2
    n_fetch = rows_per_step * nf
    half = xg_ref.shape[-1]

    def row_copy(tile, slot_, r):
        tok = tok_ref[row0_ref[tile] + r]
        return pltpu.make_async_copy(hp_ref.at[pl.ds(tok, 1)], xg_ref.at[slot_, pl.ds(r, 1)], gsem.at[slot_])

    def wait_rows(slot_):
        pltpu.make_async_copy(hp_ref.at[pl.ds(0, n_fetch)], xg_ref.at[slot_], gsem.at[slot_]).wait()

    def out_copies(tile):
        r0 = pl.multiple_of(row0_ref[tile], MOE_SUB)
        return [pltpu.make_async_copy(yb_ref.at[pl.ds(k * MOE_SUB, MOE_SUB)],
                                      y_ref.at[pl.ds(r0 + k * MOE_SUB, MOE_SUB)], osem.at[k])
                for k in range(SUBS_PER_TILE)]

    def start_out(tile):
        for k, cp in enumerate(out_copies(tile)):
            pl.when(k < nsub_ref[tile])(cp.start)

    def wait_out(tile):
        for k, cp in enumerate(out_copies(tile)):
            pl.when(k < nsub_ref[tile])(cp.wait)

    @pl.when(jnp.logical_and(s == 0, f == 0))
    def _():
        def body(r, _):
            row_copy(0, 0, r).start()
            return 0
        lax.fori_loop(0, n_fetch, body, 0)

    @pl.when(jnp.logical_and(f == 0, s <= n_used))
    def _():
        wait_rows(slot)

    @pl.when(jnp.logical_and(used, f == 0))
    def _():
        lo, hi = _unpack_halves(xg_ref[slot, 0:MOE_SUPER, :])
        xb_ref[:, :half] = lo.astype(BF16)
        xb_ref[:, half:] = hi.astype(BF16)
        acc_ref[...] = jnp.zeros_like(acc_ref)

    def step(n_rows):
        for k in range(rows_per_step):
            row_copy(s + 1, 1 - slot, f * rows_per_step + k).start()
        h = xb_ref[0:n_rows, :]
        a = jnp.dot(h, wg_ref[...].astype(BF16), preferred_element_type=F32)
        b = jnp.dot(h, wu_ref[...].astype(BF16), preferred_element_type=F32)
        act = (a * jax.nn.sigmoid(a)) * b
        acc_ref[0:n_rows, :] += jnp.dot(act.astype(BF16), wd_ref[...].astype(BF16), preferred_element_type=F32)

    for n in range(1, SUBS_PER_TILE + 1):
        pl.when(jnp.logical_and(used, nsub_ref[s] == n))(functools.partial(step, n * MOE_SUB))

    @pl.when(jnp.logical_and(used, f == nf - 1))
    def _():
        pl.when(s > 0)(functools.partial(wait_out, s - 1))
        yb_ref[...] = _pack_halves(acc_ref[...])
        start_out(s)

    @pl.when(jnp.logical_and(s == n_used, f == 0))
    def _():
        wait_out(s - 1)
        yb_ref[0:MOE_SUB, :] = jnp.zeros((MOE_SUB, half), yb_ref.dtype)
        fills = []
        for k in range(n_fill):
            sub = nsu_ref[0] + k
            cp = pltpu.make_async_copy(yb_ref.at[pl.ds(0, MOE_SUB)],
                                       y_ref.at[pl.ds(pl.multiple_of(sub * MOE_SUB, MOE_SUB), MOE_SUB)], fsem.at[k])
            fills.append((sub < n_sub_alloc, cp))
        for cond, cp in fills:
            pl.when(cond)(cp.start)
        for cond, cp in fills:
            pl.when(cond)(cp.wait)


def _moe_experts(hp, tables, w_gate, w_up, w_down, tf=512):
    n_tok, half = hp.shape
    d = 2 * half
    dff = w_gate.shape[2]
    nf = dff // tf
    n_sub, n_super, rows_per_step, n_fetch = _moe_dims(n_tok, nf)
    _, row_token, tile_expert, tile_row0, tile_nsub, n_used, n_sub_used = tables
    n_unused_max = n_sub - n_tok * TOP_K // MOE_SUB
    tile_f = lambda s, f, nu: jnp.where(s < nu[0], f, nf - 1)
    w_in_spec = pl.BlockSpec((None, d, tf), lambda s, f, te, r0, ns, nu, nsu, tok: (te[s], 0, tile_f(s, f, nu)))
    w_out_spec = pl.BlockSpec((None, tf, d), lambda s, f, te, r0, ns, nu, nsu, tok: (te[s], tile_f(s, f, nu), 0))
    return pl.pallas_call(
        functools.partial(_moe_kernel, nf=nf, rows_per_step=rows_per_step, n_sub_alloc=n_sub,
                          n_fill=n_unused_max),
        grid_spec=pltpu.PrefetchScalarGridSpec(
            num_scalar_prefetch=6,
            grid=(n_super, nf),
            in_specs=[pl.BlockSpec(memory_space=pl.ANY), w_in_spec, w_in_spec, w_out_spec],
            out_specs=pl.BlockSpec(memory_space=pl.ANY),
            scratch_shapes=[pltpu.VMEM((2, n_fetch, half), jnp.uint32),
                            pltpu.VMEM((MOE_SUPER, d), BF16),
                            pltpu.VMEM((MOE_SUPER, d), F32),
                            pltpu.VMEM((MOE_SUPER, half), jnp.uint32),
                            pltpu.SemaphoreType.DMA((2,)), pltpu.SemaphoreType.DMA((SUBS_PER_TILE,)),
                            pltpu.SemaphoreType.DMA((n_unused_max,))]),
        out_shape=jax.ShapeDtypeStruct((n_sub * MOE_SUB, half), jnp.uint32),
        compiler_params=_params(("arbitrary", "arbitrary"), 58),
        name="moe_experts",
    )(tile_expert, tile_row0, tile_nsub, n_used, n_sub_used, row_token, hp, w_gate, w_up, w_down)


COMBINE_UNROLL = 8


def _combine_kernel(dest_ref, y_ref, x_ref, gates_ref, gate_f_ref, gn_ref, o_ref, rows_ref, sems,
                    *, tm, n_steps, final_norm):
    t = pl.program_id(0)
    slot = t % 2
    n = tm * TOP_K
    half = rows_ref.shape[-1]

    def row_copy(base, j, row, slot_):
        return pltpu.make_async_copy(y_ref.at[pl.ds(dest_ref[base + j], 1)],
                                     rows_ref.at[slot_, pl.ds(row, 1)], sems.at[slot_])

    def wait_rows(slot_):
        pltpu.make_async_copy(y_ref.at[pl.ds(0, n)], rows_ref.at[slot_], sems.at[slot_]).wait()

    @pl.when(t == 0)
    def _():
        def body(i, _):
            for u in range(COMBINE_UNROLL):
                row = (u % TOP_K) * tm + i * (COMBINE_UNROLL // TOP_K) + u // TOP_K
                row_copy(0, i * COMBINE_UNROLL + u, row, 0).start()
            return 0
        lax.fori_loop(0, n // COMBINE_UNROLL, body, 0)

    wait_rows(slot)
    next_base = jnp.minimum(t + 1, n_steps - 1) * n
    for j in range(n):
        row_copy(next_base, j, (j % TOP_K) * tm + j // TOP_K, 1 - slot).start()

    gates = gates_ref[...]
    g0, g1 = gates[:, 0:1], gates[:, 1:2]
    lo0, hi0 = _unpack_halves(rows_ref[slot, 0:tm, :])
    lo1, hi1 = _unpack_halves(rows_ref[slot, tm:2 * tm, :])
    out_lo = x_ref[:, :half] + gate_f_ref[:, :half] * (g0 * lo0 + g1 * lo1)
    out_hi = x_ref[:, half:] + gate_f_ref[:, half:] * (g0 * hi0 + g1 * hi1)
    if final_norm:
        ssq = jnp.sum(out_lo * out_lo, axis=-1, keepdims=True) + jnp.sum(out_hi * out_hi, axis=-1, keepdims=True)
        inv = lax.rsqrt(ssq / (2 * half) + EPS)
        out_lo = (out_lo * inv) * gn_ref[:, :half]
        out_hi = (out_hi * inv) * gn_ref[:, half:]
    o_ref[:, :half] = out_lo
    o_ref[:, half:] = out_hi
    pl.when(t == n_steps - 1)(functools.partial(wait_rows, 1 - slot))


def _combine(x, y_rows, dest, top_gate, gate_f, final_g, tm=256):
    bsz, seqlen, d = x.shape
    final_norm = final_g is not None
    gn = (final_g if final_norm else jnp.ones((d,), F32)).reshape(1, d)
    per_batch = seqlen // tm
    n_steps = bsz * per_batch
    row = lambda width: pl.BlockSpec((None, tm, width), lambda t, dst: (t // per_batch, t % per_batch, 0))
    return pl.pallas_call(
        functools.partial(_combine_kernel, tm=tm, n_steps=n_steps, final_norm=final_norm),
        grid_spec=pltpu.PrefetchScalarGridSpec(
            num_scalar_prefetch=1,
            grid=(n_steps,),
            in_specs=[pl.BlockSpec(memory_space=pl.ANY), row(d), row(LANES),
                      pl.BlockSpec((None, 1, d), lambda t, dst: (t // per_batch, 0, 0)),
                      pl.BlockSpec((1, d), lambda t, dst: (0, 0))],
            out_specs=row(d),
            scratch_shapes=[pltpu.VMEM((2, TOP_K * tm, d // 2), jnp.uint32), pltpu.SemaphoreType.DMA((2,))]),
        out_shape=jax.ShapeDtypeStruct(x.shape, F32),
        compiler_params=_params(("arbitrary",), 32),
        name="moe_combine",
    )(dest, y_rows, x, top_gate, gate_f[:, None, :], gn)


def _moe_ffn(x, g, shift, scale, gate_f, w_router, w_gate, w_up, w_down, final_g, tf=512):
    bsz, seqlen, d = x.shape
    n_tok = bsz * seqlen
    hp, top_idx, top_gate = _router(x, g, shift, scale, w_router)
    tables = _routing_tables(top_idx[..., :TOP_K], n_tok, w_gate.shape[2] // tf)
    y_rows = _moe_experts(hp.reshape(n_tok, d // 2), tables, w_gate, w_up, w_down, tf=tf)
    return _combine(x, y_rows, tables[0], top_gate, gate_f, final_g)


def _final_norm_kernel(x_ref, g_ref, o_ref):
    x = x_ref[...]
    ms = jnp.mean(x * x, axis=-1, keepdims=True)
    o_ref[...] = (x * lax.rsqrt(ms + EPS)) * g_ref[...]


def _final_norm(x, g, tm=512):
    bsz, seqlen, d = x.shape
    return pl.pallas_call(
        _final_norm_kernel,
        grid=(bsz, seqlen // tm),
        in_specs=[pl.BlockSpec((None, tm, d), lambda b, i: (b, i, 0)),
                  pl.BlockSpec((1, d), lambda b, i: (0, 0))],
        out_specs=pl.BlockSpec((None, tm, d), lambda b, i: (b, i, 0)),
        out_shape=jax.ShapeDtypeStruct(x.shape, F32),
        compiler_params=_params(("arbitrary", "arbitrary"), 32),
        name="final_norm",
    )(x, g.reshape(1, d))


def kernel(x, c, w_mod, b_mod, norm_mix_g, norm_ffn_g, w_in, ssm_a_re, ssm_a_im, ssm_log_dt, ssm_b_re, ssm_b_im, ssm_c_re, ssm_c_im, ssm_d, w_glu, b_glu, rel_bias, w_branch_ssm, w_branch_att, w_out, ffn_w_gate, ffn_w_up, ffn_w_down, moe_router, moe_w_gate, moe_w_up, moe_w_down, final_norm_g):
    depth = w_mod.shape[0]
    mod = _modulation(c, w_mod, b_mod)
    biases = [_bias_tile(rel_bias, g) for g in range(N_ATT_GROUPS)]
    for i in range(depth):
        shift_m, scale_m, gate_m, shift_f, scale_f, gate_f = jnp.split(mod[i], N_MOD, axis=-1)
        main, qkv = _in_projection(x, norm_mix_g[i], shift_m, scale_m, w_in, i)
        tables = _s5_tables(ssm_a_re[i], ssm_a_im[i], ssm_log_dt[i], ssm_b_re[i], ssm_b_im[i],
                            ssm_c_re[i], ssm_c_im[i])
        y_ssm = _s5_branch(main, tables, ssm_d[i], w_glu[i], b_glu[i])
        att = [_attention_group(qkv[g], biases[g], g) for g in range(N_ATT_GROUPS)]
        x = _merge(x, gate_m, y_ssm, att, main, w_branch_ssm[i], w_branch_att[i], w_out[i])
        j = i // 2
        last = i == depth - 1
        if i % 2 == 0:
            x = _dense_ffn(x, norm_ffn_g[i], shift_f, scale_f, gate_f,
                           ffn_w_gate[j], ffn_w_up[j], ffn_w_down[j])
            if last:
                x = _final_norm(x, final_norm_g)
        else:
            x = _moe_ffn(x, norm_ffn_g[i], shift_f, scale_f, gate_f, moe_router[j],
                         moe_w_gate[j], moe_w_up[j], moe_w_down[j], final_norm_g if last else None)
    return x
```

```python
import functools
import math

import jax
import jax.numpy as jnp
from jax import lax
from jax.experimental import pallas as pl
from jax.experimental.pallas import tpu as pltpu

F32 = jnp.float32
BF16 = jnp.bfloat16

LANES = 128
SUBLANES = 8
VMEM_BYTES = 64 * 1024 * 1024

SSM_GROUP = 16
SSM_STATE = 64
SSM_WIDTH = 1024
HEAD_DIM = 128
DILATION_PATTERN = ((128, 1), (512, 4), (2048, 16))
HEADS_PER_GROUP = 4
N_ATT_GROUPS = len(DILATION_PATTERN)
ATT_OUT_WIDTH = HEADS_PER_GROUP * HEAD_DIM
Q_BLOCK = 128
NEG_INF = -1e30
N_BUCKETS = 32
MAX_DISTANCE = 2048
N_EXPERTS = 8
TOP_K = 2
N_MOD = 6
EPS = 1e-6

PROJ_TILE = 512
U_TILE0 = 0
Q_TILE0 = SSM_WIDTH // PROJ_TILE
K_TILE0 = Q_TILE0 + N_ATT_GROUPS
V_TILE0 = K_TILE0 + N_ATT_GROUPS
GS_TILE0 = V_TILE0 + N_ATT_GROUPS
MAIN_GATE0 = Q_TILE0


def _params(dims, vmem_mb):
    return pltpu.CompilerParams(dimension_semantics=dims,
                                vmem_limit_bytes=vmem_mb * 1024 * 1024)


def _norm_modulate(x, g, shift, scale):
    ms = jnp.mean(x * x, axis=-1, keepdims=True)
    y = x * lax.rsqrt(ms + EPS)
    return (y * g) * (1.0 + scale) + shift


def _mod_kernel(c_ref, w_ref, b_ref, o_ref):
    c = c_ref[...]
    cond = (c * jax.nn.sigmoid(c)).astype(BF16)
    o_ref[...] = jnp.dot(cond, w_ref[...].astype(BF16),
                         preferred_element_type=F32) + b_ref[...]


def _modulation(c, w_mod, b_mod):
    depth, d, n = w_mod.shape
    bsz = c.shape[0]
    rows = SUBLANES
    c_pad = jnp.zeros((rows, d), F32).at[:bsz].set(c)
    tn = 1536
    out = pl.pallas_call(
        _mod_kernel,
        grid=(depth, n // tn),
        in_specs=[pl.BlockSpec((rows, d), lambda l, j: (0, 0)),
                  pl.BlockSpec((None, d, tn), lambda l, j: (l, 0, j)),
                  pl.BlockSpec((None, 1, tn), lambda l, j: (l, 0, j))],
        out_specs=pl.BlockSpec((None, rows, tn), lambda l, j: (l, 0, j)),
        out_shape=jax.ShapeDtypeStruct((depth, rows, n), F32),
        compiler_params=_params(("arbitrary", "arbitrary"), 40),
        name="modulation",
    )(c_pad, w_mod, b_mod.reshape(depth, 1, n))
    return out[:, :bsz]


def _inproj_kernel(x_ref, g_ref, shift_ref, scale_ref, w_ref, main_ref, *rest, tm):
    qkv_refs, (h_ref, res_ref) = rest[:N_ATT_GROUPS], rest[N_ATT_GROUPS:]
    j = pl.program_id(2)

    @pl.when(j == 0)
    def _():
        h = _norm_modulate(x_ref[...], g_ref[...], shift_ref[...], scale_ref[...])
        h_ref[...] = h.astype(BF16)

    def project():
        return jnp.dot(h_ref[...], w_ref[...].astype(BF16), preferred_element_type=F32)

    @pl.when(jnp.logical_or(j < Q_TILE0, j >= GS_TILE0))
    def _():
        main_ref[...] = project().astype(BF16)

    for g, (_, dil) in enumerate(DILATION_PATTERN):
        is_g = functools.reduce(jnp.logical_or, [j == t0 + g for t0 in (Q_TILE0, K_TILE0, V_TILE0)])

        @pl.when(is_g)
        def _(g=g, dil=dil):
            res = project()
            if dil == 1:
                qkv_refs[g][0] = res.astype(BF16)
            else:
                n_chunks = PROJ_TILE // LANES
                for ch in range(n_chunks):
                    res_ref[ch] = res[:, ch * LANES:(ch + 1) * LANES]
                for r in range(dil):
                    rows = [res_ref[ch, pl.ds(r, tm // dil, stride=dil), :] for ch in range(n_chunks)]
                    qkv_refs[g][r] = jnp.concatenate(rows, axis=-1).astype(BF16)


def _in_projection(x, g, shift, scale, w_in, layer, tm=1024):
    bsz, seqlen, d = x.shape
    n_tiles = w_in.shape[2] // PROJ_TILE
    n_main = n_tiles - 3 * N_ATT_GROUPS

    def main_map(b, i, j):
        return (jnp.where(j < Q_TILE0, j, jnp.where(j < GS_TILE0, Q_TILE0 - 1, j - 3 * N_ATT_GROUPS)), b, i, 0)

    def qkv_map(g):
        return lambda b, i, j: ((j > Q_TILE0 + g).astype(jnp.int32) + (j > K_TILE0 + g).astype(jnp.int32),
                                b, 0, i, 0)

    qkv_specs = [pl.BlockSpec((None, None, dil, tm // dil, PROJ_TILE), qkv_map(g))
                 for g, (_, dil) in enumerate(DILATION_PATTERN)]
    qkv_shapes = [jax.ShapeDtypeStruct((3, bsz, dil, seqlen // dil, PROJ_TILE), BF16)
                  for _, dil in DILATION_PATTERN]
    outs = pl.pallas_call(
        functools.partial(_inproj_kernel, tm=tm),
        grid=(bsz, seqlen // tm, n_tiles),
        in_specs=[pl.BlockSpec((None, tm, d), lambda b, i, j: (b, i, 0)),
                  pl.BlockSpec((1, d), lambda b, i, j: (0, 0)),
                  pl.BlockSpec((None, 1, d), lambda b, i, j: (b, 0, 0)),
                  pl.BlockSpec((None, 1, d), lambda b, i, j: (b, 0, 0)),
                  pl.BlockSpec((None, d, PROJ_TILE), lambda b, i, j: (layer, 0, j))],
        out_specs=[pl.BlockSpec((None, None, tm, PROJ_TILE), main_map)] + qkv_specs,
        out_shape=[jax.ShapeDtypeStruct((n_main, bsz, seqlen, PROJ_TILE), BF16)] + qkv_shapes,
        scratch_shapes=[pltpu.VMEM((tm, d), BF16), pltpu.VMEM((PROJ_TILE // LANES, tm, LANES), F32)],
        compiler_params=_params(("arbitrary", "arbitrary", "arbitrary"), 52),
        name="in_projection",
    )(x, g.reshape(1, d), shift[:, None, :], scale[:, None, :], w_in)
    return outs[0], outs[1:]


SSM_GB = 16
SSM_NGB = (SSM_WIDTH // SSM_GROUP) // SSM_GB
SSM_GB_IN = SSM_GB * SSM_GROUP
SSM_GB_RE = SSM_GB * SSM_STATE
SSM_CB = 2 * SSM_GB_RE // LANES
SSM_SLOTS = SSM_NGB * SSM_CB + SUBLANES


def _s5_kernel(u0_ref, u1_ref, bm_ref, cm_ref, ar_ref, ai_ref, d_ref, wglu_ref, bglu_ref,
               o_ref, bu_ref, xs_ref, st_ref, *, bsz, tt):
    stride = tt + SUBLANES
    rows = bsz * tt

    @pl.when(pl.program_id(0) == 0)
    def _():
        st_ref[...] = jnp.zeros_like(st_ref)

    u_halves = (u0_ref[...].reshape(rows, PROJ_TILE), u1_ref[...].reshape(rows, PROJ_TILE))

    for gb in range(SSM_NGB):
        half, off = divmod(gb * SSM_GB_IN, PROJ_TILE)
        bu = jnp.dot(u_halves[half][:, off:off + SSM_GB_IN], bm_ref[gb],
                     preferred_element_type=F32)
        for b in range(bsz):
            for cb in range(SSM_CB):
                blk = (b * SSM_NGB + gb) * SSM_CB + cb
                bu_ref[blk * stride: blk * stride + tt, :] = (
                    bu[b * tt:(b + 1) * tt, cb * LANES:(cb + 1) * LANES])

    a_re = [ar_ref[gb] for gb in range(SSM_NGB)]
    a_im = [ai_ref[gb] for gb in range(SSM_NGB)]
    half_cb = SSM_CB // 2

    def step(t, carry):
        new = []
        for b in range(bsz):
            for gb in range(SSM_NGB):
                k = (b * SSM_NGB + gb) * 2
                s_re, s_im = carry[k], carry[k + 1]
                base = ((b * SSM_NGB + gb) * SSM_CB) * stride
                bu_re = bu_ref[pl.ds(base + t, half_cb, stride=stride), :]
                bu_im = bu_ref[pl.ds(base + half_cb * stride + t, half_cb, stride=stride), :]
                n_re = a_re[gb] * s_re - a_im[gb] * s_im + bu_re
                n_im = a_re[gb] * s_im + a_im[gb] * s_re + bu_im
                row = pl.multiple_of((b * tt + t) * SSM_SLOTS + gb * SSM_CB, SUBLANES)
                xs_ref[pl.ds(row, half_cb), :] = n_re
                xs_ref[pl.ds(row + half_cb, half_cb), :] = n_im
                new += [n_re, n_im]
        return tuple(new)

    n_carry = bsz * SSM_NGB * 2
    carry = lax.fori_loop(0, tt, step, tuple(st_ref[k] for k in range(n_carry)), unroll=4)
    for k in range(n_carry):
        st_ref[k] = carry[k]

    ys = []
    for gb in range(SSM_NGB):
        cols = [xs_ref[pl.ds(gb * SSM_CB + cb, rows, stride=SSM_SLOTS), :] for cb in range(SSM_CB)]
        xg = jnp.concatenate(cols, axis=-1).astype(BF16)
        ys.append(jnp.dot(xg, cm_ref[gb], preferred_element_type=F32))
    y = jnp.concatenate(ys, axis=-1)
    u32 = jnp.concatenate(u_halves, axis=-1).astype(F32)
    y = jax.nn.gelu(y + d_ref[...] * u32)
    z = jnp.dot(y.astype(BF16), wglu_ref[...], preferred_element_type=F32) + bglu_ref[...]
    o_ref[...] = (y * jax.nn.sigmoid(z)).astype(o_ref.dtype).reshape(bsz, tt, SSM_WIDTH)


def _block_diag(m):
    ngb, rows, c = m.shape
    r = rows // SSM_GB
    row_group = lax.broadcasted_iota(jnp.int32, (rows, SSM_GB * c), 0) // r
    col_group = lax.broadcasted_iota(jnp.int32, (rows, SSM_GB * c), 1) // c
    return jnp.where(row_group == col_group, jnp.tile(m, (1, 1, SSM_GB)), 0.0)


def _s5_tables(a_re, a_im, log_dt, b_re, b_im, c_re, c_im):
    depth = a_re.shape[0]
    lam = lax.complex(a_re.astype(F32), a_im.astype(F32))
    dt = jnp.exp(log_dt.astype(F32))[..., None]
    a_bar = jnp.exp(lam * dt)
    b_bar = ((a_bar - 1.0) / lam)[..., None] * lax.complex(b_re.astype(F32), b_im.astype(F32))
    _, g, p, h = b_bar.shape
    n = depth * SSM_NGB
    bt = jnp.transpose(b_bar, (0, 1, 3, 2)).reshape(n, SSM_GB * h, p)
    bm = jnp.concatenate([_block_diag(bt.real), _block_diag(bt.imag)], axis=-1)
    ct_re = jnp.transpose(c_re.astype(F32), (0, 1, 3, 2)).reshape(n, SSM_GB * p, h)
    ct_im = jnp.transpose(c_im.astype(F32), (0, 1, 3, 2)).reshape(n, SSM_GB * p, h)
    cm = jnp.concatenate([_block_diag(ct_re), -_block_diag(ct_im)], axis=1)
    lead = (depth, SSM_NGB)
    return (bm.astype(BF16).reshape(lead + bm.shape[1:]), cm.astype(BF16).reshape(lead + cm.shape[1:]),
            a_bar.real.reshape(lead + (SSM_CB // 2, LANES)), a_bar.imag.reshape(lead + (SSM_CB // 2, LANES)))


def _s5_branch(proj, tables, layer, d_skip, w_glu, b_glu, tt=256):
    _, bsz, seqlen, _ = proj.shape
    bm, cm, ar, ai = tables
    stride = tt + SUBLANES
    n_blocks = bsz * SSM_NGB * SSM_CB
    d_skip = d_skip.reshape(1, SSM_WIDTH).astype(F32)
    w_glu = w_glu.astype(BF16)
    b_glu = b_glu.reshape(1, SSM_WIDTH).astype(F32)
    return pl.pallas_call(
        functools.partial(_s5_kernel, bsz=bsz, tt=tt),
        grid=(seqlen // tt,),
        in_specs=[pl.BlockSpec((None, bsz, tt, PROJ_TILE), lambda t: (U_TILE0, 0, t, 0)),
                  pl.BlockSpec((None, bsz, tt, PROJ_TILE), lambda t: (U_TILE0 + 1, 0, t, 0)),
                  _resident(bm, layer), _resident(cm, layer), _resident(ar, layer), _resident(ai, layer),
                  _resident(d_skip), _resident(w_glu), _resident(b_glu)],
        out_specs=pl.BlockSpec((bsz, tt, SSM_WIDTH), lambda t: (0, t, 0)),
        out_shape=jax.ShapeDtypeStruct((bsz, seqlen, SSM_WIDTH), BF16),
        scratch_shapes=[pltpu.VMEM((n_blocks * stride, LANES), F32),
                        pltpu.VMEM((bsz * tt * SSM_SLOTS, LANES), F32),
                        pltpu.VMEM((bsz * SSM_NGB * 2, SSM_CB // 2, LANES), F32)],
        compiler_params=_params(("arbitrary",), 60),
        name="s5_branch",
    )(proj, proj, bm, cm, ar, ai, d_skip, w_glu, b_glu)


def _t5_causal_bucket(dist):
    max_exact = N_BUCKETS // 2
    d32 = jnp.maximum(dist, 1).astype(F32)
    large = max_exact + (jnp.log(d32 / max_exact) / math.log(MAX_DISTANCE / max_exact)
                         * (N_BUCKETS - max_exact)).astype(jnp.int32)
    return jnp.where(dist < max_exact, dist, jnp.minimum(large, N_BUCKETS - 1))


def _bias_tile(rel_bias, group):
    window, dilation = DILATION_PATTERN[group]
    steps = window // dilation
    assert steps == Q_BLOCK
    heads = slice(group * HEADS_PER_GROUP, (group + 1) * HEADS_PER_GROUP)
    back = jnp.arange(steps, -1, -1, dtype=jnp.int32)
    vals = rel_bias[_t5_causal_bucket(back * dilation)][:, heads].astype(F32).T
    period = 3 * Q_BLOCK
    v = jnp.concatenate([vals, jnp.full((HEADS_PER_GROUP, period - steps - 1), NEG_INF, F32)], axis=1)
    flat = jnp.tile(v, (1, Q_BLOCK))[:, :Q_BLOCK * (period - 1)]
    tile = flat.reshape(HEADS_PER_GROUP, Q_BLOCK, period - 1)[:, :, :2 * Q_BLOCK]
    col = lax.broadcasted_iota(jnp.int32, tile.shape, 2)
    return jnp.concatenate([tile, jnp.where(col < Q_BLOCK, NEG_INF, tile)], axis=0)


LSE_LANES = LANES // HEADS_PER_GROUP


def _attn_kernel(q_ref, kc_ref, kp_ref, vc_ref, vp_ref, bias_ref, o_ref, lse_ref, kf_ref, vf_ref, *, tq):
    kf_ref[0:Q_BLOCK, :] = kp_ref[...]
    kf_ref[Q_BLOCK:, :] = kc_ref[...]
    vf_ref[0:Q_BLOCK, :] = vp_ref[...]
    vf_ref[Q_BLOCK:, :] = vc_ref[...]
    scale = HEAD_DIM ** -0.5
    first_tile = pl.program_id(2) == 0
    lane_head = lax.broadcasted_iota(jnp.int32, (Q_BLOCK, LANES), 1) // LSE_LANES

    def block(jb, _):
        r0 = pl.multiple_of(jb * Q_BLOCK, Q_BLOCK)
        bias_set = jnp.logical_and(first_tile, jb == 0).astype(jnp.int32) * HEADS_PER_GROUP
        lse = jnp.zeros((Q_BLOCK, LANES), F32)
        for h in range(HEADS_PER_GROUP):
            hs = slice(h * HEAD_DIM, (h + 1) * HEAD_DIM)
            q = q_ref[pl.ds(r0, Q_BLOCK), hs]
            k2 = kf_ref[pl.ds(r0, 2 * Q_BLOCK), hs]
            v2 = vf_ref[pl.ds(r0, 2 * Q_BLOCK), hs]
            s = lax.dot_general(q, k2, (((1,), (1,)), ((), ())), preferred_element_type=F32)
            s = s * scale + bias_ref[bias_set + h]
            m = jnp.max(s, axis=-1, keepdims=True)
            p = jnp.exp(s - m)
            l = jnp.sum(p, axis=-1, keepdims=True)
            o = jnp.dot(p.astype(BF16), v2, preferred_element_type=F32) / l
            o_ref[pl.ds(r0, Q_BLOCK), hs] = o.astype(o_ref.dtype)
            lse = jnp.where(lane_head == h, m + jnp.log(l), lse)
        lse_ref[pl.ds(r0, Q_BLOCK), :] = lse
        return 0

    lax.fori_loop(0, tq // Q_BLOCK, block, 0, unroll=min(4, tq // Q_BLOCK))


def _attention_group(qkv, bias, group, tile0=0):
    _, bsz, d, lc, _ = qkv.shape
    tq = min(lc, 1024)
    per_tq = tq // Q_BLOCK
    cur = lambda which: pl.BlockSpec((None, None, None, tq, PROJ_TILE),
                                     lambda b, r, i: (tile0 + which, b, r, i, 0))
    prev = lambda which: pl.BlockSpec((None, None, None, Q_BLOCK, PROJ_TILE),
                                      lambda b, r, i: (tile0 + which, b, r, jnp.maximum(i * per_tq - 1, 0), 0))
    out_spec = lambda width: pl.BlockSpec((None, None, tq, width), lambda b, r, i: (b, r, i, 0))
    return pl.pallas_call(
        functools.partial(_attn_kernel, tq=tq),
        grid=(bsz, d, lc // tq),
        in_specs=[cur(0), cur(1), prev(1), cur(2), prev(2),
                  pl.BlockSpec(bias.shape, lambda b, r, i: (0, 0, 0))],
        out_specs=[out_spec(ATT_OUT_WIDTH), out_spec(LANES)],
        out_shape=[jax.ShapeDtypeStruct((bsz, d, lc, ATT_OUT_WIDTH), BF16),
                   jax.ShapeDtypeStruct((bsz, d, lc, LANES), F32)],
        scratch_shapes=[pltpu.VMEM((Q_BLOCK + tq, PROJ_TILE), BF16),
                        pltpu.VMEM((Q_BLOCK + tq, PROJ_TILE), BF16)],
        compiler_params=_params(("arbitrary", "arbitrary", "arbitrary"), 40),
        name=f"attention_group{group}",
    )(qkv, qkv, qkv, qkv, qkv, bias)


def _merge_kernel(*refs, tm):
    ys_ref = refs[0]
    o_refs = refs[1:1 + N_ATT_GROUPS]
    l_refs = refs[1 + N_ATT_GROUPS:1 + 2 * N_ATT_GROUPS]
    k = 1 + 2 * N_ATT_GROUPS
    n_gate = (len(refs) - k - 7) // 2
    gs_refs = refs[k:k + n_gate]
    ga_refs = refs[k + n_gate:k + 2 * n_gate]
    x_ref, gate_ref, wbs_ref, wba_ref, wout_ref, out_ref, tok_ref = refs[k + 2 * n_gate:]

    def token_order(ref, g):
        dil = DILATION_PATTERN[g][1]
        if dil == 1:
            return ref[0].astype(F32)
        n_chunks = ref.shape[-1] // LANES
        for r in range(dil):
            for ch in range(n_chunks):
                tok_ref[ch, pl.ds(r, tm // dil, stride=dil), :] = (
                    ref[r, :, ch * LANES:(ch + 1) * LANES].astype(F32))
        return jnp.concatenate([tok_ref[ch] for ch in range(n_chunks)], axis=-1)

    lses = [token_order(r, g) for g, r in enumerate(l_refs)]
    m = functools.reduce(jnp.maximum, lses)
    es = [jnp.exp(l - m) for l in lses]
    den = functools.reduce(lambda a, b: a + b, es)

    def per_head(w):
        return jnp.concatenate([jnp.broadcast_to(w[:, h * LSE_LANES:h * LSE_LANES + 1], (tm, HEAD_DIM))
                                for h in range(HEADS_PER_GROUP)], axis=-1)

    y_att = functools.reduce(lambda a, b: a + b,
                             [per_head(e / den) * token_order(r, g) for g, (e, r) in enumerate(zip(es, o_refs))])

    m_ssm = jnp.dot(ys_ref[...], wbs_ref[...], preferred_element_type=F32)
    m_att = jnp.dot(y_att.astype(BF16), wba_ref[...], preferred_element_type=F32)
    g_ssm = jnp.concatenate([r[...] for r in gs_refs], axis=-1).astype(F32)
    g_att = jnp.concatenate([r[...] for r in ga_refs], axis=-1).astype(F32)
    merged = jax.nn.sigmoid(g_ssm) * m_ssm + jax.nn.sigmoid(g_att) * m_att
    mixed = jnp.dot(merged.astype(BF16), wout_ref[...], preferred_element_type=F32)
    out_ref[...] = x_ref[...] + gate_ref[...] * mixed


def _resident(a, layer=None):
    nd = a.ndim
    if layer is None:
        return pl.BlockSpec(a.shape, lambda *_: (0,) * nd, pipeline_mode=pl.Buffered(1))
    return pl.BlockSpec((None,) + a.shape[1:], lambda *_: (layer,) + (0,) * (nd - 1), pipeline_mode=pl.Buffered(1))


def _merge(x, gate, y_ssm, att, main, w_branch_ssm, w_branch_att, w_out, tm=512):
    bsz, seqlen, d = x.shape
    n_gate = d // PROJ_TILE
    row = lambda width: pl.BlockSpec((None, tm, width), lambda b, i: (b, i, 0))
    tile = lambda t: pl.BlockSpec((None, None, tm, PROJ_TILE), lambda b, i: (t, b, i, 0))
    res_major = lambda dil, width: pl.BlockSpec((None, dil, tm // dil, width), lambda b, i: (b, 0, i, 0))
    wbs, wba, wout = (w.astype(BF16) for w in (w_branch_ssm, w_branch_att, w_out))
    os_, ls_ = zip(*att)
    in_specs = ([row(SSM_WIDTH)]
                + [res_major(dil, ATT_OUT_WIDTH) for _, dil in DILATION_PATTERN]
                + [res_major(dil, LANES) for _, dil in DILATION_PATTERN]
                + [tile(MAIN_GATE0 + t) for t in range(2 * n_gate)]
                + [row(d), pl.BlockSpec((None, 1, d), lambda b, i: (b, 0, 0)),
                   _resident(wbs), _resident(wba), _resident(wout)])
    return pl.pallas_call(
        functools.partial(_merge_kernel, tm=tm),
        grid=(bsz, seqlen // tm),
        in_specs=in_specs,
        out_specs=row(d),
        out_shape=jax.ShapeDtypeStruct(x.shape, F32),
        scratch_shapes=[pltpu.VMEM((ATT_OUT_WIDTH // LANES, tm, LANES), F32)],
        compiler_params=_params(("arbitrary", "arbitrary"), 56),
        name="merge",
    )(y_ssm, *os_, *ls_, *([main] * (2 * n_gate)), x, gate[:, None, :], wbs, wba, wout)


def _ffn_kernel(x_ref, g_ref, shift_ref, scale_ref, gate_ref, wg_ref, wu_ref, wd_ref,
                o_ref, h_ref, acc_ref):
    f = pl.program_id(2)

    @pl.when(f == 0)
    def _():
        h = _norm_modulate(x_ref[...], g_ref[...], shift_ref[...], scale_ref[...])
        h_ref[...] = h.astype(BF16)
        acc_ref[...] = jnp.zeros_like(acc_ref)

    h = h_ref[...]
    a = jnp.dot(h, wg_ref[...].astype(BF16), preferred_element_type=F32)
    b = jnp.dot(h, wu_ref[...].astype(BF16), preferred_element_type=F32)
    act = (a * jax.nn.sigmoid(a)) * b
    acc_ref[...] += jnp.dot(act.astype(BF16), wd_ref[...].astype(BF16), preferred_element_type=F32)

    @pl.when(f == pl.num_programs(2) - 1)
    def _():
        o_ref[...] = x_ref[...] + gate_ref[...] * acc_ref[...]


def _dense_ffn(x, g, shift, scale, gate, w_gate, w_up, w_down, tm=1024, tf=512):
    bsz, seqlen, d = x.shape
    dff = w_gate.shape[1]
    vec = pl.BlockSpec((None, 1, d), lambda b, i, f: (b, 0, 0))
    row = pl.BlockSpec((None, tm, d), lambda b, i, f: (b, i, 0), pipeline_mode=pl.Buffered(1))
    return pl.pallas_call(
        _ffn_kernel,
        grid=(bsz, seqlen // tm, dff // tf),
        in_specs=[row,
                  pl.BlockSpec((1, d), lambda b, i, f: (0, 0)),
                  vec, vec, vec,
                  pl.BlockSpec((d, tf), lambda b, i, f: (0, f)),
                  pl.BlockSpec((d, tf), lambda b, i, f: (0, f)),
                  pl.BlockSpec((tf, d), lambda b, i, f: (f, 0))],
        out_specs=row,
        out_shape=jax.ShapeDtypeStruct(x.shape, F32),
        scratch_shapes=[pltpu.VMEM((tm, d), BF16), pltpu.VMEM((tm, d), F32)],
        compiler_params=_params(("arbitrary", "arbitrary", "arbitrary"), 56),
        name="dense_ffn",
    )(x, g.reshape(1, d), shift[:, None, :], scale[:, None, :], gate[:, None, :],
      w_gate.astype(BF16), w_up.astype(BF16), w_down.astype(BF16))


MOE_SUB = 256
SUBS_PER_TILE = 4
MOE_SUPER = SUBS_PER_TILE * MOE_SUB


def _pack_halves(x):
    half = x.shape[-1] // 2
    bits = lambda v: lax.bitcast_convert_type(v.astype(BF16).astype(F32), jnp.uint32)
    return bits(x[:, half:]) | (bits(x[:, :half]) >> 16)


def _unpack_halves(w):
    lo = lax.bitcast_convert_type(w << 16, F32)
    hi = lax.bitcast_convert_type(w & jnp.uint32(0xFFFF0000), F32)
    return lo, hi


def _router_kernel(x_ref, g_ref, shift_ref, scale_ref, wr_ref, h_ref, idx_ref, gate_ref):
    h = _norm_modulate(x_ref[...], g_ref[...], shift_ref[...], scale_ref[...])
    h_ref[...] = _pack_halves(h)
    logits = jnp.dot(h, wr_ref[...], preferred_element_type=F32, precision=lax.Precision.HIGHEST)
    lane_i = lax.broadcasted_iota(jnp.int32, logits.shape, 1)
    lane = lane_i.astype(F32)
    logits = jnp.where(lane_i < N_EXPERTS, logits, -jnp.inf)
    m1 = jnp.max(logits, axis=-1, keepdims=True)
    i1 = jnp.min(jnp.where(logits == m1, lane, float(LANES)), axis=-1, keepdims=True)
    rest = jnp.where(lane == i1, -jnp.inf, logits)
    m2 = jnp.max(rest, axis=-1, keepdims=True)
    i2 = jnp.min(jnp.where(rest == m2, lane, float(LANES)), axis=-1, keepdims=True)
    e2 = jnp.exp(m2 - m1)
    den = 1.0 + e2
    idx_ref[...] = jnp.where(lane_i == 0, i1, jnp.where(lane_i == 1, i2, 0.0)).astype(jnp.int32)
    gate_ref[...] = jnp.where(lane_i == 0, 1.0 / den, jnp.where(lane_i == 1, e2 / den, 0.0))


def _router(x, g, shift, scale, w_router, tm=512):
    bsz, seqlen, d = x.shape
    wr = jnp.zeros((d, LANES), F32).at[:, :N_EXPERTS].set(w_router)
    row = lambda width: pl.BlockSpec((None, tm, width), lambda b, i: (b, i, 0))
    vec = pl.BlockSpec((None, 1, d), lambda b, i: (b, 0, 0))
    return pl.pallas_call(
        _router_kernel,
        grid=(bsz, seqlen // tm),
        in_specs=[row(d), pl.BlockSpec((1, d), lambda b, i: (0, 0)), vec, vec,
                  pl.BlockSpec((d, LANES), lambda b, i: (0, 0))],
        out_specs=[row(d // 2), row(LANES), row(LANES)],
        out_shape=[jax.ShapeDtypeStruct((bsz, seqlen, d // 2), jnp.uint32),
                   jax.ShapeDtypeStruct((bsz, seqlen, LANES), jnp.int32),
                   jax.ShapeDtypeStruct((bsz, seqlen, LANES), F32)],
        compiler_params=_params(("arbitrary", "arbitrary"), 40),
        name="moe_router",
    )(x, g.reshape(1, d), shift[:, None, :], scale[:, None, :], wr)


def _moe_dims(n_tok, nf):
    n_sub = n_tok * TOP_K // MOE_SUB + N_EXPERTS
    n_super = (n_sub + (SUBS_PER_TILE - 1) * N_EXPERTS) // SUBS_PER_TILE + 1
    rows_per_step = -(-MOE_SUPER // nf)
    while (rows_per_step * nf) % SUBLANES:
        rows_per_step += 1
    n_fetch = rows_per_step * nf
    return n_sub, n_super, rows_per_step, n_fetch


def _routing_tables(top_expert, n_tok, nf):
    n_sub, n_super, _, n_fetch = _moe_dims(n_tok, nf)
    i32 = jnp.int32
    flat_e = top_expert.reshape(-1)
    onehot = (flat_e[:, None] == jnp.arange(N_EXPERTS, dtype=i32)[None, :]).astype(i32)
    csum = jnp.cumsum(onehot, axis=0)
    rank = jnp.sum(csum * onehot, axis=1) - 1
    counts = csum[-1]
    subs = (counts + MOE_SUB - 1) // MOE_SUB
    pend = jnp.cumsum(subs) * MOE_SUB
    pstart = pend - subs * MOE_SUB
    dest = (pstart[flat_e] + rank).astype(i32)
    flat_token = jnp.arange(n_tok * TOP_K, dtype=i32) // TOP_K
    row_token = jnp.zeros((n_sub * MOE_SUB + n_fetch,), i32).at[dest].set(flat_token)
    supers = (subs + SUBS_PER_TILE - 1) // SUBS_PER_TILE
    send = jnp.cumsum(supers)
    sstart = send - supers
    s = jnp.arange(n_super + 1, dtype=i32)
    e = jnp.minimum(jnp.searchsorted(send, s, side='right'), N_EXPERTS - 1).astype(i32)
    local = s - sstart[e]
    used = s < send[-1]
    tile_expert = jnp.where(used, e, e[jnp.maximum(send[-1] - 1, 0)]).astype(i32)
    n_tiles_e = jnp.maximum(supers[e], 1)
    base, rem = subs[e] // n_tiles_e, subs[e] % n_tiles_e
    first_sub = local * base + jnp.minimum(local, rem)
    tile_row0 = jnp.where(used, pstart[e] + first_sub * MOE_SUB, 0).astype(i32)
    tile_nsub = jnp.where(used, base + (local < rem).astype(i32), 0).astype(i32)
    n_used = send[-1].astype(i32).reshape(1)
    n_sub_used = (pend[-1] // MOE_SUB).astype(i32).reshape(1)
    return dest, row_token, tile_expert, tile_row0, tile_nsub, n_used, n_sub_used


def _moe_kernel(te_ref, row0_ref, nsub_ref, nu_ref, nsu_ref, tok_ref, hp_ref, wg_ref, wu_ref, wd_ref, y_ref,
                xg_ref, xb_ref, acc_ref, yb_ref, gsem, osem, fsem, *, nf, rows_per_step, n_sub_alloc, n_fill):
    s, f = pl.program_id(0), pl.program_id(1)
    n_used = nu_ref[0]
    used = s < n_used
    slot = s % 2
    n_fetch = rows_per_step * nf
    half = xg_ref.shape[-1]

    def row_copy(tile, slot_, r):
        tok = tok_ref[row0_ref[tile] + r]
        return pltpu.make_async_copy(hp_ref.at[pl.ds(tok, 1)], xg_ref.at[slot_, pl.ds(r, 1)], gsem.at[slot_])

    def wait_rows(slot_):
        pltpu.make_async_copy(hp_ref.at[pl.ds(0, n_fetch)], xg_ref.at[slot_], gsem.at[slot_]).wait()

    def out_copies(tile):
        r0 = pl.multiple_of(row0_ref[tile], MOE_SUB)
        return [pltpu.make_async_copy(yb_ref.at[pl.ds(k * MOE_SUB, MOE_SUB)],
                                      y_ref.at[pl.ds(r0 + k * MOE_SUB, MOE_SUB)], osem.at[k])
                for k in range(SUBS_PER_TILE)]

    def start_out(tile):
        for k, cp in enumerate(out_copies(tile)):
            pl.when(k < nsub_ref[tile])(cp.start)

    def wait_out(tile):
        for k, cp in enumerate(out_copies(tile)):
            pl.when(k < nsub_ref[tile])(cp.wait)

    @pl.when(jnp.logical_and(s == 0, f == 0))
    def _():
        def body(r, _):
            row_copy(0, 0, r).start()
            return 0
        lax.fori_loop(0, n_fetch, body, 0)

    @pl.when(jnp.logical_and(f == 0, s <= n_used))
    def _():
        wait_rows(slot)

    @pl.when(jnp.logical_and(used, f == 0))
    def _():
        lo, hi = _unpack_halves(xg_ref[slot, 0:MOE_SUPER, :])
        xb_ref[:, :half] = lo.astype(BF16)
        xb_ref[:, half:] = hi.astype(BF16)
        acc_ref[...] = jnp.zeros_like(acc_ref)

    def step(n_rows):
        for k in range(rows_per_step):
            row_copy(s + 1, 1 - slot, f * rows_per_step + k).start()
        h = xb_ref[0:n_rows, :]
        a = jnp.dot(h, wg_ref[...].astype(BF16), preferred_element_type=F32)
        b = jnp.dot(h, wu_ref[...].astype(BF16), preferred_element_type=F32)
        act = (a * jax.nn.sigmoid(a)) * b
        acc_ref[0:n_rows, :] += jnp.dot(act.astype(BF16), wd_ref[...].astype(BF16), preferred_element_type=F32)

    for n in range(1, SUBS_PER_TILE + 1):
        pl.when(jnp.logical_and(used, nsub_ref[s] == n))(functools.partial(step, n * MOE_SUB))

    @pl.when(jnp.logical_and(used, f == nf - 1))
    def _():
        pl.when(s > 0)(functools.partial(wait_out, s - 1))
        yb_ref[...] = _pack_halves(acc_ref[...])
        start_out(s)

    @pl.when(jnp.logical_and(s == n_used, f == 0))
    def _():
        wait_out(s - 1)
        yb_ref[0:MOE_SUB, :] = jnp.zeros((MOE_SUB, half), yb_ref.dtype)
        fills = []
        for k in range(n_fill):
            sub = nsu_ref[0] + k
            cp = pltpu.make_async_copy(yb_ref.at[pl.ds(0, MOE_SUB)],
                                       y_ref.at[pl.ds(pl.multiple_of(sub * MOE_SUB, MOE_SUB), MOE_SUB)], fsem.at[k])
            fills.append((sub < n_sub_alloc, cp))
        for cond, cp in fills:
            pl.when(cond)(cp.start)
        for cond, cp in fills:
            pl.when(cond)(cp.wait)


def _moe_experts(hp, tables, w_gate, w_up, w_down, tf=512):
    n_tok, half = hp.shape
    d = 2 * half
    dff = w_gate.shape[2]
    nf = dff // tf
    n_sub, n_super, rows_per_step, n_fetch = _moe_dims(n_tok, nf)
    _, row_token, tile_expert, tile_row0, tile_nsub, n_used, n_sub_used = tables
    n_unused_max = n_sub - n_tok * TOP_K // MOE_SUB
    tile_f = lambda s, f, nu: jnp.where(s < nu[0], f, nf - 1)
    w_in_spec = pl.BlockSpec((None, d, tf), lambda s, f, te, r0, ns, nu, nsu, tok: (te[s], 0, tile_f(s, f, nu)))
    w_out_spec = pl.BlockSpec((None, tf, d), lambda s, f, te, r0, ns, nu, nsu, tok: (te[s], tile_f(s, f, nu), 0))
    return pl.pallas_call(
        functools.partial(_moe_kernel, nf=nf, rows_per_step=rows_per_step, n_sub_alloc=n_sub,
                          n_fill=n_unused_max),
        grid_spec=pltpu.PrefetchScalarGridSpec(
            num_scalar_prefetch=6,
            grid=(n_super, nf),
            in_specs=[pl.BlockSpec(memory_space=pl.ANY), w_in_spec, w_in_spec, w_out_spec],
            out_specs=pl.BlockSpec(memory_space=pl.ANY),
            scratch_shapes=[pltpu.VMEM((2, n_fetch, half), jnp.uint32),
                            pltpu.VMEM((MOE_SUPER, d), BF16),
                            pltpu.VMEM((MOE_SUPER, d), F32),
                            pltpu.VMEM((MOE_SUPER, half), jnp.uint32),
                            pltpu.SemaphoreType.DMA((2,)), pltpu.SemaphoreType.DMA((SUBS_PER_TILE,)),
                            pltpu.SemaphoreType.DMA((n_unused_max,))]),
        out_shape=jax.ShapeDtypeStruct((n_sub * MOE_SUB, half), jnp.uint32),
        compiler_params=_params(("arbitrary", "arbitrary"), 58),
        name="moe_experts",
    )(tile_expert, tile_row0, tile_nsub, n_used, n_sub_used, row_token, hp, w_gate, w_up, w_down)


COMBINE_UNROLL = 8


def _combine_kernel(dest_ref, y_ref, x_ref, gates_ref, gate_f_ref, gn_ref, o_ref, rows_ref, sems,
                    *, tm, n_steps, final_norm):
    t = pl.program_id(0)
    slot = t % 2
    n = tm * TOP_K
    half = rows_ref.shape[-1]

    def row_copy(base, j, row, slot_):
        return pltpu.make_async_copy(y_ref.at[pl.ds(dest_ref[base + j], 1)],
                                     rows_ref.at[slot_, pl.ds(row, 1)], sems.at[slot_])

    def wait_rows(slot_):
        pltpu.make_async_copy(y_ref.at[pl.ds(0, n)], rows_ref.at[slot_], sems.at[slot_]).wait()

    @pl.when(t == 0)
    def _():
        def body(i, _):
            for u in range(COMBINE_UNROLL):
                row = (u % TOP_K) * tm + i * (COMBINE_UNROLL // TOP_K) + u // TOP_K
                row_copy(0, i * COMBINE_UNROLL + u, row, 0).start()
            return 0
        lax.fori_loop(0, n // COMBINE_UNROLL, body, 0)

    wait_rows(slot)
    next_base = jnp.minimum(t + 1, n_steps - 1) * n
    for j in range(n):
        row_copy(next_base, j, (j % TOP_K) * tm + j // TOP_K, 1 - slot).start()

    gates = gates_ref[...]
    g0, g1 = gates[:, 0:1], gates[:, 1:2]
    lo0, hi0 = _unpack_halves(rows_ref[slot, 0:tm, :])
    lo1, hi1 = _unpack_halves(rows_ref[slot, tm:2 * tm, :])
    out_lo = x_ref[:, :half] + gate_f_ref[:, :half] * (g0 * lo0 + g1 * lo1)
    out_hi = x_ref[:, half:] + gate_f_ref[:, half:] * (g0 * hi0 + g1 * hi1)
    if final_norm:
        ssq = jnp.sum(out_lo * out_lo, axis=-1, keepdims=True) + jnp.sum(out_hi * out_hi, axis=-1, keepdims=True)
        inv = lax.rsqrt(ssq / (2 * half) + EPS)
        out_lo = (out_lo * inv) * gn_ref[:, :half]
        out_hi = (out_hi * inv) * gn_ref[:, half:]
    o_ref[:, :half] = out_lo
    o_ref[:, half:] = out_hi
    pl.when(t == n_steps - 1)(functools.partial(wait_rows, 1 - slot))


def _combine(x, y_rows, dest, top_gate, gate_f, final_g, tm=256):
    bsz, seqlen, d = x.shape
    final_norm = final_g is not None
    gn = (final_g if final_norm else jnp.ones((d,), F32)).reshape(1, d)
    per_batch = seqlen // tm
    n_steps = bsz * per_batch
    row = lambda width: pl.BlockSpec((None, tm, width), lambda t, dst: (t // per_batch, t % per_batch, 0))
    return pl.pallas_call(
        functools.partial(_combine_kernel, tm=tm, n_steps=n_steps, final_norm=final_norm),
        grid_spec=pltpu.PrefetchScalarGridSpec(
            num_scalar_prefetch=1,
            grid=(n_steps,),
            in_specs=[pl.BlockSpec(memory_space=pl.ANY), row(d), row(LANES),
                      pl.BlockSpec((None, 1, d), lambda t, dst: (t // per_batch, 0, 0)),
                      pl.BlockSpec((1, d), lambda t, dst: (0, 0))],
            out_specs=row(d),
            scratch_shapes=[pltpu.VMEM((2, TOP_K * tm, d // 2), jnp.uint32), pltpu.SemaphoreType.DMA((2,))]),
        out_shape=jax.ShapeDtypeStruct(x.shape, F32),
        compiler_params=_params(("arbitrary",), 32),
        name="moe_combine",
    )(dest, y_rows, x, top_gate, gate_f[:, None, :], gn)


def _moe_ffn(x, g, shift, scale, gate_f, w_router, w_gate, w_up, w_down, final_g, tf=512):
    bsz, seqlen, d = x.shape
    n_tok = bsz * seqlen
    hp, top_idx, top_gate = _router(x, g, shift, scale, w_router)
    tables = _routing_tables(top_idx[..., :TOP_K], n_tok, w_gate.shape[2] // tf)
    y_rows = _moe_experts(hp.reshape(n_tok, d // 2), tables, w_gate, w_up, w_down, tf=tf)
    return _combine(x, y_rows, tables[0], top_gate, gate_f, final_g)


def _final_norm_kernel(x_ref, g_ref, o_ref):
    x = x_ref[...]
    ms = jnp.mean(x * x, axis=-1, keepdims=True)
    o_ref[...] = (x * lax.rsqrt(ms + EPS)) * g_ref[...]


def _final_norm(x, g, tm=512):
    bsz, seqlen, d = x.shape
    return pl.pallas_call(
        _final_norm_kernel,
        grid=(bsz, seqlen // tm),
        in_specs=[pl.BlockSpec((None, tm, d), lambda b, i: (b, i, 0)),
                  pl.BlockSpec((1, d), lambda b, i: (0, 0))],
        out_specs=pl.BlockSpec((None, tm, d), lambda b, i: (b, i, 0)),
        out_shape=jax.ShapeDtypeStruct(x.shape, F32),
        compiler_params=_params(("arbitrary", "arbitrary"), 32),
        name="final_norm",
    )(x, g.reshape(1, d))


def kernel(x, c, w_mod, b_mod, norm_mix_g, norm_ffn_g, w_in, ssm_a_re, ssm_a_im, ssm_log_dt, ssm_b_re, ssm_b_im, ssm_c_re, ssm_c_im, ssm_d, w_glu, b_glu, rel_bias, w_branch_ssm, w_branch_att, w_out, ffn_w_gate, ffn_w_up, ffn_w_down, moe_router, moe_w_gate, moe_w_up, moe_w_down, final_norm_g):
    depth = w_mod.shape[0]
    mod = _modulation(c, w_mod, b_mod)
    biases = [_bias_tile(rel_bias, g) for g in range(N_ATT_GROUPS)]
    tables = _s5_tables(ssm_a_re, ssm_a_im, ssm_log_dt, ssm_b_re, ssm_b_im, ssm_c_re, ssm_c_im)
    for i in range(depth):
        shift_m, scale_m, gate_m, shift_f, scale_f, gate_f = jnp.split(mod[i], N_MOD, axis=-1)
        main, qkv = _in_projection(x, norm_mix_g[i], shift_m, scale_m, w_in, i)
        y_ssm = _s5_branch(main, tables, i, ssm_d[i], w_glu[i], b_glu[i])
        att = [_attention_group(qkv[g], biases[g], g) for g in range(N_ATT_GROUPS)]
        x = _merge(x, gate_m, y_ssm, att, main, w_branch_ssm[i], w_branch_att[i], w_out[i])
        j = i // 2
        last = i == depth - 1
        if i % 2 == 0:
            x = _dense_ffn(x, norm_ffn_g[i], shift_f, scale_f, gate_f,
                           ffn_w_gate[j], ffn_w_up[j], ffn_w_down[j])
            if last:
                x = _final_norm(x, final_norm_g)
        else:
            x = _moe_ffn(x, norm_ffn_g[i], shift_f, scale_f, gate_f, moe_router[j],
                         moe_w_gate[j], moe_w_up[j], moe_w_down[j], final_norm_g if last else None)
    return x
```

```python
import functools
import math

import jax
import jax.numpy as jnp
from jax import lax
from jax.experimental import pallas as pl
from jax.experimental.pallas import tpu as pltpu

F32 = jnp.float32
BF16 = jnp.bfloat16

LANES = 128
SUBLANES = 8
VMEM_BYTES = 64 * 1024 * 1024

SSM_GROUP = 16
SSM_STATE = 64
SSM_WIDTH = 1024
HEAD_DIM = 128
DILATION_PATTERN = ((128, 1), (512, 4), (2048, 16))
HEADS_PER_GROUP = 4
N_ATT_GROUPS = len(DILATION_PATTERN)
ATT_OUT_WIDTH = HEADS_PER_GROUP * HEAD_DIM
Q_BLOCK = 128
NEG_INF = -1e30
N_BUCKETS = 32
MAX_DISTANCE = 2048
N_EXPERTS = 8
TOP_K = 2
N_MOD = 6
EPS = 1e-6

PROJ_TILE = 512
U_TILE0 = 0
Q_TILE0 = SSM_WIDTH // PROJ_TILE
K_TILE0 = Q_TILE0 + N_ATT_GROUPS
V_TILE0 = K_TILE0 + N_ATT_GROUPS
GS_TILE0 = V_TILE0 + N_ATT_GROUPS
MAIN_GATE0 = Q_TILE0


def _params(dims, vmem_mb):
    return pltpu.CompilerParams(dimension_semantics=dims,
                                vmem_limit_bytes=vmem_mb * 1024 * 1024)


def _norm_modulate(x, g, shift, scale):
    ms = jnp.mean(x * x, axis=-1, keepdims=True)
    y = x * lax.rsqrt(ms + EPS)
    return (y * g) * (1.0 + scale) + shift


def _mod_kernel(c_ref, w_ref, b_ref, o_ref):
    c = c_ref[...]
    cond = (c * jax.nn.sigmoid(c)).astype(BF16)
    o_ref[...] = jnp.dot(cond, w_ref[...].astype(BF16),
                         preferred_element_type=F32) + b_ref[...]


def _modulation(c, w_mod, b_mod):
    depth, d, n = w_mod.shape
    bsz = c.shape[0]
    rows = SUBLANES
    c_pad = jnp.zeros((rows, d), F32).at[:bsz].set(c)
    tn = 1536
    out = pl.pallas_call(
        _mod_kernel,
        grid=(depth, n // tn),
        in_specs=[pl.BlockSpec((rows, d), lambda l, j: (0, 0)),
                  pl.BlockSpec((None, d, tn), lambda l, j: (l, 0, j)),
                  pl.BlockSpec((None, 1, tn), lambda l, j: (l, 0, j))],
        out_specs=pl.BlockSpec((None, rows, tn), lambda l, j: (l, 0, j)),
        out_shape=jax.ShapeDtypeStruct((depth, rows, n), F32),
        compiler_params=_params(("arbitrary", "arbitrary"), 40),
        name="modulation",
    )(c_pad, w_mod, b_mod.reshape(depth, 1, n))
    return out[:, :bsz]


def _inproj_kernel(x_ref, g_ref, shift_ref, scale_ref, w_ref, main_ref, *rest, tm):
    qkv_refs, (h_ref, res_ref) = rest[:N_ATT_GROUPS], rest[N_ATT_GROUPS:]
    j = pl.program_id(2)

    @pl.when(j == 0)
    def _():
        h = _norm_modulate(x_ref[...], g_ref[...], shift_ref[...], scale_ref[...])
        h_ref[...] = h.astype(BF16)

    def project():
        return jnp.dot(h_ref[...], w_ref[...].astype(BF16), preferred_element_type=F32)

    @pl.when(jnp.logical_or(j < Q_TILE0, j >= GS_TILE0))
    def _():
        main_ref[...] = project().astype(BF16)

    for g, (_, dil) in enumerate(DILATION_PATTERN):
        is_g = functools.reduce(jnp.logical_or, [j == t0 + g for t0 in (Q_TILE0, K_TILE0, V_TILE0)])

        @pl.when(is_g)
        def _(g=g, dil=dil):
            res = project()
            if dil == 1:
                qkv_refs[g][0] = res.astype(BF16)
            else:
                n_chunks = PROJ_TILE // LANES
                for ch in range(n_chunks):
                    res_ref[ch] = res[:, ch * LANES:(ch + 1) * LANES]
                for r in range(dil):
                    rows = [res_ref[ch, pl.ds(r, tm // dil, stride=dil), :] for ch in range(n_chunks)]
                    qkv_refs[g][r] = jnp.concatenate(rows, axis=-1).astype(BF16)


def _in_projection(x, g, shift, scale, w_in, layer, tm=1024):
    bsz, seqlen, d = x.shape
    n_tiles = w_in.shape[2] // PROJ_TILE
    n_main = n_tiles - 3 * N_ATT_GROUPS

    def main_map(b, i, j):
        return (jnp.where(j < Q_TILE0, j, jnp.where(j < GS_TILE0, Q_TILE0 - 1, j - 3 * N_ATT_GROUPS)), b, i, 0)

    def qkv_map(g):
        return lambda b, i, j: ((j > Q_TILE0 + g).astype(jnp.int32) + (j > K_TILE0 + g).astype(jnp.int32),
                                b, 0, i, 0)

    qkv_specs = [pl.BlockSpec((None, None, dil, tm // dil, PROJ_TILE), qkv_map(g))
                 for g, (_, dil) in enumerate(DILATION_PATTERN)]
    qkv_shapes = [jax.ShapeDtypeStruct((3, bsz, dil, seqlen // dil, PROJ_TILE), BF16)
                  for _, dil in DILATION_PATTERN]
    outs = pl.pallas_call(
        functools.partial(_inproj_kernel, tm=tm),
        grid=(bsz, seqlen // tm, n_tiles),
        in_specs=[pl.BlockSpec((None, tm, d), lambda b, i, j: (b, i, 0)),
                  pl.BlockSpec((1, d), lambda b, i, j: (0, 0)),
                  pl.BlockSpec((None, 1, d), lambda b, i, j: (b, 0, 0)),
                  pl.BlockSpec((None, 1, d), lambda b, i, j: (b, 0, 0)),
                  pl.BlockSpec((None, d, PROJ_TILE), lambda b, i, j: (layer, 0, j))],
        out_specs=[pl.BlockSpec((None, None, tm, PROJ_TILE), main_map)] + qkv_specs,
        out_shape=[jax.ShapeDtypeStruct((n_main, bsz, seqlen, PROJ_TILE), BF16)] + qkv_shapes,
        scratch_shapes=[pltpu.VMEM((tm, d), BF16), pltpu.VMEM((PROJ_TILE // LANES, tm, LANES), F32)],
        compiler_params=_params(("arbitrary", "arbitrary", "arbitrary"), 52),
        name="in_projection",
    )(x, g.reshape(1, d), shift[:, None, :], scale[:, None, :], w_in)
    return outs[0], outs[1:]


SSM_GB = 16
SSM_NGB = (SSM_WIDTH // SSM_GROUP) // SSM_GB
SSM_GB_IN = SSM_GB * SSM_GROUP
SSM_GB_RE = SSM_GB * SSM_STATE
SSM_CB = 2 * SSM_GB_RE // LANES
SSM_SLOTS = SSM_NGB * SSM_CB + SUBLANES


def _s5_kernel(ua0_ref, ua1_ref, un0_ref, un1_ref, bm_ref, cm_ref, ar_ref, ai_ref, d_ref, wglu_ref, bglu_ref,
               o_ref, xs0_ref, xs1_ref, ug_ref, st_ref, *, bsz, tt):
    xs_ref = (xs0_ref, xs1_ref)
    rows = bsz * tt
    n_nt = bm_ref.shape[1]
    n_pieces = SSM_NGB * n_nt
    unroll = 4
    assert tt == unroll * n_pieces
    half_cb = SSM_CB // 2

    def pair_halves(c):
        return tuple(r[:, c * tt:(c + 1) * tt, :].reshape(rows, PROJ_TILE) for r in (ua0_ref, ua1_ref))

    def stage_u(halves):
        for gb in range(SSM_NGB):
            half, off = divmod(gb * SSM_GB_IN, PROJ_TILE)
            ug_ref[gb] = halves[half][:, off:off + SSM_GB_IN]

    def b_piece(i, dst):
        gb, nt = i // n_nt, i % n_nt
        res = jnp.dot(ug_ref[gb], bm_ref[gb, nt], preferred_element_type=F32)
        for j in range(2):
            xs_ref[dst][pl.ds(gb * SSM_CB + 2 * nt + j, rows, stride=SSM_SLOTS), :] = (
                res[:, j * LANES:(j + 1) * LANES])

    a_re = [ar_ref[gb] for gb in range(SSM_NGB)]
    a_im = [ai_ref[gb] for gb in range(SSM_NGB)]

    def time_step(t, carry, src):
        new = []
        for b in range(bsz):
            for gb in range(SSM_NGB):
                k = (b * SSM_NGB + gb) * 2
                s_re, s_im = carry[k], carry[k + 1]
                row = (b * tt + t) * SSM_SLOTS + gb * SSM_CB
                n_re = a_re[gb] * s_re - a_im[gb] * s_im + xs_ref[src][pl.ds(row, half_cb), :]
                n_im = a_re[gb] * s_im + a_im[gb] * s_re + xs_ref[src][pl.ds(row + half_cb, half_cb), :]
                xs_ref[src][pl.ds(row, half_cb), :] = n_re
                xs_ref[src][pl.ds(row + half_cb, half_cb), :] = n_im
                new += [n_re, n_im]
        return tuple(new)

    n_carry = bsz * SSM_NGB * 2

    def scan_chunk(src):
        carry = tuple(st_ref[k] for k in range(n_carry))
        for it in range(n_pieces):
            b_piece(it, 1 - src)
            for k in range(unroll):
                carry = time_step(it * unroll + k, carry, src)
        for k in range(n_carry):
            st_ref[k] = carry[k]

    def finish_chunk(c):
        ys = []
        for gb in range(SSM_NGB):
            cols = [xs_ref[c][pl.ds(gb * SSM_CB + cb, rows, stride=SSM_SLOTS), :] for cb in range(SSM_CB)]
            xg = jnp.concatenate(cols, axis=-1).astype(BF16)
            ys.append(jnp.dot(xg, cm_ref[gb], preferred_element_type=F32))
        y = jnp.concatenate(ys, axis=-1)
        u32 = jnp.concatenate(pair_halves(c), axis=-1).astype(F32)
        y = jax.nn.gelu(y + d_ref[...] * u32)
        z = jnp.dot(y.astype(BF16), wglu_ref[...], preferred_element_type=F32) + bglu_ref[...]
        o_ref[:, c * tt:(c + 1) * tt, :] = (y * jax.nn.sigmoid(z)).astype(o_ref.dtype).reshape(bsz, tt, SSM_WIDTH)

    @pl.when(pl.program_id(0) == 0)
    def _():
        st_ref[...] = jnp.zeros_like(st_ref)
        stage_u(pair_halves(0))
        for i in range(n_pieces):
            b_piece(i, 0)

    stage_u(pair_halves(1))
    scan_chunk(0)
    finish_chunk(0)
    stage_u(tuple(r[...].reshape(rows, PROJ_TILE) for r in (un0_ref, un1_ref)))
    scan_chunk(1)
    finish_chunk(1)


def _block_diag(m):
    ngb, rows, c = m.shape
    r = rows // SSM_GB
    row_group = lax.broadcasted_iota(jnp.int32, (rows, SSM_GB * c), 0) // r
    col_group = lax.broadcasted_iota(jnp.int32, (rows, SSM_GB * c), 1) // c
    return jnp.where(row_group == col_group, jnp.tile(m, (1, 1, SSM_GB)), 0.0)


def _s5_tables(a_re, a_im, log_dt, b_re, b_im, c_re, c_im):
    depth = a_re.shape[0]
    lam = lax.complex(a_re.astype(F32), a_im.astype(F32))
    dt = jnp.exp(log_dt.astype(F32))[..., None]
    a_bar = jnp.exp(lam * dt)
    b_bar = ((a_bar - 1.0) / lam)[..., None] * lax.complex(b_re.astype(F32), b_im.astype(F32))
    _, g, p, h = b_bar.shape
    n = depth * SSM_NGB
    bt = jnp.transpose(b_bar, (0, 1, 3, 2)).reshape(n, SSM_GB * h, p)
    bm = jnp.concatenate([_block_diag(bt.real), _block_diag(bt.imag)], axis=-1)
    ct_re = jnp.transpose(c_re.astype(F32), (0, 1, 3, 2)).reshape(n, SSM_GB * p, h)
    ct_im = jnp.transpose(c_im.astype(F32), (0, 1, 3, 2)).reshape(n, SSM_GB * p, h)
    cm = jnp.concatenate([_block_diag(ct_re), -_block_diag(ct_im)], axis=1)
    lead = (depth, SSM_NGB)
    return (bm.astype(BF16).reshape(lead + bm.shape[1:]), cm.astype(BF16).reshape(lead + cm.shape[1:]),
            a_bar.real.reshape(lead + (SSM_CB // 2, LANES)), a_bar.imag.reshape(lead + (SSM_CB // 2, LANES)))


def _s5_branch(proj, tables, layer, d_skip, w_glu, b_glu, tt=128):
    _, bsz, seqlen, _ = proj.shape
    bm, cm, ar, ai = tables
    n_nt = bm.shape[-1] // SSM_GB_IN
    bm = jnp.swapaxes(bm.reshape(bm.shape[:3] + (n_nt, SSM_GB_IN)), 2, 3)
    d_skip = d_skip.reshape(1, SSM_WIDTH).astype(F32)
    w_glu = w_glu.astype(BF16)
    b_glu = b_glu.reshape(1, SSM_WIDTH).astype(F32)
    n_chunks = seqlen // tt
    pair = lambda tile: pl.BlockSpec((None, bsz, 2 * tt, PROJ_TILE), lambda s: (tile, 0, s, 0))
    nxt = lambda tile: pl.BlockSpec((None, bsz, tt, PROJ_TILE),
                                    lambda s: (tile, 0, jnp.minimum(2 * s + 2, n_chunks - 1), 0))
    return pl.pallas_call(
        functools.partial(_s5_kernel, bsz=bsz, tt=tt),
        grid=(n_chunks // 2,),
        in_specs=[pair(U_TILE0), pair(U_TILE0 + 1), nxt(U_TILE0), nxt(U_TILE0 + 1),
                  _resident(bm, layer), _resident(cm, layer), _resident(ar, layer), _resident(ai, layer),
                  _resident(d_skip), _resident(w_glu), _resident(b_glu)],
        out_specs=pl.BlockSpec((bsz, 2 * tt, SSM_WIDTH), lambda s: (0, s, 0)),
        out_shape=jax.ShapeDtypeStruct((bsz, seqlen, SSM_WIDTH), BF16),
        scratch_shapes=[pltpu.VMEM((bsz * tt * SSM_SLOTS, LANES), F32),
                        pltpu.VMEM((bsz * tt * SSM_SLOTS, LANES), F32),
                        pltpu.VMEM((SSM_NGB, bsz * tt, SSM_GB_IN), BF16),
                        pltpu.VMEM((bsz * SSM_NGB * 2, SSM_CB // 2, LANES), F32)],
        compiler_params=_params(("arbitrary",), 48),
        name="s5_branch",
    )(proj, proj, proj, proj, bm, cm, ar, ai, d_skip, w_glu, b_glu)


def _t5_causal_bucket(dist):
    max_exact = N_BUCKETS // 2
    d32 = jnp.maximum(dist, 1).astype(F32)
    large = max_exact + (jnp.log(d32 / max_exact) / math.log(MAX_DISTANCE / max_exact)
                         * (N_BUCKETS - max_exact)).astype(jnp.int32)
    return jnp.where(dist < max_exact, dist, jnp.minimum(large, N_BUCKETS - 1))


def _bias_tile(rel_bias, group):
    window, dilation = DILATION_PATTERN[group]
    steps = window // dilation
    assert steps == Q_BLOCK
    heads = slice(group * HEADS_PER_GROUP, (group + 1) * HEADS_PER_GROUP)
    back = jnp.arange(steps, -1, -1, dtype=jnp.int32)
    vals = rel_bias[_t5_causal_bucket(back * dilation)][:, heads].astype(F32).T
    period = 3 * Q_BLOCK
    v = jnp.concatenate([vals, jnp.full((HEADS_PER_GROUP, period - steps - 1), NEG_INF, F32)], axis=1)
    flat = jnp.tile(v, (1, Q_BLOCK))[:, :Q_BLOCK * (period - 1)]
    tile = flat.reshape(HEADS_PER_GROUP, Q_BLOCK, period - 1)[:, :, :2 * Q_BLOCK]
    col = lax.broadcasted_iota(jnp.int32, tile.shape, 2)
    return jnp.concatenate([tile, jnp.where(col < Q_BLOCK, NEG_INF, tile)], axis=0)


LSE_LANES = LANES // HEADS_PER_GROUP


def _attn_kernel(q_ref, kc_ref, kp_ref, vc_ref, vp_ref, bias_ref, o_ref, lse_ref, kf_ref, vf_ref, *, tq):
    kf_ref[0:Q_BLOCK, :] = kp_ref[...]
    kf_ref[Q_BLOCK:, :] = kc_ref[...]
    vf_ref[0:Q_BLOCK, :] = vp_ref[...]
    vf_ref[Q_BLOCK:, :] = vc_ref[...]
    scale = HEAD_DIM ** -0.5
    first_tile = pl.program_id(2) == 0
    lane_head = lax.broadcasted_iota(jnp.int32, (Q_BLOCK, LANES), 1) // LSE_LANES

    def block(jb, _):
        r0 = pl.multiple_of(jb * Q_BLOCK, Q_BLOCK)
        bias_set = jnp.logical_and(first_tile, jb == 0).astype(jnp.int32) * HEADS_PER_GROUP
        lse = jnp.zeros((Q_BLOCK, LANES), F32)
        for h in range(HEADS_PER_GROUP):
            hs = slice(h * HEAD_DIM, (h + 1) * HEAD_DIM)
            q = q_ref[pl.ds(r0, Q_BLOCK), hs]
            k2 = kf_ref[pl.ds(r0, 2 * Q_BLOCK), hs]
            v2 = vf_ref[pl.ds(r0, 2 * Q_BLOCK), hs]
            s = lax.dot_general(q, k2, (((1,), (1,)), ((), ())), preferred_element_type=F32)
            s = s * scale + bias_ref[bias_set + h]
            m = jnp.max(s, axis=-1, keepdims=True)
            p = jnp.exp(s - m)
            l = jnp.sum(p, axis=-1, keepdims=True)
            o = jnp.dot(p.astype(BF16), v2, preferred_element_type=F32) / l
            o_ref[pl.ds(r0, Q_BLOCK), hs] = o.astype(o_ref.dtype)
            lse = jnp.where(lane_head == h, m + jnp.log(l), lse)
        lse_ref[pl.ds(r0, Q_BLOCK), :] = lse
        return 0

    lax.fori_loop(0, tq // Q_BLOCK, block, 0, unroll=min(4, tq // Q_BLOCK))


def _attention_group(qkv, bias, group, tile0=0):
    _, bsz, d, lc, _ = qkv.shape
    tq = min(lc, 1024)
    per_tq = tq // Q_BLOCK
    cur = lambda which: pl.BlockSpec((None, None, None, tq, PROJ_TILE),
                                     lambda b, r, i: (tile0 + which, b, r, i, 0))
    prev = lambda which: pl.BlockSpec((None, None, None, Q_BLOCK, PROJ_TILE),
                                      lambda b, r, i: (tile0 + which, b, r, jnp.maximum(i * per_tq - 1, 0), 0))
    out_spec = lambda width: pl.BlockSpec((None, None, tq, width), lambda b, r, i: (b, r, i, 0))
    return pl.pallas_call(
        functools.partial(_attn_kernel, tq=tq),
        grid=(bsz, d, lc // tq),
        in_specs=[cur(0), cur(1), prev(1), cur(2), prev(2),
                  pl.BlockSpec(bias.shape, lambda b, r, i: (0, 0, 0))],
        out_specs=[out_spec(ATT_OUT_WIDTH), out_spec(LANES)],
        out_shape=[jax.ShapeDtypeStruct((bsz, d, lc, ATT_OUT_WIDTH), BF16),
                   jax.ShapeDtypeStruct((bsz, d, lc, LANES), F32)],
        scratch_shapes=[pltpu.VMEM((Q_BLOCK + tq, PROJ_TILE), BF16),
                        pltpu.VMEM((Q_BLOCK + tq, PROJ_TILE), BF16)],
        compiler_params=_params(("arbitrary", "arbitrary", "arbitrary"), 40),
        name=f"attention_group{group}",
    )(qkv, qkv, qkv, qkv, qkv, bias)


def _merge_kernel(*refs, tm):
    ys_ref = refs[0]
    o_refs = refs[1:1 + N_ATT_GROUPS]
    l_refs = refs[1 + N_ATT_GROUPS:1 + 2 * N_ATT_GROUPS]
    k = 1 + 2 * N_ATT_GROUPS
    n_gate = (len(refs) - k - 7) // 2
    gs_refs = refs[k:k + n_gate]
    ga_refs = refs[k + n_gate:k + 2 * n_gate]
    x_ref, gate_ref, wbs_ref, wba_ref, wout_ref, out_ref, tok_ref = refs[k + 2 * n_gate:]

    def token_order(ref, g):
        dil = DILATION_PATTERN[g][1]
        if dil == 1:
            return ref[0].astype(F32)
        n_chunks = ref.shape[-1] // LANES
        for r in range(dil):
            for ch in range(n_chunks):
                tok_ref[ch, pl.ds(r, tm // dil, stride=dil), :] = (
                    ref[r, :, ch * LANES:(ch + 1) * LANES].astype(F32))
        return jnp.concatenate([tok_ref[ch] for ch in range(n_chunks)], axis=-1)

    lses = [token_order(r, g) for g, r in enumerate(l_refs)]
    m = functools.reduce(jnp.maximum, lses)
    es = [jnp.exp(l - m) for l in lses]
    den = functools.reduce(lambda a, b: a + b, es)

    def per_head(w):
        return jnp.concatenate([jnp.broadcast_to(w[:, h * LSE_LANES:h * LSE_LANES + 1], (tm, HEAD_DIM))
                                for h in range(HEADS_PER_GROUP)], axis=-1)

    y_att = functools.reduce(lambda a, b: a + b,
                             [per_head(e / den) * token_order(r, g) for g, (e, r) in enumerate(zip(es, o_refs))])

    m_ssm = jnp.dot(ys_ref[...], wbs_ref[...], preferred_element_type=F32)
    m_att = jnp.dot(y_att.astype(BF16), wba_ref[...], preferred_element_type=F32)
    g_ssm = jnp.concatenate([r[...] for r in gs_refs], axis=-1).astype(F32)
    g_att = jnp.concatenate([r[...] for r in ga_refs], axis=-1).astype(F32)
    merged = jax.nn.sigmoid(g_ssm) * m_ssm + jax.nn.sigmoid(g_att) * m_att
    mixed = jnp.dot(merged.astype(BF16), wout_ref[...], preferred_element_type=F32)
    out_ref[...] = x_ref[...] + gate_ref[...] * mixed


def _resident(a, layer=None):
    nd = a.ndim
    if layer is None:
        return pl.BlockSpec(a.shape, lambda *_: (0,) * nd, pipeline_mode=pl.Buffered(1))
    return pl.BlockSpec((None,) + a.shape[1:], lambda *_: (layer,) + (0,) * (nd - 1), pipeline_mode=pl.Buffered(1))


def _merge(x, gate, y_ssm, att, main, w_branch_ssm, w_branch_att, w_out, tm=512):
    bsz, seqlen, d = x.shape
    n_gate = d // PROJ_TILE
    row = lambda width: pl.BlockSpec((None, tm, width), lambda b, i: (b, i, 0))
    tile = lambda t: pl.BlockSpec((None, None, tm, PROJ_TILE), lambda b, i: (t, b, i, 0))
    res_major = lambda dil, width: pl.BlockSpec((None, dil, tm // dil, width), lambda b, i: (b, 0, i, 0))
    wbs, wba, wout = (w.astype(BF16) for w in (w_branch_ssm, w_branch_att, w_out))
    os_, ls_ = zip(*att)
    in_specs = ([row(SSM_WIDTH)]
                + [res_major(dil, ATT_OUT_WIDTH) for _, dil in DILATION_PATTERN]
                + [res_major(dil, LANES) for _, dil in DILATION_PATTERN]
                + [tile(MAIN_GATE0 + t) for t in range(2 * n_gate)]
                + [row(d), pl.BlockSpec((None, 1, d), lambda b, i: (b, 0, 0)),
                   _resident(wbs), _resident(wba), _resident(wout)])
    return pl.pallas_call(
        functools.partial(_merge_kernel, tm=tm),
        grid=(bsz, seqlen // tm),
        in_specs=in_specs,
        out_specs=row(d),
        out_shape=jax.ShapeDtypeStruct(x.shape, F32),
        scratch_shapes=[pltpu.VMEM((ATT_OUT_WIDTH // LANES, tm, LANES), F32)],
        compiler_params=_params(("arbitrary", "arbitrary"), 56),
        name="merge",
    )(y_ssm, *os_, *ls_, *([main] * (2 * n_gate)), x, gate[:, None, :], wbs, wba, wout)


def _ffn_kernel(x_ref, g_ref, shift_ref, scale_ref, gate_ref, wg_ref, wu_ref, wd_ref,
                o_ref, h_ref):
    f = pl.program_id(2)

    @pl.when(f == 0)
    def _():
        h = _norm_modulate(x_ref[...], g_ref[...], shift_ref[...], scale_ref[...])
        h_ref[...] = h.astype(BF16)
        o_ref[...] = jnp.zeros_like(o_ref)

    h = h_ref[...]
    a = jnp.dot(h, wg_ref[...].astype(BF16), preferred_element_type=F32)
    b = jnp.dot(h, wu_ref[...].astype(BF16), preferred_element_type=F32)
    act = (a * jax.nn.sigmoid(a)) * b
    o_ref[...] += jnp.dot(act.astype(BF16), wd_ref[...].astype(BF16), preferred_element_type=F32)

    @pl.when(f == pl.num_programs(2) - 1)
    def _():
        o_ref[...] = x_ref[...] + gate_ref[...] * o_ref[...]


def _dense_ffn(x, g, shift, scale, gate, w_gate, w_up, w_down, tm=1024, tf=512):
    bsz, seqlen, d = x.shape
    dff = w_gate.shape[1]
    vec = pl.BlockSpec((None, 1, d), lambda b, i, f: (b, 0, 0))
    row = pl.BlockSpec((None, tm, d), lambda b, i, f: (b, i, 0), pipeline_mode=pl.Buffered(1))
    return pl.pallas_call(
        _ffn_kernel,
        grid=(bsz, seqlen // tm, dff // tf),
        in_specs=[pl.BlockSpec((None, tm, d), lambda b, i, f: (b, i, 0)),
                  pl.BlockSpec((1, d), lambda b, i, f: (0, 0)),
                  vec, vec, vec,
                  pl.BlockSpec((d, tf), lambda b, i, f: (0, f)),
                  pl.BlockSpec((d, tf), lambda b, i, f: (0, f)),
                  pl.BlockSpec((tf, d), lambda b, i, f: (f, 0))],
        out_specs=row,
        out_shape=jax.ShapeDtypeStruct(x.shape, F32),
        scratch_shapes=[pltpu.VMEM((tm, d), BF16)],
        compiler_params=_params(("arbitrary", "arbitrary", "arbitrary"), 56),
        name="dense_ffn",
    )(x, g.reshape(1, d), shift[:, None, :], scale[:, None, :], gate[:, None, :],
      w_gate.astype(BF16), w_up.astype(BF16), w_down.astype(BF16))


MOE_SUB = 256
SUBS_PER_TILE = 4
MOE_SUPER = SUBS_PER_TILE * MOE_SUB


def _pack_halves(x):
    half = x.shape[-1] // 2
    bits = lambda v: lax.bitcast_convert_type(v.astype(BF16).astype(F32), jnp.uint32)
    return bits(x[:, half:]) | (bits(x[:, :half]) >> 16)


def _unpack_halves(w):
    lo = lax.bitcast_convert_type(w << 16, F32)
    hi = lax.bitcast_convert_type(w & jnp.uint32(0xFFFF0000), F32)
    return lo, hi


def _router_kernel(x_ref, g_ref, shift_ref, scale_ref, wr_ref, h_ref, idx_ref, gate_ref):
    h = _norm_modulate(x_ref[...], g_ref[...], shift_ref[...], scale_ref[...])
    h_ref[...] = _pack_halves(h)
    logits = jnp.dot(h, wr_ref[...], preferred_element_type=F32, precision=lax.Precision.HIGHEST)
    lane_i = lax.broadcasted_iota(jnp.int32, logits.shape, 1)
    lane = lane_i.astype(F32)
    logits = jnp.where(lane_i < N_EXPERTS, logits, -jnp.inf)
    m1 = jnp.max(logits, axis=-1, keepdims=True)
    i1 = jnp.min(jnp.where(logits == m1, lane, float(LANES)), axis=-1, keepdims=True)
    rest = jnp.where(lane == i1, -jnp.inf, logits)
    m2 = jnp.max(rest, axis=-1, keepdims=True)
    i2 = jnp.min(jnp.where(rest == m2, lane, float(LANES)), axis=-1, keepdims=True)
    e2 = jnp.exp(m2 - m1)
    den = 1.0 + e2
    idx_ref[...] = jnp.where(lane_i == 0, i1, jnp.where(lane_i == 1, i2, 0.0)).astype(jnp.int32)
    gate_ref[...] = jnp.where(lane_i == 0, 1.0 / den, jnp.where(lane_i == 1, e2 / den, 0.0))


def _router(x, g, shift, scale, w_router, tm=512):
    bsz, seqlen, d = x.shape
    wr = jnp.zeros((d, LANES), F32).at[:, :N_EXPERTS].set(w_router)
    row = lambda width: pl.BlockSpec((None, tm, width), lambda b, i: (b, i, 0))
    vec = pl.BlockSpec((None, 1, d), lambda b, i: (b, 0, 0))
    return pl.pallas_call(
        _router_kernel,
        grid=(bsz, seqlen // tm),
        in_specs=[row(d), pl.BlockSpec((1, d), lambda b, i: (0, 0)), vec, vec,
                  pl.BlockSpec((d, LANES), lambda b, i: (0, 0))],
        out_specs=[row(d // 2), row(LANES), row(LANES)],
        out_shape=[jax.ShapeDtypeStruct((bsz, seqlen, d // 2), jnp.uint32),
                   jax.ShapeDtypeStruct((bsz, seqlen, LANES), jnp.int32),
                   jax.ShapeDtypeStruct((bsz, seqlen, LANES), F32)],
        compiler_params=_params(("arbitrary", "arbitrary"), 40),
        name="moe_router",
    )(x, g.reshape(1, d), shift[:, None, :], scale[:, None, :], wr)


def _moe_dims(n_tok, nf):
    n_sub = n_tok * TOP_K // MOE_SUB + N_EXPERTS
    n_super = (n_sub + (SUBS_PER_TILE - 1) * N_EXPERTS) // SUBS_PER_TILE + 1
    rows_per_step = -(-MOE_SUPER // nf)
    while (rows_per_step * nf) % SUBLANES:
        rows_per_step += 1
    n_fetch = rows_per_step * nf
    return n_sub, n_super, rows_per_step, n_fetch


def _routing_tables(top_expert, n_tok, nf):
    n_sub, n_super, _, n_fetch = _moe_dims(n_tok, nf)
    i32 = jnp.int32
    flat_e = top_expert.reshape(-1)
    onehot = (flat_e[:, None] == jnp.arange(N_EXPERTS, dtype=i32)[None, :]).astype(i32)
    csum = jnp.cumsum(onehot, axis=0)
    rank = jnp.sum(csum * onehot, axis=1) - 1
    counts = csum[-1]
    subs = (counts + MOE_SUB - 1) // MOE_SUB
    pend = jnp.cumsum(subs) * MOE_SUB
    pstart = pend - subs * MOE_SUB
    dest = (pstart[flat_e] + rank).astype(i32)
    flat_token = jnp.arange(n_tok * TOP_K, dtype=i32) // TOP_K
    row_token = jnp.zeros((n_sub * MOE_SUB + n_fetch,), i32).at[dest].set(flat_token)
    supers = (subs + SUBS_PER_TILE - 1) // SUBS_PER_TILE
    send = jnp.cumsum(supers)
    sstart = send - supers
    s = jnp.arange(n_super + 1, dtype=i32)
    e = jnp.minimum(jnp.searchsorted(send, s, side='right'), N_EXPERTS - 1).astype(i32)
    local = s - sstart[e]
    used = s < send[-1]
    tile_expert = jnp.where(used, e, e[jnp.maximum(send[-1] - 1, 0)]).astype(i32)
    n_tiles_e = jnp.maximum(supers[e], 1)
    base, rem = subs[e] // n_tiles_e, subs[e] % n_tiles_e
    first_sub = local * base + jnp.minimum(local, rem)
    tile_row0 = jnp.where(used, pstart[e] + first_sub * MOE_SUB, 0).astype(i32)
    tile_nsub = jnp.where(used, base + (local < rem).astype(i32), 0).astype(i32)
    n_used = send[-1].astype(i32).reshape(1)
    n_sub_used = (pend[-1] // MOE_SUB).astype(i32).reshape(1)
    return dest, row_token, tile_expert, tile_row0, tile_nsub, n_used, n_sub_used


def _moe_kernel(te_ref, row0_ref, nsub_ref, nu_ref, nsu_ref, tok_ref, hp_ref, wg_ref, wu_ref, wd_ref, y_ref,
                xg_ref, xb_ref, acc_ref, yb_ref, gsem, osem, fsem, *, nf, rows_per_step, n_sub_alloc, n_fill):
    s, f = pl.program_id(0), pl.program_id(1)
    n_used = nu_ref[0]
    used = s < n_used
    slot = s % 2
    n_fetch = rows_per_step * nf
    half = xg_ref.shape[-1]

    def row_copy(tile, slot_, r):
        tok = tok_ref[row0_ref[tile] + r]
        return pltpu.make_async_copy(hp_ref.at[pl.ds(tok, 1)], xg_ref.at[slot_, pl.ds(r, 1)], gsem.at[slot_])

    def wait_rows(slot_):
        pltpu.make_async_copy(hp_ref.at[pl.ds(0, n_fetch)], xg_ref.at[slot_], gsem.at[slot_]).wait()

    def out_copies(tile):
        r0 = pl.multiple_of(row0_ref[tile], MOE_SUB)
        return [pltpu.make_async_copy(yb_ref.at[pl.ds(k * MOE_SUB, MOE_SUB)],
                                      y_ref.at[pl.ds(r0 + k * MOE_SUB, MOE_SUB)], osem.at[k])
                for k in range(SUBS_PER_TILE)]

    def start_out(tile):
        for k, cp in enumerate(out_copies(tile)):
            pl.when(k < nsub_ref[tile])(cp.start)

    def wait_out(tile):
        for k, cp in enumerate(out_copies(tile)):
            pl.when(k < nsub_ref[tile])(cp.wait)

    @pl.when(jnp.logical_and(s == 0, f == 0))
    def _():
        def body(r, _):
            row_copy(0, 0, r).start()
            return 0
        lax.fori_loop(0, n_fetch, body, 0)

    @pl.when(jnp.logical_and(f == 0, s <= n_used))
    def _():
        wait_rows(slot)

    @pl.when(jnp.logical_and(used, f == 0))
    def _():
        lo, hi = _unpack_halves(xg_ref[slot, 0:MOE_SUPER, :])
        xb_ref[:, :half] = lo.astype(BF16)
        xb_ref[:, half:] = hi.astype(BF16)
        acc_ref[...] = jnp.zeros_like(acc_ref)

    def step(n_rows):
        for k in range(rows_per_step):
            row_copy(s + 1, 1 - slot, f * rows_per_step + k).start()
        h = xb_ref[0:n_rows, :]
        a = jnp.dot(h, wg_ref[...].astype(BF16), preferred_element_type=F32)
        b = jnp.dot(h, wu_ref[...].astype(BF16), preferred_element_type=F32)
        act = (a * jax.nn.sigmoid(a)) * b
        acc_ref[0:n_rows, :] += jnp.dot(act.astype(BF16), wd_ref[...].astype(BF16), preferred_element_type=F32)

    for n in range(1, SUBS_PER_TILE + 1):
        pl.when(jnp.logical_and(used, nsub_ref[s] == n))(functools.partial(step, n * MOE_SUB))

    @pl.when(jnp.logical_and(used, f == nf - 1))
    def _():
        pl.when(s > 0)(functools.partial(wait_out, s - 1))
        yb_ref[...] = _pack_halves(acc_ref[...])
        start_out(s)

    @pl.when(jnp.logical_and(s == n_used, f == 0))
    def _():
        wait_out(s - 1)
        yb_ref[0:MOE_SUB, :] = jnp.zeros((MOE_SUB, half), yb_ref.dtype)
        fills = []
        for k in range(n_fill):
            sub = nsu_ref[0] + k
            cp = pltpu.make_async_copy(yb_ref.at[pl.ds(0, MOE_SUB)],
                                       y_ref.at[pl.ds(pl.multiple_of(sub * MOE_SUB, MOE_SUB), MOE_SUB)], fsem.at[k])
            fills.append((sub < n_sub_alloc, cp))
        for cond, cp in fills:
            pl.when(cond)(cp.start)
        for cond, cp in fills:
            pl.when(cond)(cp.wait)


def _moe_experts(hp, tables, w_gate, w_up, w_down, tf=512):
    n_tok, half = hp.shape
    d = 2 * half
    dff = w_gate.shape[2]
    nf = dff // tf
    n_sub, n_super, rows_per_step, n_fetch = _moe_dims(n_tok, nf)
    _, row_token, tile_expert, tile_row0, tile_nsub, n_used, n_sub_used = tables
    n_unused_max = n_sub - n_tok * TOP_K // MOE_SUB
    tile_f = lambda s, f, nu: jnp.where(s < nu[0], f, nf - 1)
    w_in_spec = pl.BlockSpec((None, d, tf), lambda s, f, te, r0, ns, nu, nsu, tok: (te[s], 0, tile_f(s, f, nu)))
    w_out_spec = pl.BlockSpec((None, tf, d), lambda s, f, te, r0, ns, nu, nsu, tok: (te[s], tile_f(s, f, nu), 0))
    return pl.pallas_call(
        functools.partial(_moe_kernel, nf=nf, rows_per_step=rows_per_step, n_sub_alloc=n_sub,
                          n_fill=n_unused_max),
        grid_spec=pltpu.PrefetchScalarGridSpec(
            num_scalar_prefetch=6,
            grid=(n_super, nf),
            in_specs=[pl.BlockSpec(memory_space=pl.ANY), w_in_spec, w_in_spec, w_out_spec],
            out_specs=pl.BlockSpec(memory_space=pl.ANY),
            scratch_shapes=[pltpu.VMEM((2, n_fetch, half), jnp.uint32),
                            pltpu.VMEM((MOE_SUPER, d), BF16),
                            pltpu.VMEM((MOE_SUPER, d), F32),
                            pltpu.VMEM((MOE_SUPER, half), jnp.uint32),
                            pltpu.SemaphoreType.DMA((2,)), pltpu.SemaphoreType.DMA((SUBS_PER_TILE,)),
                            pltpu.SemaphoreType.DMA((n_unused_max,))]),
        out_shape=jax.ShapeDtypeStruct((n_sub * MOE_SUB, half), jnp.uint32),
        compiler_params=_params(("arbitrary", "arbitrary"), 58),
        name="moe_experts",
    )(tile_expert, tile_row0, tile_nsub, n_used, n_sub_used, row_token, hp, w_gate, w_up, w_down)


COMBINE_UNROLL = 8


def _combine_kernel(dest_ref, y_ref, x_ref, gates_ref, gate_f_ref, gn_ref, o_ref, rows_ref, sems,
                    *, tm, n_steps, final_norm):
    t = pl.program_id(0)
    slot = t % 2
    n = tm * TOP_K
    half = rows_ref.shape[-1]

    def row_copy(base, j, row, slot_):
        return pltpu.make_async_copy(y_ref.at[pl.ds(dest_ref[base + j], 1)],
                                     rows_ref.at[slot_, pl.ds(row, 1)], sems.at[slot_])

    def wait_rows(slot_):
        pltpu.make_async_copy(y_ref.at[pl.ds(0, n)], rows_ref.at[slot_], sems.at[slot_]).wait()

    @pl.when(t == 0)
    def _():
        def body(i, _):
            for u in range(COMBINE_UNROLL):
                row = (u % TOP_K) * tm + i * (COMBINE_UNROLL // TOP_K) + u // TOP_K
                row_copy(0, i * COMBINE_UNROLL + u, row, 0).start()
            return 0
        lax.fori_loop(0, n // COMBINE_UNROLL, body, 0)

    wait_rows(slot)
    next_base = jnp.minimum(t + 1, n_steps - 1) * n
    for j in range(n):
        row_copy(next_base, j, (j % TOP_K) * tm + j // TOP_K, 1 - slot).start()

    gates = gates_ref[...]
    g0, g1 = gates[:, 0:1], gates[:, 1:2]
    lo0, hi0 = _unpack_halves(rows_ref[slot, 0:tm, :])
    lo1, hi1 = _unpack_halves(rows_ref[slot, tm:2 * tm, :])
    out_lo = x_ref[:, :half] + gate_f_ref[:, :half] * (g0 * lo0 + g1 * lo1)
    out_hi = x_ref[:, half:] + gate_f_ref[:, half:] * (g0 * hi0 + g1 * hi1)
    if final_norm:
        ssq = jnp.sum(out_lo * out_lo, axis=-1, keepdims=True) + jnp.sum(out_hi * out_hi, axis=-1, keepdims=True)
        inv = lax.rsqrt(ssq / (2 * half) + EPS)
        out_lo = (out_lo * inv) * gn_ref[:, :half]
        out_hi = (out_hi * inv) * gn_ref[:, half:]
    o_ref[:, :half] = out_lo
    o_ref[:, half:] = out_hi
    pl.when(t == n_steps - 1)(functools.partial(wait_rows, 1 - slot))


def _combine(x, y_rows, dest, top_gate, gate_f, final_g, tm=256):
    bsz, seqlen, d = x.shape
    final_norm = final_g is not None
    gn = (final_g if final_norm else jnp.ones((d,), F32)).reshape(1, d)
    per_batch = seqlen // tm
    n_steps = bsz * per_batch
    row = lambda width: pl.BlockSpec((None, tm, width), lambda t, dst: (t // per_batch, t % per_batch, 0))
    return pl.pallas_call(
        functools.partial(_combine_kernel, tm=tm, n_steps=n_steps, final_norm=final_norm),
        grid_spec=pltpu.PrefetchScalarGridSpec(
            num_scalar_prefetch=1,
            grid=(n_steps,),
            in_specs=[pl.BlockSpec(memory_space=pl.ANY), row(d), row(LANES),
                      pl.BlockSpec((None, 1, d), lambda t, dst: (t // per_batch, 0, 0)),
                      pl.BlockSpec((1, d), lambda t, dst: (0, 0))],
            out_specs=row(d),
            scratch_shapes=[pltpu.VMEM((2, TOP_K * tm, d // 2), jnp.uint32), pltpu.SemaphoreType.DMA((2,))]),
        out_shape=jax.ShapeDtypeStruct(x.shape, F32),
        compiler_params=_params(("arbitrary",), 32),
        name="moe_combine",
    )(dest, y_rows, x, top_gate, gate_f[:, None, :], gn)


def _moe_ffn(x, g, shift, scale, gate_f, w_router, w_gate, w_up, w_down, final_g, tf=512):
    bsz, seqlen, d = x.shape
    n_tok = bsz * seqlen
    hp, top_idx, top_gate = _router(x, g, shift, scale, w_router)
    tables = _routing_tables(top_idx[..., :TOP_K], n_tok, w_gate.shape[2] // tf)
    y_rows = _moe_experts(hp.reshape(n_tok, d // 2), tables, w_gate, w_up, w_down, tf=tf)
    return _combine(x, y_rows, tables[0], top_gate, gate_f, final_g)


def _final_norm_kernel(x_ref, g_ref, o_ref):
    x = x_ref[...]
    ms = jnp.mean(x * x, axis=-1, keepdims=True)
    o_ref[...] = (x * lax.rsqrt(ms + EPS)) * g_ref[...]


def _final_norm(x, g, tm=512):
    bsz, seqlen, d = x.shape
    return pl.pallas_call(
        _final_norm_kernel,
        grid=(bsz, seqlen // tm),
        in_specs=[pl.BlockSpec((None, tm, d), lambda b, i: (b, i, 0)),
                  pl.BlockSpec((1, d), lambda b, i: (0, 0))],
        out_specs=pl.BlockSpec((None, tm, d), lambda b, i: (b, i, 0)),
        out_shape=jax.ShapeDtypeStruct(x.shape, F32),
        compiler_params=_params(("arbitrary", "arbitrary"), 32),
        name="final_norm",
    )(x, g.reshape(1, d))


def kernel(x, c, w_mod, b_mod, norm_mix_g, norm_ffn_g, w_in, ssm_a_re, ssm_a_im, ssm_log_dt, ssm_b_re, ssm_b_im, ssm_c_re, ssm_c_im, ssm_d, w_glu, b_glu, rel_bias, w_branch_ssm, w_branch_att, w_out, ffn_w_gate, ffn_w_up, ffn_w_down, moe_router, moe_w_gate, moe_w_up, moe_w_down, final_norm_g):
    depth = w_mod.shape[0]
    mod = _modulation(c, w_mod, b_mod)
    biases = [_bias_tile(rel_bias, g) for g in range(N_ATT_GROUPS)]
    tables = _s5_tables(ssm_a_re, ssm_a_im, ssm_log_dt, ssm_b_re, ssm_b_im, ssm_c_re, ssm_c_im)
    for i in range(depth):
        shift_m, scale_m, gate_m, shift_f, scale_f, gate_f = jnp.split(mod[i], N_MOD, axis=-1)
        main, qkv = _in_projection(x, norm_mix_g[i], shift_m, scale_m, w_in, i)
        y_ssm = _s5_branch(main, tables, i, ssm_d[i], w_glu[i], b_glu[i])
        att = [_attention_group(qkv[g], biases[g], g) for g in range(N_ATT_GROUPS)]
        x = _merge(x, gate_m, y_ssm, att, main, w_branch_ssm[i], w_branch_att[i], w_out[i])
        j = i // 2
        last = i == depth - 1
        if i % 2 == 0:
            x = _dense_ffn(x, norm_ffn_g[i], shift_f, scale_f, gate_f,
                           ffn_w_gate[j], ffn_w_up[j], ffn_w_down[j])
            if last:
                x = _final_norm(x, final_norm_g)
        else:
            x = _moe_ffn(x, norm_ffn_g[i], shift_f, scale_f, gate_f, moe_router[j],
                         moe_w_gate[j], moe_w_up[j], moe_w_down[j], final_norm_g if last else None)
    return x
```

```python
import functools
import math

import jax
import jax.numpy as jnp
from jax import lax
from jax.experimental import pallas as pl
from jax.experimental.pallas import tpu as pltpu

F32 = jnp.float32
BF16 = jnp.bfloat16

LANES = 128
SUBLANES = 8
VMEM_BYTES = 64 * 1024 * 1024

SSM_GROUP = 16
SSM_STATE = 64
SSM_WIDTH = 1024
HEAD_DIM = 128
DILATION_PATTERN = ((128, 1), (512, 4), (2048, 16))
HEADS_PER_GROUP = 4
N_ATT_GROUPS = len(DILATION_PATTERN)
ATT_OUT_WIDTH = HEADS_PER_GROUP * HEAD_DIM
Q_BLOCK = 128
NEG_INF = -1e30
N_BUCKETS = 32
MAX_DISTANCE = 2048
N_EXPERTS = 8
TOP_K = 2
N_MOD = 6
EPS = 1e-6

PROJ_TILE = 512
U_TILE0 = 0
Q_TILE0 = SSM_WIDTH // PROJ_TILE
K_TILE0 = Q_TILE0 + N_ATT_GROUPS
V_TILE0 = K_TILE0 + N_ATT_GROUPS
GS_TILE0 = V_TILE0 + N_ATT_GROUPS
MAIN_GATE0 = Q_TILE0


def _params(dims, vmem_mb):
    return pltpu.CompilerParams(dimension_semantics=dims,
                                vmem_limit_bytes=vmem_mb * 1024 * 1024)


def _norm_modulate(x, g, shift, scale):
    ms = jnp.mean(x * x, axis=-1, keepdims=True)
    y = x * lax.rsqrt(ms + EPS)
    return (y * g) * (1.0 + scale) + shift


NORM_ROWS = 64


def _norm_modulate_into(h_ref, x_ref, g_ref, shift_ref, scale_ref, inv_ref):
    n_blocks = x_ref.shape[0] // NORM_ROWS
    reps = x_ref.shape[1] // LANES

    def block(i):
        return pl.ds(pl.multiple_of(i * NORM_ROWS, NORM_ROWS), NORM_ROWS)

    def rms(i, _):
        x = x_ref[block(i), :]
        inv = lax.rsqrt(jnp.mean(x * x, axis=-1, keepdims=True) + EPS)
        inv_ref[block(i), :] = jnp.broadcast_to(inv, (NORM_ROWS, LANES))
        return 0

    gain = g_ref[...] * (1.0 + scale_ref[...])
    shift = shift_ref[...]

    def scale_rows(i, _):
        y = x_ref[block(i), :] * jnp.concatenate([inv_ref[block(i), :]] * reps, axis=-1)
        h_ref[block(i), :] = (y * gain + shift).astype(h_ref.dtype)
        return 0

    lax.fori_loop(0, n_blocks, rms, 0, unroll=4)
    lax.fori_loop(0, n_blocks, scale_rows, 0)


def _mod_kernel(c_ref, w_ref, b_ref, o_ref):
    c = c_ref[...]
    cond = (c * jax.nn.sigmoid(c)).astype(BF16)
    o_ref[...] = jnp.dot(cond, w_ref[...].astype(BF16),
                         preferred_element_type=F32) + b_ref[...]


def _modulation(c, w_mod, b_mod):
    depth, d, n = w_mod.shape
    bsz = c.shape[0]
    rows = SUBLANES
    c_pad = jnp.zeros((rows, d), F32).at[:bsz].set(c)
    tn = 1536
    out = pl.pallas_call(
        _mod_kernel,
        grid=(depth, n // tn),
        in_specs=[pl.BlockSpec((rows, d), lambda l, j: (0, 0)),
                  pl.BlockSpec((None, d, tn), lambda l, j: (l, 0, j)),
                  pl.BlockSpec((None, 1, tn), lambda l, j: (l, 0, j))],
        out_specs=pl.BlockSpec((None, rows, tn), lambda l, j: (l, 0, j)),
        out_shape=jax.ShapeDtypeStruct((depth, rows, n), F32),
        compiler_params=_params(("arbitrary", "arbitrary"), 40),
        name="modulation",
    )(c_pad, w_mod, b_mod.reshape(depth, 1, n))
    return out[:, :bsz]


def _inproj_kernel(x_ref, g_ref, shift_ref, scale_ref, w_ref, main_ref, *rest, tm):
    qkv_refs, (h_ref, res_ref, inv_ref) = rest[:N_ATT_GROUPS], rest[N_ATT_GROUPS:]
    j = pl.program_id(2)

    @pl.when(j == 0)
    def _():
        _norm_modulate_into(h_ref, x_ref, g_ref, shift_ref, scale_ref, inv_ref)

    def project():
        return jnp.dot(h_ref[...], w_ref[...].astype(BF16), preferred_element_type=F32)

    @pl.when(jnp.logical_or(j < Q_TILE0, j >= GS_TILE0))
    def _():
        main_ref[...] = project().astype(BF16)

    for g, (_, dil) in enumerate(DILATION_PATTERN):
        is_g = functools.reduce(jnp.logical_or, [j == t0 + g for t0 in (Q_TILE0, K_TILE0, V_TILE0)])

        @pl.when(is_g)
        def _(g=g, dil=dil):
            res = project()
            if dil == 1:
                qkv_refs[g][0] = res.astype(BF16)
            else:
                n_chunks = PROJ_TILE // LANES
                for ch in range(n_chunks):
                    res_ref[ch] = res[:, ch * LANES:(ch + 1) * LANES]
                for r in range(dil):
                    rows = [res_ref[ch, pl.ds(r, tm // dil, stride=dil), :] for ch in range(n_chunks)]
                    qkv_refs[g][r] = jnp.concatenate(rows, axis=-1).astype(BF16)


def _in_projection(x, g, shift, scale, w_in, layer, tm=1024):
    bsz, seqlen, d = x.shape
    n_tiles = w_in.shape[2] // PROJ_TILE
    n_main = n_tiles - 3 * N_ATT_GROUPS

    def main_map(b, i, j):
        return (jnp.where(j < Q_TILE0, j, jnp.where(j < GS_TILE0, Q_TILE0 - 1, j - 3 * N_ATT_GROUPS)), b, i, 0)

    def qkv_map(g):
        return lambda b, i, j: ((j > Q_TILE0 + g).astype(jnp.int32) + (j > K_TILE0 + g).astype(jnp.int32),
                                b, 0, i, 0)

    qkv_specs = [pl.BlockSpec((None, None, dil, tm // dil, PROJ_TILE), qkv_map(g))
                 for g, (_, dil) in enumerate(DILATION_PATTERN)]
    qkv_shapes = [jax.ShapeDtypeStruct((3, bsz, dil, seqlen // dil, PROJ_TILE), BF16)
                  for _, dil in DILATION_PATTERN]
    outs = pl.pallas_call(
        functools.partial(_inproj_kernel, tm=tm),
        grid=(bsz, seqlen // tm, n_tiles),
        in_specs=[pl.BlockSpec((None, tm, d), lambda b, i, j: (b, i, 0)),
                  pl.BlockSpec((1, d), lambda b, i, j: (0, 0)),
                  pl.BlockSpec((None, 1, d), lambda b, i, j: (b, 0, 0)),
                  pl.BlockSpec((None, 1, d), lambda b, i, j: (b, 0, 0)),
                  pl.BlockSpec((None, d, PROJ_TILE), lambda b, i, j: (layer, 0, j))],
        out_specs=[pl.BlockSpec((None, None, tm, PROJ_TILE), main_map)] + qkv_specs,
        out_shape=[jax.ShapeDtypeStruct((n_main, bsz, seqlen, PROJ_TILE), BF16)] + qkv_shapes,
        scratch_shapes=[pltpu.VMEM((tm, d), BF16), pltpu.VMEM((PROJ_TILE // LANES, tm, LANES), F32),
                        pltpu.VMEM((tm, LANES), F32)],
        compiler_params=_params(("arbitrary", "arbitrary", "arbitrary"), 52),
        name="in_projection",
    )(x, g.reshape(1, d), shift[:, None, :], scale[:, None, :], w_in)
    return outs[0], outs[1:]


SSM_GB = 16
SSM_NGB = (SSM_WIDTH // SSM_GROUP) // SSM_GB
SSM_GB_IN = SSM_GB * SSM_GROUP
SSM_GB_RE = SSM_GB * SSM_STATE
SSM_CB = 2 * SSM_GB_RE // LANES
SSM_SLOTS = SSM_NGB * SSM_CB + SUBLANES


def _s5_kernel(ua0_ref, ua1_ref, un0_ref, un1_ref, bm_ref, cm_ref, ar_ref, ai_ref, d_ref, wglu_ref, bglu_ref,
               o_ref, xs0_ref, xs1_ref, ug_ref, st_ref, *, bsz, tt):
    xs_ref = (xs0_ref, xs1_ref)
    rows = bsz * tt
    n_nt = bm_ref.shape[1]
    n_pieces = SSM_NGB * n_nt
    unroll = 4
    assert tt == unroll * n_pieces
    half_cb = SSM_CB // 2

    def pair_halves(c):
        return tuple(r[:, c * tt:(c + 1) * tt, :].reshape(rows, PROJ_TILE) for r in (ua0_ref, ua1_ref))

    def stage_u(halves):
        for gb in range(SSM_NGB):
            half, off = divmod(gb * SSM_GB_IN, PROJ_TILE)
            ug_ref[gb] = halves[half][:, off:off + SSM_GB_IN]

    def b_piece(i, dst):
        gb, nt = i // n_nt, i % n_nt
        res = jnp.dot(ug_ref[gb], bm_ref[gb, nt], preferred_element_type=F32)
        for j in range(2):
            xs_ref[dst][pl.ds(gb * SSM_CB + 2 * nt + j, rows, stride=SSM_SLOTS), :] = (
                res[:, j * LANES:(j + 1) * LANES])

    a_re = [ar_ref[gb] for gb in range(SSM_NGB)]
    a_im = [ai_ref[gb] for gb in range(SSM_NGB)]

    def time_step(t, carry, src):
        new = []
        for b in range(bsz):
            for gb in range(SSM_NGB):
                k = (b * SSM_NGB + gb) * 2
                s_re, s_im = carry[k], carry[k + 1]
                row = (b * tt + t) * SSM_SLOTS + gb * SSM_CB
                n_re = a_re[gb] * s_re - a_im[gb] * s_im + xs_ref[src][pl.ds(row, half_cb), :]
                n_im = a_re[gb] * s_im + a_im[gb] * s_re + xs_ref[src][pl.ds(row + half_cb, half_cb), :]
                xs_ref[src][pl.ds(row, half_cb), :] = n_re
                xs_ref[src][pl.ds(row + half_cb, half_cb), :] = n_im
                new += [n_re, n_im]
        return tuple(new)

    n_carry = bsz * SSM_NGB * 2

    def scan_chunk(src):
        carry = tuple(st_ref[k] for k in range(n_carry))
        for it in range(n_pieces):
            b_piece(it, 1 - src)
            for k in range(unroll):
                carry = time_step(it * unroll + k, carry, src)
        for k in range(n_carry):
            st_ref[k] = carry[k]

    def finish_chunk(c):
        ys = []
        for gb in range(SSM_NGB):
            cols = [xs_ref[c][pl.ds(gb * SSM_CB + cb, rows, stride=SSM_SLOTS), :] for cb in range(SSM_CB)]
            xg = jnp.concatenate(cols, axis=-1).astype(BF16)
            ys.append(jnp.dot(xg, cm_ref[gb], preferred_element_type=F32))
        y = jnp.concatenate(ys, axis=-1)
        u32 = jnp.concatenate(pair_halves(c), axis=-1).astype(F32)
        y = jax.nn.gelu(y + d_ref[...] * u32)
        z = jnp.dot(y.astype(BF16), wglu_ref[...], preferred_element_type=F32) + bglu_ref[...]
        o_ref[:, c * tt:(c + 1) * tt, :] = (y * jax.nn.sigmoid(z)).astype(o_ref.dtype).reshape(bsz, tt, SSM_WIDTH)

    @pl.when(pl.program_id(0) == 0)
    def _():
        st_ref[...] = jnp.zeros_like(st_ref)
        stage_u(pair_halves(0))
        for i in range(n_pieces):
            b_piece(i, 0)

    stage_u(pair_halves(1))
    scan_chunk(0)
    finish_chunk(0)
    stage_u(tuple(r[...].reshape(rows, PROJ_TILE) for r in (un0_ref, un1_ref)))
    scan_chunk(1)
    finish_chunk(1)


def _block_diag(m):
    ngb, rows, c = m.shape
    r = rows // SSM_GB
    row_group = lax.broadcasted_iota(jnp.int32, (rows, SSM_GB * c), 0) // r
    col_group = lax.broadcasted_iota(jnp.int32, (rows, SSM_GB * c), 1) // c
    return jnp.where(row_group == col_group, jnp.tile(m, (1, 1, SSM_GB)), 0.0)


def _s5_tables(a_re, a_im, log_dt, b_re, b_im, c_re, c_im):
    depth = a_re.shape[0]
    lam = lax.complex(a_re.astype(F32), a_im.astype(F32))
    dt = jnp.exp(log_dt.astype(F32))[..., None]
    a_bar = jnp.exp(lam * dt)
    b_bar = ((a_bar - 1.0) / lam)[..., None] * lax.complex(b_re.astype(F32), b_im.astype(F32))
    _, g, p, h = b_bar.shape
    n = depth * SSM_NGB
    bt = jnp.transpose(b_bar, (0, 1, 3, 2)).reshape(n, SSM_GB * h, p)
    bm = jnp.concatenate([_block_diag(bt.real), _block_diag(bt.imag)], axis=-1)
    ct_re = jnp.transpose(c_re.astype(F32), (0, 1, 3, 2)).reshape(n, SSM_GB * p, h)
    ct_im = jnp.transpose(c_im.astype(F32), (0, 1, 3, 2)).reshape(n, SSM_GB * p, h)
    cm = jnp.concatenate([_block_diag(ct_re), -_block_diag(ct_im)], axis=1)
    lead = (depth, SSM_NGB)
    return (bm.astype(BF16).reshape(lead + bm.shape[1:]), cm.astype(BF16).reshape(lead + cm.shape[1:]),
            a_bar.real.reshape(lead + (SSM_CB // 2, LANES)), a_bar.imag.reshape(lead + (SSM_CB // 2, LANES)))


def _s5_branch(proj, tables, layer, d_skip, w_glu, b_glu, tt=128):
    _, bsz, seqlen, _ = proj.shape
    bm, cm, ar, ai = tables
    n_nt = bm.shape[-1] // SSM_GB_IN
    bm = jnp.swapaxes(bm.reshape(bm.shape[:3] + (n_nt, SSM_GB_IN)), 2, 3)
    d_skip = d_skip.reshape(1, SSM_WIDTH).astype(F32)
    w_glu = w_glu.astype(BF16)
    b_glu = b_glu.reshape(1, SSM_WIDTH).astype(F32)
    n_chunks = seqlen // tt
    pair = lambda tile: pl.BlockSpec((None, bsz, 2 * tt, PROJ_TILE), lambda s: (tile, 0, s, 0))
    nxt = lambda tile: pl.BlockSpec((None, bsz, tt, PROJ_TILE),
                                    lambda s: (tile, 0, jnp.minimum(2 * s + 2, n_chunks - 1), 0))
    return pl.pallas_call(
        functools.partial(_s5_kernel, bsz=bsz, tt=tt),
        grid=(n_chunks // 2,),
        in_specs=[pair(U_TILE0), pair(U_TILE0 + 1), nxt(U_TILE0), nxt(U_TILE0 + 1),
                  _resident(bm, layer), _resident(cm, layer), _resident(ar, layer), _resident(ai, layer),
                  _resident(d_skip), _resident(w_glu), _resident(b_glu)],
        out_specs=pl.BlockSpec((bsz, 2 * tt, SSM_WIDTH), lambda s: (0, s, 0)),
        out_shape=jax.ShapeDtypeStruct((bsz, seqlen, SSM_WIDTH), BF16),
        scratch_shapes=[pltpu.VMEM((bsz * tt * SSM_SLOTS, LANES), F32),
                        pltpu.VMEM((bsz * tt * SSM_SLOTS, LANES), F32),
                        pltpu.VMEM((SSM_NGB, bsz * tt, SSM_GB_IN), BF16),
                        pltpu.VMEM((bsz * SSM_NGB * 2, SSM_CB // 2, LANES), F32)],
        compiler_params=_params(("arbitrary",), 48),
        name="s5_branch",
    )(proj, proj, proj, proj, bm, cm, ar, ai, d_skip, w_glu, b_glu)


def _t5_causal_bucket(dist):
    max_exact = N_BUCKETS // 2
    d32 = jnp.maximum(dist, 1).astype(F32)
    large = max_exact + (jnp.log(d32 / max_exact) / math.log(MAX_DISTANCE / max_exact)
                         * (N_BUCKETS - max_exact)).astype(jnp.int32)
    return jnp.where(dist < max_exact, dist, jnp.minimum(large, N_BUCKETS - 1))


def _bias_tile(rel_bias, group):
    window, dilation = DILATION_PATTERN[group]
    steps = window // dilation
    assert steps == Q_BLOCK
    heads = slice(group * HEADS_PER_GROUP, (group + 1) * HEADS_PER_GROUP)
    back = jnp.arange(steps, -1, -1, dtype=jnp.int32)
    vals = rel_bias[_t5_causal_bucket(back * dilation)][:, heads].astype(F32).T
    period = 3 * Q_BLOCK
    v = jnp.concatenate([vals, jnp.full((HEADS_PER_GROUP, period - steps - 1), NEG_INF, F32)], axis=1)
    flat = jnp.tile(v, (1, Q_BLOCK))[:, :Q_BLOCK * (period - 1)]
    tile = flat.reshape(HEADS_PER_GROUP, Q_BLOCK, period - 1)[:, :, :2 * Q_BLOCK]
    col = lax.broadcasted_iota(jnp.int32, tile.shape, 2)
    return jnp.concatenate([tile, jnp.where(col < Q_BLOCK, NEG_INF, tile)], axis=0)


LSE_LANES = LANES // HEADS_PER_GROUP


def _attn_kernel(q_ref, kc_ref, kp_ref, vc_ref, vp_ref, bias_ref, o_ref, lse_ref, kf_ref, vf_ref, *, tq):
    kf_ref[0:Q_BLOCK, :] = kp_ref[...]
    kf_ref[Q_BLOCK:, :] = kc_ref[...]
    vf_ref[0:Q_BLOCK, :] = vp_ref[...]
    vf_ref[Q_BLOCK:, :] = vc_ref[...]
    scale = HEAD_DIM ** -0.5
    first_tile = pl.program_id(2) == 0
    lane_head = lax.broadcasted_iota(jnp.int32, (Q_BLOCK, LANES), 1) // LSE_LANES

    def block(jb, _):
        r0 = pl.multiple_of(jb * Q_BLOCK, Q_BLOCK)
        bias_set = jnp.logical_and(first_tile, jb == 0).astype(jnp.int32) * HEADS_PER_GROUP
        lse = jnp.zeros((Q_BLOCK, LANES), F32)
        for h in range(HEADS_PER_GROUP):
            hs = slice(h * HEAD_DIM, (h + 1) * HEAD_DIM)
            q = q_ref[pl.ds(r0, Q_BLOCK), hs]
            k2 = kf_ref[pl.ds(r0, 2 * Q_BLOCK), hs]
            v2 = vf_ref[pl.ds(r0, 2 * Q_BLOCK), hs]
            s = lax.dot_general(q, k2, (((1,), (1,)), ((), ())), preferred_element_type=F32)
            s = s * scale + bias_ref[bias_set + h]
            m = jnp.max(s, axis=-1, keepdims=True)
            p = jnp.exp(s - m)
            l = jnp.sum(p, axis=-1, keepdims=True)
            o = jnp.dot(p.astype(BF16), v2, preferred_element_type=F32) / l
            o_ref[pl.ds(r0, Q_BLOCK), hs] = o.astype(o_ref.dtype)
            lse = jnp.where(lane_head == h, m + jnp.log(l), lse)
        lse_ref[pl.ds(r0, Q_BLOCK), :] = lse
        return 0

    lax.fori_loop(0, tq // Q_BLOCK, block, 0, unroll=min(4, tq // Q_BLOCK))


def _attention_group(qkv, bias, group, tile0=0):
    _, bsz, d, lc, _ = qkv.shape
    tq = min(lc, 1024)
    per_tq = tq // Q_BLOCK
    cur = lambda which: pl.BlockSpec((None, None, None, tq, PROJ_TILE),
                                     lambda b, r, i: (tile0 + which, b, r, i, 0))
    prev = lambda which: pl.BlockSpec((None, None, None, Q_BLOCK, PROJ_TILE),
                                      lambda b, r, i: (tile0 + which, b, r, jnp.maximum(i * per_tq - 1, 0), 0))
    out_spec = lambda width: pl.BlockSpec((None, None, tq, width), lambda b, r, i: (b, r, i, 0))
    return pl.pallas_call(
        functools.partial(_attn_kernel, tq=tq),
        grid=(bsz, d, lc // tq),
        in_specs=[cur(0), cur(1), prev(1), cur(2), prev(2),
                  pl.BlockSpec(bias.shape, lambda b, r, i: (0, 0, 0))],
        out_specs=[out_spec(ATT_OUT_WIDTH), out_spec(LANES)],
        out_shape=[jax.ShapeDtypeStruct((bsz, d, lc, ATT_OUT_WIDTH), BF16),
                   jax.ShapeDtypeStruct((bsz, d, lc, LANES), F32)],
        scratch_shapes=[pltpu.VMEM((Q_BLOCK + tq, PROJ_TILE), BF16),
                        pltpu.VMEM((Q_BLOCK + tq, PROJ_TILE), BF16)],
        compiler_params=_params(("arbitrary", "arbitrary", "arbitrary"), 40),
        name=f"attention_group{group}",
    )(qkv, qkv, qkv, qkv, qkv, bias)


def _merge_kernel(*refs, tm):
    ys_ref = refs[0]
    o_refs = refs[1:1 + N_ATT_GROUPS]
    l_refs = refs[1 + N_ATT_GROUPS:1 + 2 * N_ATT_GROUPS]
    k = 1 + 2 * N_ATT_GROUPS
    n_gate = (len(refs) - k - 7) // 2
    gs_refs = refs[k:k + n_gate]
    ga_refs = refs[k + n_gate:k + 2 * n_gate]
    x_ref, gate_ref, wbs_ref, wba_ref, wout_ref, out_ref, tok_ref = refs[k + 2 * n_gate:]

    def token_order(ref, g):
        dil = DILATION_PATTERN[g][1]
        if dil == 1:
            return ref[0].astype(F32)
        n_chunks = ref.shape[-1] // LANES
        for r in range(dil):
            for ch in range(n_chunks):
                tok_ref[ch, pl.ds(r, tm // dil, stride=dil), :] = (
                    ref[r, :, ch * LANES:(ch + 1) * LANES].astype(F32))
        return jnp.concatenate([tok_ref[ch] for ch in range(n_chunks)], axis=-1)

    lses = [token_order(r, g) for g, r in enumerate(l_refs)]
    m = functools.reduce(jnp.maximum, lses)
    es = [jnp.exp(l - m) for l in lses]
    den = functools.reduce(lambda a, b: a + b, es)

    def per_head(w):
        return jnp.concatenate([jnp.broadcast_to(w[:, h * LSE_LANES:h * LSE_LANES + 1], (tm, HEAD_DIM))
                                for h in range(HEADS_PER_GROUP)], axis=-1)

    y_att = functools.reduce(lambda a, b: a + b,
                             [per_head(e / den) * token_order(r, g) for g, (e, r) in enumerate(zip(es, o_refs))])

    m_ssm = jnp.dot(ys_ref[...], wbs_ref[...], preferred_element_type=F32)
    m_att = jnp.dot(y_att.astype(BF16), wba_ref[...], preferred_element_type=F32)
    g_ssm = jnp.concatenate([r[...] for r in gs_refs], axis=-1).astype(F32)
    g_att = jnp.concatenate([r[...] for r in ga_refs], axis=-1).astype(F32)
    merged = jax.nn.sigmoid(g_ssm) * m_ssm + jax.nn.sigmoid(g_att) * m_att
    mixed = jnp.dot(merged.astype(BF16), wout_ref[...], preferred_element_type=F32)
    out_ref[...] = x_ref[...] + gate_ref[...] * mixed


def _resident(a, layer=None):
    nd = a.ndim
    if layer is None:
        return pl.BlockSpec(a.shape, lambda *_: (0,) * nd, pipeline_mode=pl.Buffered(1))
    return pl.BlockSpec((None,) + a.shape[1:], lambda *_: (layer,) + (0,) * (nd - 1), pipeline_mode=pl.Buffered(1))


def _merge(x, gate, y_ssm, att, main, w_branch_ssm, w_branch_att, w_out, tm=512):
    bsz, seqlen, d = x.shape
    n_gate = d // PROJ_TILE
    row = lambda width: pl.BlockSpec((None, tm, width), lambda b, i: (b, i, 0))
    tile = lambda t: pl.BlockSpec((None, None, tm, PROJ_TILE), lambda b, i: (t, b, i, 0))
    res_major = lambda dil, width: pl.BlockSpec((None, dil, tm // dil, width), lambda b, i: (b, 0, i, 0))
    wbs, wba, wout = (w.astype(BF16) for w in (w_branch_ssm, w_branch_att, w_out))
    os_, ls_ = zip(*att)
    in_specs = ([row(SSM_WIDTH)]
                + [res_major(dil, ATT_OUT_WIDTH) for _, dil in DILATION_PATTERN]
                + [res_major(dil, LANES) for _, dil in DILATION_PATTERN]
                + [tile(MAIN_GATE0 + t) for t in range(2 * n_gate)]
                + [row(d), pl.BlockSpec((None, 1, d), lambda b, i: (b, 0, 0)),
                   _resident(wbs), _resident(wba), _resident(wout)])
    return pl.pallas_call(
        functools.partial(_merge_kernel, tm=tm),
        grid=(bsz, seqlen // tm),
        in_specs=in_specs,
        out_specs=row(d),
        out_shape=jax.ShapeDtypeStruct(x.shape, F32),
        scratch_shapes=[pltpu.VMEM((ATT_OUT_WIDTH // LANES, tm, LANES), F32)],
        compiler_params=_params(("arbitrary", "arbitrary"), 56),
        name="merge",
    )(y_ssm, *os_, *ls_, *([main] * (2 * n_gate)), x, gate[:, None, :], wbs, wba, wout)


def _ffn_kernel(x_ref, g_ref, shift_ref, scale_ref, gate_ref, wg_ref, wu_ref, wd_ref,
                o_ref, h_ref, inv_ref):
    f = pl.program_id(2)

    @pl.when(f == 0)
    def _():
        _norm_modulate_into(h_ref, x_ref, g_ref, shift_ref, scale_ref, inv_ref)
        o_ref[...] = jnp.zeros_like(o_ref)

    h = h_ref[...]
    a = jnp.dot(h, wg_ref[...].astype(BF16), preferred_element_type=F32)
    b = jnp.dot(h, wu_ref[...].astype(BF16), preferred_element_type=F32)
    act = (a * jax.nn.sigmoid(a)) * b
    o_ref[...] += jnp.dot(act.astype(BF16), wd_ref[...].astype(BF16), preferred_element_type=F32)

    @pl.when(f == pl.num_programs(2) - 1)
    def _():
        o_ref[...] = x_ref[...] + gate_ref[...] * o_ref[...]


def _dense_ffn(x, g, shift, scale, gate, w_gate, w_up, w_down, tm=1024, tf=512):
    bsz, seqlen, d = x.shape
    dff = w_gate.shape[1]
    vec = pl.BlockSpec((None, 1, d), lambda b, i, f: (b, 0, 0))
    row = pl.BlockSpec((None, tm, d), lambda b, i, f: (b, i, 0), pipeline_mode=pl.Buffered(1))
    return pl.pallas_call(
        _ffn_kernel,
        grid=(bsz, seqlen // tm, dff // tf),
        in_specs=[pl.BlockSpec((None, tm, d), lambda b, i, f: (b, i, 0)),
                  pl.BlockSpec((1, d), lambda b, i, f: (0, 0)),
                  vec, vec, vec,
                  pl.BlockSpec((d, tf), lambda b, i, f: (0, f)),
                  pl.BlockSpec((d, tf), lambda b, i, f: (0, f)),
                  pl.BlockSpec((tf, d), lambda b, i, f: (f, 0))],
        out_specs=row,
        out_shape=jax.ShapeDtypeStruct(x.shape, F32),
        scratch_shapes=[pltpu.VMEM((tm, d), BF16), pltpu.VMEM((tm, LANES), F32)],
        compiler_params=_params(("arbitrary", "arbitrary", "arbitrary"), 56),
        name="dense_ffn",
    )(x, g.reshape(1, d), shift[:, None, :], scale[:, None, :], gate[:, None, :],
      w_gate.astype(BF16), w_up.astype(BF16), w_down.astype(BF16))


MOE_SUB = 256
SUBS_PER_TILE = 4
MOE_SUPER = SUBS_PER_TILE * MOE_SUB


def _pack_halves(x):
    half = x.shape[-1] // 2
    bits = lambda v: lax.bitcast_convert_type(v.astype(BF16).astype(F32), jnp.uint32)
    return bits(x[:, half:]) | (bits(x[:, :half]) >> 16)


def _unpack_halves(w):
    lo = lax.bitcast_convert_type(w << 16, F32)
    hi = lax.bitcast_convert_type(w & jnp.uint32(0xFFFF0000), F32)
    return lo, hi


def _router_kernel(x_ref, g_ref, shift_ref, scale_ref, w2_ref, h_ref, idx_ref, gate_ref, hf_ref, inv_ref):
    _norm_modulate_into(hf_ref, x_ref, g_ref, shift_ref, scale_ref, inv_ref)
    h = hf_ref[...]
    h_ref[...] = _pack_halves(h)
    h_hi = h.astype(BF16)
    h_lo = (h - h_hi.astype(F32)).astype(BF16)
    prod = (jnp.dot(h_hi, w2_ref[...], preferred_element_type=F32)
            + jnp.dot(h_lo, w2_ref[...], preferred_element_type=F32))
    logits = prod[:, :LANES] + prod[:, LANES:]
    lane_i = lax.broadcasted_iota(jnp.int32, logits.shape, 1)
    lane = lane_i.astype(F32)
    logits = jnp.where(lane_i < N_EXPERTS, logits, -jnp.inf)
    m1 = jnp.max(logits, axis=-1, keepdims=True)
    i1 = jnp.min(jnp.where(logits == m1, lane, float(LANES)), axis=-1, keepdims=True)
    rest = jnp.where(lane == i1, -jnp.inf, logits)
    m2 = jnp.max(rest, axis=-1, keepdims=True)
    i2 = jnp.min(jnp.where(rest == m2, lane, float(LANES)), axis=-1, keepdims=True)
    e2 = jnp.exp(m2 - m1)
    den = 1.0 + e2
    idx_ref[...] = jnp.where(lane_i == 0, i1, jnp.where(lane_i == 1, i2, 0.0)).astype(jnp.int32)
    gate_ref[...] = jnp.where(lane_i == 0, 1.0 / den, jnp.where(lane_i == 1, e2 / den, 0.0))


def _router(x, g, shift, scale, w_router, tm=512):
    bsz, seqlen, d = x.shape
    wr = jnp.zeros((d, LANES), F32).at[:, :N_EXPERTS].set(w_router)
    w_hi = wr.astype(BF16)
    w_lo = (wr - w_hi.astype(F32)).astype(BF16)
    w2 = jnp.concatenate([w_hi, w_lo], axis=1)
    row = lambda width: pl.BlockSpec((None, tm, width), lambda b, i: (b, i, 0))
    vec = pl.BlockSpec((None, 1, d), lambda b, i: (b, 0, 0))
    return pl.pallas_call(
        _router_kernel,
        grid=(bsz, seqlen // tm),
        in_specs=[row(d), pl.BlockSpec((1, d), lambda b, i: (0, 0)), vec, vec,
                  pl.BlockSpec((d, 2 * LANES), lambda b, i: (0, 0))],
        out_specs=[row(d // 2), row(LANES), row(LANES)],
        out_shape=[jax.ShapeDtypeStruct((bsz, seqlen, d // 2), jnp.uint32),
                   jax.ShapeDtypeStruct((bsz, seqlen, LANES), jnp.int32),
                   jax.ShapeDtypeStruct((bsz, seqlen, LANES), F32)],
        scratch_shapes=[pltpu.VMEM((tm, d), F32), pltpu.VMEM((tm, LANES), F32)],
        compiler_params=_params(("arbitrary", "arbitrary"), 40),
        name="moe_router",
    )(x, g.reshape(1, d), shift[:, None, :], scale[:, None, :], w2)


def _moe_dims(n_tok, nf):
    n_sub = n_tok * TOP_K // MOE_SUB + N_EXPERTS
    n_super = (n_sub + (SUBS_PER_TILE - 1) * N_EXPERTS) // SUBS_PER_TILE + 1
    rows_per_step = -(-MOE_SUPER // nf)
    while (rows_per_step * nf) % SUBLANES:
        rows_per_step += 1
    n_fetch = rows_per_step * nf
    return n_sub, n_super, rows_per_step, n_fetch


def _routing_tables(top_expert, n_tok, nf):
    n_sub, n_super, _, n_fetch = _moe_dims(n_tok, nf)
    i32 = jnp.int32
    flat_e = top_expert.reshape(-1)
    onehot = (flat_e[:, None] == jnp.arange(N_EXPERTS, dtype=i32)[None, :]).astype(i32)
    csum = jnp.cumsum(onehot, axis=0)
    rank = jnp.sum(csum * onehot, axis=1) - 1
    counts = csum[-1]
    subs = (counts + MOE_SUB - 1) // MOE_SUB
    pend = jnp.cumsum(subs) * MOE_SUB
    pstart = pend - subs * MOE_SUB
    dest = (pstart[flat_e] + rank).astype(i32)
    flat_token = jnp.arange(n_tok * TOP_K, dtype=i32) // TOP_K
    row_token = jnp.zeros((n_sub * MOE_SUB + n_fetch,), i32).at[dest].set(flat_token)
    supers = (subs + SUBS_PER_TILE - 1) // SUBS_PER_TILE
    send = jnp.cumsum(supers)
    sstart = send - supers
    s = jnp.arange(n_super + 1, dtype=i32)
    e = jnp.minimum(jnp.searchsorted(send, s, side='right'), N_EXPERTS - 1).astype(i32)
    local = s - sstart[e]
    used = s < send[-1]
    tile_expert = jnp.where(used, e, e[jnp.maximum(send[-1] - 1, 0)]).astype(i32)
    n_tiles_e = jnp.maximum(supers[e], 1)
    base, rem = subs[e] // n_tiles_e, subs[e] % n_tiles_e
    first_sub = local * base + jnp.minimum(local, rem)
    tile_row0 = jnp.where(used, pstart[e] + first_sub * MOE_SUB, 0).astype(i32)
    tile_nsub = jnp.where(used, base + (local < rem).astype(i32), 0).astype(i32)
    n_used = send[-1].astype(i32).reshape(1)
    n_sub_used = (pend[-1] // MOE_SUB).astype(i32).reshape(1)
    return dest, row_token, tile_expert, tile_row0, tile_nsub, n_used, n_sub_used


def _moe_kernel(te_ref, row0_ref, nsub_ref, nu_ref, nsu_ref, tok_ref, hp_ref, wg_ref, wu_ref, wd_ref, y_ref,
                xg_ref, xb_ref, acc_ref, yb_ref, gsem, osem, fsem, *, nf, rows_per_step, n_sub_alloc, n_fill):
    s, f = pl.program_id(0), pl.program_id(1)
    n_used = nu_ref[0]
    used = s < n_used
    slot = s % 2
    n_fetch = rows_per_step * nf
    half = xg_ref.shape[-1]

    def row_copy(tile, slot_, r):
        tok = tok_ref[row0_ref[tile] + r]
        return pltpu.make_async_copy(hp_ref.at[pl.ds(tok, 1)], xg_ref.at[slot_, pl.ds(r, 1)], gsem.at[slot_])

    def wait_rows(slot_):
        pltpu.make_async_copy(hp_ref.at[pl.ds(0, n_fetch)], xg_ref.at[slot_], gsem.at[slot_]).wait()

    def out_copies(tile):
        r0 = pl.multiple_of(row0_ref[tile], MOE_SUB)
        return [pltpu.make_async_copy(yb_ref.at[pl.ds(k * MOE_SUB, MOE_SUB)],
                                      y_ref.at[pl.ds(r0 + k * MOE_SUB, MOE_SUB)], osem.at[k])
                for k in range(SUBS_PER_TILE)]

    def start_out(tile):
        for k, cp in enumerate(out_copies(tile)):
            pl.when(k < nsub_ref[tile])(cp.start)

    def wait_out(tile):
        for k, cp in enumerate(out_copies(tile)):
            pl.when(k < nsub_ref[tile])(cp.wait)

    @pl.when(jnp.logical_and(s == 0, f == 0))
    def _():
        def body(r, _):
            row_copy(0, 0, r).start()
            return 0
        lax.fori_loop(0, n_fetch, body, 0)

    @pl.when(jnp.logical_and(f == 0, s <= n_used))
    def _():
        wait_rows(slot)

    @pl.when(jnp.logical_and(used, f == 0))
    def _():
        lo, hi = _unpack_halves(xg_ref[slot, 0:MOE_SUPER, :])
        xb_ref[:, :half] = lo.astype(BF16)
        xb_ref[:, half:] = hi.astype(BF16)
        acc_ref[...] = jnp.zeros_like(acc_ref)

    def step(n_rows):
        for k in range(rows_per_step):
            row_copy(s + 1, 1 - slot, f * rows_per_step + k).start()
        h = xb_ref[0:n_rows, :]
        a = jnp.dot(h, wg_ref[...].astype(BF16), preferred_element_type=F32)
        b = jnp.dot(h, wu_ref[...].astype(BF16), preferred_element_type=F32)
        act = (a * jax.nn.sigmoid(a)) * b
        acc_ref[0:n_rows, :] += jnp.dot(act.astype(BF16), wd_ref[...].astype(BF16), preferred_element_type=F32)

    for n in range(1, SUBS_PER_TILE + 1):
        pl.when(jnp.logical_and(used, nsub_ref[s] == n))(functools.partial(step, n * MOE_SUB))

    @pl.when(jnp.logical_and(used, f == nf - 1))
    def _():
        pl.when(s > 0)(functools.partial(wait_out, s - 1))
        yb_ref[...] = _pack_halves(acc_ref[...])
        start_out(s)

    @pl.when(jnp.logical_and(s == n_used, f == 0))
    def _():
        wait_out(s - 1)
        yb_ref[0:MOE_SUB, :] = jnp.zeros((MOE_SUB, half), yb_ref.dtype)
        fills = []
        for k in range(n_fill):
            sub = nsu_ref[0] + k
            cp = pltpu.make_async_copy(yb_ref.at[pl.ds(0, MOE_SUB)],
                                       y_ref.at[pl.ds(pl.multiple_of(sub * MOE_SUB, MOE_SUB), MOE_SUB)], fsem.at[k])
            fills.append((sub < n_sub_alloc, cp))
        for cond, cp in fills:
            pl.when(cond)(cp.start)
        for cond, cp in fills:
            pl.when(cond)(cp.wait)


def _moe_experts(hp, tables, w_gate, w_up, w_down, tf=512):
    n_tok, half = hp.shape
    d = 2 * half
    dff = w_gate.shape[2]
    nf = dff // tf
    n_sub, n_super, rows_per_step, n_fetch = _moe_dims(n_tok, nf)
    _, row_token, tile_expert, tile_row0, tile_nsub, n_used, n_sub_used = tables
    n_unused_max = n_sub - n_tok * TOP_K // MOE_SUB
    tile_f = lambda s, f, nu: jnp.where(s < nu[0], f, nf - 1)
    w_in_spec = pl.BlockSpec((None, d, tf), lambda s, f, te, r0, ns, nu, nsu, tok: (te[s], 0, tile_f(s, f, nu)))
    w_out_spec = pl.BlockSpec((None, tf, d), lambda s, f, te, r0, ns, nu, nsu, tok: (te[s], tile_f(s, f, nu), 0))
    return pl.pallas_call(
        functools.partial(_moe_kernel, nf=nf, rows_per_step=rows_per_step, n_sub_alloc=n_sub,
                          n_fill=n_unused_max),
        grid_spec=pltpu.PrefetchScalarGridSpec(
            num_scalar_prefetch=6,
            grid=(n_super, nf),
            in_specs=[pl.BlockSpec(memory_space=pl.ANY), w_in_spec, w_in_spec, w_out_spec],
            out_specs=pl.BlockSpec(memory_space=pl.ANY),
            scratch_shapes=[pltpu.VMEM((2, n_fetch, half), jnp.uint32),
                            pltpu.VMEM((MOE_SUPER, d), BF16),
                            pltpu.VMEM((MOE_SUPER, d), F32),
                            pltpu.VMEM((MOE_SUPER, half), jnp.uint32),
                            pltpu.SemaphoreType.DMA((2,)), pltpu.SemaphoreType.DMA((SUBS_PER_TILE,)),
                            pltpu.SemaphoreType.DMA((n_unused_max,))]),
        out_shape=jax.ShapeDtypeStruct((n_sub * MOE_SUB, half), jnp.uint32),
        compiler_params=_params(("arbitrary", "arbitrary"), 58),
        name="moe_experts",
    )(tile_expert, tile_row0, tile_nsub, n_used, n_sub_used, row_token, hp, w_gate, w_up, w_down)


COMBINE_UNROLL = 8


def _combine_kernel(dest_ref, y_ref, x_ref, gates_ref, gate_f_ref, gn_ref, o_ref, rows_ref, sems,
                    *, tm, n_steps, final_norm):
    t = pl.program_id(0)
    slot = t % 2
    n = tm * TOP_K
    half = rows_ref.shape[-1]

    def row_copy(base, j, row, slot_):
        return pltpu.make_async_copy(y_ref.at[pl.ds(dest_ref[base + j], 1)],
                                     rows_ref.at[slot_, pl.ds(row, 1)], sems.at[slot_])

    def wait_rows(slot_):
        pltpu.make_async_copy(y_ref.at[pl.ds(0, n)], rows_ref.at[slot_], sems.at[slot_]).wait()

    @pl.when(t == 0)
    def _():
        def body(i, _):
            for u in range(COMBINE_UNROLL):
                row = (u % TOP_K) * tm + i * (COMBINE_UNROLL // TOP_K) + u // TOP_K
                row_copy(0, i * COMBINE_UNROLL + u, row, 0).start()
            return 0
        lax.fori_loop(0, n // COMBINE_UNROLL, body, 0)

    wait_rows(slot)
    next_base = jnp.minimum(t + 1, n_steps - 1) * n
    for j in range(n):
        row_copy(next_base, j, (j % TOP_K) * tm + j // TOP_K, 1 - slot).start()

    gates = gates_ref[...]
    g0, g1 = gates[:, 0:1], gates[:, 1:2]
    lo0, hi0 = _unpack_halves(rows_ref[slot, 0:tm, :])
    lo1, hi1 = _unpack_halves(rows_ref[slot, tm:2 * tm, :])
    out_lo = x_ref[:, :half] + gate_f_ref[:, :half] * (g0 * lo0 + g1 * lo1)
    out_hi = x_ref[:, half:] + gate_f_ref[:, half:] * (g0 * hi0 + g1 * hi1)
    if final_norm:
        ssq = jnp.sum(out_lo * out_lo, axis=-1, keepdims=True) + jnp.sum(out_hi * out_hi, axis=-1, keepdims=True)
        inv = lax.rsqrt(ssq / (2 * half) + EPS)
        out_lo = (out_lo * inv) * gn_ref[:, :half]
        out_hi = (out_hi * inv) * gn_ref[:, half:]
    o_ref[:, :half] = out_lo
    o_ref[:, half:] = out_hi
    pl.when(t == n_steps - 1)(functools.partial(wait_rows, 1 - slot))


def _combine(x, y_rows, dest, top_gate, gate_f, final_g, tm=256):
    bsz, seqlen, d = x.shape
    final_norm = final_g is not None
    gn = (final_g if final_norm else jnp.ones((d,), F32)).reshape(1, d)
    per_batch = seqlen // tm
    n_steps = bsz * per_batch
    row = lambda width: pl.BlockSpec((None, tm, width), lambda t, dst: (t // per_batch, t % per_batch, 0))
    return pl.pallas_call(
        functools.partial(_combine_kernel, tm=tm, n_steps=n_steps, final_norm=final_norm),
        grid_spec=pltpu.PrefetchScalarGridSpec(
            num_scalar_prefetch=1,
            grid=(n_steps,),
            in_specs=[pl.BlockSpec(memory_space=pl.ANY), row(d), row(LANES),
                      pl.BlockSpec((None, 1, d), lambda t, dst: (t // per_batch, 0, 0)),
                      pl.BlockSpec((1, d), lambda t, dst: (0, 0))],
            out_specs=row(d),
            scratch_shapes=[pltpu.VMEM((2, TOP_K * tm, d // 2), jnp.uint32), pltpu.SemaphoreType.DMA((2,))]),
        out_shape=jax.ShapeDtypeStruct(x.shape, F32),
        compiler_params=_params(("arbitrary",), 32),
        name="moe_combine",
    )(dest, y_rows, x, top_gate, gate_f[:, None, :], gn)


def _moe_ffn(x, g, shift, scale, gate_f, w_router, w_gate, w_up, w_down, final_g, tf=512):
    bsz, seqlen, d = x.shape
    n_tok = bsz * seqlen
    hp, top_idx, top_gate = _router(x, g, shift, scale, w_router)
    tables = _routing_tables(top_idx[..., :TOP_K], n_tok, w_gate.shape[2] // tf)
    y_rows = _moe_experts(hp.reshape(n_tok, d // 2), tables, w_gate, w_up, w_down, tf=tf)
    return _combine(x, y_rows, tables[0], top_gate, gate_f, final_g)


def _final_norm_kernel(x_ref, g_ref, o_ref):
    x = x_ref[...]
    ms = jnp.mean(x * x, axis=-1, keepdims=True)
    o_ref[...] = (x * lax.rsqrt(ms + EPS)) * g_ref[...]


def _final_norm(x, g, tm=512):
    bsz, seqlen, d = x.shape
    return pl.pallas_call(
        _final_norm_kernel,
        grid=(bsz, seqlen // tm),
        in_specs=[pl.BlockSpec((None, tm, d), lambda b, i: (b, i, 0)),
                  pl.BlockSpec((1, d), lambda b, i: (0, 0))],
        out_specs=pl.BlockSpec((None, tm, d), lambda b, i: (b, i, 0)),
        out_shape=jax.ShapeDtypeStruct(x.shape, F32),
        compiler_params=_params(("arbitrary", "arbitrary"), 32),
        name="final_norm",
    )(x, g.reshape(1, d))


def kernel(x, c, w_mod, b_mod, norm_mix_g, norm_ffn_g, w_in, ssm_a_re, ssm_a_im, ssm_log_dt, ssm_b_re, ssm_b_im, ssm_c_re, ssm_c_im, ssm_d, w_glu, b_glu, rel_bias, w_branch_ssm, w_branch_att, w_out, ffn_w_gate, ffn_w_up, ffn_w_down, moe_router, moe_w_gate, moe_w_up, moe_w_down, final_norm_g):
    depth = w_mod.shape[0]
    mod = _modulation(c, w_mod, b_mod)
    biases = [_bias_tile(rel_bias, g) for g in range(N_ATT_GROUPS)]
    tables = _s5_tables(ssm_a_re, ssm_a_im, ssm_log_dt, ssm_b_re, ssm_b_im, ssm_c_re, ssm_c_im)
    for i in range(depth):
        shift_m, scale_m, gate_m, shift_f, scale_f, gate_f = jnp.split(mod[i], N_MOD, axis=-1)
        main, qkv = _in_projection(x, norm_mix_g[i], shift_m, scale_m, w_in, i)
        y_ssm = _s5_branch(main, tables, i, ssm_d[i], w_glu[i], b_glu[i])
        att = [_attention_group(qkv[g], biases[g], g) for g in range(N_ATT_GROUPS)]
        x = _merge(x, gate_m, y_ssm, att, main, w_branch_ssm[i], w_branch_att[i], w_out[i])
        j = i // 2
        last = i == depth - 1
        if i % 2 == 0:
            x = _dense_ffn(x, norm_ffn_g[i], shift_f, scale_f, gate_f,
                           ffn_w_gate[j], ffn_w_up[j], ffn_w_down[j])
            if last:
                x = _final_norm(x, final_norm_g)
        else:
            x = _moe_ffn(x, norm_ffn_g[i], shift_f, scale_f, gate_f, moe_router[j],
                         moe_w_gate[j], moe_w_up[j], moe_w_down[j], final_norm_g if last else None)
    return x
```

```python
import functools
import math

import jax
import jax.numpy as jnp
from jax import lax
from jax.experimental import pallas as pl
from jax.experimental.pallas import tpu as pltpu

F32 = jnp.float32
BF16 = jnp.bfloat16

LANES = 128
SUBLANES = 8
VMEM_BYTES = 64 * 1024 * 1024

SSM_GROUP = 16
SSM_STATE = 64
SSM_WIDTH = 1024
HEAD_DIM = 128
DILATION_PATTERN = ((128, 1), (512, 4), (2048, 16))
HEADS_PER_GROUP = 4
N_ATT_GROUPS = len(DILATION_PATTERN)
ATT_OUT_WIDTH = HEADS_PER_GROUP * HEAD_DIM
Q_BLOCK = 128
NEG_INF = -1e30
N_BUCKETS = 32
MAX_DISTANCE = 2048
N_EXPERTS = 8
TOP_K = 2
N_MOD = 6
EPS = 1e-6

PROJ_TILE = 512
U_TILE0 = 0
Q_TILE0 = SSM_WIDTH // PROJ_TILE
K_TILE0 = Q_TILE0 + N_ATT_GROUPS
V_TILE0 = K_TILE0 + N_ATT_GROUPS
GS_TILE0 = V_TILE0 + N_ATT_GROUPS
MAIN_GATE0 = Q_TILE0


def _params(dims, vmem_mb):
    return pltpu.CompilerParams(dimension_semantics=dims,
                                vmem_limit_bytes=vmem_mb * 1024 * 1024)


def _norm_modulate(x, g, shift, scale):
    ms = jnp.mean(x * x, axis=-1, keepdims=True)
    y = x * lax.rsqrt(ms + EPS)
    return (y * g) * (1.0 + scale) + shift


NORM_ROWS = 64


def _norm_modulate_into(h_ref, x_ref, g_ref, shift_ref, scale_ref, inv_ref):
    n_blocks = x_ref.shape[0] // NORM_ROWS
    reps = x_ref.shape[1] // LANES

    def block(i):
        return pl.ds(pl.multiple_of(i * NORM_ROWS, NORM_ROWS), NORM_ROWS)

    def rms(i, _):
        x = x_ref[block(i), :]
        inv = lax.rsqrt(jnp.mean(x * x, axis=-1, keepdims=True) + EPS)
        inv_ref[block(i), :] = jnp.broadcast_to(inv, (NORM_ROWS, LANES))
        return 0

    gain = g_ref[...] * (1.0 + scale_ref[...])
    shift = shift_ref[...]

    def scale_rows(i, _):
        y = x_ref[block(i), :] * jnp.concatenate([inv_ref[block(i), :]] * reps, axis=-1)
        h_ref[block(i), :] = (y * gain + shift).astype(h_ref.dtype)
        return 0

    lax.fori_loop(0, n_blocks, rms, 0, unroll=4)
    lax.fori_loop(0, n_blocks, scale_rows, 0)


def _mod_kernel(c_ref, w_ref, b_ref, o_ref):
    c = c_ref[...]
    cond = (c * jax.nn.sigmoid(c)).astype(BF16)
    o_ref[...] = jnp.dot(cond, w_ref[...].astype(BF16),
                         preferred_element_type=F32) + b_ref[...]


def _modulation(c, w_mod, b_mod):
    depth, d, n = w_mod.shape
    bsz = c.shape[0]
    rows = SUBLANES
    c_pad = jnp.zeros((rows, d), F32).at[:bsz].set(c)
    tn = 1536
    out = pl.pallas_call(
        _mod_kernel,
        grid=(depth, n // tn),
        in_specs=[pl.BlockSpec((rows, d), lambda l, j: (0, 0)),
                  pl.BlockSpec((None, d, tn), lambda l, j: (l, 0, j)),
                  pl.BlockSpec((None, 1, tn), lambda l, j: (l, 0, j))],
        out_specs=pl.BlockSpec((None, rows, tn), lambda l, j: (l, 0, j)),
        out_shape=jax.ShapeDtypeStruct((depth, rows, n), F32),
        compiler_params=_params(("arbitrary", "arbitrary"), 40),
        name="modulation",
    )(c_pad, w_mod, b_mod.reshape(depth, 1, n))
    return out[:, :bsz]


def _inproj_kernel(x_ref, g_ref, shift_ref, scale_ref, w_ref, main_ref, *rest, tm):
    qkv_refs, (h_ref, res_ref, inv_ref) = rest[:N_ATT_GROUPS], rest[N_ATT_GROUPS:]
    j = pl.program_id(2)

    @pl.when(j == 0)
    def _():
        _norm_modulate_into(h_ref, x_ref, g_ref, shift_ref, scale_ref, inv_ref)

    def project():
        return jnp.dot(h_ref[...], w_ref[...].astype(BF16), preferred_element_type=F32)

    @pl.when(jnp.logical_or(j < Q_TILE0, j >= GS_TILE0))
    def _():
        main_ref[...] = project().astype(BF16)

    for g, (_, dil) in enumerate(DILATION_PATTERN):
        is_g = functools.reduce(jnp.logical_or, [j == t0 + g for t0 in (Q_TILE0, K_TILE0, V_TILE0)])

        @pl.when(is_g)
        def _(g=g, dil=dil):
            res = project()
            if dil == 1:
                qkv_refs[g][0] = res.astype(BF16)
            else:
                n_chunks = PROJ_TILE // LANES
                for ch in range(n_chunks):
                    res_ref[ch] = res[:, ch * LANES:(ch + 1) * LANES]
                for r in range(dil):
                    rows = [res_ref[ch, pl.ds(r, tm // dil, stride=dil), :] for ch in range(n_chunks)]
                    qkv_refs[g][r] = jnp.concatenate(rows, axis=-1).astype(BF16)


def _in_projection(x, g, shift, scale, w_in, layer, tm=1024):
    bsz, seqlen, d = x.shape
    n_tiles = w_in.shape[2] // PROJ_TILE
    n_main = n_tiles - 3 * N_ATT_GROUPS

    def main_map(b, i, j):
        return (jnp.where(j < Q_TILE0, j, jnp.where(j < GS_TILE0, Q_TILE0 - 1, j - 3 * N_ATT_GROUPS)), b, i, 0)

    def qkv_map(g):
        return lambda b, i, j: ((j > Q_TILE0 + g).astype(jnp.int32) + (j > K_TILE0 + g).astype(jnp.int32),
                                b, 0, i, 0)

    qkv_specs = [pl.BlockSpec((None, None, dil, tm // dil, PROJ_TILE), qkv_map(g))
                 for g, (_, dil) in enumerate(DILATION_PATTERN)]
    qkv_shapes = [jax.ShapeDtypeStruct((3, bsz, dil, seqlen // dil, PROJ_TILE), BF16)
                  for _, dil in DILATION_PATTERN]
    outs = pl.pallas_call(
        functools.partial(_inproj_kernel, tm=tm),
        grid=(bsz, seqlen // tm, n_tiles),
        in_specs=[pl.BlockSpec((None, tm, d), lambda b, i, j: (b, i, 0)),
                  pl.BlockSpec((1, d), lambda b, i, j: (0, 0)),
                  pl.BlockSpec((None, 1, d), lambda b, i, j: (b, 0, 0)),
                  pl.BlockSpec((None, 1, d), lambda b, i, j: (b, 0, 0)),
                  pl.BlockSpec((None, d, PROJ_TILE), lambda b, i, j: (layer, 0, j))],
        out_specs=[pl.BlockSpec((None, None, tm, PROJ_TILE), main_map)] + qkv_specs,
        out_shape=[jax.ShapeDtypeStruct((n_main, bsz, seqlen, PROJ_TILE), BF16)] + qkv_shapes,
        scratch_shapes=[pltpu.VMEM((tm, d), BF16), pltpu.VMEM((PROJ_TILE // LANES, tm, LANES), F32),
                        pltpu.VMEM((tm, LANES), F32)],
        compiler_params=_params(("arbitrary", "arbitrary", "arbitrary"), 52),
        name="in_projection",
    )(x, g.reshape(1, d), shift[:, None, :], scale[:, None, :], w_in)
    return outs[0], outs[1:]


SSM_GB = 16
SSM_NGB = (SSM_WIDTH // SSM_GROUP) // SSM_GB
SSM_GB_IN = SSM_GB * SSM_GROUP
SSM_GB_RE = SSM_GB * SSM_STATE
SSM_CB = 2 * SSM_GB_RE // LANES
SSM_SLOTS = SSM_NGB * SSM_CB + SUBLANES


def _s5_kernel(ua0_ref, ua1_ref, un0_ref, un1_ref, bm_ref, cm_ref, ar_ref, ai_ref, d_ref, wglu_ref, bglu_ref,
               o_ref, xs0_ref, xs1_ref, ug_ref, st_ref, *, bsz, tt):
    xs_ref = (xs0_ref, xs1_ref)
    rows = bsz * tt
    n_nt = bm_ref.shape[1]
    n_pieces = SSM_NGB * n_nt
    unroll = 4
    assert tt == unroll * n_pieces
    half_cb = SSM_CB // 2

    def pair_halves(c):
        return tuple(r[:, c * tt:(c + 1) * tt, :].reshape(rows, PROJ_TILE) for r in (ua0_ref, ua1_ref))

    def stage_u(halves):
        for gb in range(SSM_NGB):
            half, off = divmod(gb * SSM_GB_IN, PROJ_TILE)
            ug_ref[gb] = halves[half][:, off:off + SSM_GB_IN]

    def b_piece(i, dst):
        gb, nt = i // n_nt, i % n_nt
        res = jnp.dot(ug_ref[gb], bm_ref[gb, nt], preferred_element_type=F32)
        for j in range(2):
            xs_ref[dst][pl.ds(gb * SSM_CB + 2 * nt + j, rows, stride=SSM_SLOTS), :] = (
                res[:, j * LANES:(j + 1) * LANES])

    a_re = [ar_ref[gb] for gb in range(SSM_NGB)]
    a_im = [ai_ref[gb] for gb in range(SSM_NGB)]

    def time_step(t, carry, src):
        new = []
        for b in range(bsz):
            for gb in range(SSM_NGB):
                k = (b * SSM_NGB + gb) * 2
                s_re, s_im = carry[k], carry[k + 1]
                row = (b * tt + t) * SSM_SLOTS + gb * SSM_CB
                n_re = a_re[gb] * s_re - a_im[gb] * s_im + xs_ref[src][pl.ds(row, half_cb), :]
                n_im = a_re[gb] * s_im + a_im[gb] * s_re + xs_ref[src][pl.ds(row + half_cb, half_cb), :]
                xs_ref[src][pl.ds(row, half_cb), :] = n_re
                xs_ref[src][pl.ds(row + half_cb, half_cb), :] = n_im
                new += [n_re, n_im]
        return tuple(new)

    n_carry = bsz * SSM_NGB * 2

    def scan_chunk(src):
        carry = tuple(st_ref[k] for k in range(n_carry))
        for it in range(n_pieces):
            b_piece(it, 1 - src)
            for k in range(unroll):
                carry = time_step(it * unroll + k, carry, src)
        for k in range(n_carry):
            st_ref[k] = carry[k]

    def finish_chunk(c):
        ys = []
        for gb in range(SSM_NGB):
            cols = [xs_ref[c][pl.ds(gb * SSM_CB + cb, rows, stride=SSM_SLOTS), :] for cb in range(SSM_CB)]
            xg = jnp.concatenate(cols, axis=-1).astype(BF16)
            ys.append(jnp.dot(xg, cm_ref[gb], preferred_element_type=F32))
        y = jnp.concatenate(ys, axis=-1)
        u32 = jnp.concatenate(pair_halves(c), axis=-1).astype(F32)
        y = jax.nn.gelu(y + d_ref[...] * u32)
        z = jnp.dot(y.astype(BF16), wglu_ref[...], preferred_element_type=F32) + bglu_ref[...]
        o_ref[:, c * tt:(c + 1) * tt, :] = (y * jax.nn.sigmoid(z)).astype(o_ref.dtype).reshape(bsz, tt, SSM_WIDTH)

    @pl.when(pl.program_id(0) == 0)
    def _():
        st_ref[...] = jnp.zeros_like(st_ref)
        stage_u(pair_halves(0))
        for i in range(n_pieces):
            b_piece(i, 0)

    stage_u(pair_halves(1))
    scan_chunk(0)
    finish_chunk(0)
    stage_u(tuple(r[...].reshape(rows, PROJ_TILE) for r in (un0_ref, un1_ref)))
    scan_chunk(1)
    finish_chunk(1)


def _block_diag(m):
    ngb, rows, c = m.shape
    r = rows // SSM_GB
    row_group = lax.broadcasted_iota(jnp.int32, (rows, SSM_GB * c), 0) // r
    col_group = lax.broadcasted_iota(jnp.int32, (rows, SSM_GB * c), 1) // c
    return jnp.where(row_group == col_group, jnp.tile(m, (1, 1, SSM_GB)), 0.0)


def _s5_tables(a_re, a_im, log_dt, b_re, b_im, c_re, c_im):
    depth = a_re.shape[0]
    lam = lax.complex(a_re.astype(F32), a_im.astype(F32))
    dt = jnp.exp(log_dt.astype(F32))[..., None]
    a_bar = jnp.exp(lam * dt)
    b_bar = ((a_bar - 1.0) / lam)[..., None] * lax.complex(b_re.astype(F32), b_im.astype(F32))
    _, g, p, h = b_bar.shape
    n = depth * SSM_NGB
    bt = jnp.transpose(b_bar, (0, 1, 3, 2)).reshape(n, SSM_GB * h, p)
    bm = jnp.concatenate([_block_diag(bt.real), _block_diag(bt.imag)], axis=-1)
    ct_re = jnp.transpose(c_re.astype(F32), (0, 1, 3, 2)).reshape(n, SSM_GB * p, h)
    ct_im = jnp.transpose(c_im.astype(F32), (0, 1, 3, 2)).reshape(n, SSM_GB * p, h)
    cm = jnp.concatenate([_block_diag(ct_re), -_block_diag(ct_im)], axis=1)
    lead = (depth, SSM_NGB)
    return (bm.astype(BF16).reshape(lead + bm.shape[1:]), cm.astype(BF16).reshape(lead + cm.shape[1:]),
            a_bar.real.reshape(lead + (SSM_CB // 2, LANES)), a_bar.imag.reshape(lead + (SSM_CB // 2, LANES)))


def _s5_branch(proj, tables, layer, d_skip, w_glu, b_glu, tt=128):
    _, bsz, seqlen, _ = proj.shape
    bm, cm, ar, ai = tables
    n_nt = bm.shape[-1] // SSM_GB_IN
    bm = jnp.swapaxes(bm.reshape(bm.shape[:3] + (n_nt, SSM_GB_IN)), 2, 3)
    d_skip = d_skip.reshape(1, SSM_WIDTH).astype(F32)
    w_glu = w_glu.astype(BF16)
    b_glu = b_glu.reshape(1, SSM_WIDTH).astype(F32)
    n_chunks = seqlen // tt
    pair = lambda tile: pl.BlockSpec((None, bsz, 2 * tt, PROJ_TILE), lambda s: (tile, 0, s, 0))
    nxt = lambda tile: pl.BlockSpec((None, bsz, tt, PROJ_TILE),
                                    lambda s: (tile, 0, jnp.minimum(2 * s + 2, n_chunks - 1), 0))
    return pl.pallas_call(
        functools.partial(_s5_kernel, bsz=bsz, tt=tt),
        grid=(n_chunks // 2,),
        in_specs=[pair(U_TILE0), pair(U_TILE0 + 1), nxt(U_TILE0), nxt(U_TILE0 + 1),
                  _resident(bm, layer), _resident(cm, layer), _resident(ar, layer), _resident(ai, layer),
                  _resident(d_skip), _resident(w_glu), _resident(b_glu)],
        out_specs=pl.BlockSpec((bsz, 2 * tt, SSM_WIDTH), lambda s: (0, s, 0)),
        out_shape=jax.ShapeDtypeStruct((bsz, seqlen, SSM_WIDTH), BF16),
        scratch_shapes=[pltpu.VMEM((bsz * tt * SSM_SLOTS, LANES), F32),
                        pltpu.VMEM((bsz * tt * SSM_SLOTS, LANES), F32),
                        pltpu.VMEM((SSM_NGB, bsz * tt, SSM_GB_IN), BF16),
                        pltpu.VMEM((bsz * SSM_NGB * 2, SSM_CB // 2, LANES), F32)],
        compiler_params=_params(("arbitrary",), 48),
        name="s5_branch",
    )(proj, proj, proj, proj, bm, cm, ar, ai, d_skip, w_glu, b_glu)


def _t5_causal_bucket(dist):
    max_exact = N_BUCKETS // 2
    d32 = jnp.maximum(dist, 1).astype(F32)
    large = max_exact + (jnp.log(d32 / max_exact) / math.log(MAX_DISTANCE / max_exact)
                         * (N_BUCKETS - max_exact)).astype(jnp.int32)
    return jnp.where(dist < max_exact, dist, jnp.minimum(large, N_BUCKETS - 1))


def _bias_tile(rel_bias, group):
    window, dilation = DILATION_PATTERN[group]
    steps = window // dilation
    assert steps == Q_BLOCK
    heads = slice(group * HEADS_PER_GROUP, (group + 1) * HEADS_PER_GROUP)
    back = jnp.arange(steps, -1, -1, dtype=jnp.int32)
    vals = rel_bias[_t5_causal_bucket(back * dilation)][:, heads].astype(F32).T
    period = 3 * Q_BLOCK
    v = jnp.concatenate([vals, jnp.full((HEADS_PER_GROUP, period - steps - 1), NEG_INF, F32)], axis=1)
    flat = jnp.tile(v, (1, Q_BLOCK))[:, :Q_BLOCK * (period - 1)]
    tile = flat.reshape(HEADS_PER_GROUP, Q_BLOCK, period - 1)[:, :, :2 * Q_BLOCK]
    col = lax.broadcasted_iota(jnp.int32, tile.shape, 2)
    return jnp.concatenate([tile, jnp.where(col < Q_BLOCK, NEG_INF, tile)], axis=0)


LSE_LANES = LANES // HEADS_PER_GROUP


def _attn_kernel(q_ref, kc_ref, kp_ref, vc_ref, vp_ref, bias_ref, o_ref, lse_ref, kf_ref, vf_ref, *, tq):
    kf_ref[0:Q_BLOCK, :] = kp_ref[...]
    kf_ref[Q_BLOCK:, :] = kc_ref[...]
    vf_ref[0:Q_BLOCK, :] = vp_ref[...]
    vf_ref[Q_BLOCK:, :] = vc_ref[...]
    scale = HEAD_DIM ** -0.5
    first_tile = pl.program_id(2) == 0
    lane_head = lax.broadcasted_iota(jnp.int32, (Q_BLOCK, LANES), 1) // LSE_LANES

    def block(jb, _):
        r0 = pl.multiple_of(jb * Q_BLOCK, Q_BLOCK)
        bias_set = jnp.logical_and(first_tile, jb == 0).astype(jnp.int32) * HEADS_PER_GROUP
        lse = jnp.zeros((Q_BLOCK, LANES), F32)
        for h in range(HEADS_PER_GROUP):
            hs = slice(h * HEAD_DIM, (h + 1) * HEAD_DIM)
            q = q_ref[pl.ds(r0, Q_BLOCK), hs]
            k2 = kf_ref[pl.ds(r0, 2 * Q_BLOCK), hs]
            v2 = vf_ref[pl.ds(r0, 2 * Q_BLOCK), hs]
            s = lax.dot_general(q, k2, (((1,), (1,)), ((), ())), preferred_element_type=F32)
            s = s * scale + bias_ref[bias_set + h]
            m = jnp.max(s, axis=-1, keepdims=True)
            p = jnp.exp(s - m)
            l = jnp.sum(p, axis=-1, keepdims=True)
            o = jnp.dot(p.astype(BF16), v2, preferred_element_type=F32) / l
            o_ref[pl.ds(r0, Q_BLOCK), hs] = o.astype(o_ref.dtype)
            lse = jnp.where(lane_head == h, m + jnp.log(l), lse)
        lse_ref[pl.ds(r0, Q_BLOCK), :] = lse
        return 0

    lax.fori_loop(0, tq // Q_BLOCK, block, 0, unroll=min(4, tq // Q_BLOCK))


def _attention_group(qkv, bias, group, tile0=0):
    _, bsz, d, lc, _ = qkv.shape
    tq = min(lc, 1024)
    per_tq = tq // Q_BLOCK
    cur = lambda which: pl.BlockSpec((None, None, None, tq, PROJ_TILE),
                                     lambda b, r, i: (tile0 + which, b, r, i, 0))
    prev = lambda which: pl.BlockSpec((None, None, None, Q_BLOCK, PROJ_TILE),
                                      lambda b, r, i: (tile0 + which, b, r, jnp.maximum(i * per_tq - 1, 0), 0))
    out_spec = lambda width: pl.BlockSpec((None, None, tq, width), lambda b, r, i: (b, r, i, 0))
    return pl.pallas_call(
        functools.partial(_attn_kernel, tq=tq),
        grid=(bsz, d, lc // tq),
        in_specs=[cur(0), cur(1), prev(1), cur(2), prev(2),
                  pl.BlockSpec(bias.shape, lambda b, r, i: (0, 0, 0))],
        out_specs=[out_spec(ATT_OUT_WIDTH), out_spec(LANES)],
        out_shape=[jax.ShapeDtypeStruct((bsz, d, lc, ATT_OUT_WIDTH), BF16),
                   jax.ShapeDtypeStruct((bsz, d, lc, LANES), F32)],
        scratch_shapes=[pltpu.VMEM((Q_BLOCK + tq, PROJ_TILE), BF16),
                        pltpu.VMEM((Q_BLOCK + tq, PROJ_TILE), BF16)],
        compiler_params=_params(("arbitrary", "arbitrary", "arbitrary"), 40),
        name=f"attention_group{group}",
    )(qkv, qkv, qkv, qkv, qkv, bias)


def _merge_kernel(*refs, tm):
    ys_ref = refs[0]
    o_refs = refs[1:1 + N_ATT_GROUPS]
    l_refs = refs[1 + N_ATT_GROUPS:1 + 2 * N_ATT_GROUPS]
    k = 1 + 2 * N_ATT_GROUPS
    n_gate = (len(refs) - k - 7) // 2
    gs_refs = refs[k:k + n_gate]
    ga_refs = refs[k + n_gate:k + 2 * n_gate]
    x_ref, gate_ref, wbs_ref, wba_ref, wout_ref, out_ref, tok_ref = refs[k + 2 * n_gate:]

    def token_order(ref, g):
        dil = DILATION_PATTERN[g][1]
        if dil == 1:
            return ref[0].astype(F32)
        n_chunks = ref.shape[-1] // LANES
        for r in range(dil):
            for ch in range(n_chunks):
                tok_ref[ch, pl.ds(r, tm // dil, stride=dil), :] = (
                    ref[r, :, ch * LANES:(ch + 1) * LANES].astype(F32))
        return jnp.concatenate([tok_ref[ch] for ch in range(n_chunks)], axis=-1)

    lses = [token_order(r, g) for g, r in enumerate(l_refs)]
    m = functools.reduce(jnp.maximum, lses)
    es = [jnp.exp(l - m) for l in lses]
    den = functools.reduce(lambda a, b: a + b, es)

    def per_head(w):
        return jnp.concatenate([jnp.broadcast_to(w[:, h * LSE_LANES:h * LSE_LANES + 1], (tm, HEAD_DIM))
                                for h in range(HEADS_PER_GROUP)], axis=-1)

    y_att = functools.reduce(lambda a, b: a + b,
                             [per_head(e / den) * token_order(r, g) for g, (e, r) in enumerate(zip(es, o_refs))])

    m_ssm = jnp.dot(ys_ref[...], wbs_ref[...], preferred_element_type=F32)
    m_att = jnp.dot(y_att.astype(BF16), wba_ref[...], preferred_element_type=F32)
    g_ssm = jnp.concatenate([r[...] for r in gs_refs], axis=-1).astype(F32)
    g_att = jnp.concatenate([r[...] for r in ga_refs], axis=-1).astype(F32)
    merged = jax.nn.sigmoid(g_ssm) * m_ssm + jax.nn.sigmoid(g_att) * m_att
    mixed = jnp.dot(merged.astype(BF16), wout_ref[...], preferred_element_type=F32)
    out_ref[...] = x_ref[...] + gate_ref[...] * mixed


def _resident(a, layer=None):
    nd = a.ndim
    if layer is None:
        return pl.BlockSpec(a.shape, lambda *_: (0,) * nd, pipeline_mode=pl.Buffered(1))
    return pl.BlockSpec((None,) + a.shape[1:], lambda *_: (layer,) + (0,) * (nd - 1), pipeline_mode=pl.Buffered(1))


def _merge(x, gate, y_ssm, att, main, w_branch_ssm, w_branch_att, w_out, tm=512):
    bsz, seqlen, d = x.shape
    n_gate = d // PROJ_TILE
    row = lambda width: pl.BlockSpec((None, tm, width), lambda b, i: (b, i, 0))
    tile = lambda t: pl.BlockSpec((None, None, tm, PROJ_TILE), lambda b, i: (t, b, i, 0))
    res_major = lambda dil, width: pl.BlockSpec((None, dil, tm // dil, width), lambda b, i: (b, 0, i, 0))
    wbs, wba, wout = (w.astype(BF16) for w in (w_branch_ssm, w_branch_att, w_out))
    os_, ls_ = zip(*att)
    in_specs = ([row(SSM_WIDTH)]
                + [res_major(dil, ATT_OUT_WIDTH) for _, dil in DILATION_PATTERN]
                + [res_major(dil, LANES) for _, dil in DILATION_PATTERN]
                + [tile(MAIN_GATE0 + t) for t in range(2 * n_gate)]
                + [row(d), pl.BlockSpec((None, 1, d), lambda b, i: (b, 0, 0)),
                   _resident(wbs), _resident(wba), _resident(wout)])
    return pl.pallas_call(
        functools.partial(_merge_kernel, tm=tm),
        grid=(bsz, seqlen // tm),
        in_specs=in_specs,
        out_specs=row(d),
        out_shape=jax.ShapeDtypeStruct(x.shape, F32),
        scratch_shapes=[pltpu.VMEM((ATT_OUT_WIDTH // LANES, tm, LANES), F32)],
        compiler_params=_params(("arbitrary", "arbitrary"), 56),
        name="merge",
    )(y_ssm, *os_, *ls_, *([main] * (2 * n_gate)), x, gate[:, None, :], wbs, wba, wout)


def _ffn_kernel(x_ref, g_ref, shift_ref, scale_ref, gate_ref, wg_ref, wu_ref, wd_ref,
                o_ref, h_ref, inv_ref):
    f = pl.program_id(2)

    @pl.when(f == 0)
    def _():
        _norm_modulate_into(h_ref, x_ref, g_ref, shift_ref, scale_ref, inv_ref)
        o_ref[...] = jnp.zeros_like(o_ref)

    h = h_ref[...]
    a = jnp.dot(h, wg_ref[...].astype(BF16), preferred_element_type=F32)
    b = jnp.dot(h, wu_ref[...].astype(BF16), preferred_element_type=F32)
    act = (a * jax.nn.sigmoid(a)) * b
    o_ref[...] += jnp.dot(act.astype(BF16), wd_ref[...].astype(BF16), preferred_element_type=F32)

    @pl.when(f == pl.num_programs(2) - 1)
    def _():
        o_ref[...] = x_ref[...] + gate_ref[...] * o_ref[...]


def _dense_ffn(x, g, shift, scale, gate, w_gate, w_up, w_down, tm=1024, tf=512):
    bsz, seqlen, d = x.shape
    dff = w_gate.shape[1]
    vec = pl.BlockSpec((None, 1, d), lambda b, i, f: (b, 0, 0))
    row = pl.BlockSpec((None, tm, d), lambda b, i, f: (b, i, 0), pipeline_mode=pl.Buffered(1))
    return pl.pallas_call(
        _ffn_kernel,
        grid=(bsz, seqlen // tm, dff // tf),
        in_specs=[pl.BlockSpec((None, tm, d), lambda b, i, f: (b, i, 0)),
                  pl.BlockSpec((1, d), lambda b, i, f: (0, 0)),
                  vec, vec, vec,
                  pl.BlockSpec((d, tf), lambda b, i, f: (0, f)),
                  pl.BlockSpec((d, tf), lambda b, i, f: (0, f)),
                  pl.BlockSpec((tf, d), lambda b, i, f: (f, 0))],
        out_specs=row,
        out_shape=jax.ShapeDtypeStruct(x.shape, F32),
        scratch_shapes=[pltpu.VMEM((tm, d), BF16), pltpu.VMEM((tm, LANES), F32)],
        compiler_params=_params(("arbitrary", "arbitrary", "arbitrary"), 56),
        name="dense_ffn",
    )(x, g.reshape(1, d), shift[:, None, :], scale[:, None, :], gate[:, None, :],
      w_gate.astype(BF16), w_up.astype(BF16), w_down.astype(BF16))


MOE_SUB = 256
SUBS_PER_TILE = 5
MOE_SUPER = SUBS_PER_TILE * MOE_SUB


def _pack_halves(x):
    half = x.shape[-1] // 2
    bits = lambda v: lax.bitcast_convert_type(v.astype(BF16).astype(F32), jnp.uint32)
    return bits(x[:, half:]) | (bits(x[:, :half]) >> 16)


def _unpack_halves(w):
    lo = lax.bitcast_convert_type(w << 16, F32)
    hi = lax.bitcast_convert_type(w & jnp.uint32(0xFFFF0000), F32)
    return lo, hi


def _router_kernel(x_ref, g_ref, shift_ref, scale_ref, w2_ref, h_ref, idx_ref, gate_ref, hf_ref, inv_ref):
    _norm_modulate_into(hf_ref, x_ref, g_ref, shift_ref, scale_ref, inv_ref)
    h = hf_ref[...]
    h_ref[...] = _pack_halves(h)
    h_hi = h.astype(BF16)
    h_lo = (h - h_hi.astype(F32)).astype(BF16)
    prod = (jnp.dot(h_hi, w2_ref[...], preferred_element_type=F32)
            + jnp.dot(h_lo, w2_ref[...], preferred_element_type=F32))
    logits = prod[:, :LANES] + prod[:, LANES:]
    lane_i = lax.broadcasted_iota(jnp.int32, logits.shape, 1)
    lane = lane_i.astype(F32)
    logits = jnp.where(lane_i < N_EXPERTS, logits, -jnp.inf)
    m1 = jnp.max(logits, axis=-1, keepdims=True)
    i1 = jnp.min(jnp.where(logits == m1, lane, float(LANES)), axis=-1, keepdims=True)
    rest = jnp.where(lane == i1, -jnp.inf, logits)
    m2 = jnp.max(rest, axis=-1, keepdims=True)
    i2 = jnp.min(jnp.where(rest == m2, lane, float(LANES)), axis=-1, keepdims=True)
    e2 = jnp.exp(m2 - m1)
    den = 1.0 + e2
    idx_ref[...] = jnp.where(lane_i == 0, i1, jnp.where(lane_i == 1, i2, 0.0)).astype(jnp.int32)
    gate_ref[...] = jnp.where(lane_i == 0, 1.0 / den, jnp.where(lane_i == 1, e2 / den, 0.0))


def _router(x, g, shift, scale, w_router, tm=512):
    bsz, seqlen, d = x.shape
    wr = jnp.zeros((d, LANES), F32).at[:, :N_EXPERTS].set(w_router)
    w_hi = wr.astype(BF16)
    w_lo = (wr - w_hi.astype(F32)).astype(BF16)
    w2 = jnp.concatenate([w_hi, w_lo], axis=1)
    row = lambda width: pl.BlockSpec((None, tm, width), lambda b, i: (b, i, 0))
    vec = pl.BlockSpec((None, 1, d), lambda b, i: (b, 0, 0))
    return pl.pallas_call(
        _router_kernel,
        grid=(bsz, seqlen // tm),
        in_specs=[row(d), pl.BlockSpec((1, d), lambda b, i: (0, 0)), vec, vec,
                  pl.BlockSpec((d, 2 * LANES), lambda b, i: (0, 0))],
        out_specs=[row(d // 2), row(LANES), row(LANES)],
        out_shape=[jax.ShapeDtypeStruct((bsz, seqlen, d // 2), jnp.uint32),
                   jax.ShapeDtypeStruct((bsz, seqlen, LANES), jnp.int32),
                   jax.ShapeDtypeStruct((bsz, seqlen, LANES), F32)],
        scratch_shapes=[pltpu.VMEM((tm, d), F32), pltpu.VMEM((tm, LANES), F32)],
        compiler_params=_params(("arbitrary", "arbitrary"), 40),
        name="moe_router",
    )(x, g.reshape(1, d), shift[:, None, :], scale[:, None, :], w2)


def _moe_dims(n_tok, nf):
    n_sub = n_tok * TOP_K // MOE_SUB + N_EXPERTS
    n_super = (n_sub + (SUBS_PER_TILE - 1) * N_EXPERTS) // SUBS_PER_TILE + 1
    rows_per_step = -(-MOE_SUPER // nf)
    while (rows_per_step * nf) % SUBLANES:
        rows_per_step += 1
    n_fetch = rows_per_step * nf
    return n_sub, n_super, rows_per_step, n_fetch


def _routing_tables(top_expert, n_tok, nf):
    n_sub, n_super, _, n_fetch = _moe_dims(n_tok, nf)
    i32 = jnp.int32
    flat_e = top_expert.reshape(-1)
    onehot = (flat_e[:, None] == jnp.arange(N_EXPERTS, dtype=i32)[None, :]).astype(i32)
    csum = jnp.cumsum(onehot, axis=0)
    rank = jnp.sum(csum * onehot, axis=1) - 1
    counts = csum[-1]
    subs = (counts + MOE_SUB - 1) // MOE_SUB
    pend = jnp.cumsum(subs) * MOE_SUB
    pstart = pend - subs * MOE_SUB
    dest = (pstart[flat_e] + rank).astype(i32)
    flat_token = jnp.arange(n_tok * TOP_K, dtype=i32) // TOP_K
    row_token = jnp.zeros((n_sub * MOE_SUB + n_fetch,), i32).at[dest].set(flat_token)
    supers = (subs + SUBS_PER_TILE - 1) // SUBS_PER_TILE
    send = jnp.cumsum(supers)
    sstart = send - supers
    s = jnp.arange(n_super + 1, dtype=i32)
    e = jnp.minimum(jnp.searchsorted(send, s, side='right'), N_EXPERTS - 1).astype(i32)
    local = s - sstart[e]
    used = s < send[-1]
    tile_expert = jnp.where(used, e, e[jnp.maximum(send[-1] - 1, 0)]).astype(i32)
    n_tiles_e = jnp.maximum(supers[e], 1)
    base, rem = subs[e] // n_tiles_e, subs[e] % n_tiles_e
    first_sub = local * base + jnp.minimum(local, rem)
    tile_row0 = jnp.where(used, pstart[e] + first_sub * MOE_SUB, 0).astype(i32)
    tile_nsub = jnp.where(used, base + (local < rem).astype(i32), 0).astype(i32)
    n_used = send[-1].astype(i32).reshape(1)
    n_sub_used = (pend[-1] // MOE_SUB).astype(i32).reshape(1)
    return dest, row_token, tile_expert, tile_row0, tile_nsub, n_used, n_sub_used


def _moe_kernel(te_ref, row0_ref, nsub_ref, nu_ref, nsu_ref, tok_ref, hp_ref, wg_ref, wu_ref, wd_ref, y_ref,
                xg_ref, xb_ref, acc_ref, yb_ref, gsem, osem, fsem, *, nf, rows_per_step, n_sub_alloc, n_fill):
    s, f = pl.program_id(0), pl.program_id(1)
    n_used = nu_ref[0]
    used = s < n_used
    n_fetch = rows_per_step * nf
    half = xg_ref.shape[-1]

    def row_copy(tile, r):
        tok = tok_ref[row0_ref[tile] + r]
        return pltpu.make_async_copy(hp_ref.at[pl.ds(tok, 1)], xg_ref.at[pl.ds(r, 1)], gsem.at[0])

    def wait_rows():
        pltpu.make_async_copy(hp_ref.at[pl.ds(0, n_fetch)], xg_ref, gsem.at[0]).wait()

    def out_copies(tile):
        r0 = pl.multiple_of(row0_ref[tile], MOE_SUB)
        return [pltpu.make_async_copy(yb_ref.at[pl.ds(k * MOE_SUB, MOE_SUB)],
                                      y_ref.at[pl.ds(r0 + k * MOE_SUB, MOE_SUB)], osem.at[k])
                for k in range(SUBS_PER_TILE)]

    def start_out(tile):
        for k, cp in enumerate(out_copies(tile)):
            pl.when(k < nsub_ref[tile])(cp.start)

    def wait_out(tile):
        for k, cp in enumerate(out_copies(tile)):
            pl.when(k < nsub_ref[tile])(cp.wait)

    @pl.when(jnp.logical_and(s == 0, f == 0))
    def _():
        def body(r, _):
            row_copy(0, r).start()
            return 0
        lax.fori_loop(0, n_fetch, body, 0)

    @pl.when(jnp.logical_and(f == 0, s <= n_used))
    def _():
        wait_rows()

    @pl.when(jnp.logical_and(used, f == 0))
    def _():
        lo, hi = _unpack_halves(xg_ref[0:MOE_SUPER, :])
        xb_ref[:, :half] = lo.astype(BF16)
        xb_ref[:, half:] = hi.astype(BF16)
        acc_ref[...] = jnp.zeros_like(acc_ref)

    def step(n_rows):
        for k in range(rows_per_step):
            row_copy(s + 1, f * rows_per_step + k).start()
        h = xb_ref[0:n_rows, :]
        a = jnp.dot(h, wg_ref[...].astype(BF16), preferred_element_type=F32)
        b = jnp.dot(h, wu_ref[...].astype(BF16), preferred_element_type=F32)
        act = (a * jax.nn.sigmoid(a)) * b
        acc_ref[0:n_rows, :] += jnp.dot(act.astype(BF16), wd_ref[...].astype(BF16), preferred_element_type=F32)

    for n in range(1, SUBS_PER_TILE + 1):
        pl.when(jnp.logical_and(used, nsub_ref[s] == n))(functools.partial(step, n * MOE_SUB))

    @pl.when(jnp.logical_and(used, f == nf - 1))
    def _():
        pl.when(s > 0)(functools.partial(wait_out, s - 1))
        yb_ref[...] = _pack_halves(acc_ref[...])
        start_out(s)

    @pl.when(jnp.logical_and(s == n_used, f == 0))
    def _():
        wait_out(s - 1)
        yb_ref[0:MOE_SUB, :] = jnp.zeros((MOE_SUB, half), yb_ref.dtype)
        fills = []
        for k in range(n_fill):
            sub = nsu_ref[0] + k
            cp = pltpu.make_async_copy(yb_ref.at[pl.ds(0, MOE_SUB)],
                                       y_ref.at[pl.ds(pl.multiple_of(sub * MOE_SUB, MOE_SUB), MOE_SUB)], fsem.at[k])
            fills.append((sub < n_sub_alloc, cp))
        for cond, cp in fills:
            pl.when(cond)(cp.start)
        for cond, cp in fills:
            pl.when(cond)(cp.wait)


def _moe_experts(hp, tables, w_gate, w_up, w_down, tf=512):
    n_tok, half = hp.shape
    d = 2 * half
    dff = w_gate.shape[2]
    nf = dff // tf
    n_sub, n_super, rows_per_step, n_fetch = _moe_dims(n_tok, nf)
    _, row_token, tile_expert, tile_row0, tile_nsub, n_used, n_sub_used = tables
    n_unused_max = n_sub - n_tok * TOP_K // MOE_SUB
    tile_f = lambda s, f, nu: jnp.where(s < nu[0], f, nf - 1)
    w_in_spec = pl.BlockSpec((None, d, tf), lambda s, f, te, r0, ns, nu, nsu, tok: (te[s], 0, tile_f(s, f, nu)))
    w_out_spec = pl.BlockSpec((None, tf, d), lambda s, f, te, r0, ns, nu, nsu, tok: (te[s], tile_f(s, f, nu), 0))
    return pl.pallas_call(
        functools.partial(_moe_kernel, nf=nf, rows_per_step=rows_per_step, n_sub_alloc=n_sub,
                          n_fill=n_unused_max),
        grid_spec=pltpu.PrefetchScalarGridSpec(
            num_scalar_prefetch=6,
            grid=(n_super, nf),
            in_specs=[pl.BlockSpec(memory_space=pl.ANY), w_in_spec, w_in_spec, w_out_spec],
            out_specs=pl.BlockSpec(memory_space=pl.ANY),
            scratch_shapes=[pltpu.VMEM((n_fetch, half), jnp.uint32),
                            pltpu.VMEM((MOE_SUPER, d), BF16),
                            pltpu.VMEM((MOE_SUPER, d), F32),
                            pltpu.VMEM((MOE_SUPER, half), jnp.uint32),
                            pltpu.SemaphoreType.DMA((1,)), pltpu.SemaphoreType.DMA((SUBS_PER_TILE,)),
                            pltpu.SemaphoreType.DMA((n_unused_max,))]),
        out_shape=jax.ShapeDtypeStruct((n_sub * MOE_SUB, half), jnp.uint32),
        compiler_params=_params(("arbitrary", "arbitrary"), 58),
        name="moe_experts",
    )(tile_expert, tile_row0, tile_nsub, n_used, n_sub_used, row_token, hp, w_gate, w_up, w_down)


COMBINE_UNROLL = 8


def _combine_kernel(dest_ref, y_ref, x_ref, gates_ref, gate_f_ref, gn_ref, o_ref, rows_ref, sems,
                    *, tm, n_steps, final_norm):
    t = pl.program_id(0)
    slot = t % 2
    n = tm * TOP_K
    half = rows_ref.shape[-1]

    def row_copy(base, j, row, slot_):
        return pltpu.make_async_copy(y_ref.at[pl.ds(dest_ref[base + j], 1)],
                                     rows_ref.at[slot_, pl.ds(row, 1)], sems.at[slot_])

    def wait_rows(slot_):
        pltpu.make_async_copy(y_ref.at[pl.ds(0, n)], rows_ref.at[slot_], sems.at[slot_]).wait()

    @pl.when(t == 0)
    def _():
        def body(i, _):
            for u in range(COMBINE_UNROLL):
                row = (u % TOP_K) * tm + i * (COMBINE_UNROLL // TOP_K) + u // TOP_K
                row_copy(0, i * COMBINE_UNROLL + u, row, 0).start()
            return 0
        lax.fori_loop(0, n // COMBINE_UNROLL, body, 0)

    wait_rows(slot)
    next_base = jnp.minimum(t + 1, n_steps - 1) * n
    for j in range(n):
        row_copy(next_base, j, (j % TOP_K) * tm + j // TOP_K, 1 - slot).start()

    gates = gates_ref[...]
    g0, g1 = gates[:, 0:1], gates[:, 1:2]
    lo0, hi0 = _unpack_halves(rows_ref[slot, 0:tm, :])
    lo1, hi1 = _unpack_halves(rows_ref[slot, tm:2 * tm, :])
    out_lo = x_ref[:, :half] + gate_f_ref[:, :half] * (g0 * lo0 + g1 * lo1)
    out_hi = x_ref[:, half:] + gate_f_ref[:, half:] * (g0 * hi0 + g1 * hi1)
    if final_norm:
        ssq = jnp.sum(out_lo * out_lo, axis=-1, keepdims=True) + jnp.sum(out_hi * out_hi, axis=-1, keepdims=True)
        inv = lax.rsqrt(ssq / (2 * half) + EPS)
        out_lo = (out_lo * inv) * gn_ref[:, :half]
        out_hi = (out_hi * inv) * gn_ref[:, half:]
    o_ref[:, :half] = out_lo
    o_ref[:, half:] = out_hi
    pl.when(t == n_steps - 1)(functools.partial(wait_rows, 1 - slot))


def _combine(x, y_rows, dest, top_gate, gate_f, final_g, tm=256):
    bsz, seqlen, d = x.shape
    final_norm = final_g is not None
    gn = (final_g if final_norm else jnp.ones((d,), F32)).reshape(1, d)
    per_batch = seqlen // tm
    n_steps = bsz * per_batch
    row = lambda width: pl.BlockSpec((None, tm, width), lambda t, dst: (t // per_batch, t % per_batch, 0))
    return pl.pallas_call(
        functools.partial(_combine_kernel, tm=tm, n_steps=n_steps, final_norm=final_norm),
        grid_spec=pltpu.PrefetchScalarGridSpec(
            num_scalar_prefetch=1,
            grid=(n_steps,),
            in_specs=[pl.BlockSpec(memory_space=pl.ANY), row(d), row(LANES),
                      pl.BlockSpec((None, 1, d), lambda t, dst: (t // per_batch, 0, 0)),
                      pl.BlockSpec((1, d), lambda t, dst: (0, 0))],
            out_specs=row(d),
            scratch_shapes=[pltpu.VMEM((2, TOP_K * tm, d // 2), jnp.uint32), pltpu.SemaphoreType.DMA((2,))]),
        out_shape=jax.ShapeDtypeStruct(x.shape, F32),
        compiler_params=_params(("arbitrary",), 32),
        name="moe_combine",
    )(dest, y_rows, x, top_gate, gate_f[:, None, :], gn)


def _moe_ffn(x, g, shift, scale, gate_f, w_router, w_gate, w_up, w_down, final_g, tf=512):
    bsz, seqlen, d = x.shape
    n_tok = bsz * seqlen
    hp, top_idx, top_gate = _router(x, g, shift, scale, w_router)
    tables = _routing_tables(top_idx[..., :TOP_K], n_tok, w_gate.shape[2] // tf)
    y_rows = _moe_experts(hp.reshape(n_tok, d // 2), tables, w_gate, w_up, w_down, tf=tf)
    return _combine(x, y_rows, tables[0], top_gate, gate_f, final_g)


def _final_norm_kernel(x_ref, g_ref, o_ref):
    x = x_ref[...]
    ms = jnp.mean(x * x, axis=-1, keepdims=True)
    o_ref[...] = (x * lax.rsqrt(ms + EPS)) * g_ref[...]


def _final_norm(x, g, tm=512):
    bsz, seqlen, d = x.shape
    return pl.pallas_call(
        _final_norm_kernel,
        grid=(bsz, seqlen // tm),
        in_specs=[pl.BlockSpec((None, tm, d), lambda b, i: (b, i, 0)),
                  pl.BlockSpec((1, d), lambda b, i: (0, 0))],
        out_specs=pl.BlockSpec((None, tm, d), lambda b, i: (b, i, 0)),
        out_shape=jax.ShapeDtypeStruct(x.shape, F32),
        compiler_params=_params(("arbitrary", "arbitrary"), 32),
        name="final_norm",
    )(x, g.reshape(1, d))


def kernel(x, c, w_mod, b_mod, norm_mix_g, norm_ffn_g, w_in, ssm_a_re, ssm_a_im, ssm_log_dt, ssm_b_re, ssm_b_im, ssm_c_re, ssm_c_im, ssm_d, w_glu, b_glu, rel_bias, w_branch_ssm, w_branch_att, w_out, ffn_w_gate, ffn_w_up, ffn_w_down, moe_router, moe_w_gate, moe_w_up, moe_w_down, final_norm_g):
    depth = w_mod.shape[0]
    mod = _modulation(c, w_mod, b_mod)
    biases = [_bias_tile(rel_bias, g) for g in range(N_ATT_GROUPS)]
    tables = _s5_tables(ssm_a_re, ssm_a_im, ssm_log_dt, ssm_b_re, ssm_b_im, ssm_c_re, ssm_c_im)
    for i in range(depth):
        shift_m, scale_m, gate_m, shift_f, scale_f, gate_f = jnp.split(mod[i], N_MOD, axis=-1)
        main, qkv = _in_projection(x, norm_mix_g[i], shift_m, scale_m, w_in, i)
        y_ssm = _s5_branch(main, tables, i, ssm_d[i], w_glu[i], b_glu[i])
        att = [_attention_group(qkv[g], biases[g], g) for g in range(N_ATT_GROUPS)]
        x = _merge(x, gate_m, y_ssm, att, main, w_branch_ssm[i], w_branch_att[i], w_out[i])
        j = i // 2
        last = i == depth - 1
        if i % 2 == 0:
            x = _dense_ffn(x, norm_ffn_g[i], shift_f, scale_f, gate_f,
                           ffn_w_gate[j], ffn_w_up[j], ffn_w_down[j])
            if last:
                x = _final_norm(x, final_norm_g)
        else:
            x = _moe_ffn(x, norm_ffn_g[i], shift_f, scale_f, gate_f, moe_router[j],
                         moe_w_gate[j], moe_w_up[j], moe_w_down[j], final_norm_g if last else None)
    return x
```

```python
import functools
import math

import jax
import jax.numpy as jnp
from jax import lax
from jax.experimental import pallas as pl
from jax.experimental.pallas import tpu as pltpu

F32 = jnp.float32
BF16 = jnp.bfloat16

LANES = 128
SUBLANES = 8
VMEM_BYTES = 64 * 1024 * 1024

SSM_GROUP = 16
SSM_STATE = 64
SSM_WIDTH = 1024
HEAD_DIM = 128
DILATION_PATTERN = ((128, 1), (512, 4), (2048, 16))
HEADS_PER_GROUP = 4
N_ATT_GROUPS = len(DILATION_PATTERN)
ATT_OUT_WIDTH = HEADS_PER_GROUP * HEAD_DIM
Q_BLOCK = 128
NEG_INF = -1e30
N_BUCKETS = 32
MAX_DISTANCE = 2048
N_EXPERTS = 8
TOP_K = 2
N_MOD = 6
EPS = 1e-6

PROJ_TILE = 512
U_TILE0 = 0
Q_TILE0 = SSM_WIDTH // PROJ_TILE
K_TILE0 = Q_TILE0 + N_ATT_GROUPS
V_TILE0 = K_TILE0 + N_ATT_GROUPS
GS_TILE0 = V_TILE0 + N_ATT_GROUPS
MAIN_GATE0 = Q_TILE0


def _params(dims, vmem_mb):
    return pltpu.CompilerParams(dimension_semantics=dims,
                                vmem_limit_bytes=vmem_mb * 1024 * 1024)


def _norm_modulate(x, g, shift, scale):
    ms = jnp.mean(x * x, axis=-1, keepdims=True)
    y = x * lax.rsqrt(ms + EPS)
    return (y * g) * (1.0 + scale) + shift


NORM_ROWS = 64


def _norm_modulate_into(h_ref, x_ref, g_ref, shift_ref, scale_ref, inv_ref):
    n_blocks = x_ref.shape[0] // NORM_ROWS
    reps = x_ref.shape[1] // LANES

    def block(i):
        return pl.ds(pl.multiple_of(i * NORM_ROWS, NORM_ROWS), NORM_ROWS)

    def rms(i, _):
        x = x_ref[block(i), :]
        inv = lax.rsqrt(jnp.mean(x * x, axis=-1, keepdims=True) + EPS)
        inv_ref[block(i), :] = jnp.broadcast_to(inv, (NORM_ROWS, LANES))
        return 0

    gain = g_ref[...] * (1.0 + scale_ref[...])
    shift = shift_ref[...]

    def scale_rows(i, _):
        y = x_ref[block(i), :] * jnp.concatenate([inv_ref[block(i), :]] * reps, axis=-1)
        h_ref[block(i), :] = (y * gain + shift).astype(h_ref.dtype)
        return 0

    lax.fori_loop(0, n_blocks, rms, 0, unroll=4)
    lax.fori_loop(0, n_blocks, scale_rows, 0)


def _mod_kernel(c_ref, w_ref, b_ref, o_ref):
    c = c_ref[...]
    cond = (c * jax.nn.sigmoid(c)).astype(BF16)
    o_ref[...] = jnp.dot(cond, w_ref[...].astype(BF16),
                         preferred_element_type=F32) + b_ref[...]


def _modulation(c, w_mod, b_mod):
    depth, d, n = w_mod.shape
    bsz = c.shape[0]
    rows = SUBLANES
    c_pad = jnp.zeros((rows, d), F32).at[:bsz].set(c)
    tn = 1536
    out = pl.pallas_call(
        _mod_kernel,
        grid=(depth, n // tn),
        in_specs=[pl.BlockSpec((rows, d), lambda l, j: (0, 0)),
                  pl.BlockSpec((None, d, tn), lambda l, j: (l, 0, j)),
                  pl.BlockSpec((None, 1, tn), lambda l, j: (l, 0, j))],
        out_specs=pl.BlockSpec((None, rows, tn), lambda l, j: (l, 0, j)),
        out_shape=jax.ShapeDtypeStruct((depth, rows, n), F32),
        compiler_params=_params(("arbitrary", "arbitrary"), 40),
        name="modulation",
    )(c_pad, w_mod, b_mod.reshape(depth, 1, n))
    return out[:, :bsz]


def _inproj_kernel(x_ref, g_ref, shift_ref, scale_ref, w_ref, main_ref, *rest, tm):
    qkv_refs, (h_ref, res_ref, inv_ref) = rest[:N_ATT_GROUPS], rest[N_ATT_GROUPS:]
    j = pl.program_id(2)

    @pl.when(j == 0)
    def _():
        _norm_modulate_into(h_ref, x_ref, g_ref, shift_ref, scale_ref, inv_ref)

    def project():
        return jnp.dot(h_ref[...], w_ref[...].astype(BF16), preferred_element_type=F32)

    @pl.when(jnp.logical_or(j < Q_TILE0, j >= GS_TILE0))
    def _():
        main_ref[...] = project().astype(BF16)

    for g, (_, dil) in enumerate(DILATION_PATTERN):
        is_g = functools.reduce(jnp.logical_or, [j == t0 + g for t0 in (Q_TILE0, K_TILE0, V_TILE0)])

        @pl.when(is_g)
        def _(g=g, dil=dil):
            res = project()
            if dil == 1:
                qkv_refs[g][0] = res.astype(BF16)
            else:
                n_chunks = PROJ_TILE // LANES
                for ch in range(n_chunks):
                    res_ref[ch] = res[:, ch * LANES:(ch + 1) * LANES]
                for r in range(dil):
                    rows = [res_ref[ch, pl.ds(r, tm // dil, stride=dil), :] for ch in range(n_chunks)]
                    qkv_refs[g][r] = jnp.concatenate(rows, axis=-1).astype(BF16)


def _in_projection(x, g, shift, scale, w_in, layer, tm=1024):
    bsz, seqlen, d = x.shape
    n_tiles = w_in.shape[2] // PROJ_TILE
    n_main = n_tiles - 3 * N_ATT_GROUPS

    def main_map(b, i, j):
        return (jnp.where(j < Q_TILE0, j, jnp.where(j < GS_TILE0, Q_TILE0 - 1, j - 3 * N_ATT_GROUPS)), b, i, 0)

    def qkv_map(g):
        return lambda b, i, j: ((j > Q_TILE0 + g).astype(jnp.int32) + (j > K_TILE0 + g).astype(jnp.int32),
                                b, 0, i, 0)

    qkv_specs = [pl.BlockSpec((None, None, dil, tm // dil, PROJ_TILE), qkv_map(g))
                 for g, (_, dil) in enumerate(DILATION_PATTERN)]
    qkv_shapes = [jax.ShapeDtypeStruct((3, bsz, dil, seqlen // dil, PROJ_TILE), BF16)
                  for _, dil in DILATION_PATTERN]
    outs = pl.pallas_call(
        functools.partial(_inproj_kernel, tm=tm),
        grid=(bsz, seqlen // tm, n_tiles),
        in_specs=[pl.BlockSpec((None, tm, d), lambda b, i, j: (b, i, 0)),
                  pl.BlockSpec((1, d), lambda b, i, j: (0, 0)),
                  pl.BlockSpec((None, 1, d), lambda b, i, j: (b, 0, 0)),
                  pl.BlockSpec((None, 1, d), lambda b, i, j: (b, 0, 0)),
                  pl.BlockSpec((None, d, PROJ_TILE), lambda b, i, j: (layer, 0, j))],
        out_specs=[pl.BlockSpec((None, None, tm, PROJ_TILE), main_map)] + qkv_specs,
        out_shape=[jax.ShapeDtypeStruct((n_main, bsz, seqlen, PROJ_TILE), BF16)] + qkv_shapes,
        scratch_shapes=[pltpu.VMEM((tm, d), BF16), pltpu.VMEM((PROJ_TILE // LANES, tm, LANES), F32),
                        pltpu.VMEM((tm, LANES), F32)],
        compiler_params=_params(("arbitrary", "arbitrary", "arbitrary"), 52),
        name="in_projection",
    )(x, g.reshape(1, d), shift[:, None, :], scale[:, None, :], w_in)
    return outs[0], outs[1:]


SSM_GB = 16
SSM_NGB = (SSM_WIDTH // SSM_GROUP) // SSM_GB
SSM_GB_IN = SSM_GB * SSM_GROUP
SSM_GB_RE = SSM_GB * SSM_STATE
SSM_CB = 2 * SSM_GB_RE // LANES
SSM_SLOTS = SSM_NGB * SSM_CB + SUBLANES


def _s5_kernel(ua0_ref, ua1_ref, un0_ref, un1_ref, bm_ref, cm_ref, ar_ref, ai_ref, d_ref, wglu_ref, bglu_ref,
               o_ref, xs0_ref, xs1_ref, ug_ref, st_ref, *, bsz, tt):
    xs_ref = (xs0_ref, xs1_ref)
    rows = bsz * tt
    n_nt = bm_ref.shape[1]
    n_pieces = SSM_NGB * n_nt
    unroll = 4
    assert tt == unroll * n_pieces
    half_cb = SSM_CB // 2

    def pair_halves(c):
        return tuple(r[:, c * tt:(c + 1) * tt, :].reshape(rows, PROJ_TILE) for r in (ua0_ref, ua1_ref))

    def stage_u(halves):
        for gb in range(SSM_NGB):
            half, off = divmod(gb * SSM_GB_IN, PROJ_TILE)
            ug_ref[gb] = halves[half][:, off:off + SSM_GB_IN]

    def b_piece(i, dst):
        gb, nt = i // n_nt, i % n_nt
        res = jnp.dot(ug_ref[gb], bm_ref[gb, nt], preferred_element_type=F32)
        for j in range(2):
            xs_ref[dst][pl.ds(gb * SSM_CB + 2 * nt + j, rows, stride=SSM_SLOTS), :] = (
                res[:, j * LANES:(j + 1) * LANES])

    a_re = [ar_ref[gb] for gb in range(SSM_NGB)]
    a_im = [ai_ref[gb] for gb in range(SSM_NGB)]

    def time_step(t, carry, src):
        new = []
        for b in range(bsz):
            for gb in range(SSM_NGB):
                k = (b * SSM_NGB + gb) * 2
                s_re, s_im = carry[k], carry[k + 1]
                row = (b * tt + t) * SSM_SLOTS + gb * SSM_CB
                n_re = a_re[gb] * s_re - a_im[gb] * s_im + xs_ref[src][pl.ds(row, half_cb), :]
                n_im = a_re[gb] * s_im + a_im[gb] * s_re + xs_ref[src][pl.ds(row + half_cb, half_cb), :]
                xs_ref[src][pl.ds(row, half_cb), :] = n_re
                xs_ref[src][pl.ds(row + half_cb, half_cb), :] = n_im
                new += [n_re, n_im]
        return tuple(new)

    n_carry = bsz * SSM_NGB * 2

    def scan_chunk(src):
        carry = tuple(st_ref[k] for k in range(n_carry))
        for it in range(n_pieces):
            b_piece(it, 1 - src)
            for k in range(unroll):
                carry = time_step(it * unroll + k, carry, src)
        for k in range(n_carry):
            st_ref[k] = carry[k]

    def finish_chunk(c):
        ys = []
        for gb in range(SSM_NGB):
            cols = [xs_ref[c][pl.ds(gb * SSM_CB + cb, rows, stride=SSM_SLOTS), :] for cb in range(SSM_CB)]
            xg = jnp.concatenate(cols, axis=-1).astype(BF16)
            ys.append(jnp.dot(xg, cm_ref[gb], preferred_element_type=F32))
        y = jnp.concatenate(ys, axis=-1)
        u32 = jnp.concatenate(pair_halves(c), axis=-1).astype(F32)
        y = jax.nn.gelu(y + d_ref[...] * u32)
        z = jnp.dot(y.astype(BF16), wglu_ref[...], preferred_element_type=F32) + bglu_ref[...]
        o_ref[:, c * tt:(c + 1) * tt, :] = (y * jax.nn.sigmoid(z)).astype(o_ref.dtype).reshape(bsz, tt, SSM_WIDTH)

    @pl.when(pl.program_id(0) == 0)
    def _():
        st_ref[...] = jnp.zeros_like(st_ref)
        stage_u(pair_halves(0))
        for i in range(n_pieces):
            b_piece(i, 0)

    stage_u(pair_halves(1))
    scan_chunk(0)
    finish_chunk(0)
    stage_u(tuple(r[...].reshape(rows, PROJ_TILE) for r in (un0_ref, un1_ref)))
    scan_chunk(1)
    finish_chunk(1)


def _block_diag(m):
    ngb, rows, c = m.shape
    r = rows // SSM_GB
    row_group = lax.broadcasted_iota(jnp.int32, (rows, SSM_GB * c), 0) // r
    col_group = lax.broadcasted_iota(jnp.int32, (rows, SSM_GB * c), 1) // c
    return jnp.where(row_group == col_group, jnp.tile(m, (1, 1, SSM_GB)), 0.0)


def _s5_tables(a_re, a_im, log_dt, b_re, b_im, c_re, c_im):
    depth = a_re.shape[0]
    lam = lax.complex(a_re.astype(F32), a_im.astype(F32))
    dt = jnp.exp(log_dt.astype(F32))[..., None]
    a_bar = jnp.exp(lam * dt)
    b_bar = ((a_bar - 1.0) / lam)[..., None] * lax.complex(b_re.astype(F32), b_im.astype(F32))
    _, g, p, h = b_bar.shape
    n = depth * SSM_NGB
    bt = jnp.transpose(b_bar, (0, 1, 3, 2)).reshape(n, SSM_GB * h, p)
    bm = jnp.concatenate([_block_diag(bt.real), _block_diag(bt.imag)], axis=-1)
    ct_re = jnp.transpose(c_re.astype(F32), (0, 1, 3, 2)).reshape(n, SSM_GB * p, h)
    ct_im = jnp.transpose(c_im.astype(F32), (0, 1, 3, 2)).reshape(n, SSM_GB * p, h)
    cm = jnp.concatenate([_block_diag(ct_re), -_block_diag(ct_im)], axis=1)
    lead = (depth, SSM_NGB)
    return (bm.astype(BF16).reshape(lead + bm.shape[1:]), cm.astype(BF16).reshape(lead + cm.shape[1:]),
            a_bar.real.reshape(lead + (SSM_CB // 2, LANES)), a_bar.imag.reshape(lead + (SSM_CB // 2, LANES)))


def _s5_branch(proj, tables, layer, d_skip, w_glu, b_glu, tt=128):
    _, bsz, seqlen, _ = proj.shape
    bm, cm, ar, ai = tables
    n_nt = bm.shape[-1] // SSM_GB_IN
    bm = jnp.swapaxes(bm.reshape(bm.shape[:3] + (n_nt, SSM_GB_IN)), 2, 3)
    d_skip = d_skip.reshape(1, SSM_WIDTH).astype(F32)
    w_glu = w_glu.astype(BF16)
    b_glu = b_glu.reshape(1, SSM_WIDTH).astype(F32)
    n_chunks = seqlen // tt
    pair = lambda tile: pl.BlockSpec((None, bsz, 2 * tt, PROJ_TILE), lambda s: (tile, 0, s, 0))
    nxt = lambda tile: pl.BlockSpec((None, bsz, tt, PROJ_TILE),
                                    lambda s: (tile, 0, jnp.minimum(2 * s + 2, n_chunks - 1), 0))
    return pl.pallas_call(
        functools.partial(_s5_kernel, bsz=bsz, tt=tt),
        grid=(n_chunks // 2,),
        in_specs=[pair(U_TILE0), pair(U_TILE0 + 1), nxt(U_TILE0), nxt(U_TILE0 + 1),
                  _resident(bm, layer), _resident(cm, layer), _resident(ar, layer), _resident(ai, layer),
                  _resident(d_skip), _resident(w_glu), _resident(b_glu)],
        out_specs=pl.BlockSpec((bsz, 2 * tt, SSM_WIDTH), lambda s: (0, s, 0)),
        out_shape=jax.ShapeDtypeStruct((bsz, seqlen, SSM_WIDTH), BF16),
        scratch_shapes=[pltpu.VMEM((bsz * tt * SSM_SLOTS, LANES), F32),
                        pltpu.VMEM((bsz * tt * SSM_SLOTS, LANES), F32),
                        pltpu.VMEM((SSM_NGB, bsz * tt, SSM_GB_IN), BF16),
                        pltpu.VMEM((bsz * SSM_NGB * 2, SSM_CB // 2, LANES), F32)],
        compiler_params=_params(("arbitrary",), 48),
        name="s5_branch",
    )(proj, proj, proj, proj, bm, cm, ar, ai, d_skip, w_glu, b_glu)


def _t5_causal_bucket(dist):
    max_exact = N_BUCKETS // 2
    d32 = jnp.maximum(dist, 1).astype(F32)
    large = max_exact + (jnp.log(d32 / max_exact) / math.log(MAX_DISTANCE / max_exact)
                         * (N_BUCKETS - max_exact)).astype(jnp.int32)
    return jnp.where(dist < max_exact, dist, jnp.minimum(large, N_BUCKETS - 1))


def _bias_tile(rel_bias, group):
    window, dilation = DILATION_PATTERN[group]
    steps = window // dilation
    assert steps == Q_BLOCK
    heads = slice(group * HEADS_PER_GROUP, (group + 1) * HEADS_PER_GROUP)
    back = jnp.arange(steps, -1, -1, dtype=jnp.int32)
    vals = rel_bias[_t5_causal_bucket(back * dilation)][:, heads].astype(F32).T
    period = 3 * Q_BLOCK
    v = jnp.concatenate([vals, jnp.full((HEADS_PER_GROUP, period - steps - 1), NEG_INF, F32)], axis=1)
    flat = jnp.tile(v, (1, Q_BLOCK))[:, :Q_BLOCK * (period - 1)]
    tile = flat.reshape(HEADS_PER_GROUP, Q_BLOCK, period - 1)[:, :, :2 * Q_BLOCK]
    col = lax.broadcasted_iota(jnp.int32, tile.shape, 2)
    return jnp.concatenate([tile, jnp.where(col < Q_BLOCK, NEG_INF, tile)], axis=0)


LSE_LANES = LANES // HEADS_PER_GROUP


def _attn_kernel(q_ref, kc_ref, kp_ref, vc_ref, vp_ref, bias_ref, o_ref, lse_ref, kf_ref, vf_ref, *, tq, n_res):
    scale = HEAD_DIM ** -0.5
    first_tile = pl.program_id(2) == 0
    lane_head = lax.broadcasted_iota(jnp.int32, (Q_BLOCK, LANES), 1) // LSE_LANES

    for rr in range(n_res):
        kf_ref[rr, 0:Q_BLOCK, :] = kp_ref[rr]
        kf_ref[rr, Q_BLOCK:, :] = kc_ref[rr]
        vf_ref[rr, 0:Q_BLOCK, :] = vp_ref[rr]
        vf_ref[rr, Q_BLOCK:, :] = vc_ref[rr]

        def block(jb, _, rr=rr):
            r0 = pl.multiple_of(jb * Q_BLOCK, Q_BLOCK)
            bias_set = jnp.logical_and(first_tile, jb == 0).astype(jnp.int32) * HEADS_PER_GROUP
            lse = jnp.zeros((Q_BLOCK, LANES), F32)
            for h in range(HEADS_PER_GROUP):
                hs = slice(h * HEAD_DIM, (h + 1) * HEAD_DIM)
                q = q_ref[rr, pl.ds(r0, Q_BLOCK), hs]
                k2 = kf_ref[rr, pl.ds(r0, 2 * Q_BLOCK), hs]
                v2 = vf_ref[rr, pl.ds(r0, 2 * Q_BLOCK), hs]
                s = lax.dot_general(q, k2, (((1,), (1,)), ((), ())), preferred_element_type=F32)
                s = s * scale + bias_ref[bias_set + h]
                m = jnp.max(s, axis=-1, keepdims=True)
                p = jnp.exp(s - m)
                l = jnp.sum(p, axis=-1, keepdims=True)
                o = jnp.dot(p.astype(BF16), v2, preferred_element_type=F32) / l
                o_ref[rr, pl.ds(r0, Q_BLOCK), hs] = o.astype(o_ref.dtype)
                lse = jnp.where(lane_head == h, m + jnp.log(l), lse)
            lse_ref[rr, pl.ds(r0, Q_BLOCK), :] = lse
            return 0

        lax.fori_loop(0, tq // Q_BLOCK, block, 0, unroll=min(4, tq // Q_BLOCK))


ATTN_ROWS = 1024


def _attention_group(qkv, bias, group, tile0=0):
    _, bsz, d, lc, _ = qkv.shape
    tq = min(lc, ATTN_ROWS)
    n_res = min(d, ATTN_ROWS // tq)
    per_tq = tq // Q_BLOCK
    cur = lambda which: pl.BlockSpec((None, None, n_res, tq, PROJ_TILE),
                                     lambda b, r, i: (tile0 + which, b, r, i, 0))
    prev = lambda which: pl.BlockSpec((None, None, n_res, Q_BLOCK, PROJ_TILE),
                                      lambda b, r, i: (tile0 + which, b, r, jnp.maximum(i * per_tq - 1, 0), 0))
    out_spec = lambda width: pl.BlockSpec((None, n_res, tq, width), lambda b, r, i: (b, r, i, 0))
    return pl.pallas_call(
        functools.partial(_attn_kernel, tq=tq, n_res=n_res),
        grid=(bsz, d // n_res, lc // tq),
        in_specs=[cur(0), cur(1), prev(1), cur(2), prev(2),
                  pl.BlockSpec(bias.shape, lambda b, r, i: (0, 0, 0))],
        out_specs=[out_spec(ATT_OUT_WIDTH), out_spec(LANES)],
        out_shape=[jax.ShapeDtypeStruct((bsz, d, lc, ATT_OUT_WIDTH), BF16),
                   jax.ShapeDtypeStruct((bsz, d, lc, LANES), F32)],
        scratch_shapes=[pltpu.VMEM((n_res, Q_BLOCK + tq, PROJ_TILE), BF16),
                        pltpu.VMEM((n_res, Q_BLOCK + tq, PROJ_TILE), BF16)],
        compiler_params=_params(("arbitrary", "arbitrary", "arbitrary"), 40),
        name=f"attention_group{group}",
    )(qkv, qkv, qkv, qkv, qkv, bias)


def _merge_kernel(*refs, tm):
    ys_ref = refs[0]
    o_refs = refs[1:1 + N_ATT_GROUPS]
    l_refs = refs[1 + N_ATT_GROUPS:1 + 2 * N_ATT_GROUPS]
    k = 1 + 2 * N_ATT_GROUPS
    n_gate = (len(refs) - k - 7) // 2
    gs_refs = refs[k:k + n_gate]
    ga_refs = refs[k + n_gate:k + 2 * n_gate]
    x_ref, gate_ref, wbs_ref, wba_ref, wout_ref, out_ref, tok_ref = refs[k + 2 * n_gate:]

    def token_order(ref, g):
        dil = DILATION_PATTERN[g][1]
        if dil == 1:
            return ref[0].astype(F32)
        n_chunks = ref.shape[-1] // LANES
        for r in range(dil):
            for ch in range(n_chunks):
                tok_ref[ch, pl.ds(r, tm // dil, stride=dil), :] = (
                    ref[r, :, ch * LANES:(ch + 1) * LANES].astype(F32))
        return jnp.concatenate([tok_ref[ch] for ch in range(n_chunks)], axis=-1)

    lses = [token_order(r, g) for g, r in enumerate(l_refs)]
    m = functools.reduce(jnp.maximum, lses)
    es = [jnp.exp(l - m) for l in lses]
    den = functools.reduce(lambda a, b: a + b, es)

    def per_head(w):
        return jnp.concatenate([jnp.broadcast_to(w[:, h * LSE_LANES:h * LSE_LANES + 1], (tm, HEAD_DIM))
                                for h in range(HEADS_PER_GROUP)], axis=-1)

    y_att = functools.reduce(lambda a, b: a + b,
                             [per_head(e / den) * token_order(r, g) for g, (e, r) in enumerate(zip(es, o_refs))])

    m_ssm = jnp.dot(ys_ref[...], wbs_ref[...], preferred_element_type=F32)
    m_att = jnp.dot(y_att.astype(BF16), wba_ref[...], preferred_element_type=F32)
    g_ssm = jnp.concatenate([r[...] for r in gs_refs], axis=-1).astype(F32)
    g_att = jnp.concatenate([r[...] for r in ga_refs], axis=-1).astype(F32)
    merged = jax.nn.sigmoid(g_ssm) * m_ssm + jax.nn.sigmoid(g_att) * m_att
    mixed = jnp.dot(merged.astype(BF16), wout_ref[...], preferred_element_type=F32)
    out_ref[...] = x_ref[...] + gate_ref[...] * mixed


def _resident(a, layer=None):
    nd = a.ndim
    if layer is None:
        return pl.BlockSpec(a.shape, lambda *_: (0,) * nd, pipeline_mode=pl.Buffered(1))
    return pl.BlockSpec((None,) + a.shape[1:], lambda *_: (layer,) + (0,) * (nd - 1), pipeline_mode=pl.Buffered(1))


def _merge(x, gate, y_ssm, att, main, w_branch_ssm, w_branch_att, w_out, tm=512):
    bsz, seqlen, d = x.shape
    n_gate = d // PROJ_TILE
    row = lambda width: pl.BlockSpec((None, tm, width), lambda b, i: (b, i, 0))
    tile = lambda t: pl.BlockSpec((None, None, tm, PROJ_TILE), lambda b, i: (t, b, i, 0))
    res_major = lambda dil, width: pl.BlockSpec((None, dil, tm // dil, width), lambda b, i: (b, 0, i, 0))
    wbs, wba, wout = (w.astype(BF16) for w in (w_branch_ssm, w_branch_att, w_out))
    os_, ls_ = zip(*att)
    in_specs = ([row(SSM_WIDTH)]
                + [res_major(dil, ATT_OUT_WIDTH) for _, dil in DILATION_PATTERN]
                + [res_major(dil, LANES) for _, dil in DILATION_PATTERN]
                + [tile(MAIN_GATE0 + t) for t in range(2 * n_gate)]
                + [row(d), pl.BlockSpec((None, 1, d), lambda b, i: (b, 0, 0)),
                   _resident(wbs), _resident(wba), _resident(wout)])
    return pl.pallas_call(
        functools.partial(_merge_kernel, tm=tm),
        grid=(bsz, seqlen // tm),
        in_specs=in_specs,
        out_specs=row(d),
        out_shape=jax.ShapeDtypeStruct(x.shape, F32),
        scratch_shapes=[pltpu.VMEM((ATT_OUT_WIDTH // LANES, tm, LANES), F32)],
        compiler_params=_params(("arbitrary", "arbitrary"), 56),
        name="merge",
    )(y_ssm, *os_, *ls_, *([main] * (2 * n_gate)), x, gate[:, None, :], wbs, wba, wout)


def _ffn_kernel(x_ref, g_ref, shift_ref, scale_ref, gate_ref, wg_ref, wu_ref, wd_ref,
                o_ref, h_ref, inv_ref):
    f = pl.program_id(2)

    @pl.when(f == 0)
    def _():
        _norm_modulate_into(h_ref, x_ref, g_ref, shift_ref, scale_ref, inv_ref)
        o_ref[...] = jnp.zeros_like(o_ref)

    h = h_ref[...]
    a = jnp.dot(h, wg_ref[...].astype(BF16), preferred_element_type=F32)
    b = jnp.dot(h, wu_ref[...].astype(BF16), preferred_element_type=F32)
    act = (a * jax.nn.sigmoid(a)) * b
    o_ref[...] += jnp.dot(act.astype(BF16), wd_ref[...].astype(BF16), preferred_element_type=F32)

    @pl.when(f == pl.num_programs(2) - 1)
    def _():
        o_ref[...] = x_ref[...] + gate_ref[...] * o_ref[...]


def _dense_ffn(x, g, shift, scale, gate, w_gate, w_up, w_down, tm=1024, tf=512):
    bsz, seqlen, d = x.shape
    dff = w_gate.shape[1]
    vec = pl.BlockSpec((None, 1, d), lambda b, i, f: (b, 0, 0))
    row = pl.BlockSpec((None, tm, d), lambda b, i, f: (b, i, 0), pipeline_mode=pl.Buffered(1))
    return pl.pallas_call(
        _ffn_kernel,
        grid=(bsz, seqlen // tm, dff // tf),
        in_specs=[pl.BlockSpec((None, tm, d), lambda b, i, f: (b, i, 0)),
                  pl.BlockSpec((1, d), lambda b, i, f: (0, 0)),
                  vec, vec, vec,
                  pl.BlockSpec((d, tf), lambda b, i, f: (0, f)),
                  pl.BlockSpec((d, tf), lambda b, i, f: (0, f)),
                  pl.BlockSpec((tf, d), lambda b, i, f: (f, 0))],
        out_specs=row,
        out_shape=jax.ShapeDtypeStruct(x.shape, F32),
        scratch_shapes=[pltpu.VMEM((tm, d), BF16), pltpu.VMEM((tm, LANES), F32)],
        compiler_params=_params(("arbitrary", "arbitrary", "arbitrary"), 56),
        name="dense_ffn",
    )(x, g.reshape(1, d), shift[:, None, :], scale[:, None, :], gate[:, None, :],
      w_gate.astype(BF16), w_up.astype(BF16), w_down.astype(BF16))


MOE_SUB = 256
SUBS_PER_TILE = 5
MOE_SUPER = SUBS_PER_TILE * MOE_SUB


def _pack_halves(x):
    half = x.shape[-1] // 2
    bits = lambda v: lax.bitcast_convert_type(v.astype(BF16).astype(F32), jnp.uint32)
    return bits(x[:, half:]) | (bits(x[:, :half]) >> 16)


def _unpack_halves(w):
    lo = lax.bitcast_convert_type(w << 16, F32)
    hi = lax.bitcast_convert_type(w & jnp.uint32(0xFFFF0000), F32)
    return lo, hi


def _router_kernel(x_ref, g_ref, shift_ref, scale_ref, w2_ref, h_ref, idx_ref, gate_ref, hf_ref, inv_ref):
    _norm_modulate_into(hf_ref, x_ref, g_ref, shift_ref, scale_ref, inv_ref)
    h = hf_ref[...]
    h_ref[...] = _pack_halves(h)
    h_hi = h.astype(BF16)
    h_lo = (h - h_hi.astype(F32)).astype(BF16)
    prod = (jnp.dot(h_hi, w2_ref[...], preferred_element_type=F32)
            + jnp.dot(h_lo, w2_ref[...], preferred_element_type=F32))
    logits = prod[:, :LANES] + prod[:, LANES:]
    lane_i = lax.broadcasted_iota(jnp.int32, logits.shape, 1)
    lane = lane_i.astype(F32)
    logits = jnp.where(lane_i < N_EXPERTS, logits, -jnp.inf)
    m1 = jnp.max(logits, axis=-1, keepdims=True)
    i1 = jnp.min(jnp.where(logits == m1, lane, float(LANES)), axis=-1, keepdims=True)
    rest = jnp.where(lane == i1, -jnp.inf, logits)
    m2 = jnp.max(rest, axis=-1, keepdims=True)
    i2 = jnp.min(jnp.where(rest == m2, lane, float(LANES)), axis=-1, keepdims=True)
    e2 = jnp.exp(m2 - m1)
    den = 1.0 + e2
    idx_ref[...] = jnp.where(lane_i == 0, i1, jnp.where(lane_i == 1, i2, 0.0)).astype(jnp.int32)
    gate_ref[...] = jnp.where(lane_i == 0, 1.0 / den, jnp.where(lane_i == 1, e2 / den, 0.0))


def _router(x, g, shift, scale, w_router, tm=512):
    bsz, seqlen, d = x.shape
    wr = jnp.zeros((d, LANES), F32).at[:, :N_EXPERTS].set(w_router)
    w_hi = wr.astype(BF16)
    w_lo = (wr - w_hi.astype(F32)).astype(BF16)
    w2 = jnp.concatenate([w_hi, w_lo], axis=1)
    row = lambda width: pl.BlockSpec((None, tm, width), lambda b, i: (b, i, 0))
    vec = pl.BlockSpec((None, 1, d), lambda b, i: (b, 0, 0))
    return pl.pallas_call(
        _router_kernel,
        grid=(bsz, seqlen // tm),
        in_specs=[row(d), pl.BlockSpec((1, d), lambda b, i: (0, 0)), vec, vec,
                  pl.BlockSpec((d, 2 * LANES), lambda b, i: (0, 0))],
        out_specs=[row(d // 2), row(LANES), row(LANES)],
        out_shape=[jax.ShapeDtypeStruct((bsz, seqlen, d // 2), jnp.uint32),
                   jax.ShapeDtypeStruct((bsz, seqlen, LANES), jnp.int32),
                   jax.ShapeDtypeStruct((bsz, seqlen, LANES), F32)],
        scratch_shapes=[pltpu.VMEM((tm, d), F32), pltpu.VMEM((tm, LANES), F32)],
        compiler_params=_params(("arbitrary", "arbitrary"), 40),
        name="moe_router",
    )(x, g.reshape(1, d), shift[:, None, :], scale[:, None, :], w2)


def _moe_dims(n_tok, nf):
    n_sub = n_tok * TOP_K // MOE_SUB + N_EXPERTS
    n_super = (n_sub + (SUBS_PER_TILE - 1) * N_EXPERTS) // SUBS_PER_TILE + 1
    rows_per_step = -(-MOE_SUPER // nf)
    while (rows_per_step * nf) % SUBLANES:
        rows_per_step += 1
    n_fetch = rows_per_step * nf
    return n_sub, n_super, rows_per_step, n_fetch


def _routing_tables(top_expert, n_tok, nf):
    n_sub, n_super, _, n_fetch = _moe_dims(n_tok, nf)
    i32 = jnp.int32
    flat_e = top_expert.reshape(-1)
    onehot = (flat_e[:, None] == jnp.arange(N_EXPERTS, dtype=i32)[None, :]).astype(i32)
    csum = jnp.cumsum(onehot, axis=0)
    rank = jnp.sum(csum * onehot, axis=1) - 1
    counts = csum[-1]
    subs = (counts + MOE_SUB - 1) // MOE_SUB
    pend = jnp.cumsum(subs) * MOE_SUB
    pstart = pend - subs * MOE_SUB
    dest = (pstart[flat_e] + rank).astype(i32)
    flat_token = jnp.arange(n_tok * TOP_K, dtype=i32) // TOP_K
    row_token = jnp.zeros((n_sub * MOE_SUB + n_fetch,), i32).at[dest].set(flat_token)
    supers = (subs + SUBS_PER_TILE - 1) // SUBS_PER_TILE
    send = jnp.cumsum(supers)
    sstart = send - supers
    s = jnp.arange(n_super + 1, dtype=i32)
    e = jnp.minimum(jnp.searchsorted(send, s, side='right'), N_EXPERTS - 1).astype(i32)
    local = s - sstart[e]
    used = s < send[-1]
    tile_expert = jnp.where(used, e, e[jnp.maximum(send[-1] - 1, 0)]).astype(i32)
    n_tiles_e = jnp.maximum(supers[e], 1)
    base, rem = subs[e] // n_tiles_e, subs[e] % n_tiles_e
    first_sub = local * base + jnp.minimum(local, rem)
    tile_row0 = jnp.where(used, pstart[e] + first_sub * MOE_SUB, 0).astype(i32)
    tile_nsub = jnp.where(used, base + (local < rem).astype(i32), 0).astype(i32)
    n_used = send[-1].astype(i32).reshape(1)
    n_sub_used = (pend[-1] // MOE_SUB).astype(i32).reshape(1)
    return dest, row_token, tile_expert, tile_row0, tile_nsub, n_used, n_sub_used


def _moe_kernel(te_ref, row0_ref, nsub_ref, nu_ref, nsu_ref, tok_ref, hp_ref, wg_ref, wu_ref, wd_ref, y_ref,
                xg_ref, xb_ref, acc_ref, yb_ref, gsem, osem, fsem, *, nf, rows_per_step, n_sub_alloc, n_fill):
    s, f = pl.program_id(0), pl.program_id(1)
    n_used = nu_ref[0]
    used = s < n_used
    n_fetch = rows_per_step * nf
    half = xg_ref.shape[-1]

    def row_copy(tile, r):
        tok = tok_ref[row0_ref[tile] + r]
        return pltpu.make_async_copy(hp_ref.at[pl.ds(tok, 1)], xg_ref.at[pl.ds(r, 1)], gsem.at[0])

    def wait_rows():
        pltpu.make_async_copy(hp_ref.at[pl.ds(0, n_fetch)], xg_ref, gsem.at[0]).wait()

    def out_copies(tile):
        r0 = pl.multiple_of(row0_ref[tile], MOE_SUB)
        return [pltpu.make_async_copy(yb_ref.at[pl.ds(k * MOE_SUB, MOE_SUB)],
                                      y_ref.at[pl.ds(r0 + k * MOE_SUB, MOE_SUB)], osem.at[k])
                for k in range(SUBS_PER_TILE)]

    def start_out(tile):
        for k, cp in enumerate(out_copies(tile)):
            pl.when(k < nsub_ref[tile])(cp.start)

    def wait_out(tile):
        for k, cp in enumerate(out_copies(tile)):
            pl.when(k < nsub_ref[tile])(cp.wait)

    @pl.when(jnp.logical_and(s == 0, f == 0))
    def _():
        def body(r, _):
            row_copy(0, r).start()
            return 0
        lax.fori_loop(0, n_fetch, body, 0)

    @pl.when(jnp.logical_and(f == 0, s <= n_used))
    def _():
        wait_rows()

    @pl.when(jnp.logical_and(used, f == 0))
    def _():
        lo, hi = _unpack_halves(xg_ref[0:MOE_SUPER, :])
        xb_ref[:, :half] = lo.astype(BF16)
        xb_ref[:, half:] = hi.astype(BF16)
        acc_ref[...] = jnp.zeros_like(acc_ref)

    def step(n_rows):
        for k in range(rows_per_step):
            row_copy(s + 1, f * rows_per_step + k).start()
        h = xb_ref[0:n_rows, :]
        a = jnp.dot(h, wg_ref[...].astype(BF16), preferred_element_type=F32)
        b = jnp.dot(h, wu_ref[...].astype(BF16), preferred_element_type=F32)
        act = (a * jax.nn.sigmoid(a)) * b
        acc_ref[0:n_rows, :] += jnp.dot(act.astype(BF16), wd_ref[...].astype(BF16), preferred_element_type=F32)

    for n in range(1, SUBS_PER_TILE + 1):
        pl.when(jnp.logical_and(used, nsub_ref[s] == n))(functools.partial(step, n * MOE_SUB))

    @pl.when(jnp.logical_and(used, f == nf - 1))
    def _():
        pl.when(s > 0)(functools.partial(wait_out, s - 1))
        yb_ref[...] = _pack_halves(acc_ref[...])
        start_out(s)

    @pl.when(jnp.logical_and(s == n_used, f == 0))
    def _():
        wait_out(s - 1)
        yb_ref[0:MOE_SUB, :] = jnp.zeros((MOE_SUB, half), yb_ref.dtype)
        fills = []
        for k in range(n_fill):
            sub = nsu_ref[0] + k
            cp = pltpu.make_async_copy(yb_ref.at[pl.ds(0, MOE_SUB)],
                                       y_ref.at[pl.ds(pl.multiple_of(sub * MOE_SUB, MOE_SUB), MOE_SUB)], fsem.at[k])
            fills.append((sub < n_sub_alloc, cp))
        for cond, cp in fills:
            pl.when(cond)(cp.start)
        for cond, cp in fills:
            pl.when(cond)(cp.wait)


def _moe_experts(hp, tables, w_gate, w_up, w_down, tf=512):
    n_tok, half = hp.shape
    d = 2 * half
    dff = w_gate.shape[2]
    nf = dff // tf
    n_sub, n_super, rows_per_step, n_fetch = _moe_dims(n_tok, nf)
    _, row_token, tile_expert, tile_row0, tile_nsub, n_used, n_sub_used = tables
    n_unused_max = n_sub - n_tok * TOP_K // MOE_SUB
    tile_f = lambda s, f, nu: jnp.where(s < nu[0], f, nf - 1)
    w_in_spec = pl.BlockSpec((None, d, tf), lambda s, f, te, r0, ns, nu, nsu, tok: (te[s], 0, tile_f(s, f, nu)))
    w_out_spec = pl.BlockSpec((None, tf, d), lambda s, f, te, r0, ns, nu, nsu, tok: (te[s], tile_f(s, f, nu), 0))
    return pl.pallas_call(
        functools.partial(_moe_kernel, nf=nf, rows_per_step=rows_per_step, n_sub_alloc=n_sub,
                          n_fill=n_unused_max),
        grid_spec=pltpu.PrefetchScalarGridSpec(
            num_scalar_prefetch=6,
            grid=(n_super, nf),
            in_specs=[pl.BlockSpec(memory_space=pl.ANY), w_in_spec, w_in_spec, w_out_spec],
            out_specs=pl.BlockSpec(memory_space=pl.ANY),
            scratch_shapes=[pltpu.VMEM((n_fetch, half), jnp.uint32),
                            pltpu.VMEM((MOE_SUPER, d), BF16),
                            pltpu.VMEM((MOE_SUPER, d), F32),
                            pltpu.VMEM((MOE_SUPER, half), jnp.uint32),
                            pltpu.SemaphoreType.DMA((1,)), pltpu.SemaphoreType.DMA((SUBS_PER_TILE,)),
                            pltpu.SemaphoreType.DMA((n_unused_max,))]),
        out_shape=jax.ShapeDtypeStruct((n_sub * MOE_SUB, half), jnp.uint32),
        compiler_params=_params(("arbitrary", "arbitrary"), 58),
        name="moe_experts",
    )(tile_expert, tile_row0, tile_nsub, n_used, n_sub_used, row_token, hp, w_gate, w_up, w_down)


COMBINE_UNROLL = 8


def _combine_kernel(dest_ref, y_ref, x_ref, gates_ref, gate_f_ref, gn_ref, o_ref, rows_ref, sems,
                    *, tm, n_steps, final_norm):
    t = pl.program_id(0)
    slot = t % 2
    n = tm * TOP_K
    half = rows_ref.shape[-1]

    def row_copy(base, j, row, slot_):
        return pltpu.make_async_copy(y_ref.at[pl.ds(dest_ref[base + j], 1)],
                                     rows_ref.at[slot_, pl.ds(row, 1)], sems.at[slot_])

    def wait_rows(slot_):
        pltpu.make_async_copy(y_ref.at[pl.ds(0, n)], rows_ref.at[slot_], sems.at[slot_]).wait()

    @pl.when(t == 0)
    def _():
        def body(i, _):
            for u in range(COMBINE_UNROLL):
                row = (u % TOP_K) * tm + i * (COMBINE_UNROLL // TOP_K) + u // TOP_K
                row_copy(0, i * COMBINE_UNROLL + u, row, 0).start()
            return 0
        lax.fori_loop(0, n // COMBINE_UNROLL, body, 0)

    wait_rows(slot)
    next_base = jnp.minimum(t + 1, n_steps - 1) * n
    for j in range(n):
        row_copy(next_base, j, (j % TOP_K) * tm + j // TOP_K, 1 - slot).start()

    gates = gates_ref[...]
    g0, g1 = gates[:, 0:1], gates[:, 1:2]
    lo0, hi0 = _unpack_halves(rows_ref[slot, 0:tm, :])
    lo1, hi1 = _unpack_halves(rows_ref[slot, tm:2 * tm, :])
    out_lo = x_ref[:, :half] + gate_f_ref[:, :half] * (g0 * lo0 + g1 * lo1)
    out_hi = x_ref[:, half:] + gate_f_ref[:, half:] * (g0 * hi0 + g1 * hi1)
    if final_norm:
        ssq = jnp.sum(out_lo * out_lo, axis=-1, keepdims=True) + jnp.sum(out_hi * out_hi, axis=-1, keepdims=True)
        inv = lax.rsqrt(ssq / (2 * half) + EPS)
        out_lo = (out_lo * inv) * gn_ref[:, :half]
        out_hi = (out_hi * inv) * gn_ref[:, half:]
    o_ref[:, :half] = out_lo
    o_ref[:, half:] = out_hi
    pl.when(t == n_steps - 1)(functools.partial(wait_rows, 1 - slot))


def _combine(x, y_rows, dest, top_gate, gate_f, final_g, tm=256):
    bsz, seqlen, d = x.shape
    final_norm = final_g is not None
    gn = (final_g if final_norm else jnp.ones((d,), F32)).reshape(1, d)
    per_batch = seqlen // tm
    n_steps = bsz * per_batch
    row = lambda width: pl.BlockSpec((None, tm, width), lambda t, dst: (t // per_batch, t % per_batch, 0))
    return pl.pallas_call(
        functools.partial(_combine_kernel, tm=tm, n_steps=n_steps, final_norm=final_norm),
        grid_spec=pltpu.PrefetchScalarGridSpec(
            num_scalar_prefetch=1,
            grid=(n_steps,),
            in_specs=[pl.BlockSpec(memory_space=pl.ANY), row(d), row(LANES),
                      pl.BlockSpec((None, 1, d), lambda t, dst: (t // per_batch, 0, 0)),
                      pl.BlockSpec((1, d), lambda t, dst: (0, 0))],
            out_specs=row(d),
            scratch_shapes=[pltpu.VMEM((2, TOP_K * tm, d // 2), jnp.uint32), pltpu.SemaphoreType.DMA((2,))]),
        out_shape=jax.ShapeDtypeStruct(x.shape, F32),
        compiler_params=_params(("arbitrary",), 32),
        name="moe_combine",
    )(dest, y_rows, x, top_gate, gate_f[:, None, :], gn)


def _moe_ffn(x, g, shift, scale, gate_f, w_router, w_gate, w_up, w_down, final_g, tf=512):
    bsz, seqlen, d = x.shape
    n_tok = bsz * seqlen
    hp, top_idx, top_gate = _router(x, g, shift, scale, w_router)
    tables = _routing_tables(top_idx[..., :TOP_K], n_tok, w_gate.shape[2] // tf)
    y_rows = _moe_experts(hp.reshape(n_tok, d // 2), tables, w_gate, w_up, w_down, tf=tf)
    return _combine(x, y_rows, tables[0], top_gate, gate_f, final_g)


def _final_norm_kernel(x_ref, g_ref, o_ref):
    x = x_ref[...]
    ms = jnp.mean(x * x, axis=-1, keepdims=True)
    o_ref[...] = (x * lax.rsqrt(ms + EPS)) * g_ref[...]


def _final_norm(x, g, tm=512):
    bsz, seqlen, d = x.shape
    return pl.pallas_call(
        _final_norm_kernel,
        grid=(bsz, seqlen // tm),
        in_specs=[pl.BlockSpec((None, tm, d), lambda b, i: (b, i, 0)),
                  pl.BlockSpec((1, d), lambda b, i: (0, 0))],
        out_specs=pl.BlockSpec((None, tm, d), lambda b, i: (b, i, 0)),
        out_shape=jax.ShapeDtypeStruct(x.shape, F32),
        compiler_params=_params(("arbitrary", "arbitrary"), 32),
        name="final_norm",
    )(x, g.reshape(1, d))


def kernel(x, c, w_mod, b_mod, norm_mix_g, norm_ffn_g, w_in, ssm_a_re, ssm_a_im, ssm_log_dt, ssm_b_re, ssm_b_im, ssm_c_re, ssm_c_im, ssm_d, w_glu, b_glu, rel_bias, w_branch_ssm, w_branch_att, w_out, ffn_w_gate, ffn_w_up, ffn_w_down, moe_router, moe_w_gate, moe_w_up, moe_w_down, final_norm_g):
    depth = w_mod.shape[0]
    mod = _modulation(c, w_mod, b_mod)
    biases = [_bias_tile(rel_bias, g) for g in range(N_ATT_GROUPS)]
    tables = _s5_tables(ssm_a_re, ssm_a_im, ssm_log_dt, ssm_b_re, ssm_b_im, ssm_c_re, ssm_c_im)
    for i in range(depth):
        shift_m, scale_m, gate_m, shift_f, scale_f, gate_f = jnp.split(mod[i], N_MOD, axis=-1)
        main, qkv = _in_projection(x, norm_mix_g[i], shift_m, scale_m, w_in, i)
        y_ssm = _s5_branch(main, tables, i, ssm_d[i], w_glu[i], b_glu[i])
        att = [_attention_group(qkv[g], biases[g], g) for g in range(N_ATT_GROUPS)]
        x = _merge(x, gate_m, y_ssm, att, main, w_branch_ssm[i], w_branch_att[i], w_out[i])
        j = i // 2
        last = i == depth - 1
        if i % 2 == 0:
            x = _dense_ffn(x, norm_ffn_g[i], shift_f, scale_f, gate_f,
                           ffn_w_gate[j], ffn_w_up[j], ffn_w_down[j])
            if last:
                x = _final_norm(x, final_norm_g)
        else:
            x = _moe_ffn(x, norm_ffn_g[i], shift_f, scale_f, gate_f, moe_router[j],
                         moe_w_gate[j], moe_w_up[j], moe_w_down[j], final_norm_g if last else None)
    return x
```

```python
import functools
import math

import jax
import jax.numpy as jnp
from jax import lax
from jax.experimental import pallas as pl
from jax.experimental.pallas import tpu as pltpu

F32 = jnp.float32
BF16 = jnp.bfloat16

LANES = 128
SUBLANES = 8
VMEM_BYTES = 64 * 1024 * 1024

SSM_GROUP = 16
SSM_STATE = 64
SSM_WIDTH = 1024
HEAD_DIM = 128
DILATION_PATTERN = ((128, 1), (512, 4), (2048, 16))
HEADS_PER_GROUP = 4
N_ATT_GROUPS = len(DILATION_PATTERN)
ATT_OUT_WIDTH = HEADS_PER_GROUP * HEAD_DIM
Q_BLOCK = 128
NEG_INF = -1e30
N_BUCKETS = 32
MAX_DISTANCE = 2048
N_EXPERTS = 8
TOP_K = 2
N_MOD = 6
EPS = 1e-6

PROJ_TILE = 512
U_TILE0 = 0
Q_TILE0 = SSM_WIDTH // PROJ_TILE
K_TILE0 = Q_TILE0 + N_ATT_GROUPS
V_TILE0 = K_TILE0 + N_ATT_GROUPS
GS_TILE0 = V_TILE0 + N_ATT_GROUPS
MAIN_GATE0 = Q_TILE0


def _params(dims, vmem_mb):
    return pltpu.CompilerParams(dimension_semantics=dims,
                                vmem_limit_bytes=vmem_mb * 1024 * 1024)


def _norm_modulate(x, g, shift, scale):
    ms = jnp.mean(x * x, axis=-1, keepdims=True)
    y = x * lax.rsqrt(ms + EPS)
    return (y * g) * (1.0 + scale) + shift


NORM_ROWS = 64


def _norm_modulate_into(h_ref, x_ref, g_ref, shift_ref, scale_ref, inv_ref):
    n_blocks = x_ref.shape[0] // NORM_ROWS
    reps = x_ref.shape[1] // LANES

    def block(i):
        return pl.ds(pl.multiple_of(i * NORM_ROWS, NORM_ROWS), NORM_ROWS)

    def rms(i, _):
        x = x_ref[block(i), :]
        inv = lax.rsqrt(jnp.mean(x * x, axis=-1, keepdims=True) + EPS)
        inv_ref[block(i), :] = jnp.broadcast_to(inv, (NORM_ROWS, LANES))
        return 0

    gain = g_ref[...] * (1.0 + scale_ref[...])
    shift = shift_ref[...]

    def scale_rows(i, _):
        y = x_ref[block(i), :] * jnp.concatenate([inv_ref[block(i), :]] * reps, axis=-1)
        h_ref[block(i), :] = (y * gain + shift).astype(h_ref.dtype)
        return 0

    lax.fori_loop(0, n_blocks, rms, 0, unroll=4)
    lax.fori_loop(0, n_blocks, scale_rows, 0)


def _mod_kernel(c_ref, w_ref, b_ref, o_ref):
    c = c_ref[...]
    cond = (c * jax.nn.sigmoid(c)).astype(BF16)
    o_ref[...] = jnp.dot(cond, w_ref[...].astype(BF16),
                         preferred_element_type=F32) + b_ref[...]


def _modulation(c, w_mod, b_mod):
    depth, d, n = w_mod.shape
    bsz = c.shape[0]
    rows = SUBLANES
    c_pad = jnp.zeros((rows, d), F32).at[:bsz].set(c)
    tn = 1536
    out = pl.pallas_call(
        _mod_kernel,
        grid=(depth, n // tn),
        in_specs=[pl.BlockSpec((rows, d), lambda l, j: (0, 0)),
                  pl.BlockSpec((None, d, tn), lambda l, j: (l, 0, j)),
                  pl.BlockSpec((None, 1, tn), lambda l, j: (l, 0, j))],
        out_specs=pl.BlockSpec((None, rows, tn), lambda l, j: (l, 0, j)),
        out_shape=jax.ShapeDtypeStruct((depth, rows, n), F32),
        compiler_params=_params(("arbitrary", "arbitrary"), 40),
        name="modulation",
    )(c_pad, w_mod, b_mod.reshape(depth, 1, n))
    return out[:, :bsz]


DEINTERLEAVE = 4


def _inproj_kernel(x_ref, g_ref, shift_ref, scale_ref, w_ref, main_ref, *rest, tm):
    qkv_refs, (h_ref, res_ref, res2_ref, inv_ref) = rest[:N_ATT_GROUPS], rest[N_ATT_GROUPS:]
    j = pl.program_id(2)

    @pl.when(j == 0)
    def _():
        _norm_modulate_into(h_ref, x_ref, g_ref, shift_ref, scale_ref, inv_ref)

    def project():
        return jnp.dot(h_ref[...], w_ref[...].astype(BF16), preferred_element_type=F32)

    @pl.when(jnp.logical_or(j < Q_TILE0, j >= GS_TILE0))
    def _():
        main_ref[...] = project().astype(BF16)

    for g, (_, dil) in enumerate(DILATION_PATTERN):
        is_g = functools.reduce(jnp.logical_or, [j == t0 + g for t0 in (Q_TILE0, K_TILE0, V_TILE0)])

        @pl.when(is_g)
        def _(g=g, dil=dil):
            res = project()
            if dil == 1:
                qkv_refs[g][0] = res.astype(BF16)
            else:
                n_chunks = PROJ_TILE // LANES
                for ch in range(n_chunks):
                    res_ref[ch] = res[:, ch * LANES:(ch + 1) * LANES]
                if dil <= DEINTERLEAVE:
                    for r in range(dil):
                        rows = [res_ref[ch, pl.ds(r, tm // dil, stride=dil), :] for ch in range(n_chunks)]
                        qkv_refs[g][r] = jnp.concatenate(rows, axis=-1).astype(BF16)
                else:
                    assert dil == DEINTERLEAVE * DEINTERLEAVE
                    sub = tm // DEINTERLEAVE
                    for a in range(DEINTERLEAVE):
                        for ch in range(n_chunks):
                            res2_ref[ch, a * sub:(a + 1) * sub, :] = (
                                res_ref[ch, pl.ds(a, sub, stride=DEINTERLEAVE), :])
                    for a in range(DEINTERLEAVE):
                        for m in range(DEINTERLEAVE):
                            rows = [res2_ref[ch, pl.ds(a * sub + m, tm // dil, stride=DEINTERLEAVE), :]
                                    for ch in range(n_chunks)]
                            qkv_refs[g][a + DEINTERLEAVE * m] = jnp.concatenate(rows, axis=-1).astype(BF16)


def _in_projection(x, g, shift, scale, w_in, layer, tm=1024):
    bsz, seqlen, d = x.shape
    n_tiles = w_in.shape[2] // PROJ_TILE
    n_main = n_tiles - 3 * N_ATT_GROUPS

    def main_map(b, i, j):
        return (jnp.where(j < Q_TILE0, j, jnp.where(j < GS_TILE0, Q_TILE0 - 1, j - 3 * N_ATT_GROUPS)), b, i, 0)

    def qkv_map(g):
        return lambda b, i, j: ((j > Q_TILE0 + g).astype(jnp.int32) + (j > K_TILE0 + g).astype(jnp.int32),
                                b, 0, i, 0)

    qkv_specs = [pl.BlockSpec((None, None, dil, tm // dil, PROJ_TILE), qkv_map(g))
                 for g, (_, dil) in enumerate(DILATION_PATTERN)]
    qkv_shapes = [jax.ShapeDtypeStruct((3, bsz, dil, seqlen // dil, PROJ_TILE), BF16)
                  for _, dil in DILATION_PATTERN]
    outs = pl.pallas_call(
        functools.partial(_inproj_kernel, tm=tm),
        grid=(bsz, seqlen // tm, n_tiles),
        in_specs=[pl.BlockSpec((None, tm, d), lambda b, i, j: (b, i, 0)),
                  pl.BlockSpec((1, d), lambda b, i, j: (0, 0)),
                  pl.BlockSpec((None, 1, d), lambda b, i, j: (b, 0, 0)),
                  pl.BlockSpec((None, 1, d), lambda b, i, j: (b, 0, 0)),
                  pl.BlockSpec((None, d, PROJ_TILE), lambda b, i, j: (layer, 0, j))],
        out_specs=[pl.BlockSpec((None, None, tm, PROJ_TILE), main_map)] + qkv_specs,
        out_shape=[jax.ShapeDtypeStruct((n_main, bsz, seqlen, PROJ_TILE), BF16)] + qkv_shapes,
        scratch_shapes=[pltpu.VMEM((tm, d), BF16), pltpu.VMEM((PROJ_TILE // LANES, tm, LANES), F32),
                        pltpu.VMEM((PROJ_TILE // LANES, tm, LANES), F32), pltpu.VMEM((tm, LANES), F32)],
        compiler_params=_params(("arbitrary", "arbitrary", "arbitrary"), 52),
        name="in_projection",
    )(x, g.reshape(1, d), shift[:, None, :], scale[:, None, :], w_in)
    return outs[0], outs[1:]


SSM_GB = 16
SSM_NGB = (SSM_WIDTH // SSM_GROUP) // SSM_GB
SSM_GB_IN = SSM_GB * SSM_GROUP
SSM_GB_RE = SSM_GB * SSM_STATE
SSM_CB = 2 * SSM_GB_RE // LANES
SSM_SLOTS = SSM_NGB * SSM_CB + SUBLANES


def _s5_kernel(ua0_ref, ua1_ref, un0_ref, un1_ref, bm_ref, cm_ref, ar_ref, ai_ref, d_ref, wglu_ref, bglu_ref,
               o_ref, xs0_ref, xs1_ref, ug_ref, st_ref, *, bsz, tt):
    xs_ref = (xs0_ref, xs1_ref)
    rows = bsz * tt
    n_nt = bm_ref.shape[1]
    n_pieces = SSM_NGB * n_nt
    unroll = 4
    assert tt == unroll * n_pieces
    half_cb = SSM_CB // 2

    def pair_halves(c):
        return tuple(r[:, c * tt:(c + 1) * tt, :].reshape(rows, PROJ_TILE) for r in (ua0_ref, ua1_ref))

    def stage_u(halves):
        for gb in range(SSM_NGB):
            half, off = divmod(gb * SSM_GB_IN, PROJ_TILE)
            ug_ref[gb] = halves[half][:, off:off + SSM_GB_IN]

    def b_piece(i, dst):
        gb, nt = i // n_nt, i % n_nt
        res = jnp.dot(ug_ref[gb], bm_ref[gb, nt], preferred_element_type=F32)
        for j in range(2):
            xs_ref[dst][pl.ds(gb * SSM_CB + 2 * nt + j, rows, stride=SSM_SLOTS), :] = (
                res[:, j * LANES:(j + 1) * LANES])

    a_re = [ar_ref[gb] for gb in range(SSM_NGB)]
    a_im = [ai_ref[gb] for gb in range(SSM_NGB)]

    def time_step(t, carry, src):
        new = []
        for b in range(bsz):
            for gb in range(SSM_NGB):
                k = (b * SSM_NGB + gb) * 2
                s_re, s_im = carry[k], carry[k + 1]
                row = (b * tt + t) * SSM_SLOTS + gb * SSM_CB
                n_re = a_re[gb] * s_re - a_im[gb] * s_im + xs_ref[src][pl.ds(row, half_cb), :]
                n_im = a_re[gb] * s_im + a_im[gb] * s_re + xs_ref[src][pl.ds(row + half_cb, half_cb), :]
                xs_ref[src][pl.ds(row, half_cb), :] = n_re
                xs_ref[src][pl.ds(row + half_cb, half_cb), :] = n_im
                new += [n_re, n_im]
        return tuple(new)

    n_carry = bsz * SSM_NGB * 2

    def scan_chunk(src):
        carry = tuple(st_ref[k] for k in range(n_carry))
        for it in range(n_pieces):
            b_piece(it, 1 - src)
            for k in range(unroll):
                carry = time_step(it * unroll + k, carry, src)
        for k in range(n_carry):
            st_ref[k] = carry[k]

    def finish_chunk(c):
        ys = []
        for gb in range(SSM_NGB):
            cols = [xs_ref[c][pl.ds(gb * SSM_CB + cb, rows, stride=SSM_SLOTS), :] for cb in range(SSM_CB)]
            xg = jnp.concatenate(cols, axis=-1).astype(BF16)
            ys.append(jnp.dot(xg, cm_ref[gb], preferred_element_type=F32))
        y = jnp.concatenate(ys, axis=-1)
        u32 = jnp.concatenate(pair_halves(c), axis=-1).astype(F32)
        y = jax.nn.gelu(y + d_ref[...] * u32)
        z = jnp.dot(y.astype(BF16), wglu_ref[...], preferred_element_type=F32) + bglu_ref[...]
        o_ref[:, c * tt:(c + 1) * tt, :] = (y * jax.nn.sigmoid(z)).astype(o_ref.dtype).reshape(bsz, tt, SSM_WIDTH)

    @pl.when(pl.program_id(0) == 0)
    def _():
        st_ref[...] = jnp.zeros_like(st_ref)
        stage_u(pair_halves(0))
        for i in range(n_pieces):
            b_piece(i, 0)

    stage_u(pair_halves(1))
    scan_chunk(0)
    finish_chunk(0)
    stage_u(tuple(r[...].reshape(rows, PROJ_TILE) for r in (un0_ref, un1_ref)))
    scan_chunk(1)
    finish_chunk(1)


def _block_diag(m):
    ngb, rows, c = m.shape
    r = rows // SSM_GB
    row_group = lax.broadcasted_iota(jnp.int32, (rows, SSM_GB * c), 0) // r
    col_group = lax.broadcasted_iota(jnp.int32, (rows, SSM_GB * c), 1) // c
    return jnp.where(row_group == col_group, jnp.tile(m, (1, 1, SSM_GB)), 0.0)


def _s5_tables(a_re, a_im, log_dt, b_re, b_im, c_re, c_im):
    depth = a_re.shape[0]
    lam = lax.complex(a_re.astype(F32), a_im.astype(F32))
    dt = jnp.exp(log_dt.astype(F32))[..., None]
    a_bar = jnp.exp(lam * dt)
    b_bar = ((a_bar - 1.0) / lam)[..., None] * lax.complex(b_re.astype(F32), b_im.astype(F32))
    _, g, p, h = b_bar.shape
    n = depth * SSM_NGB
    bt = jnp.transpose(b_bar, (0, 1, 3, 2)).reshape(n, SSM_GB * h, p)
    bm = jnp.concatenate([_block_diag(bt.real), _block_diag(bt.imag)], axis=-1)
    ct_re = jnp.transpose(c_re.astype(F32), (0, 1, 3, 2)).reshape(n, SSM_GB * p, h)
    ct_im = jnp.transpose(c_im.astype(F32), (0, 1, 3, 2)).reshape(n, SSM_GB * p, h)
    cm = jnp.concatenate([_block_diag(ct_re), -_block_diag(ct_im)], axis=1)
    lead = (depth, SSM_NGB)
    return (bm.astype(BF16).reshape(lead + bm.shape[1:]), cm.astype(BF16).reshape(lead + cm.shape[1:]),
            a_bar.real.reshape(lead + (SSM_CB // 2, LANES)), a_bar.imag.reshape(lead + (SSM_CB // 2, LANES)))


def _s5_branch(proj, tables, layer, d_skip, w_glu, b_glu, tt=128):
    _, bsz, seqlen, _ = proj.shape
    bm, cm, ar, ai = tables
    n_nt = bm.shape[-1] // SSM_GB_IN
    bm = jnp.swapaxes(bm.reshape(bm.shape[:3] + (n_nt, SSM_GB_IN)), 2, 3)
    d_skip = d_skip.reshape(1, SSM_WIDTH).astype(F32)
    w_glu = w_glu.astype(BF16)
    b_glu = b_glu.reshape(1, SSM_WIDTH).astype(F32)
    n_chunks = seqlen // tt
    pair = lambda tile: pl.BlockSpec((None, bsz, 2 * tt, PROJ_TILE), lambda s: (tile, 0, s, 0))
    nxt = lambda tile: pl.BlockSpec((None, bsz, tt, PROJ_TILE),
                                    lambda s: (tile, 0, jnp.minimum(2 * s + 2, n_chunks - 1), 0))
    return pl.pallas_call(
        functools.partial(_s5_kernel, bsz=bsz, tt=tt),
        grid=(n_chunks // 2,),
        in_specs=[pair(U_TILE0), pair(U_TILE0 + 1), nxt(U_TILE0), nxt(U_TILE0 + 1),
                  _resident(bm, layer), _resident(cm, layer), _resident(ar, layer), _resident(ai, layer),
                  _resident(d_skip), _resident(w_glu), _resident(b_glu)],
        out_specs=pl.BlockSpec((bsz, 2 * tt, SSM_WIDTH), lambda s: (0, s, 0)),
        out_shape=jax.ShapeDtypeStruct((bsz, seqlen, SSM_WIDTH), BF16),
        scratch_shapes=[pltpu.VMEM((bsz * tt * SSM_SLOTS, LANES), F32),
                        pltpu.VMEM((bsz * tt * SSM_SLOTS, LANES), F32),
                        pltpu.VMEM((SSM_NGB, bsz * tt, SSM_GB_IN), BF16),
                        pltpu.VMEM((bsz * SSM_NGB * 2, SSM_CB // 2, LANES), F32)],
        compiler_params=_params(("arbitrary",), 48),
        name="s5_branch",
    )(proj, proj, proj, proj, bm, cm, ar, ai, d_skip, w_glu, b_glu)


def _t5_causal_bucket(dist):
    max_exact = N_BUCKETS // 2
    d32 = jnp.maximum(dist, 1).astype(F32)
    large = max_exact + (jnp.log(d32 / max_exact) / math.log(MAX_DISTANCE / max_exact)
                         * (N_BUCKETS - max_exact)).astype(jnp.int32)
    return jnp.where(dist < max_exact, dist, jnp.minimum(large, N_BUCKETS - 1))


def _bias_tile(rel_bias, group):
    window, dilation = DILATION_PATTERN[group]
    steps = window // dilation
    assert steps == Q_BLOCK
    heads = slice(group * HEADS_PER_GROUP, (group + 1) * HEADS_PER_GROUP)
    back = jnp.arange(steps, -1, -1, dtype=jnp.int32)
    vals = rel_bias[_t5_causal_bucket(back * dilation)][:, heads].astype(F32).T
    period = 3 * Q_BLOCK
    v = jnp.concatenate([vals, jnp.full((HEADS_PER_GROUP, period - steps - 1), NEG_INF, F32)], axis=1)
    flat = jnp.tile(v, (1, Q_BLOCK))[:, :Q_BLOCK * (period - 1)]
    tile = flat.reshape(HEADS_PER_GROUP, Q_BLOCK, period - 1)[:, :, :2 * Q_BLOCK]
    col = lax.broadcasted_iota(jnp.int32, tile.shape, 2)
    return jnp.concatenate([tile, jnp.where(col < Q_BLOCK, NEG_INF, tile)], axis=0)


LSE_LANES = LANES // HEADS_PER_GROUP


def _attn_kernel(q_ref, kc_ref, kp_ref, vc_ref, vp_ref, bias_ref, o_ref, lse_ref, kf_ref, vf_ref, *, tq, n_res):
    scale = HEAD_DIM ** -0.5
    first_tile = pl.program_id(2) == 0
    lane_head = lax.broadcasted_iota(jnp.int32, (Q_BLOCK, LANES), 1) // LSE_LANES

    for rr in range(n_res):
        kf_ref[rr, 0:Q_BLOCK, :] = kp_ref[rr]
        kf_ref[rr, Q_BLOCK:, :] = kc_ref[rr]
        vf_ref[rr, 0:Q_BLOCK, :] = vp_ref[rr]
        vf_ref[rr, Q_BLOCK:, :] = vc_ref[rr]

        def block(jb, _, rr=rr):
            r0 = pl.multiple_of(jb * Q_BLOCK, Q_BLOCK)
            bias_set = jnp.logical_and(first_tile, jb == 0).astype(jnp.int32) * HEADS_PER_GROUP
            lse = jnp.zeros((Q_BLOCK, LANES), F32)
            for h in range(HEADS_PER_GROUP):
                hs = slice(h * HEAD_DIM, (h + 1) * HEAD_DIM)
                q = q_ref[rr, pl.ds(r0, Q_BLOCK), hs]
                k2 = kf_ref[rr, pl.ds(r0, 2 * Q_BLOCK), hs]
                v2 = vf_ref[rr, pl.ds(r0, 2 * Q_BLOCK), hs]
                s = lax.dot_general(q, k2, (((1,), (1,)), ((), ())), preferred_element_type=F32)
                s = s * scale + bias_ref[bias_set + h]
                m = jnp.max(s, axis=-1, keepdims=True)
                p = jnp.exp(s - m)
                l = jnp.sum(p, axis=-1, keepdims=True)
                o = jnp.dot(p.astype(BF16), v2, preferred_element_type=F32) / l
                o_ref[rr, pl.ds(r0, Q_BLOCK), hs] = o.astype(o_ref.dtype)
                lse = jnp.where(lane_head == h, m + jnp.log(l), lse)
            lse_ref[rr, pl.ds(r0, Q_BLOCK), :] = lse
            return 0

        lax.fori_loop(0, tq // Q_BLOCK, block, 0, unroll=min(4, tq // Q_BLOCK))


ATTN_ROWS = 1024


def _attention_group(qkv, bias, group, tile0=0):
    _, bsz, d, lc, _ = qkv.shape
    tq = min(lc, ATTN_ROWS)
    n_res = min(d, ATTN_ROWS // tq)
    per_tq = tq // Q_BLOCK
    cur = lambda which: pl.BlockSpec((None, None, n_res, tq, PROJ_TILE),
                                     lambda b, r, i: (tile0 + which, b, r, i, 0))
    prev = lambda which: pl.BlockSpec((None, None, n_res, Q_BLOCK, PROJ_TILE),
                                      lambda b, r, i: (tile0 + which, b, r, jnp.maximum(i * per_tq - 1, 0), 0))
    out_spec = lambda width: pl.BlockSpec((None, n_res, tq, width), lambda b, r, i: (b, r, i, 0))
    return pl.pallas_call(
        functools.partial(_attn_kernel, tq=tq, n_res=n_res),
        grid=(bsz, d // n_res, lc // tq),
        in_specs=[cur(0), cur(1), prev(1), cur(2), prev(2),
                  pl.BlockSpec(bias.shape, lambda b, r, i: (0, 0, 0))],
        out_specs=[out_spec(ATT_OUT_WIDTH), out_spec(LANES)],
        out_shape=[jax.ShapeDtypeStruct((bsz, d, lc, ATT_OUT_WIDTH), BF16),
                   jax.ShapeDtypeStruct((bsz, d, lc, LANES), F32)],
        scratch_shapes=[pltpu.VMEM((n_res, Q_BLOCK + tq, PROJ_TILE), BF16),
                        pltpu.VMEM((n_res, Q_BLOCK + tq, PROJ_TILE), BF16)],
        compiler_params=_params(("arbitrary", "arbitrary", "arbitrary"), 40),
        name=f"attention_group{group}",
    )(qkv, qkv, qkv, qkv, qkv, bias)


def _merge_kernel(*refs, tm):
    ys_ref = refs[0]
    o_refs = refs[1:1 + N_ATT_GROUPS]
    l_refs = refs[1 + N_ATT_GROUPS:1 + 2 * N_ATT_GROUPS]
    k = 1 + 2 * N_ATT_GROUPS
    n_gate = (len(refs) - k - 7) // 2
    gs_refs = refs[k:k + n_gate]
    ga_refs = refs[k + n_gate:k + 2 * n_gate]
    x_ref, gate_ref, wbs_ref, wba_ref, wout_ref, out_ref, tok_ref = refs[k + 2 * n_gate:]

    def token_order(ref, g):
        dil = DILATION_PATTERN[g][1]
        if dil == 1:
            return ref[0].astype(F32)
        n_chunks = ref.shape[-1] // LANES
        for r in range(dil):
            for ch in range(n_chunks):
                tok_ref[ch, pl.ds(r, tm // dil, stride=dil), :] = (
                    ref[r, :, ch * LANES:(ch + 1) * LANES].astype(F32))
        return jnp.concatenate([tok_ref[ch] for ch in range(n_chunks)], axis=-1)

    lses = [token_order(r, g) for g, r in enumerate(l_refs)]
    m = functools.reduce(jnp.maximum, lses)
    es = [jnp.exp(l - m) for l in lses]
    den = functools.reduce(lambda a, b: a + b, es)

    def per_head(w):
        return jnp.concatenate([jnp.broadcast_to(w[:, h * LSE_LANES:h * LSE_LANES + 1], (tm, HEAD_DIM))
                                for h in range(HEADS_PER_GROUP)], axis=-1)

    y_att = functools.reduce(lambda a, b: a + b,
                             [per_head(e / den) * token_order(r, g) for g, (e, r) in enumerate(zip(es, o_refs))])

    m_ssm = jnp.dot(ys_ref[...], wbs_ref[...], preferred_element_type=F32)
    m_att = jnp.dot(y_att.astype(BF16), wba_ref[...], preferred_element_type=F32)
    g_ssm = jnp.concatenate([r[...] for r in gs_refs], axis=-1).astype(F32)
    g_att = jnp.concatenate([r[...] for r in ga_refs], axis=-1).astype(F32)
    merged = jax.nn.sigmoid(g_ssm) * m_ssm + jax.nn.sigmoid(g_att) * m_att
    mixed = jnp.dot(merged.astype(BF16), wout_ref[...], preferred_element_type=F32)
    out_ref[...] = x_ref[...] + gate_ref[...] * mixed


def _resident(a, layer=None):
    nd = a.ndim
    if layer is None:
        return pl.BlockSpec(a.shape, lambda *_: (0,) * nd, pipeline_mode=pl.Buffered(1))
    return pl.BlockSpec((None,) + a.shape[1:], lambda *_: (layer,) + (0,) * (nd - 1), pipeline_mode=pl.Buffered(1))


def _merge(x, gate, y_ssm, att, main, w_branch_ssm, w_branch_att, w_out, tm=512):
    bsz, seqlen, d = x.shape
    n_gate = d // PROJ_TILE
    row = lambda width: pl.BlockSpec((None, tm, width), lambda b, i: (b, i, 0))
    tile = lambda t: pl.BlockSpec((None, None, tm, PROJ_TILE), lambda b, i: (t, b, i, 0))
    res_major = lambda dil, width: pl.BlockSpec((None, dil, tm // dil, width), lambda b, i: (b, 0, i, 0))
    wbs, wba, wout = (w.astype(BF16) for w in (w_branch_ssm, w_branch_att, w_out))
    os_, ls_ = zip(*att)
    in_specs = ([row(SSM_WIDTH)]
                + [res_major(dil, ATT_OUT_WIDTH) for _, dil in DILATION_PATTERN]
                + [res_major(dil, LANES) for _, dil in DILATION_PATTERN]
                + [tile(MAIN_GATE0 + t) for t in range(2 * n_gate)]
                + [row(d), pl.BlockSpec((None, 1, d), lambda b, i: (b, 0, 0)),
                   _resident(wbs), _resident(wba), _resident(wout)])
    return pl.pallas_call(
        functools.partial(_merge_kernel, tm=tm),
        grid=(bsz, seqlen // tm),
        in_specs=in_specs,
        out_specs=row(d),
        out_shape=jax.ShapeDtypeStruct(x.shape, F32),
        scratch_shapes=[pltpu.VMEM((ATT_OUT_WIDTH // LANES, tm, LANES), F32)],
        compiler_params=_params(("arbitrary", "arbitrary"), 56),
        name="merge",
    )(y_ssm, *os_, *ls_, *([main] * (2 * n_gate)), x, gate[:, None, :], wbs, wba, wout)


def _ffn_kernel(x_ref, g_ref, shift_ref, scale_ref, gate_ref, wg_ref, wu_ref, wd_ref,
                o_ref, h_ref, inv_ref):
    f = pl.program_id(2)

    @pl.when(f == 0)
    def _():
        _norm_modulate_into(h_ref, x_ref, g_ref, shift_ref, scale_ref, inv_ref)
        o_ref[...] = jnp.zeros_like(o_ref)

    h = h_ref[...]
    a = jnp.dot(h, wg_ref[...].astype(BF16), preferred_element_type=F32)
    b = jnp.dot(h, wu_ref[...].astype(BF16), preferred_element_type=F32)
    act = (a * jax.nn.sigmoid(a)) * b
    o_ref[...] += jnp.dot(act.astype(BF16), wd_ref[...].astype(BF16), preferred_element_type=F32)

    @pl.when(f == pl.num_programs(2) - 1)
    def _():
        o_ref[...] = x_ref[...] + gate_ref[...] * o_ref[...]


def _dense_ffn(x, g, shift, scale, gate, w_gate, w_up, w_down, tm=1024, tf=512):
    bsz, seqlen, d = x.shape
    dff = w_gate.shape[1]
    vec = pl.BlockSpec((None, 1, d), lambda b, i, f: (b, 0, 0))
    row = pl.BlockSpec((None, tm, d), lambda b, i, f: (b, i, 0), pipeline_mode=pl.Buffered(1))
    return pl.pallas_call(
        _ffn_kernel,
        grid=(bsz, seqlen // tm, dff // tf),
        in_specs=[pl.BlockSpec((None, tm, d), lambda b, i, f: (b, i, 0)),
                  pl.BlockSpec((1, d), lambda b, i, f: (0, 0)),
                  vec, vec, vec,
                  pl.BlockSpec((d, tf), lambda b, i, f: (0, f)),
                  pl.BlockSpec((d, tf), lambda b, i, f: (0, f)),
                  pl.BlockSpec((tf, d), lambda b, i, f: (f, 0))],
        out_specs=row,
        out_shape=jax.ShapeDtypeStruct(x.shape, F32),
        scratch_shapes=[pltpu.VMEM((tm, d), BF16), pltpu.VMEM((tm, LANES), F32)],
        compiler_params=_params(("arbitrary", "arbitrary", "arbitrary"), 56),
        name="dense_ffn",
    )(x, g.reshape(1, d), shift[:, None, :], scale[:, None, :], gate[:, None, :],
      w_gate.astype(BF16), w_up.astype(BF16), w_down.astype(BF16))


MOE_SUB = 256
SUBS_PER_TILE = 5
MOE_SUPER = SUBS_PER_TILE * MOE_SUB


def _pack_halves(x):
    half = x.shape[-1] // 2
    bits = lambda v: lax.bitcast_convert_type(v.astype(BF16).astype(F32), jnp.uint32)
    return bits(x[:, half:]) | (bits(x[:, :half]) >> 16)


def _unpack_halves(w):
    lo = lax.bitcast_convert_type(w << 16, F32)
    hi = lax.bitcast_convert_type(w & jnp.uint32(0xFFFF0000), F32)
    return lo, hi


def _router_kernel(x_ref, g_ref, shift_ref, scale_ref, w2_ref, h_ref, idx_ref, gate_ref, hf_ref, inv_ref):
    _norm_modulate_into(hf_ref, x_ref, g_ref, shift_ref, scale_ref, inv_ref)
    h = hf_ref[...]
    h_ref[...] = _pack_halves(h)
    h_hi = h.astype(BF16)
    h_lo = (h - h_hi.astype(F32)).astype(BF16)
    prod = (jnp.dot(h_hi, w2_ref[...], preferred_element_type=F32)
            + jnp.dot(h_lo, w2_ref[...], preferred_element_type=F32))
    logits = prod[:, :LANES] + prod[:, LANES:]
    lane_i = lax.broadcasted_iota(jnp.int32, logits.shape, 1)
    lane = lane_i.astype(F32)
    logits = jnp.where(lane_i < N_EXPERTS, logits, -jnp.inf)
    m1 = jnp.max(logits, axis=-1, keepdims=True)
    i1 = jnp.min(jnp.where(logits == m1, lane, float(LANES)), axis=-1, keepdims=True)
    rest = jnp.where(lane == i1, -jnp.inf, logits)
    m2 = jnp.max(rest, axis=-1, keepdims=True)
    i2 = jnp.min(jnp.where(rest == m2, lane, float(LANES)), axis=-1, keepdims=True)
    e2 = jnp.exp(m2 - m1)
    den = 1.0 + e2
    idx_ref[...] = jnp.where(lane_i == 0, i1, jnp.where(lane_i == 1, i2, 0.0)).astype(jnp.int32)
    gate_ref[...] = jnp.where(lane_i == 0, 1.0 / den, jnp.where(lane_i == 1, e2 / den, 0.0))


def _router(x, g, shift, scale, w_router, tm=512):
    bsz, seqlen, d = x.shape
    wr = jnp.zeros((d, LANES), F32).at[:, :N_EXPERTS].set(w_router)
    w_hi = wr.astype(BF16)
    w_lo = (wr - w_hi.astype(F32)).astype(BF16)
    w2 = jnp.concatenate([w_hi, w_lo], axis=1)
    row = lambda width: pl.BlockSpec((None, tm, width), lambda b, i: (b, i, 0))
    vec = pl.BlockSpec((None, 1, d), lambda b, i: (b, 0, 0))
    return pl.pallas_call(
        _router_kernel,
        grid=(bsz, seqlen // tm),
        in_specs=[row(d), pl.BlockSpec((1, d), lambda b, i: (0, 0)), vec, vec,
                  pl.BlockSpec((d, 2 * LANES), lambda b, i: (0, 0))],
        out_specs=[row(d // 2), row(LANES), row(LANES)],
        out_shape=[jax.ShapeDtypeStruct((bsz, seqlen, d // 2), jnp.uint32),
                   jax.ShapeDtypeStruct((bsz, seqlen, LANES), jnp.int32),
                   jax.ShapeDtypeStruct((bsz, seqlen, LANES), F32)],
        scratch_shapes=[pltpu.VMEM((tm, d), F32), pltpu.VMEM((tm, LANES), F32)],
        compiler_params=_params(("arbitrary", "arbitrary"), 40),
        name="moe_router",
    )(x, g.reshape(1, d), shift[:, None, :], scale[:, None, :], w2)


def _moe_dims(n_tok, nf):
    n_sub = n_tok * TOP_K // MOE_SUB + N_EXPERTS
    n_super = (n_sub + (SUBS_PER_TILE - 1) * N_EXPERTS) // SUBS_PER_TILE + 1
    rows_per_step = -(-MOE_SUPER // nf)
    while (rows_per_step * nf) % SUBLANES:
        rows_per_step += 1
    n_fetch = rows_per_step * nf
    return n_sub, n_super, rows_per_step, n_fetch


def _routing_tables(top_expert, n_tok, nf):
    n_sub, n_super, _, n_fetch = _moe_dims(n_tok, nf)
    i32 = jnp.int32
    flat_e = top_expert.reshape(-1)
    onehot = (flat_e[:, None] == jnp.arange(N_EXPERTS, dtype=i32)[None, :]).astype(i32)
    csum = jnp.cumsum(onehot, axis=0)
    rank = jnp.sum(csum * onehot, axis=1) - 1
    counts = csum[-1]
    subs = (counts + MOE_SUB - 1) // MOE_SUB
    pend = jnp.cumsum(subs) * MOE_SUB
    pstart = pend - subs * MOE_SUB
    dest = (pstart[flat_e] + rank).astype(i32)
    flat_token = jnp.arange(n_tok * TOP_K, dtype=i32) // TOP_K
    row_token = jnp.zeros((n_sub * MOE_SUB + n_fetch,), i32).at[dest].set(flat_token)
    supers = (subs + SUBS_PER_TILE - 1) // SUBS_PER_TILE
    send = jnp.cumsum(supers)
    sstart = send - supers
    s = jnp.arange(n_super + 1, dtype=i32)
    e = jnp.minimum(jnp.searchsorted(send, s, side='right'), N_EXPERTS - 1).astype(i32)
    local = s - sstart[e]
    used = s < send[-1]
    tile_expert = jnp.where(used, e, e[jnp.maximum(send[-1] - 1, 0)]).astype(i32)
    n_tiles_e = jnp.maximum(supers[e], 1)
    base, rem = subs[e] // n_tiles_e, subs[e] % n_tiles_e
    first_sub = local * base + jnp.minimum(local, rem)
    tile_row0 = jnp.where(used, pstart[e] + first_sub * MOE_SUB, 0).astype(i32)
    tile_nsub = jnp.where(used, base + (local < rem).astype(i32), 0).astype(i32)
    n_used = send[-1].astype(i32).reshape(1)
    n_sub_used = (pend[-1] // MOE_SUB).astype(i32).reshape(1)
    return dest, row_token, tile_expert, tile_row0, tile_nsub, n_used, n_sub_used


def _moe_kernel(te_ref, row0_ref, nsub_ref, nu_ref, nsu_ref, tok_ref, hp_ref, wg_ref, wu_ref, wd_ref, y_ref,
                xg_ref, xb_ref, acc_ref, yb_ref, gsem, osem, fsem, *, nf, rows_per_step, n_sub_alloc, n_fill):
    s, f = pl.program_id(0), pl.program_id(1)
    n_used = nu_ref[0]
    used = s < n_used
    n_fetch = rows_per_step * nf
    half = xg_ref.shape[-1]

    def row_copy(tile, r):
        tok = tok_ref[row0_ref[tile] + r]
        return pltpu.make_async_copy(hp_ref.at[pl.ds(tok, 1)], xg_ref.at[pl.ds(r, 1)], gsem.at[0])

    def wait_rows():
        pltpu.make_async_copy(hp_ref.at[pl.ds(0, n_fetch)], xg_ref, gsem.at[0]).wait()

    def out_copies(tile):
        r0 = pl.multiple_of(row0_ref[tile], MOE_SUB)
        return [pltpu.make_async_copy(yb_ref.at[pl.ds(k * MOE_SUB, MOE_SUB)],
                                      y_ref.at[pl.ds(r0 + k * MOE_SUB, MOE_SUB)], osem.at[k])
                for k in range(SUBS_PER_TILE)]

    def start_out(tile):
        for k, cp in enumerate(out_copies(tile)):
            pl.when(k < nsub_ref[tile])(cp.start)

    def wait_out(tile):
        for k, cp in enumerate(out_copies(tile)):
            pl.when(k < nsub_ref[tile])(cp.wait)

    @pl.when(jnp.logical_and(s == 0, f == 0))
    def _():
        def body(r, _):
            row_copy(0, r).start()
            return 0
        lax.fori_loop(0, n_fetch, body, 0)

    @pl.when(jnp.logical_and(f == 0, s <= n_used))
    def _():
        wait_rows()

    @pl.when(jnp.logical_and(used, f == 0))
    def _():
        lo, hi = _unpack_halves(xg_ref[0:MOE_SUPER, :])
        xb_ref[:, :half] = lo.astype(BF16)
        xb_ref[:, half:] = hi.astype(BF16)
        acc_ref[...] = jnp.zeros_like(acc_ref)

    def step(n_rows):
        for k in range(rows_per_step):
            row_copy(s + 1, f * rows_per_step + k).start()
        h = xb_ref[0:n_rows, :]
        a = jnp.dot(h, wg_ref[...].astype(BF16), preferred_element_type=F32)
        b = jnp.dot(h, wu_ref[...].astype(BF16), preferred_element_type=F32)
        act = (a * jax.nn.sigmoid(a)) * b
        acc_ref[0:n_rows, :] += jnp.dot(act.astype(BF16), wd_ref[...].astype(BF16), preferred_element_type=F32)

    for n in range(1, SUBS_PER_TILE + 1):
        pl.when(jnp.logical_and(used, nsub_ref[s] == n))(functools.partial(step, n * MOE_SUB))

    @pl.when(jnp.logical_and(used, f == nf - 1))
    def _():
        pl.when(s > 0)(functools.partial(wait_out, s - 1))
        yb_ref[...] = _pack_halves(acc_ref[...])
        start_out(s)

    @pl.when(jnp.logical_and(s == n_used, f == 0))
    def _():
        wait_out(s - 1)
        yb_ref[0:MOE_SUB, :] = jnp.zeros((MOE_SUB, half), yb_ref.dtype)
        fills = []
        for k in range(n_fill):
            sub = nsu_ref[0] + k
            cp = pltpu.make_async_copy(yb_ref.at[pl.ds(0, MOE_SUB)],
                                       y_ref.at[pl.ds(pl.multiple_of(sub * MOE_SUB, MOE_SUB), MOE_SUB)], fsem.at[k])
            fills.append((sub < n_sub_alloc, cp))
        for cond, cp in fills:
            pl.when(cond)(cp.start)
        for cond, cp in fills:
            pl.when(cond)(cp.wait)


def _moe_experts(hp, tables, w_gate, w_up, w_down, tf=512):
    n_tok, half = hp.shape
    d = 2 * half
    dff = w_gate.shape[2]
    nf = dff // tf
    n_sub, n_super, rows_per_step, n_fetch = _moe_dims(n_tok, nf)
    _, row_token, tile_expert, tile_row0, tile_nsub, n_used, n_sub_used = tables
    n_unused_max = n_sub - n_tok * TOP_K // MOE_SUB
    tile_f = lambda s, f, nu: jnp.where(s < nu[0], f, nf - 1)
    w_in_spec = pl.BlockSpec((None, d, tf), lambda s, f, te, r0, ns, nu, nsu, tok: (te[s], 0, tile_f(s, f, nu)))
    w_out_spec = pl.BlockSpec((None, tf, d), lambda s, f, te, r0, ns, nu, nsu, tok: (te[s], tile_f(s, f, nu), 0))
    return pl.pallas_call(
        functools.partial(_moe_kernel, nf=nf, rows_per_step=rows_per_step, n_sub_alloc=n_sub,
                          n_fill=n_unused_max),
        grid_spec=pltpu.PrefetchScalarGridSpec(
            num_scalar_prefetch=6,
            grid=(n_super, nf),
            in_specs=[pl.BlockSpec(memory_space=pl.ANY), w_in_spec, w_in_spec, w_out_spec],
            out_specs=pl.BlockSpec(memory_space=pl.ANY),
            scratch_shapes=[pltpu.VMEM((n_fetch, half), jnp.uint32),
                            pltpu.VMEM((MOE_SUPER, d), BF16),
                            pltpu.VMEM((MOE_SUPER, d), F32),
                            pltpu.VMEM((MOE_SUPER, half), jnp.uint32),
                            pltpu.SemaphoreType.DMA((1,)), pltpu.SemaphoreType.DMA((SUBS_PER_TILE,)),
                            pltpu.SemaphoreType.DMA((n_unused_max,))]),
        out_shape=jax.ShapeDtypeStruct((n_sub * MOE_SUB, half), jnp.uint32),
        compiler_params=_params(("arbitrary", "arbitrary"), 58),
        name="moe_experts",
    )(tile_expert, tile_row0, tile_nsub, n_used, n_sub_used, row_token, hp, w_gate, w_up, w_down)


COMBINE_UNROLL = 8


def _combine_kernel(dest_ref, y_ref, x_ref, gates_ref, gate_f_ref, gn_ref, o_ref, rows_ref, sems,
                    *, tm, n_steps, final_norm):
    t = pl.program_id(0)
    slot = t % 2
    n = tm * TOP_K
    half = rows_ref.shape[-1]

    def row_copy(base, j, row, slot_):
        return pltpu.make_async_copy(y_ref.at[pl.ds(dest_ref[base + j], 1)],
                                     rows_ref.at[slot_, pl.ds(row, 1)], sems.at[slot_])

    def wait_rows(slot_):
        pltpu.make_async_copy(y_ref.at[pl.ds(0, n)], rows_ref.at[slot_], sems.at[slot_]).wait()

    @pl.when(t == 0)
    def _():
        def body(i, _):
            for u in range(COMBINE_UNROLL):
                row = (u % TOP_K) * tm + i * (COMBINE_UNROLL // TOP_K) + u // TOP_K
                row_copy(0, i * COMBINE_UNROLL + u, row, 0).start()
            return 0
        lax.fori_loop(0, n // COMBINE_UNROLL, body, 0)

    wait_rows(slot)
    next_base = jnp.minimum(t + 1, n_steps - 1) * n
    for j in range(n):
        row_copy(next_base, j, (j % TOP_K) * tm + j // TOP_K, 1 - slot).start()

    gates = gates_ref[...]
    g0, g1 = gates[:, 0:1], gates[:, 1:2]
    lo0, hi0 = _unpack_halves(rows_ref[slot, 0:tm, :])
    lo1, hi1 = _unpack_halves(rows_ref[slot, tm:2 * tm, :])
    out_lo = x_ref[:, :half] + gate_f_ref[:, :half] * (g0 * lo0 + g1 * lo1)
    out_hi = x_ref[:, half:] + gate_f_ref[:, half:] * (g0 * hi0 + g1 * hi1)
    if final_norm:
        ssq = jnp.sum(out_lo * out_lo, axis=-1, keepdims=True) + jnp.sum(out_hi * out_hi, axis=-1, keepdims=True)
        inv = lax.rsqrt(ssq / (2 * half) + EPS)
        out_lo = (out_lo * inv) * gn_ref[:, :half]
        out_hi = (out_hi * inv) * gn_ref[:, half:]
    o_ref[:, :half] = out_lo
    o_ref[:, half:] = out_hi
    pl.when(t == n_steps - 1)(functools.partial(wait_rows, 1 - slot))


def _combine(x, y_rows, dest, top_gate, gate_f, final_g, tm=256):
    bsz, seqlen, d = x.shape
    final_norm = final_g is not None
    gn = (final_g if final_norm else jnp.ones((d,), F32)).reshape(1, d)
    per_batch = seqlen // tm
    n_steps = bsz * per_batch
    row = lambda width: pl.BlockSpec((None, tm, width), lambda t, dst: (t // per_batch, t % per_batch, 0))
    return pl.pallas_call(
        functools.partial(_combine_kernel, tm=tm, n_steps=n_steps, final_norm=final_norm),
        grid_spec=pltpu.PrefetchScalarGridSpec(
            num_scalar_prefetch=1,
            grid=(n_steps,),
            in_specs=[pl.BlockSpec(memory_space=pl.ANY), row(d), row(LANES),
                      pl.BlockSpec((None, 1, d), lambda t, dst: (t // per_batch, 0, 0)),
                      pl.BlockSpec((1, d), lambda t, dst: (0, 0))],
            out_specs=row(d),
            scratch_shapes=[pltpu.VMEM((2, TOP_K * tm, d // 2), jnp.uint32), pltpu.SemaphoreType.DMA((2,))]),
        out_shape=jax.ShapeDtypeStruct(x.shape, F32),
        compiler_params=_params(("arbitrary",), 32),
        name="moe_combine",
    )(dest, y_rows, x, top_gate, gate_f[:, None, :], gn)


def _moe_ffn(x, g, shift, scale, gate_f, w_router, w_gate, w_up, w_down, final_g, tf=512):
    bsz, seqlen, d = x.shape
    n_tok = bsz * seqlen
    hp, top_idx, top_gate = _router(x, g, shift, scale, w_router)
    tables = _routing_tables(top_idx[..., :TOP_K], n_tok, w_gate.shape[2] // tf)
    y_rows = _moe_experts(hp.reshape(n_tok, d // 2), tables, w_gate, w_up, w_down, tf=tf)
    return _combine(x, y_rows, tables[0], top_gate, gate_f, final_g)


def _final_norm_kernel(x_ref, g_ref, o_ref):
    x = x_ref[...]
    ms = jnp.mean(x * x, axis=-1, keepdims=True)
    o_ref[...] = (x * lax.rsqrt(ms + EPS)) * g_ref[...]


def _final_norm(x, g, tm=512):
    bsz, seqlen, d = x.shape
    return pl.pallas_call(
        _final_norm_kernel,
        grid=(bsz, seqlen // tm),
        in_specs=[pl.BlockSpec((None, tm, d), lambda b, i: (b, i, 0)),
                  pl.BlockSpec((1, d), lambda b, i: (0, 0))],
        out_specs=pl.BlockSpec((None, tm, d), lambda b, i: (b, i, 0)),
        out_shape=jax.ShapeDtypeStruct(x.shape, F32),
        compiler_params=_params(("arbitrary", "arbitrary"), 32),
        name="final_norm",
    )(x, g.reshape(1, d))


def kernel(x, c, w_mod, b_mod, norm_mix_g, norm_ffn_g, w_in, ssm_a_re, ssm_a_im, ssm_log_dt, ssm_b_re, ssm_b_im, ssm_c_re, ssm_c_im, ssm_d, w_glu, b_glu, rel_bias, w_branch_ssm, w_branch_att, w_out, ffn_w_gate, ffn_w_up, ffn_w_down, moe_router, moe_w_gate, moe_w_up, moe_w_down, final_norm_g):
    depth = w_mod.shape[0]
    mod = _modulation(c, w_mod, b_mod)
    biases = [_bias_tile(rel_bias, g) for g in range(N_ATT_GROUPS)]
    tables = _s5_tables(ssm_a_re, ssm_a_im, ssm_log_dt, ssm_b_re, ssm_b_im, ssm_c_re, ssm_c_im)
    for i in range(depth):
        shift_m, scale_m, gate_m, shift_f, scale_f, gate_f = jnp.split(mod[i], N_MOD, axis=-1)
        main, qkv = _in_projection(x, norm_mix_g[i], shift_m, scale_m, w_in, i)
        y_ssm = _s5_branch(main, tables, i, ssm_d[i], w_glu[i], b_glu[i])
        att = [_attention_group(qkv[g], biases[g], g) for g in range(N_ATT_GROUPS)]
        x = _merge(x, gate_m, y_ssm, att, main, w_branch_ssm[i], w_branch_att[i], w_out[i])
        j = i // 2
        last = i == depth - 1
        if i % 2 == 0:
            x = _dense_ffn(x, norm_ffn_g[i], shift_f, scale_f, gate_f,
                           ffn_w_gate[j], ffn_w_up[j], ffn_w_down[j])
            if last:
                x = _final_norm(x, final_norm_g)
        else:
            x = _moe_ffn(x, norm_ffn_g[i], shift_f, scale_f, gate_f, moe_router[j],
                         moe_w_gate[j], moe_w_up[j], moe_w_down[j], final_norm_g if last else None)
    return x
```

```python
import functools
import math

import jax
import jax.numpy as jnp
from jax import lax
from jax.experimental import pallas as pl
from jax.experimental.pallas import tpu as pltpu

F32 = jnp.float32
BF16 = jnp.bfloat16

LANES = 128
SUBLANES = 8
VMEM_BYTES = 64 * 1024 * 1024

SSM_GROUP = 16
SSM_STATE = 64
SSM_WIDTH = 1024
HEAD_DIM = 128
DILATION_PATTERN = ((128, 1), (512, 4), (2048, 16))
HEADS_PER_GROUP = 4
N_ATT_GROUPS = len(DILATION_PATTERN)
ATT_OUT_WIDTH = HEADS_PER_GROUP * HEAD_DIM
Q_BLOCK = 128
NEG_INF = -1e30
N_BUCKETS = 32
MAX_DISTANCE = 2048
N_EXPERTS = 8
TOP_K = 2
N_MOD = 6
EPS = 1e-6

PROJ_TILE = 512
U_TILE0 = 0
Q_TILE0 = SSM_WIDTH // PROJ_TILE
K_TILE0 = Q_TILE0 + N_ATT_GROUPS
V_TILE0 = K_TILE0 + N_ATT_GROUPS
GS_TILE0 = V_TILE0 + N_ATT_GROUPS
MAIN_GATE0 = Q_TILE0


def _params(dims, vmem_mb):
    return pltpu.CompilerParams(dimension_semantics=dims,
                                vmem_limit_bytes=vmem_mb * 1024 * 1024)


def _norm_modulate(x, g, shift, scale):
    ms = jnp.mean(x * x, axis=-1, keepdims=True)
    y = x * lax.rsqrt(ms + EPS)
    return (y * g) * (1.0 + scale) + shift


NORM_ROWS = 64


def _norm_modulate_into(h_ref, x_ref, g_ref, shift_ref, scale_ref, inv_ref):
    n_blocks = x_ref.shape[0] // NORM_ROWS
    reps = x_ref.shape[1] // LANES

    def block(i):
        return pl.ds(pl.multiple_of(i * NORM_ROWS, NORM_ROWS), NORM_ROWS)

    def rms(i, _):
        x = x_ref[block(i), :]
        inv = lax.rsqrt(jnp.mean(x * x, axis=-1, keepdims=True) + EPS)
        inv_ref[block(i), :] = jnp.broadcast_to(inv, (NORM_ROWS, LANES))
        return 0

    gain = g_ref[...] * (1.0 + scale_ref[...])
    shift = shift_ref[...]

    def scale_rows(i, _):
        y = x_ref[block(i), :] * jnp.concatenate([inv_ref[block(i), :]] * reps, axis=-1)
        h_ref[block(i), :] = (y * gain + shift).astype(h_ref.dtype)
        return 0

    lax.fori_loop(0, n_blocks, rms, 0, unroll=4)
    lax.fori_loop(0, n_blocks, scale_rows, 0)


def _mod_kernel(c_ref, w_ref, b_ref, o_ref):
    c = c_ref[...]
    cond = (c * jax.nn.sigmoid(c)).astype(BF16)
    o_ref[...] = jnp.dot(cond, w_ref[...].astype(BF16),
                         preferred_element_type=F32) + b_ref[...]


def _modulation(c, w_mod, b_mod):
    depth, d, n = w_mod.shape
    bsz = c.shape[0]
    rows = SUBLANES
    c_pad = jnp.zeros((rows, d), F32).at[:bsz].set(c)
    tn = 1536
    out = pl.pallas_call(
        _mod_kernel,
        grid=(depth, n // tn),
        in_specs=[pl.BlockSpec((rows, d), lambda l, j: (0, 0)),
                  pl.BlockSpec((None, d, tn), lambda l, j: (l, 0, j)),
                  pl.BlockSpec((None, 1, tn), lambda l, j: (l, 0, j))],
        out_specs=pl.BlockSpec((None, rows, tn), lambda l, j: (l, 0, j)),
        out_shape=jax.ShapeDtypeStruct((depth, rows, n), F32),
        compiler_params=_params(("arbitrary", "arbitrary"), 40),
        name="modulation",
    )(c_pad, w_mod, b_mod.reshape(depth, 1, n))
    return out[:, :bsz]


DEINTERLEAVE = 4


def _inproj_kernel(x_ref, g_ref, shift_ref, scale_ref, w_ref, main_ref, *rest, tm):
    qkv_refs, (h_ref, res_ref, res2_ref, inv_ref) = rest[:N_ATT_GROUPS], rest[N_ATT_GROUPS:]
    j = pl.program_id(2)

    @pl.when(j == 0)
    def _():
        _norm_modulate_into(h_ref, x_ref, g_ref, shift_ref, scale_ref, inv_ref)

    def project():
        return jnp.dot(h_ref[...], w_ref[...].astype(BF16), preferred_element_type=F32)

    @pl.when(jnp.logical_or(j < Q_TILE0, j >= GS_TILE0))
    def _():
        main_ref[...] = project().astype(BF16)

    for g, (_, dil) in enumerate(DILATION_PATTERN):
        is_g = functools.reduce(jnp.logical_or, [j == t0 + g for t0 in (Q_TILE0, K_TILE0, V_TILE0)])

        @pl.when(is_g)
        def _(g=g, dil=dil):
            res = project()
            if dil == 1:
                qkv_refs[g][0] = res.astype(BF16)
            else:
                n_chunks = PROJ_TILE // LANES
                for ch in range(n_chunks):
                    res_ref[ch] = res[:, ch * LANES:(ch + 1) * LANES]
                if dil <= DEINTERLEAVE:
                    for r in range(dil):
                        rows = [res_ref[ch, pl.ds(r, tm // dil, stride=dil), :] for ch in range(n_chunks)]
                        qkv_refs[g][r] = jnp.concatenate(rows, axis=-1).astype(BF16)
                else:
                    assert dil == DEINTERLEAVE * DEINTERLEAVE
                    sub = tm // DEINTERLEAVE
                    for a in range(DEINTERLEAVE):
                        for ch in range(n_chunks):
                            res2_ref[ch, a * sub:(a + 1) * sub, :] = (
                                res_ref[ch, pl.ds(a, sub, stride=DEINTERLEAVE), :])
                    for a in range(DEINTERLEAVE):
                        for m in range(DEINTERLEAVE):
                            rows = [res2_ref[ch, pl.ds(a * sub + m, tm // dil, stride=DEINTERLEAVE), :]
                                    for ch in range(n_chunks)]
                            qkv_refs[g][a + DEINTERLEAVE * m] = jnp.concatenate(rows, axis=-1).astype(BF16)


def _in_projection(x, g, shift, scale, w_in, layer, tm=1024):
    bsz, seqlen, d = x.shape
    n_tiles = w_in.shape[2] // PROJ_TILE
    n_main = n_tiles - 3 * N_ATT_GROUPS

    def main_map(b, i, j):
        return (jnp.where(j < Q_TILE0, j, jnp.where(j < GS_TILE0, Q_TILE0 - 1, j - 3 * N_ATT_GROUPS)), b, i, 0)

    def qkv_map(g):
        return lambda b, i, j: ((j > Q_TILE0 + g).astype(jnp.int32) + (j > K_TILE0 + g).astype(jnp.int32),
                                b, 0, i, 0)

    qkv_specs = [pl.BlockSpec((None, None, dil, tm // dil, PROJ_TILE), qkv_map(g))
                 for g, (_, dil) in enumerate(DILATION_PATTERN)]
    qkv_shapes = [jax.ShapeDtypeStruct((3, bsz, dil, seqlen // dil, PROJ_TILE), BF16)
                  for _, dil in DILATION_PATTERN]
    outs = pl.pallas_call(
        functools.partial(_inproj_kernel, tm=tm),
        grid=(bsz, seqlen // tm, n_tiles),
        in_specs=[pl.BlockSpec((None, tm, d), lambda b, i, j: (b, i, 0)),
                  pl.BlockSpec((1, d), lambda b, i, j: (0, 0)),
                  pl.BlockSpec((None, 1, d), lambda b, i, j: (b, 0, 0)),
                  pl.BlockSpec((None, 1, d), lambda b, i, j: (b, 0, 0)),
                  pl.BlockSpec((None, d, PROJ_TILE), lambda b, i, j: (layer, 0, j))],
        out_specs=[pl.BlockSpec((None, None, tm, PROJ_TILE), main_map)] + qkv_specs,
        out_shape=[jax.ShapeDtypeStruct((n_main, bsz, seqlen, PROJ_TILE), BF16)] + qkv_shapes,
        scratch_shapes=[pltpu.VMEM((tm, d), BF16), pltpu.VMEM((PROJ_TILE // LANES, tm, LANES), F32),
                        pltpu.VMEM((PROJ_TILE // LANES, tm, LANES), F32), pltpu.VMEM((tm, LANES), F32)],
        compiler_params=_params(("arbitrary", "arbitrary", "arbitrary"), 52),
        name="in_projection",
    )(x, g.reshape(1, d), shift[:, None, :], scale[:, None, :], w_in)
    return outs[0], outs[1:]


SSM_GB = 16
SSM_NGB = (SSM_WIDTH // SSM_GROUP) // SSM_GB
SSM_GB_IN = SSM_GB * SSM_GROUP
SSM_GB_RE = SSM_GB * SSM_STATE
SSM_CB = 2 * SSM_GB_RE // LANES
SSM_SLOTS = SSM_NGB * SSM_CB + SUBLANES


def _s5_kernel(ua0_ref, ua1_ref, un0_ref, un1_ref, bm_ref, cm_ref, ar_ref, ai_ref, d_ref, wglu_ref, bglu_ref,
               o_ref, xs0_ref, xs1_ref, ug_ref, st_ref, *, bsz, tt):
    xs_ref = (xs0_ref, xs1_ref)
    rows = bsz * tt
    n_nt = bm_ref.shape[1]
    n_pieces = SSM_NGB * n_nt
    unroll = 4
    assert tt == unroll * n_pieces
    half_cb = SSM_CB // 2

    def pair_halves(c):
        return tuple(r[:, c * tt:(c + 1) * tt, :].reshape(rows, PROJ_TILE) for r in (ua0_ref, ua1_ref))

    def stage_u(halves):
        for gb in range(SSM_NGB):
            half, off = divmod(gb * SSM_GB_IN, PROJ_TILE)
            ug_ref[gb] = halves[half][:, off:off + SSM_GB_IN]

    def b_piece(i, dst):
        gb, nt = i // n_nt, i % n_nt
        res = jnp.dot(ug_ref[gb], bm_ref[gb, nt], preferred_element_type=F32)
        for j in range(2):
            xs_ref[dst][pl.ds(gb * SSM_CB + 2 * nt + j, rows, stride=SSM_SLOTS), :] = (
                res[:, j * LANES:(j + 1) * LANES])

    a_re = [ar_ref[gb] for gb in range(SSM_NGB)]
    a_im = [ai_ref[gb] for gb in range(SSM_NGB)]

    def time_step(t, carry, src):
        new = []
        for b in range(bsz):
            for gb in range(SSM_NGB):
                k = (b * SSM_NGB + gb) * 2
                s_re, s_im = carry[k], carry[k + 1]
                row = (b * tt + t) * SSM_SLOTS + gb * SSM_CB
                n_re = a_re[gb] * s_re - a_im[gb] * s_im + xs_ref[src][pl.ds(row, half_cb), :]
                n_im = a_re[gb] * s_im + a_im[gb] * s_re + xs_ref[src][pl.ds(row + half_cb, half_cb), :]
                xs_ref[src][pl.ds(row, half_cb), :] = n_re
                xs_ref[src][pl.ds(row + half_cb, half_cb), :] = n_im
                new += [n_re, n_im]
        return tuple(new)

    n_carry = bsz * SSM_NGB * 2

    def scan_chunk(src):
        carry = tuple(st_ref[k] for k in range(n_carry))
        for it in range(n_pieces):
            b_piece(it, 1 - src)
            for k in range(unroll):
                carry = time_step(it * unroll + k, carry, src)
        for k in range(n_carry):
            st_ref[k] = carry[k]

    def finish_chunk(c):
        ys = []
        for gb in range(SSM_NGB):
            cols = [xs_ref[c][pl.ds(gb * SSM_CB + cb, rows, stride=SSM_SLOTS), :] for cb in range(SSM_CB)]
            xg = jnp.concatenate(cols, axis=-1).astype(BF16)
            ys.append(jnp.dot(xg, cm_ref[gb], preferred_element_type=F32))
        y = jnp.concatenate(ys, axis=-1)
        u32 = jnp.concatenate(pair_halves(c), axis=-1).astype(F32)
        y = jax.nn.gelu(y + d_ref[...] * u32)
        z = jnp.dot(y.astype(BF16), wglu_ref[...], preferred_element_type=F32) + bglu_ref[...]
        o_ref[:, c * tt:(c + 1) * tt, :] = (y * jax.nn.sigmoid(z)).astype(o_ref.dtype).reshape(bsz, tt, SSM_WIDTH)

    @pl.when(pl.program_id(0) == 0)
    def _():
        st_ref[...] = jnp.zeros_like(st_ref)
        stage_u(pair_halves(0))
        for i in range(n_pieces):
            b_piece(i, 0)

    stage_u(pair_halves(1))
    scan_chunk(0)
    finish_chunk(0)
    stage_u(tuple(r[...].reshape(rows, PROJ_TILE) for r in (un0_ref, un1_ref)))
    scan_chunk(1)
    finish_chunk(1)


def _block_diag(m):
    ngb, rows, c = m.shape
    r = rows // SSM_GB
    row_group = lax.broadcasted_iota(jnp.int32, (rows, SSM_GB * c), 0) // r
    col_group = lax.broadcasted_iota(jnp.int32, (rows, SSM_GB * c), 1) // c
    return jnp.where(row_group == col_group, jnp.tile(m, (1, 1, SSM_GB)), 0.0)


def _s5_tables(a_re, a_im, log_dt, b_re, b_im, c_re, c_im):
    depth = a_re.shape[0]
    lam = lax.complex(a_re.astype(F32), a_im.astype(F32))
    dt = jnp.exp(log_dt.astype(F32))[..., None]
    a_bar = jnp.exp(lam * dt)
    b_bar = ((a_bar - 1.0) / lam)[..., None] * lax.complex(b_re.astype(F32), b_im.astype(F32))
    _, g, p, h = b_bar.shape
    n = depth * SSM_NGB
    bt = jnp.transpose(b_bar, (0, 1, 3, 2)).reshape(n, SSM_GB * h, p)
    bm = jnp.concatenate([_block_diag(bt.real), _block_diag(bt.imag)], axis=-1)
    ct_re = jnp.transpose(c_re.astype(F32), (0, 1, 3, 2)).reshape(n, SSM_GB * p, h)
    ct_im = jnp.transpose(c_im.astype(F32), (0, 1, 3, 2)).reshape(n, SSM_GB * p, h)
    cm = jnp.concatenate([_block_diag(ct_re), -_block_diag(ct_im)], axis=1)
    lead = (depth, SSM_NGB)
    return (bm.astype(BF16).reshape(lead + bm.shape[1:]), cm.astype(BF16).reshape(lead + cm.shape[1:]),
            a_bar.real.reshape(lead + (SSM_CB // 2, LANES)), a_bar.imag.reshape(lead + (SSM_CB // 2, LANES)))


def _s5_branch(proj, tables, layer, d_skip, w_glu, b_glu, tt=128):
    _, bsz, seqlen, _ = proj.shape
    bm, cm, ar, ai = tables
    n_nt = bm.shape[-1] // SSM_GB_IN
    bm = jnp.swapaxes(bm.reshape(bm.shape[:3] + (n_nt, SSM_GB_IN)), 2, 3)
    d_skip = d_skip.reshape(1, SSM_WIDTH).astype(F32)
    w_glu = w_glu.astype(BF16)
    b_glu = b_glu.reshape(1, SSM_WIDTH).astype(F32)
    n_chunks = seqlen // tt
    pair = lambda tile: pl.BlockSpec((None, bsz, 2 * tt, PROJ_TILE), lambda s: (tile, 0, s, 0))
    nxt = lambda tile: pl.BlockSpec((None, bsz, tt, PROJ_TILE),
                                    lambda s: (tile, 0, jnp.minimum(2 * s + 2, n_chunks - 1), 0))
    return pl.pallas_call(
        functools.partial(_s5_kernel, bsz=bsz, tt=tt),
        grid=(n_chunks // 2,),
        in_specs=[pair(U_TILE0), pair(U_TILE0 + 1), nxt(U_TILE0), nxt(U_TILE0 + 1),
                  _resident(bm, layer), _resident(cm, layer), _resident(ar, layer), _resident(ai, layer),
                  _resident(d_skip), _resident(w_glu, layer), _resident(b_glu)],
        out_specs=pl.BlockSpec((bsz, 2 * tt, SSM_WIDTH), lambda s: (0, s, 0)),
        out_shape=jax.ShapeDtypeStruct((bsz, seqlen, SSM_WIDTH), BF16),
        scratch_shapes=[pltpu.VMEM((bsz * tt * SSM_SLOTS, LANES), F32),
                        pltpu.VMEM((bsz * tt * SSM_SLOTS, LANES), F32),
                        pltpu.VMEM((SSM_NGB, bsz * tt, SSM_GB_IN), BF16),
                        pltpu.VMEM((bsz * SSM_NGB * 2, SSM_CB // 2, LANES), F32)],
        compiler_params=_params(("arbitrary",), 48),
        name="s5_branch",
    )(proj, proj, proj, proj, bm, cm, ar, ai, d_skip, w_glu, b_glu)


def _t5_causal_bucket(dist):
    max_exact = N_BUCKETS // 2
    d32 = jnp.maximum(dist, 1).astype(F32)
    large = max_exact + (jnp.log(d32 / max_exact) / math.log(MAX_DISTANCE / max_exact)
                         * (N_BUCKETS - max_exact)).astype(jnp.int32)
    return jnp.where(dist < max_exact, dist, jnp.minimum(large, N_BUCKETS - 1))


def _bias_tile(rel_bias, group):
    window, dilation = DILATION_PATTERN[group]
    steps = window // dilation
    assert steps == Q_BLOCK
    heads = slice(group * HEADS_PER_GROUP, (group + 1) * HEADS_PER_GROUP)
    back = jnp.arange(steps, -1, -1, dtype=jnp.int32)
    vals = rel_bias[_t5_causal_bucket(back * dilation)][:, heads].astype(F32).T
    period = 3 * Q_BLOCK
    v = jnp.concatenate([vals, jnp.full((HEADS_PER_GROUP, period - steps - 1), NEG_INF, F32)], axis=1)
    flat = jnp.tile(v, (1, Q_BLOCK))[:, :Q_BLOCK * (period - 1)]
    tile = flat.reshape(HEADS_PER_GROUP, Q_BLOCK, period - 1)[:, :, :2 * Q_BLOCK]
    col = lax.broadcasted_iota(jnp.int32, tile.shape, 2)
    return jnp.concatenate([tile, jnp.where(col < Q_BLOCK, NEG_INF, tile)], axis=0)


LSE_LANES = LANES // HEADS_PER_GROUP


def _attn_kernel(q_ref, kc_ref, kp_ref, vc_ref, vp_ref, bias_ref, o_ref, lse_ref, kf_ref, vf_ref, *, tq, n_res):
    scale = HEAD_DIM ** -0.5
    first_tile = pl.program_id(2) == 0
    lane_head = lax.broadcasted_iota(jnp.int32, (Q_BLOCK, LANES), 1) // LSE_LANES

    for rr in range(n_res):
        kf_ref[rr, 0:Q_BLOCK, :] = kp_ref[rr]
        kf_ref[rr, Q_BLOCK:, :] = kc_ref[rr]
        vf_ref[rr, 0:Q_BLOCK, :] = vp_ref[rr]
        vf_ref[rr, Q_BLOCK:, :] = vc_ref[rr]

        def block(jb, _, rr=rr):
            r0 = pl.multiple_of(jb * Q_BLOCK, Q_BLOCK)
            bias_set = jnp.logical_and(first_tile, jb == 0).astype(jnp.int32) * HEADS_PER_GROUP
            lse = jnp.zeros((Q_BLOCK, LANES), F32)
            for h in range(HEADS_PER_GROUP):
                hs = slice(h * HEAD_DIM, (h + 1) * HEAD_DIM)
                q = q_ref[rr, pl.ds(r0, Q_BLOCK), hs]
                k2 = kf_ref[rr, pl.ds(r0, 2 * Q_BLOCK), hs]
                v2 = vf_ref[rr, pl.ds(r0, 2 * Q_BLOCK), hs]
                s = lax.dot_general(q, k2, (((1,), (1,)), ((), ())), preferred_element_type=F32)
                s = s * scale + bias_ref[bias_set + h]
                m = jnp.max(s, axis=-1, keepdims=True)
                p = jnp.exp(s - m)
                l = jnp.sum(p, axis=-1, keepdims=True)
                o = jnp.dot(p.astype(BF16), v2, preferred_element_type=F32) / l
                o_ref[rr, pl.ds(r0, Q_BLOCK), hs] = o.astype(o_ref.dtype)
                lse = jnp.where(lane_head == h, m + jnp.log(l), lse)
            lse_ref[rr, pl.ds(r0, Q_BLOCK), :] = lse
            return 0

        lax.fori_loop(0, tq // Q_BLOCK, block, 0, unroll=min(4, tq // Q_BLOCK))


ATTN_ROWS = 1024


def _attention_group(qkv, bias, group, tile0=0):
    _, bsz, d, lc, _ = qkv.shape
    tq = min(lc, ATTN_ROWS)
    n_res = min(d, ATTN_ROWS // tq)
    per_tq = tq // Q_BLOCK
    cur = lambda which: pl.BlockSpec((None, None, n_res, tq, PROJ_TILE),
                                     lambda b, r, i: (tile0 + which, b, r, i, 0))
    prev = lambda which: pl.BlockSpec((None, None, n_res, Q_BLOCK, PROJ_TILE),
                                      lambda b, r, i: (tile0 + which, b, r, jnp.maximum(i * per_tq - 1, 0), 0))
    out_spec = lambda width: pl.BlockSpec((None, n_res, tq, width), lambda b, r, i: (b, r, i, 0))
    return pl.pallas_call(
        functools.partial(_attn_kernel, tq=tq, n_res=n_res),
        grid=(bsz, d // n_res, lc // tq),
        in_specs=[cur(0), cur(1), prev(1), cur(2), prev(2),
                  pl.BlockSpec(bias.shape, lambda b, r, i: (0, 0, 0))],
        out_specs=[out_spec(ATT_OUT_WIDTH), out_spec(LANES)],
        out_shape=[jax.ShapeDtypeStruct((bsz, d, lc, ATT_OUT_WIDTH), BF16),
                   jax.ShapeDtypeStruct((bsz, d, lc, LANES), F32)],
        scratch_shapes=[pltpu.VMEM((n_res, Q_BLOCK + tq, PROJ_TILE), BF16),
                        pltpu.VMEM((n_res, Q_BLOCK + tq, PROJ_TILE), BF16)],
        compiler_params=_params(("arbitrary", "arbitrary", "arbitrary"), 40),
        name=f"attention_group{group}",
    )(qkv, qkv, qkv, qkv, qkv, bias)


def _merge_kernel(*refs, tm):
    ys_ref = refs[0]
    o_refs = refs[1:1 + N_ATT_GROUPS]
    l_refs = refs[1 + N_ATT_GROUPS:1 + 2 * N_ATT_GROUPS]
    k = 1 + 2 * N_ATT_GROUPS
    n_gate = (len(refs) - k - 7) // 2
    gs_refs = refs[k:k + n_gate]
    ga_refs = refs[k + n_gate:k + 2 * n_gate]
    x_ref, gate_ref, wbs_ref, wba_ref, wout_ref, out_ref, tok_ref = refs[k + 2 * n_gate:]

    def token_order(ref, g):
        dil = DILATION_PATTERN[g][1]
        if dil == 1:
            return ref[0].astype(F32)
        n_chunks = ref.shape[-1] // LANES
        for r in range(dil):
            for ch in range(n_chunks):
                tok_ref[ch, pl.ds(r, tm // dil, stride=dil), :] = (
                    ref[r, :, ch * LANES:(ch + 1) * LANES].astype(F32))
        return jnp.concatenate([tok_ref[ch] for ch in range(n_chunks)], axis=-1)

    lses = [token_order(r, g) for g, r in enumerate(l_refs)]
    m = functools.reduce(jnp.maximum, lses)
    es = [jnp.exp(l - m) for l in lses]
    den = functools.reduce(lambda a, b: a + b, es)

    def per_head(w):
        return jnp.concatenate([jnp.broadcast_to(w[:, h * LSE_LANES:h * LSE_LANES + 1], (tm, HEAD_DIM))
                                for h in range(HEADS_PER_GROUP)], axis=-1)

    y_att = functools.reduce(lambda a, b: a + b,
                             [per_head(e / den) * token_order(r, g) for g, (e, r) in enumerate(zip(es, o_refs))])

    m_ssm = jnp.dot(ys_ref[...], wbs_ref[...], preferred_element_type=F32)
    m_att = jnp.dot(y_att.astype(BF16), wba_ref[...], preferred_element_type=F32)
    g_ssm = jnp.concatenate([r[...] for r in gs_refs], axis=-1).astype(F32)
    g_att = jnp.concatenate([r[...] for r in ga_refs], axis=-1).astype(F32)
    merged = jax.nn.sigmoid(g_ssm) * m_ssm + jax.nn.sigmoid(g_att) * m_att
    mixed = jnp.dot(merged.astype(BF16), wout_ref[...], preferred_element_type=F32)
    out_ref[...] = x_ref[...] + gate_ref[...] * mixed


def _resident(a, layer=None):
    nd = a.ndim
    if layer is None:
        return pl.BlockSpec(a.shape, lambda *_: (0,) * nd, pipeline_mode=pl.Buffered(1))
    return pl.BlockSpec((None,) + a.shape[1:], lambda *_: (layer,) + (0,) * (nd - 1), pipeline_mode=pl.Buffered(1))


def _merge(x, gate, y_ssm, att, main, w_branch_ssm, w_branch_att, w_out, layer, tm=512):
    bsz, seqlen, d = x.shape
    n_gate = d // PROJ_TILE
    row = lambda width: pl.BlockSpec((None, tm, width), lambda b, i: (b, i, 0))
    tile = lambda t: pl.BlockSpec((None, None, tm, PROJ_TILE), lambda b, i: (t, b, i, 0))
    res_major = lambda dil, width: pl.BlockSpec((None, dil, tm // dil, width), lambda b, i: (b, 0, i, 0))
    wbs, wba, wout = (w.astype(BF16) for w in (w_branch_ssm, w_branch_att, w_out))
    os_, ls_ = zip(*att)
    in_specs = ([row(SSM_WIDTH)]
                + [res_major(dil, ATT_OUT_WIDTH) for _, dil in DILATION_PATTERN]
                + [res_major(dil, LANES) for _, dil in DILATION_PATTERN]
                + [tile(MAIN_GATE0 + t) for t in range(2 * n_gate)]
                + [row(d), pl.BlockSpec((None, 1, d), lambda b, i: (b, 0, 0)),
                   _resident(wbs, layer), _resident(wba, layer), _resident(wout, layer)])
    return pl.pallas_call(
        functools.partial(_merge_kernel, tm=tm),
        grid=(bsz, seqlen // tm),
        in_specs=in_specs,
        out_specs=row(d),
        out_shape=jax.ShapeDtypeStruct(x.shape, F32),
        scratch_shapes=[pltpu.VMEM((ATT_OUT_WIDTH // LANES, tm, LANES), F32)],
        compiler_params=_params(("arbitrary", "arbitrary"), 56),
        name="merge",
    )(y_ssm, *os_, *ls_, *([main] * (2 * n_gate)), x, gate[:, None, :], wbs, wba, wout)


def _ffn_kernel(x_ref, g_ref, shift_ref, scale_ref, gate_ref, wg_ref, wu_ref, wd_ref,
                o_ref, h_ref, inv_ref):
    f = pl.program_id(2)

    @pl.when(f == 0)
    def _():
        _norm_modulate_into(h_ref, x_ref, g_ref, shift_ref, scale_ref, inv_ref)
        o_ref[...] = jnp.zeros_like(o_ref)

    h = h_ref[...]
    a = jnp.dot(h, wg_ref[...].astype(BF16), preferred_element_type=F32)
    b = jnp.dot(h, wu_ref[...].astype(BF16), preferred_element_type=F32)
    act = (a * jax.nn.sigmoid(a)) * b
    o_ref[...] += jnp.dot(act.astype(BF16), wd_ref[...].astype(BF16), preferred_element_type=F32)

    @pl.when(f == pl.num_programs(2) - 1)
    def _():
        o_ref[...] = x_ref[...] + gate_ref[...] * o_ref[...]


def _dense_ffn(x, g, shift, scale, gate, w_gate, w_up, w_down, tm=1024, tf=512):
    bsz, seqlen, d = x.shape
    dff = w_gate.shape[1]
    vec = pl.BlockSpec((None, 1, d), lambda b, i, f: (b, 0, 0))
    row = pl.BlockSpec((None, tm, d), lambda b, i, f: (b, i, 0), pipeline_mode=pl.Buffered(1))
    return pl.pallas_call(
        _ffn_kernel,
        grid=(bsz, seqlen // tm, dff // tf),
        in_specs=[pl.BlockSpec((None, tm, d), lambda b, i, f: (b, i, 0)),
                  pl.BlockSpec((1, d), lambda b, i, f: (0, 0)),
                  vec, vec, vec,
                  pl.BlockSpec((d, tf), lambda b, i, f: (0, f)),
                  pl.BlockSpec((d, tf), lambda b, i, f: (0, f)),
                  pl.BlockSpec((tf, d), lambda b, i, f: (f, 0))],
        out_specs=row,
        out_shape=jax.ShapeDtypeStruct(x.shape, F32),
        scratch_shapes=[pltpu.VMEM((tm, d), BF16), pltpu.VMEM((tm, LANES), F32)],
        compiler_params=_params(("arbitrary", "arbitrary", "arbitrary"), 56),
        name="dense_ffn",
    )(x, g.reshape(1, d), shift[:, None, :], scale[:, None, :], gate[:, None, :],
      w_gate.astype(BF16), w_up.astype(BF16), w_down.astype(BF16))


MOE_SUB = 256
SUBS_PER_TILE = 5
MOE_SUPER = SUBS_PER_TILE * MOE_SUB


def _pack_halves(x):
    half = x.shape[-1] // 2
    bits = lambda v: lax.bitcast_convert_type(v.astype(BF16).astype(F32), jnp.uint32)
    return bits(x[:, half:]) | (bits(x[:, :half]) >> 16)


def _unpack_halves(w):
    lo = lax.bitcast_convert_type(w << 16, F32)
    hi = lax.bitcast_convert_type(w & jnp.uint32(0xFFFF0000), F32)
    return lo, hi


def _router_kernel(x_ref, g_ref, shift_ref, scale_ref, w2_ref, h_ref, idx_ref, gate_ref, hf_ref, inv_ref):
    _norm_modulate_into(hf_ref, x_ref, g_ref, shift_ref, scale_ref, inv_ref)
    h = hf_ref[...]
    h_ref[...] = _pack_halves(h)
    h_hi = h.astype(BF16)
    h_lo = (h - h_hi.astype(F32)).astype(BF16)
    prod = (jnp.dot(h_hi, w2_ref[...], preferred_element_type=F32)
            + jnp.dot(h_lo, w2_ref[...], preferred_element_type=F32))
    logits = prod[:, :LANES] + prod[:, LANES:]
    lane_i = lax.broadcasted_iota(jnp.int32, logits.shape, 1)
    lane = lane_i.astype(F32)
    logits = jnp.where(lane_i < N_EXPERTS, logits, -jnp.inf)
    m1 = jnp.max(logits, axis=-1, keepdims=True)
    i1 = jnp.min(jnp.where(logits == m1, lane, float(LANES)), axis=-1, keepdims=True)
    rest = jnp.where(lane == i1, -jnp.inf, logits)
    m2 = jnp.max(rest, axis=-1, keepdims=True)
    i2 = jnp.min(jnp.where(rest == m2, lane, float(LANES)), axis=-1, keepdims=True)
    e2 = jnp.exp(m2 - m1)
    den = 1.0 + e2
    idx_ref[...] = jnp.where(lane_i == 0, i1, jnp.where(lane_i == 1, i2, 0.0)).astype(jnp.int32)
    gate_ref[...] = jnp.where(lane_i == 0, 1.0 / den, jnp.where(lane_i == 1, e2 / den, 0.0))


def _router(x, g, shift, scale, w_router, tm=512):
    bsz, seqlen, d = x.shape
    wr = jnp.zeros((d, LANES), F32).at[:, :N_EXPERTS].set(w_router)
    w_hi = wr.astype(BF16)
    w_lo = (wr - w_hi.astype(F32)).astype(BF16)
    w2 = jnp.concatenate([w_hi, w_lo], axis=1)
    row = lambda width: pl.BlockSpec((None, tm, width), lambda b, i: (b, i, 0))
    vec = pl.BlockSpec((None, 1, d), lambda b, i: (b, 0, 0))
    return pl.pallas_call(
        _router_kernel,
        grid=(bsz, seqlen // tm),
        in_specs=[row(d), pl.BlockSpec((1, d), lambda b, i: (0, 0)), vec, vec,
                  pl.BlockSpec((d, 2 * LANES), lambda b, i: (0, 0))],
        out_specs=[row(d // 2), row(LANES), row(LANES)],
        out_shape=[jax.ShapeDtypeStruct((bsz, seqlen, d // 2), jnp.uint32),
                   jax.ShapeDtypeStruct((bsz, seqlen, LANES), jnp.int32),
                   jax.ShapeDtypeStruct((bsz, seqlen, LANES), F32)],
        scratch_shapes=[pltpu.VMEM((tm, d), F32), pltpu.VMEM((tm, LANES), F32)],
        compiler_params=_params(("arbitrary", "arbitrary"), 40),
        name="moe_router",
    )(x, g.reshape(1, d), shift[:, None, :], scale[:, None, :], w2)


def _moe_dims(n_tok, nf):
    n_sub = n_tok * TOP_K // MOE_SUB + N_EXPERTS
    n_super = (n_sub + (SUBS_PER_TILE - 1) * N_EXPERTS) // SUBS_PER_TILE + 1
    rows_per_step = -(-MOE_SUPER // nf)
    while (rows_per_step * nf) % SUBLANES:
        rows_per_step += 1
    n_fetch = rows_per_step * nf
    return n_sub, n_super, rows_per_step, n_fetch


def _routing_tables(top_expert, n_tok, nf):
    n_sub, n_super, _, n_fetch = _moe_dims(n_tok, nf)
    i32 = jnp.int32
    flat_e = top_expert.reshape(-1)
    onehot = (flat_e[:, None] == jnp.arange(N_EXPERTS, dtype=i32)[None, :]).astype(i32)
    csum = jnp.cumsum(onehot, axis=0)
    rank = jnp.sum(csum * onehot, axis=1) - 1
    counts = csum[-1]
    subs = (counts + MOE_SUB - 1) // MOE_SUB
    pend = jnp.cumsum(subs) * MOE_SUB
    pstart = pend - subs * MOE_SUB
    dest = (pstart[flat_e] + rank).astype(i32)
    flat_token = jnp.arange(n_tok * TOP_K, dtype=i32) // TOP_K
    row_token = jnp.zeros((n_sub * MOE_SUB + n_fetch,), i32).at[dest].set(flat_token)
    supers = (subs + SUBS_PER_TILE - 1) // SUBS_PER_TILE
    send = jnp.cumsum(supers)
    sstart = send - supers
    s = jnp.arange(n_super + 1, dtype=i32)
    e = jnp.minimum(jnp.searchsorted(send, s, side='right'), N_EXPERTS - 1).astype(i32)
    local = s - sstart[e]
    used = s < send[-1]
    tile_expert = jnp.where(used, e, e[jnp.maximum(send[-1] - 1, 0)]).astype(i32)
    n_tiles_e = jnp.maximum(supers[e], 1)
    base, rem = subs[e] // n_tiles_e, subs[e] % n_tiles_e
    first_sub = local * base + jnp.minimum(local, rem)
    tile_row0 = jnp.where(used, pstart[e] + first_sub * MOE_SUB, 0).astype(i32)
    tile_nsub = jnp.where(used, base + (local < rem).astype(i32), 0).astype(i32)
    n_used = send[-1].astype(i32).reshape(1)
    n_sub_used = (pend[-1] // MOE_SUB).astype(i32).reshape(1)
    return dest, row_token, tile_expert, tile_row0, tile_nsub, n_used, n_sub_used


def _moe_kernel(te_ref, row0_ref, nsub_ref, nu_ref, nsu_ref, tok_ref, hp_ref, wg_ref, wu_ref, wd_ref, y_ref,
                xg_ref, xb_ref, acc_ref, yb_ref, gsem, osem, fsem, *, nf, rows_per_step, n_sub_alloc, n_fill):
    s, f = pl.program_id(0), pl.program_id(1)
    n_used = nu_ref[0]
    used = s < n_used
    n_fetch = rows_per_step * nf
    half = xg_ref.shape[-1]

    def row_copy(tile, r):
        tok = tok_ref[row0_ref[tile] + r]
        return pltpu.make_async_copy(hp_ref.at[pl.ds(tok, 1)], xg_ref.at[pl.ds(r, 1)], gsem.at[0])

    def wait_rows():
        pltpu.make_async_copy(hp_ref.at[pl.ds(0, n_fetch)], xg_ref, gsem.at[0]).wait()

    def out_copies(tile):
        r0 = pl.multiple_of(row0_ref[tile], MOE_SUB)
        return [pltpu.make_async_copy(yb_ref.at[pl.ds(k * MOE_SUB, MOE_SUB)],
                                      y_ref.at[pl.ds(r0 + k * MOE_SUB, MOE_SUB)], osem.at[k])
                for k in range(SUBS_PER_TILE)]

    def start_out(tile):
        for k, cp in enumerate(out_copies(tile)):
            pl.when(k < nsub_ref[tile])(cp.start)

    def wait_out(tile):
        for k, cp in enumerate(out_copies(tile)):
            pl.when(k < nsub_ref[tile])(cp.wait)

    @pl.when(jnp.logical_and(s == 0, f == 0))
    def _():
        def body(r, _):
            row_copy(0, r).start()
            return 0
        lax.fori_loop(0, n_fetch, body, 0)

    @pl.when(jnp.logical_and(f == 0, s <= n_used))
    def _():
        wait_rows()

    @pl.when(jnp.logical_and(used, f == 0))
    def _():
        lo, hi = _unpack_halves(xg_ref[0:MOE_SUPER, :])
        xb_ref[:, :half] = lo.astype(BF16)
        xb_ref[:, half:] = hi.astype(BF16)
        acc_ref[...] = jnp.zeros_like(acc_ref)

    def step(n_rows):
        for k in range(rows_per_step):
            row_copy(s + 1, f * rows_per_step + k).start()
        h = xb_ref[0:n_rows, :]
        a = jnp.dot(h, wg_ref[...].astype(BF16), preferred_element_type=F32)
        b = jnp.dot(h, wu_ref[...].astype(BF16), preferred_element_type=F32)
        act = (a * jax.nn.sigmoid(a)) * b
        acc_ref[0:n_rows, :] += jnp.dot(act.astype(BF16), wd_ref[...].astype(BF16), preferred_element_type=F32)

    for n in range(1, SUBS_PER_TILE + 1):
        pl.when(jnp.logical_and(used, nsub_ref[s] == n))(functools.partial(step, n * MOE_SUB))

    @pl.when(jnp.logical_and(used, f == nf - 1))
    def _():
        pl.when(s > 0)(functools.partial(wait_out, s - 1))
        yb_ref[...] = _pack_halves(acc_ref[...])
        start_out(s)

    @pl.when(jnp.logical_and(s == n_used, f == 0))
    def _():
        wait_out(s - 1)
        yb_ref[0:MOE_SUB, :] = jnp.zeros((MOE_SUB, half), yb_ref.dtype)
        fills = []
        for k in range(n_fill):
            sub = nsu_ref[0] + k
            cp = pltpu.make_async_copy(yb_ref.at[pl.ds(0, MOE_SUB)],
                                       y_ref.at[pl.ds(pl.multiple_of(sub * MOE_SUB, MOE_SUB), MOE_SUB)], fsem.at[k])
            fills.append((sub < n_sub_alloc, cp))
        for cond, cp in fills:
            pl.when(cond)(cp.start)
        for cond, cp in fills:
            pl.when(cond)(cp.wait)


def _moe_experts(hp, tables, w_gate, w_up, w_down, tf=512):
    n_tok, half = hp.shape
    d = 2 * half
    dff = w_gate.shape[2]
    nf = dff // tf
    n_sub, n_super, rows_per_step, n_fetch = _moe_dims(n_tok, nf)
    _, row_token, tile_expert, tile_row0, tile_nsub, n_used, n_sub_used = tables
    n_unused_max = n_sub - n_tok * TOP_K // MOE_SUB
    tile_f = lambda s, f, nu: jnp.where(s < nu[0], f, nf - 1)
    w_in_spec = pl.BlockSpec((None, d, tf), lambda s, f, te, r0, ns, nu, nsu, tok: (te[s], 0, tile_f(s, f, nu)))
    w_out_spec = pl.BlockSpec((None, tf, d), lambda s, f, te, r0, ns, nu, nsu, tok: (te[s], tile_f(s, f, nu), 0))
    return pl.pallas_call(
        functools.partial(_moe_kernel, nf=nf, rows_per_step=rows_per_step, n_sub_alloc=n_sub,
                          n_fill=n_unused_max),
        grid_spec=pltpu.PrefetchScalarGridSpec(
            num_scalar_prefetch=6,
            grid=(n_super, nf),
            in_specs=[pl.BlockSpec(memory_space=pl.ANY), w_in_spec, w_in_spec, w_out_spec],
            out_specs=pl.BlockSpec(memory_space=pl.ANY),
            scratch_shapes=[pltpu.VMEM((n_fetch, half), jnp.uint32),
                            pltpu.VMEM((MOE_SUPER, d), BF16),
                            pltpu.VMEM((MOE_SUPER, d), F32),
                            pltpu.VMEM((MOE_SUPER, half), jnp.uint32),
                            pltpu.SemaphoreType.DMA((1,)), pltpu.SemaphoreType.DMA((SUBS_PER_TILE,)),
                            pltpu.SemaphoreType.DMA((n_unused_max,))]),
        out_shape=jax.ShapeDtypeStruct((n_sub * MOE_SUB, half), jnp.uint32),
        compiler_params=_params(("arbitrary", "arbitrary"), 58),
        name="moe_experts",
    )(tile_expert, tile_row0, tile_nsub, n_used, n_sub_used, row_token, hp, w_gate, w_up, w_down)


COMBINE_UNROLL = 8


def _combine_kernel(dest_ref, y_ref, x_ref, gates_ref, gate_f_ref, gn_ref, o_ref, rows_ref, sems,
                    *, tm, n_steps, final_norm):
    t = pl.program_id(0)
    slot = t % 2
    n = tm * TOP_K
    half = rows_ref.shape[-1]

    def row_copy(base, j, row, slot_):
        return pltpu.make_async_copy(y_ref.at[pl.ds(dest_ref[base + j], 1)],
                                     rows_ref.at[slot_, pl.ds(row, 1)], sems.at[slot_])

    def wait_rows(slot_):
        pltpu.make_async_copy(y_ref.at[pl.ds(0, n)], rows_ref.at[slot_], sems.at[slot_]).wait()

    @pl.when(t == 0)
    def _():
        def body(i, _):
            for u in range(COMBINE_UNROLL):
                row = (u % TOP_K) * tm + i * (COMBINE_UNROLL // TOP_K) + u // TOP_K
                row_copy(0, i * COMBINE_UNROLL + u, row, 0).start()
            return 0
        lax.fori_loop(0, n // COMBINE_UNROLL, body, 0)

    wait_rows(slot)
    next_base = jnp.minimum(t + 1, n_steps - 1) * n
    for j in range(n):
        row_copy(next_base, j, (j % TOP_K) * tm + j // TOP_K, 1 - slot).start()

    gates = gates_ref[...]
    g0, g1 = gates[:, 0:1], gates[:, 1:2]
    lo0, hi0 = _unpack_halves(rows_ref[slot, 0:tm, :])
    lo1, hi1 = _unpack_halves(rows_ref[slot, tm:2 * tm, :])
    out_lo = x_ref[:, :half] + gate_f_ref[:, :half] * (g0 * lo0 + g1 * lo1)
    out_hi = x_ref[:, half:] + gate_f_ref[:, half:] * (g0 * hi0 + g1 * hi1)
    if final_norm:
        ssq = jnp.sum(out_lo * out_lo, axis=-1, keepdims=True) + jnp.sum(out_hi * out_hi, axis=-1, keepdims=True)
        inv = lax.rsqrt(ssq / (2 * half) + EPS)
        out_lo = (out_lo * inv) * gn_ref[:, :half]
        out_hi = (out_hi * inv) * gn_ref[:, half:]
    o_ref[:, :half] = out_lo
    o_ref[:, half:] = out_hi
    pl.when(t == n_steps - 1)(functools.partial(wait_rows, 1 - slot))


def _combine(x, y_rows, dest, top_gate, gate_f, final_g, tm=256):
    bsz, seqlen, d = x.shape
    final_norm = final_g is not None
    gn = (final_g if final_norm else jnp.ones((d,), F32)).reshape(1, d)
    per_batch = seqlen // tm
    n_steps = bsz * per_batch
    row = lambda width: pl.BlockSpec((None, tm, width), lambda t, dst: (t // per_batch, t % per_batch, 0))
    return pl.pallas_call(
        functools.partial(_combine_kernel, tm=tm, n_steps=n_steps, final_norm=final_norm),
        grid_spec=pltpu.PrefetchScalarGridSpec(
            num_scalar_prefetch=1,
            grid=(n_steps,),
            in_specs=[pl.BlockSpec(memory_space=pl.ANY), row(d), row(LANES),
                      pl.BlockSpec((None, 1, d), lambda t, dst: (t // per_batch, 0, 0)),
                      pl.BlockSpec((1, d), lambda t, dst: (0, 0))],
            out_specs=row(d),
            scratch_shapes=[pltpu.VMEM((2, TOP_K * tm, d // 2), jnp.uint32), pltpu.SemaphoreType.DMA((2,))]),
        out_shape=jax.ShapeDtypeStruct(x.shape, F32),
        compiler_params=_params(("arbitrary",), 32),
        name="moe_combine",
    )(dest, y_rows, x, top_gate, gate_f[:, None, :], gn)


def _moe_ffn(x, g, shift, scale, gate_f, w_router, w_gate, w_up, w_down, final_g, tf=512):
    bsz, seqlen, d = x.shape
    n_tok = bsz * seqlen
    hp, top_idx, top_gate = _router(x, g, shift, scale, w_router)
    tables = _routing_tables(top_idx[..., :TOP_K], n_tok, w_gate.shape[2] // tf)
    y_rows = _moe_experts(hp.reshape(n_tok, d // 2), tables, w_gate, w_up, w_down, tf=tf)
    return _combine(x, y_rows, tables[0], top_gate, gate_f, final_g)


def _final_norm_kernel(x_ref, g_ref, o_ref):
    x = x_ref[...]
    ms = jnp.mean(x * x, axis=-1, keepdims=True)
    o_ref[...] = (x * lax.rsqrt(ms + EPS)) * g_ref[...]


def _final_norm(x, g, tm=512):
    bsz, seqlen, d = x.shape
    return pl.pallas_call(
        _final_norm_kernel,
        grid=(bsz, seqlen // tm),
        in_specs=[pl.BlockSpec((None, tm, d), lambda b, i: (b, i, 0)),
                  pl.BlockSpec((1, d), lambda b, i: (0, 0))],
        out_specs=pl.BlockSpec((None, tm, d), lambda b, i: (b, i, 0)),
        out_shape=jax.ShapeDtypeStruct(x.shape, F32),
        compiler_params=_params(("arbitrary", "arbitrary"), 32),
        name="final_norm",
    )(x, g.reshape(1, d))


def kernel(x, c, w_mod, b_mod, norm_mix_g, norm_ffn_g, w_in, ssm_a_re, ssm_a_im, ssm_log_dt, ssm_b_re, ssm_b_im, ssm_c_re, ssm_c_im, ssm_d, w_glu, b_glu, rel_bias, w_branch_ssm, w_branch_att, w_out, ffn_w_gate, ffn_w_up, ffn_w_down, moe_router, moe_w_gate, moe_w_up, moe_w_down, final_norm_g):
    depth = w_mod.shape[0]
    mod = _modulation(c, w_mod, b_mod)
    biases = [_bias_tile(rel_bias, g) for g in range(N_ATT_GROUPS)]
    tables = _s5_tables(ssm_a_re, ssm_a_im, ssm_log_dt, ssm_b_re, ssm_b_im, ssm_c_re, ssm_c_im)
    for i in range(depth):
        shift_m, scale_m, gate_m, shift_f, scale_f, gate_f = jnp.split(mod[i], N_MOD, axis=-1)
        main, qkv = _in_projection(x, norm_mix_g[i], shift_m, scale_m, w_in, i)
        y_ssm = _s5_branch(main, tables, i, ssm_d[i], w_glu, b_glu[i])
        att = [_attention_group(qkv[g], biases[g], g) for g in range(N_ATT_GROUPS)]
        x = _merge(x, gate_m, y_ssm, att, main, w_branch_ssm, w_branch_att, w_out, i)
        j = i // 2
        last = i == depth - 1
        if i % 2 == 0:
            x = _dense_ffn(x, norm_ffn_g[i], shift_f, scale_f, gate_f,
                           ffn_w_gate[j], ffn_w_up[j], ffn_w_down[j])
            if last:
                x = _final_norm(x, final_norm_g)
        else:
            x = _moe_ffn(x, norm_ffn_g[i], shift_f, scale_f, gate_f, moe_router[j],
                         moe_w_gate[j], moe_w_up[j], moe_w_down[j], final_norm_g if last else None)
    return x
```

```python
import functools
import math

import jax
import jax.numpy as jnp
from jax import lax
from jax.experimental import pallas as pl
from jax.experimental.pallas import tpu as pltpu

F32 = jnp.float32
BF16 = jnp.bfloat16

LANES = 128
SUBLANES = 8
VMEM_BYTES = 64 * 1024 * 1024

SSM_GROUP = 16
SSM_STATE = 64
SSM_WIDTH = 1024
HEAD_DIM = 128
DILATION_PATTERN = ((128, 1), (512, 4), (2048, 16))
HEADS_PER_GROUP = 4
N_ATT_GROUPS = len(DILATION_PATTERN)
ATT_OUT_WIDTH = HEADS_PER_GROUP * HEAD_DIM
Q_BLOCK = 128
NEG_INF = -1e30
N_BUCKETS = 32
MAX_DISTANCE = 2048
N_EXPERTS = 8
TOP_K = 2
N_MOD = 6
EPS = 1e-6

PROJ_TILE = 512
U_TILE0 = 0
Q_TILE0 = SSM_WIDTH // PROJ_TILE
K_TILE0 = Q_TILE0 + N_ATT_GROUPS
V_TILE0 = K_TILE0 + N_ATT_GROUPS
GS_TILE0 = V_TILE0 + N_ATT_GROUPS
MAIN_GATE0 = Q_TILE0


def _params(dims, vmem_mb):
    assert vmem_mb * 1024 * 1024 <= VMEM_BYTES
    return pltpu.CompilerParams(dimension_semantics=dims,
                                vmem_limit_bytes=vmem_mb * 1024 * 1024)


NORM_ROWS = 64


def _norm_modulate_into(h_ref, x_ref, g_ref, shift_ref, scale_ref, inv_ref):
    n_blocks = x_ref.shape[0] // NORM_ROWS
    reps = x_ref.shape[1] // LANES

    def block(i):
        return pl.ds(pl.multiple_of(i * NORM_ROWS, NORM_ROWS), NORM_ROWS)

    def rms(i, _):
        x = x_ref[block(i), :]
        inv = lax.rsqrt(jnp.mean(x * x, axis=-1, keepdims=True) + EPS)
        inv_ref[block(i), :] = jnp.broadcast_to(inv, (NORM_ROWS, LANES))
        return 0

    gain = g_ref[...] * (1.0 + scale_ref[...])
    shift = shift_ref[...]

    def scale_rows(i, _):
        y = x_ref[block(i), :] * jnp.concatenate([inv_ref[block(i), :]] * reps, axis=-1)
        h_ref[block(i), :] = (y * gain + shift).astype(h_ref.dtype)
        return 0

    lax.fori_loop(0, n_blocks, rms, 0, unroll=4)
    lax.fori_loop(0, n_blocks, scale_rows, 0)


def _mod_kernel(c_ref, w_ref, b_ref, o_ref):
    c = c_ref[...]
    cond = (c * jax.nn.sigmoid(c)).astype(BF16)
    o_ref[...] = jnp.dot(cond, w_ref[...].astype(BF16),
                         preferred_element_type=F32) + b_ref[...]


def _modulation(c, w_mod, b_mod):
    depth, d, n = w_mod.shape
    bsz = c.shape[0]
    rows = SUBLANES
    c_pad = jnp.zeros((rows, d), F32).at[:bsz].set(c)
    tn = 1536
    out = pl.pallas_call(
        _mod_kernel,
        grid=(depth, n // tn),
        in_specs=[pl.BlockSpec((rows, d), lambda l, j: (0, 0)),
                  pl.BlockSpec((None, d, tn), lambda l, j: (l, 0, j)),
                  pl.BlockSpec((None, 1, tn), lambda l, j: (l, 0, j))],
        out_specs=pl.BlockSpec((None, rows, tn), lambda l, j: (l, 0, j)),
        out_shape=jax.ShapeDtypeStruct((depth, rows, n), F32),
        compiler_params=_params(("arbitrary", "arbitrary"), 40),
        name="modulation",
    )(c_pad, w_mod, b_mod.reshape(depth, 1, n))
    return out[:, :bsz]


DEINTERLEAVE = 4


def _inproj_kernel(x_ref, g_ref, shift_ref, scale_ref, w_ref, main_ref, *rest, tm):
    qkv_refs, (h_ref, res_ref, res2_ref, inv_ref) = rest[:N_ATT_GROUPS], rest[N_ATT_GROUPS:]
    j = pl.program_id(2)

    @pl.when(j == 0)
    def _():
        _norm_modulate_into(h_ref, x_ref, g_ref, shift_ref, scale_ref, inv_ref)

    def project():
        return jnp.dot(h_ref[...], w_ref[...].astype(BF16), preferred_element_type=F32)

    @pl.when(jnp.logical_or(j < Q_TILE0, j >= GS_TILE0))
    def _():
        main_ref[...] = project().astype(BF16)

    for g, (_, dil) in enumerate(DILATION_PATTERN):
        is_g = functools.reduce(jnp.logical_or, [j == t0 + g for t0 in (Q_TILE0, K_TILE0, V_TILE0)])

        @pl.when(is_g)
        def _(g=g, dil=dil):
            res = project()
            if dil == 1:
                qkv_refs[g][0] = res.astype(BF16)
            else:
                n_chunks = PROJ_TILE // LANES
                for ch in range(n_chunks):
                    res_ref[ch] = res[:, ch * LANES:(ch + 1) * LANES]
                if dil <= DEINTERLEAVE:
                    for r in range(dil):
                        rows = [res_ref[ch, pl.ds(r, tm // dil, stride=dil), :] for ch in range(n_chunks)]
                        qkv_refs[g][r] = jnp.concatenate(rows, axis=-1).astype(BF16)
                else:
                    assert dil == DEINTERLEAVE * DEINTERLEAVE
                    sub = tm // DEINTERLEAVE
                    for a in range(DEINTERLEAVE):
                        for ch in range(n_chunks):
                            res2_ref[ch, a * sub:(a + 1) * sub, :] = (
                                res_ref[ch, pl.ds(a, sub, stride=DEINTERLEAVE), :])
                    for a in range(DEINTERLEAVE):
                        for m in range(DEINTERLEAVE):
                            rows = [res2_ref[ch, pl.ds(a * sub + m, tm // dil, stride=DEINTERLEAVE), :]
                                    for ch in range(n_chunks)]
                            qkv_refs[g][a + DEINTERLEAVE * m] = jnp.concatenate(rows, axis=-1).astype(BF16)


def _in_projection(x, g, shift, scale, w_in, layer, tm=1024):
    bsz, seqlen, d = x.shape
    n_tiles = w_in.shape[2] // PROJ_TILE
    n_main = n_tiles - 3 * N_ATT_GROUPS

    def main_map(b, i, j):
        return (jnp.where(j < Q_TILE0, j, jnp.where(j < GS_TILE0, Q_TILE0 - 1, j - 3 * N_ATT_GROUPS)), b, i, 0)

    def qkv_map(g):
        return lambda b, i, j: ((j > Q_TILE0 + g).astype(jnp.int32) + (j > K_TILE0 + g).astype(jnp.int32),
                                b, 0, i, 0)

    qkv_specs = [pl.BlockSpec((None, None, dil, tm // dil, PROJ_TILE), qkv_map(g))
                 for g, (_, dil) in enumerate(DILATION_PATTERN)]
    qkv_shapes = [jax.ShapeDtypeStruct((3, bsz, dil, seqlen // dil, PROJ_TILE), BF16)
                  for _, dil in DILATION_PATTERN]
    outs = pl.pallas_call(
        functools.partial(_inproj_kernel, tm=tm),
        grid=(bsz, seqlen // tm, n_tiles),
        in_specs=[pl.BlockSpec((None, tm, d), lambda b, i, j: (b, i, 0)),
                  pl.BlockSpec((1, d), lambda b, i, j: (0, 0)),
                  pl.BlockSpec((None, 1, d), lambda b, i, j: (b, 0, 0)),
                  pl.BlockSpec((None, 1, d), lambda b, i, j: (b, 0, 0)),
                  pl.BlockSpec((None, d, PROJ_TILE), lambda b, i, j: (layer, 0, j))],
        out_specs=[pl.BlockSpec((None, None, tm, PROJ_TILE), main_map)] + qkv_specs,
        out_shape=[jax.ShapeDtypeStruct((n_main, bsz, seqlen, PROJ_TILE), BF16)] + qkv_shapes,
        scratch_shapes=[pltpu.VMEM((tm, d), BF16), pltpu.VMEM((PROJ_TILE // LANES, tm, LANES), F32),
                        pltpu.VMEM((PROJ_TILE // LANES, tm, LANES), F32), pltpu.VMEM((tm, LANES), F32)],
        compiler_params=_params(("arbitrary", "arbitrary", "arbitrary"), 52),
        name="in_projection",
    )(x, g.reshape(1, d), shift[:, None, :], scale[:, None, :], w_in)
    return outs[0], outs[1:]


SSM_GB = 16
SSM_NGB = (SSM_WIDTH // SSM_GROUP) // SSM_GB
SSM_GB_IN = SSM_GB * SSM_GROUP
SSM_GB_RE = SSM_GB * SSM_STATE
SSM_CB = 2 * SSM_GB_RE // LANES
SSM_SLOTS = SSM_NGB * SSM_CB + SUBLANES


def _s5_kernel(ua0_ref, ua1_ref, un0_ref, un1_ref, bm_ref, cm_ref, ar_ref, ai_ref, d_ref, wglu_ref, bglu_ref,
               o_ref, xs0_ref, xs1_ref, ug_ref, st_ref, *, bsz, tt):
    xs_ref = (xs0_ref, xs1_ref)
    rows = bsz * tt
    n_nt = bm_ref.shape[1]
    n_pieces = SSM_NGB * n_nt
    unroll = 4
    assert tt == unroll * n_pieces
    half_cb = SSM_CB // 2

    def pair_halves(c):
        return tuple(r[:, c * tt:(c + 1) * tt, :].reshape(rows, PROJ_TILE) for r in (ua0_ref, ua1_ref))

    def stage_u(halves):
        for gb in range(SSM_NGB):
            half, off = divmod(gb * SSM_GB_IN, PROJ_TILE)
            ug_ref[gb] = halves[half][:, off:off + SSM_GB_IN]

    def b_piece(i, dst):
        gb, nt = i // n_nt, i % n_nt
        res = jnp.dot(ug_ref[gb], bm_ref[gb, nt], preferred_element_type=F32)
        for j in range(2):
            xs_ref[dst][pl.ds(gb * SSM_CB + 2 * nt + j, rows, stride=SSM_SLOTS), :] = (
                res[:, j * LANES:(j + 1) * LANES])

    a_re = [ar_ref[gb] for gb in range(SSM_NGB)]
    a_im = [ai_ref[gb] for gb in range(SSM_NGB)]

    def time_step(t, carry, src):
        new = []
        for b in range(bsz):
            for gb in range(SSM_NGB):
                k = (b * SSM_NGB + gb) * 2
                s_re, s_im = carry[k], carry[k + 1]
                row = (b * tt + t) * SSM_SLOTS + gb * SSM_CB
                n_re = a_re[gb] * s_re - a_im[gb] * s_im + xs_ref[src][pl.ds(row, half_cb), :]
                n_im = a_re[gb] * s_im + a_im[gb] * s_re + xs_ref[src][pl.ds(row + half_cb, half_cb), :]
                xs_ref[src][pl.ds(row, half_cb), :] = n_re
                xs_ref[src][pl.ds(row + half_cb, half_cb), :] = n_im
                new += [n_re, n_im]
        return tuple(new)

    n_carry = bsz * SSM_NGB * 2

    def scan_chunk(src):
        carry = tuple(st_ref[k] for k in range(n_carry))
        for it in range(n_pieces):
            b_piece(it, 1 - src)
            for k in range(unroll):
                carry = time_step(it * unroll + k, carry, src)
        for k in range(n_carry):
            st_ref[k] = carry[k]

    def finish_chunk(c):
        ys = []
        for gb in range(SSM_NGB):
            cols = [xs_ref[c][pl.ds(gb * SSM_CB + cb, rows, stride=SSM_SLOTS), :] for cb in range(SSM_CB)]
            xg = jnp.concatenate(cols, axis=-1).astype(BF16)
            ys.append(jnp.dot(xg, cm_ref[gb], preferred_element_type=F32))
        y = jnp.concatenate(ys, axis=-1)
        u32 = jnp.concatenate(pair_halves(c), axis=-1).astype(F32)
        y = jax.nn.gelu(y + d_ref[...] * u32)
        z = jnp.dot(y.astype(BF16), wglu_ref[...], preferred_element_type=F32) + bglu_ref[...]
        o_ref[:, c * tt:(c + 1) * tt, :] = (y * jax.nn.sigmoid(z)).astype(o_ref.dtype).reshape(bsz, tt, SSM_WIDTH)

    @pl.when(pl.program_id(0) == 0)
    def _():
        st_ref[...] = jnp.zeros_like(st_ref)
        stage_u(pair_halves(0))
        for i in range(n_pieces):
            b_piece(i, 0)

    stage_u(pair_halves(1))
    scan_chunk(0)
    finish_chunk(0)
    stage_u(tuple(r[...].reshape(rows, PROJ_TILE) for r in (un0_ref, un1_ref)))
    scan_chunk(1)
    finish_chunk(1)


def _block_diag(m):
    ngb, rows, c = m.shape
    r = rows // SSM_GB
    row_group = lax.broadcasted_iota(jnp.int32, (rows, SSM_GB * c), 0) // r
    col_group = lax.broadcasted_iota(jnp.int32, (rows, SSM_GB * c), 1) // c
    return jnp.where(row_group == col_group, jnp.tile(m, (1, 1, SSM_GB)), 0.0)


def _s5_tables(a_re, a_im, log_dt, b_re, b_im, c_re, c_im):
    depth = a_re.shape[0]
    lam = lax.complex(a_re.astype(F32), a_im.astype(F32))
    dt = jnp.exp(log_dt.astype(F32))[..., None]
    a_bar = jnp.exp(lam * dt)
    b_bar = ((a_bar - 1.0) / lam)[..., None] * lax.complex(b_re.astype(F32), b_im.astype(F32))
    _, g, p, h = b_bar.shape
    n = depth * SSM_NGB
    bt = jnp.transpose(b_bar, (0, 1, 3, 2)).reshape(n, SSM_GB * h, p)
    bm = jnp.concatenate([_block_diag(bt.real), _block_diag(bt.imag)], axis=-1)
    ct_re = jnp.transpose(c_re.astype(F32), (0, 1, 3, 2)).reshape(n, SSM_GB * p, h)
    ct_im = jnp.transpose(c_im.astype(F32), (0, 1, 3, 2)).reshape(n, SSM_GB * p, h)
    cm = jnp.concatenate([_block_diag(ct_re), -_block_diag(ct_im)], axis=1)
    lead = (depth, SSM_NGB)
    return (bm.astype(BF16).reshape(lead + bm.shape[1:]), cm.astype(BF16).reshape(lead + cm.shape[1:]),
            a_bar.real.reshape(lead + (SSM_CB // 2, LANES)), a_bar.imag.reshape(lead + (SSM_CB // 2, LANES)))


def _s5_branch(proj, tables, layer, d_skip, w_glu, b_glu, tt=128):
    _, bsz, seqlen, _ = proj.shape
    bm, cm, ar, ai = tables
    n_nt = bm.shape[-1] // SSM_GB_IN
    bm = jnp.swapaxes(bm.reshape(bm.shape[:3] + (n_nt, SSM_GB_IN)), 2, 3)
    d_skip = d_skip.reshape(1, SSM_WIDTH).astype(F32)
    w_glu = w_glu.astype(BF16)
    b_glu = b_glu.reshape(1, SSM_WIDTH).astype(F32)
    n_chunks = seqlen // tt
    pair = lambda tile: pl.BlockSpec((None, bsz, 2 * tt, PROJ_TILE), lambda s: (tile, 0, s, 0))
    nxt = lambda tile: pl.BlockSpec((None, bsz, tt, PROJ_TILE),
                                    lambda s: (tile, 0, jnp.minimum(2 * s + 2, n_chunks - 1), 0))
    return pl.pallas_call(
        functools.partial(_s5_kernel, bsz=bsz, tt=tt),
        grid=(n_chunks // 2,),
        in_specs=[pair(U_TILE0), pair(U_TILE0 + 1), nxt(U_TILE0), nxt(U_TILE0 + 1),
                  _resident(bm, layer), _resident(cm, layer), _resident(ar, layer), _resident(ai, layer),
                  _resident(d_skip), _resident(w_glu, layer), _resident(b_glu)],
        out_specs=pl.BlockSpec((bsz, 2 * tt, SSM_WIDTH), lambda s: (0, s, 0)),
        out_shape=jax.ShapeDtypeStruct((bsz, seqlen, SSM_WIDTH), BF16),
        scratch_shapes=[pltpu.VMEM((bsz * tt * SSM_SLOTS, LANES), F32),
                        pltpu.VMEM((bsz * tt * SSM_SLOTS, LANES), F32),
                        pltpu.VMEM((SSM_NGB, bsz * tt, SSM_GB_IN), BF16),
                        pltpu.VMEM((bsz * SSM_NGB * 2, SSM_CB // 2, LANES), F32)],
        compiler_params=_params(("arbitrary",), 48),
        name="s5_branch",
    )(proj, proj, proj, proj, bm, cm, ar, ai, d_skip, w_glu, b_glu)


def _t5_causal_bucket(dist):
    max_exact = N_BUCKETS // 2
    d32 = jnp.maximum(dist, 1).astype(F32)
    large = max_exact + (jnp.log(d32 / max_exact) / math.log(MAX_DISTANCE / max_exact)
                         * (N_BUCKETS - max_exact)).astype(jnp.int32)
    return jnp.where(dist < max_exact, dist, jnp.minimum(large, N_BUCKETS - 1))


def _bias_tile(rel_bias, group):
    window, dilation = DILATION_PATTERN[group]
    steps = window // dilation
    assert steps == Q_BLOCK
    heads = slice(group * HEADS_PER_GROUP, (group + 1) * HEADS_PER_GROUP)
    back = jnp.arange(steps, -1, -1, dtype=jnp.int32)
    vals = rel_bias[_t5_causal_bucket(back * dilation)][:, heads].astype(F32).T
    period = 3 * Q_BLOCK
    v = jnp.concatenate([vals, jnp.full((HEADS_PER_GROUP, period - steps - 1), NEG_INF, F32)], axis=1)
    flat = jnp.tile(v, (1, Q_BLOCK))[:, :Q_BLOCK * (period - 1)]
    tile = flat.reshape(HEADS_PER_GROUP, Q_BLOCK, period - 1)[:, :, :2 * Q_BLOCK]
    col = lax.broadcasted_iota(jnp.int32, tile.shape, 2)
    return jnp.concatenate([tile, jnp.where(col < Q_BLOCK, NEG_INF, tile)], axis=0)


LSE_LANES = LANES // HEADS_PER_GROUP


def _attn_kernel(q_ref, kc_ref, kp_ref, vc_ref, vp_ref, bias_ref, o_ref, lse_ref, kf_ref, vf_ref, *, tq, n_res):
    scale = HEAD_DIM ** -0.5
    first_tile = pl.program_id(2) == 0
    lane_head = lax.broadcasted_iota(jnp.int32, (Q_BLOCK, LANES), 1) // LSE_LANES

    for rr in range(n_res):
        kf_ref[rr, 0:Q_BLOCK, :] = kp_ref[rr]
        kf_ref[rr, Q_BLOCK:, :] = kc_ref[rr]
        vf_ref[rr, 0:Q_BLOCK, :] = vp_ref[rr]
        vf_ref[rr, Q_BLOCK:, :] = vc_ref[rr]

        def block(jb, _, rr=rr):
            r0 = pl.multiple_of(jb * Q_BLOCK, Q_BLOCK)
            bias_set = jnp.logical_and(first_tile, jb == 0).astype(jnp.int32) * HEADS_PER_GROUP
            lse = jnp.zeros((Q_BLOCK, LANES), F32)
            for h in range(HEADS_PER_GROUP):
                hs = slice(h * HEAD_DIM, (h + 1) * HEAD_DIM)
                q = q_ref[rr, pl.ds(r0, Q_BLOCK), hs]
                k2 = kf_ref[rr, pl.ds(r0, 2 * Q_BLOCK), hs]
                v2 = vf_ref[rr, pl.ds(r0, 2 * Q_BLOCK), hs]
                s = lax.dot_general(q, k2, (((1,), (1,)), ((), ())), preferred_element_type=F32)
                s = s * scale + bias_ref[bias_set + h]
                m = jnp.max(s, axis=-1, keepdims=True)
                p = jnp.exp(s - m)
                l = jnp.sum(p, axis=-1, keepdims=True)
                o = jnp.dot(p.astype(BF16), v2, preferred_element_type=F32) / l
                o_ref[rr, pl.ds(r0, Q_BLOCK), hs] = o.astype(o_ref.dtype)
                lse = jnp.where(lane_head == h, m + jnp.log(l), lse)
            lse_ref[rr, pl.ds(r0, Q_BLOCK), :] = lse
            return 0

        lax.fori_loop(0, tq // Q_BLOCK, block, 0, unroll=min(4, tq // Q_BLOCK))


ATTN_ROWS = 1024


def _attention_group(qkv, bias, group, tile0=0):
    _, bsz, d, lc, _ = qkv.shape
    tq = min(lc, ATTN_ROWS)
    n_res = min(d, ATTN_ROWS // tq)
    per_tq = tq // Q_BLOCK
    cur = lambda which: pl.BlockSpec((None, None, n_res, tq, PROJ_TILE),
                                     lambda b, r, i: (tile0 + which, b, r, i, 0))
    prev = lambda which: pl.BlockSpec((None, None, n_res, Q_BLOCK, PROJ_TILE),
                                      lambda b, r, i: (tile0 + which, b, r, jnp.maximum(i * per_tq - 1, 0), 0))
    out_spec = lambda width: pl.BlockSpec((None, n_res, tq, width), lambda b, r, i: (b, r, i, 0))
    return pl.pallas_call(
        functools.partial(_attn_kernel, tq=tq, n_res=n_res),
        grid=(bsz, d // n_res, lc // tq),
        in_specs=[cur(0), cur(1), prev(1), cur(2), prev(2),
                  pl.BlockSpec(bias.shape, lambda b, r, i: (0, 0, 0))],
        out_specs=[out_spec(ATT_OUT_WIDTH), out_spec(LANES)],
        out_shape=[jax.ShapeDtypeStruct((bsz, d, lc, ATT_OUT_WIDTH), BF16),
                   jax.ShapeDtypeStruct((bsz, d, lc, LANES), F32)],
        scratch_shapes=[pltpu.VMEM((n_res, Q_BLOCK + tq, PROJ_TILE), BF16),
                        pltpu.VMEM((n_res, Q_BLOCK + tq, PROJ_TILE), BF16)],
        compiler_params=_params(("arbitrary", "arbitrary", "arbitrary"), 40),
        name=f"attention_group{group}",
    )(qkv, qkv, qkv, qkv, qkv, bias)


def _merge_kernel(*refs, tm):
    ys_ref = refs[0]
    o_refs = refs[1:1 + N_ATT_GROUPS]
    l_refs = refs[1 + N_ATT_GROUPS:1 + 2 * N_ATT_GROUPS]
    k = 1 + 2 * N_ATT_GROUPS
    n_gate = (len(refs) - k - 7) // 2
    gs_refs = refs[k:k + n_gate]
    ga_refs = refs[k + n_gate:k + 2 * n_gate]
    x_ref, gate_ref, wbs_ref, wba_ref, wout_ref, out_ref, tok_ref = refs[k + 2 * n_gate:]

    def token_order(ref, g):
        dil = DILATION_PATTERN[g][1]
        if dil == 1:
            return ref[0].astype(F32)
        n_chunks = ref.shape[-1] // LANES
        for r in range(dil):
            for ch in range(n_chunks):
                tok_ref[ch, pl.ds(r, tm // dil, stride=dil), :] = (
                    ref[r, :, ch * LANES:(ch + 1) * LANES].astype(F32))
        return jnp.concatenate([tok_ref[ch] for ch in range(n_chunks)], axis=-1)

    lses = [token_order(r, g) for g, r in enumerate(l_refs)]
    m = functools.reduce(jnp.maximum, lses)
    es = [jnp.exp(l - m) for l in lses]
    den = functools.reduce(lambda a, b: a + b, es)

    def per_head(w):
        return jnp.concatenate([jnp.broadcast_to(w[:, h * LSE_LANES:h * LSE_LANES + 1], (tm, HEAD_DIM))
                                for h in range(HEADS_PER_GROUP)], axis=-1)

    y_att = functools.reduce(lambda a, b: a + b,
                             [per_head(e / den) * token_order(r, g) for g, (e, r) in enumerate(zip(es, o_refs))])

    m_ssm = jnp.dot(ys_ref[...], wbs_ref[...], preferred_element_type=F32)
    m_att = jnp.dot(y_att.astype(BF16), wba_ref[...], preferred_element_type=F32)
    g_ssm = jnp.concatenate([r[...] for r in gs_refs], axis=-1).astype(F32)
    g_att = jnp.concatenate([r[...] for r in ga_refs], axis=-1).astype(F32)
    merged = jax.nn.sigmoid(g_ssm) * m_ssm + jax.nn.sigmoid(g_att) * m_att
    mixed = jnp.dot(merged.astype(BF16), wout_ref[...], preferred_element_type=F32)
    out_ref[...] = x_ref[...] + gate_ref[...] * mixed


def _resident(a, layer=None):
    nd = a.ndim
    if layer is None:
        return pl.BlockSpec(a.shape, lambda *_: (0,) * nd, pipeline_mode=pl.Buffered(1))
    return pl.BlockSpec((None,) + a.shape[1:], lambda *_: (layer,) + (0,) * (nd - 1), pipeline_mode=pl.Buffered(1))


def _merge(x, gate, y_ssm, att, main, w_branch_ssm, w_branch_att, w_out, layer, tm=512):
    bsz, seqlen, d = x.shape
    n_gate = d // PROJ_TILE
    row = lambda width: pl.BlockSpec((None, tm, width), lambda b, i: (b, i, 0))
    tile = lambda t: pl.BlockSpec((None, None, tm, PROJ_TILE), lambda b, i: (t, b, i, 0))
    res_major = lambda dil, width: pl.BlockSpec((None, dil, tm // dil, width), lambda b, i: (b, 0, i, 0))
    wbs, wba, wout = (w.astype(BF16) for w in (w_branch_ssm, w_branch_att, w_out))
    os_, ls_ = zip(*att)
    in_specs = ([row(SSM_WIDTH)]
                + [res_major(dil, ATT_OUT_WIDTH) for _, dil in DILATION_PATTERN]
                + [res_major(dil, LANES) for _, dil in DILATION_PATTERN]
                + [tile(MAIN_GATE0 + t) for t in range(2 * n_gate)]
                + [row(d), pl.BlockSpec((None, 1, d), lambda b, i: (b, 0, 0)),
                   _resident(wbs, layer), _resident(wba, layer), _resident(wout, layer)])
    return pl.pallas_call(
        functools.partial(_merge_kernel, tm=tm),
        grid=(bsz, seqlen // tm),
        in_specs=in_specs,
        out_specs=row(d),
        out_shape=jax.ShapeDtypeStruct(x.shape, F32),
        scratch_shapes=[pltpu.VMEM((ATT_OUT_WIDTH // LANES, tm, LANES), F32)],
        compiler_params=_params(("arbitrary", "arbitrary"), 56),
        name="merge",
    )(y_ssm, *os_, *ls_, *([main] * (2 * n_gate)), x, gate[:, None, :], wbs, wba, wout)


def _ffn_kernel(x_ref, g_ref, shift_ref, scale_ref, gate_ref, wg_ref, wu_ref, wd_ref,
                o_ref, h_ref, inv_ref):
    f = pl.program_id(2)

    @pl.when(f == 0)
    def _():
        _norm_modulate_into(h_ref, x_ref, g_ref, shift_ref, scale_ref, inv_ref)
        o_ref[...] = jnp.zeros_like(o_ref)

    h = h_ref[...]
    a = jnp.dot(h, wg_ref[...].astype(BF16), preferred_element_type=F32)
    b = jnp.dot(h, wu_ref[...].astype(BF16), preferred_element_type=F32)
    act = (a * jax.nn.sigmoid(a)) * b
    o_ref[...] += jnp.dot(act.astype(BF16), wd_ref[...].astype(BF16), preferred_element_type=F32)

    @pl.when(f == pl.num_programs(2) - 1)
    def _():
        o_ref[...] = x_ref[...] + gate_ref[...] * o_ref[...]


def _dense_ffn(x, g, shift, scale, gate, w_gate, w_up, w_down, tm=1024, tf=512):
    bsz, seqlen, d = x.shape
    dff = w_gate.shape[1]
    vec = pl.BlockSpec((None, 1, d), lambda b, i, f: (b, 0, 0))
    row = pl.BlockSpec((None, tm, d), lambda b, i, f: (b, i, 0), pipeline_mode=pl.Buffered(1))
    return pl.pallas_call(
        _ffn_kernel,
        grid=(bsz, seqlen // tm, dff // tf),
        in_specs=[pl.BlockSpec((None, tm, d), lambda b, i, f: (b, i, 0)),
                  pl.BlockSpec((1, d), lambda b, i, f: (0, 0)),
                  vec, vec, vec,
                  pl.BlockSpec((d, tf), lambda b, i, f: (0, f)),
                  pl.BlockSpec((d, tf), lambda b, i, f: (0, f)),
                  pl.BlockSpec((tf, d), lambda b, i, f: (f, 0))],
        out_specs=row,
        out_shape=jax.ShapeDtypeStruct(x.shape, F32),
        scratch_shapes=[pltpu.VMEM((tm, d), BF16), pltpu.VMEM((tm, LANES), F32)],
        compiler_params=_params(("arbitrary", "arbitrary", "arbitrary"), 56),
        name="dense_ffn",
    )(x, g.reshape(1, d), shift[:, None, :], scale[:, None, :], gate[:, None, :],
      w_gate.astype(BF16), w_up.astype(BF16), w_down.astype(BF16))


MOE_SUB = 256
SUBS_PER_TILE = 5
MOE_SUPER = SUBS_PER_TILE * MOE_SUB


def _pack_halves(x):
    half = x.shape[-1] // 2
    bits = lambda v: lax.bitcast_convert_type(v.astype(BF16).astype(F32), jnp.uint32)
    return bits(x[:, half:]) | (bits(x[:, :half]) >> 16)


def _unpack_halves(w):
    lo = lax.bitcast_convert_type(w << 16, F32)
    hi = lax.bitcast_convert_type(w & jnp.uint32(0xFFFF0000), F32)
    return lo, hi


def _router_kernel(x_ref, g_ref, shift_ref, scale_ref, w2_ref, h_ref, idx_ref, gate_ref, hf_ref, inv_ref):
    _norm_modulate_into(hf_ref, x_ref, g_ref, shift_ref, scale_ref, inv_ref)
    h = hf_ref[...]
    h_ref[...] = _pack_halves(h)
    h_hi = h.astype(BF16)
    h_lo = (h - h_hi.astype(F32)).astype(BF16)
    prod = (jnp.dot(h_hi, w2_ref[...], preferred_element_type=F32)
            + jnp.dot(h_lo, w2_ref[...], preferred_element_type=F32))
    logits = prod[:, :LANES] + prod[:, LANES:]
    lane_i = lax.broadcasted_iota(jnp.int32, logits.shape, 1)
    lane = lane_i.astype(F32)
    logits = jnp.where(lane_i < N_EXPERTS, logits, -jnp.inf)
    m1 = jnp.max(logits, axis=-1, keepdims=True)
    i1 = jnp.min(jnp.where(logits == m1, lane, float(LANES)), axis=-1, keepdims=True)
    rest = jnp.where(lane == i1, -jnp.inf, logits)
    m2 = jnp.max(rest, axis=-1, keepdims=True)
    i2 = jnp.min(jnp.where(rest == m2, lane, float(LANES)), axis=-1, keepdims=True)
    e2 = jnp.exp(m2 - m1)
    den = 1.0 + e2
    idx_ref[...] = jnp.where(lane_i == 0, i1, jnp.where(lane_i == 1, i2, 0.0)).astype(jnp.int32)
    gate_ref[...] = jnp.where(lane_i == 0, 1.0 / den, jnp.where(lane_i == 1, e2 / den, 0.0))


def _router(x, g, shift, scale, w_router, tm=512):
    bsz, seqlen, d = x.shape
    wr = jnp.zeros((d, LANES), F32).at[:, :N_EXPERTS].set(w_router)
    w_hi = wr.astype(BF16)
    w_lo = (wr - w_hi.astype(F32)).astype(BF16)
    w2 = jnp.concatenate([w_hi, w_lo], axis=1)
    row = lambda width: pl.BlockSpec((None, tm, width), lambda b, i: (b, i, 0))
    vec = pl.BlockSpec((None, 1, d), lambda b, i: (b, 0, 0))
    return pl.pallas_call(
        _router_kernel,
        grid=(bsz, seqlen // tm),
        in_specs=[row(d), pl.BlockSpec((1, d), lambda b, i: (0, 0)), vec, vec,
                  pl.BlockSpec((d, 2 * LANES), lambda b, i: (0, 0))],
        out_specs=[row(d // 2), row(LANES), row(LANES)],
        out_shape=[jax.ShapeDtypeStruct((bsz, seqlen, d // 2), jnp.uint32),
                   jax.ShapeDtypeStruct((bsz, seqlen, LANES), jnp.int32),
                   jax.ShapeDtypeStruct((bsz, seqlen, LANES), F32)],
        scratch_shapes=[pltpu.VMEM((tm, d), F32), pltpu.VMEM((tm, LANES), F32)],
        compiler_params=_params(("arbitrary", "arbitrary"), 40),
        name="moe_router",
    )(x, g.reshape(1, d), shift[:, None, :], scale[:, None, :], w2)


def _moe_dims(n_tok, nf):
    n_sub = n_tok * TOP_K // MOE_SUB + N_EXPERTS
    n_super = (n_sub + (SUBS_PER_TILE - 1) * N_EXPERTS) // SUBS_PER_TILE + 1
    rows_per_step = -(-MOE_SUPER // nf)
    while (rows_per_step * nf) % SUBLANES:
        rows_per_step += 1
    n_fetch = rows_per_step * nf
    return n_sub, n_super, rows_per_step, n_fetch


def _routing_tables(top_expert, n_tok, nf):
    n_sub, n_super, _, n_fetch = _moe_dims(n_tok, nf)
    i32 = jnp.int32
    flat_e = top_expert.reshape(-1)
    onehot = (flat_e[:, None] == jnp.arange(N_EXPERTS, dtype=i32)[None, :]).astype(i32)
    csum = jnp.cumsum(onehot, axis=0)
    rank = jnp.sum(csum * onehot, axis=1) - 1
    counts = csum[-1]
    subs = (counts + MOE_SUB - 1) // MOE_SUB
    pend = jnp.cumsum(subs) * MOE_SUB
    pstart = pend - subs * MOE_SUB
    dest = (pstart[flat_e] + rank).astype(i32)
    flat_token = jnp.arange(n_tok * TOP_K, dtype=i32) // TOP_K
    row_token = jnp.zeros((n_sub * MOE_SUB + n_fetch,), i32).at[dest].set(flat_token)
    supers = (subs + SUBS_PER_TILE - 1) // SUBS_PER_TILE
    send = jnp.cumsum(supers)
    sstart = send - supers
    s = jnp.arange(n_super + 1, dtype=i32)
    e = jnp.minimum(jnp.searchsorted(send, s, side='right'), N_EXPERTS - 1).astype(i32)
    local = s - sstart[e]
    used = s < send[-1]
    tile_expert = jnp.where(used, e, e[jnp.maximum(send[-1] - 1, 0)]).astype(i32)
    n_tiles_e = jnp.maximum(supers[e], 1)
    base, rem = subs[e] // n_tiles_e, subs[e] % n_tiles_e
    first_sub = local * base + jnp.minimum(local, rem)
    tile_row0 = jnp.where(used, pstart[e] + first_sub * MOE_SUB, 0).astype(i32)
    tile_nsub = jnp.where(used, base + (local < rem).astype(i32), 0).astype(i32)
    n_used = send[-1].astype(i32).reshape(1)
    n_sub_used = (pend[-1] // MOE_SUB).astype(i32).reshape(1)
    return dest, row_token, tile_expert, tile_row0, tile_nsub, n_used, n_sub_used


def _moe_kernel(te_ref, row0_ref, nsub_ref, nu_ref, nsu_ref, tok_ref, hp_ref, wg_ref, wu_ref, wd_ref, y_ref,
                xg_ref, xb_ref, acc_ref, yb_ref, gsem, osem, fsem, *, nf, rows_per_step, n_sub_alloc, n_fill):
    s, f = pl.program_id(0), pl.program_id(1)
    n_used = nu_ref[0]
    used = s < n_used
    n_fetch = rows_per_step * nf
    half = xg_ref.shape[-1]

    def row_copy(tile, r):
        tok = tok_ref[row0_ref[tile] + r]
        return pltpu.make_async_copy(hp_ref.at[pl.ds(tok, 1)], xg_ref.at[pl.ds(r, 1)], gsem.at[0])

    def wait_rows():
        pltpu.make_async_copy(hp_ref.at[pl.ds(0, n_fetch)], xg_ref, gsem.at[0]).wait()

    def out_copies(tile):
        r0 = pl.multiple_of(row0_ref[tile], MOE_SUB)
        return [pltpu.make_async_copy(yb_ref.at[pl.ds(k * MOE_SUB, MOE_SUB)],
                                      y_ref.at[pl.ds(r0 + k * MOE_SUB, MOE_SUB)], osem.at[k])
                for k in range(SUBS_PER_TILE)]

    def start_out(tile):
        for k, cp in enumerate(out_copies(tile)):
            pl.when(k < nsub_ref[tile])(cp.start)

    def wait_out(tile):
        for k, cp in enumerate(out_copies(tile)):
            pl.when(k < nsub_ref[tile])(cp.wait)

    @pl.when(jnp.logical_and(s == 0, f == 0))
    def _():
        def body(r, _):
            row_copy(0, r).start()
            return 0
        lax.fori_loop(0, n_fetch, body, 0)

    @pl.when(jnp.logical_and(f == 0, s <= n_used))
    def _():
        wait_rows()

    @pl.when(jnp.logical_and(used, f == 0))
    def _():
        lo, hi = _unpack_halves(xg_ref[0:MOE_SUPER, :])
        xb_ref[:, :half] = lo.astype(BF16)
        xb_ref[:, half:] = hi.astype(BF16)
        acc_ref[...] = jnp.zeros_like(acc_ref)

    def step(n_rows):
        for k in range(rows_per_step):
            row_copy(s + 1, f * rows_per_step + k).start()
        h = xb_ref[0:n_rows, :]
        a = jnp.dot(h, wg_ref[...].astype(BF16), preferred_element_type=F32)
        b = jnp.dot(h, wu_ref[...].astype(BF16), preferred_element_type=F32)
        act = (a * jax.nn.sigmoid(a)) * b
        acc_ref[0:n_rows, :] += jnp.dot(act.astype(BF16), wd_ref[...].astype(BF16), preferred_element_type=F32)

    for n in range(1, SUBS_PER_TILE + 1):
        pl.when(jnp.logical_and(used, nsub_ref[s] == n))(functools.partial(step, n * MOE_SUB))

    @pl.when(jnp.logical_and(used, f == nf - 1))
    def _():
        pl.when(s > 0)(functools.partial(wait_out, s - 1))
        yb_ref[...] = _pack_halves(acc_ref[...])
        start_out(s)

    @pl.when(jnp.logical_and(s == n_used, f == 0))
    def _():
        wait_out(s - 1)
        yb_ref[0:MOE_SUB, :] = jnp.zeros((MOE_SUB, half), yb_ref.dtype)
        fills = []
        for k in range(n_fill):
            sub = nsu_ref[0] + k
            cp = pltpu.make_async_copy(yb_ref.at[pl.ds(0, MOE_SUB)],
                                       y_ref.at[pl.ds(pl.multiple_of(sub * MOE_SUB, MOE_SUB), MOE_SUB)], fsem.at[k])
            fills.append((sub < n_sub_alloc, cp))
        for cond, cp in fills:
            pl.when(cond)(cp.start)
        for cond, cp in fills:
            pl.when(cond)(cp.wait)


def _moe_experts(hp, tables, w_gate, w_up, w_down, tf=512):
    n_tok, half = hp.shape
    d = 2 * half
    dff = w_gate.shape[2]
    nf = dff // tf
    n_sub, n_super, rows_per_step, n_fetch = _moe_dims(n_tok, nf)
    _, row_token, tile_expert, tile_row0, tile_nsub, n_used, n_sub_used = tables
    n_unused_max = n_sub - n_tok * TOP_K // MOE_SUB
    tile_f = lambda s, f, nu: jnp.where(s < nu[0], f, nf - 1)
    w_in_spec = pl.BlockSpec((None, d, tf), lambda s, f, te, r0, ns, nu, nsu, tok: (te[s], 0, tile_f(s, f, nu)))
    w_out_spec = pl.BlockSpec((None, tf, d), lambda s, f, te, r0, ns, nu, nsu, tok: (te[s], tile_f(s, f, nu), 0))
    return pl.pallas_call(
        functools.partial(_moe_kernel, nf=nf, rows_per_step=rows_per_step, n_sub_alloc=n_sub,
                          n_fill=n_unused_max),
        grid_spec=pltpu.PrefetchScalarGridSpec(
            num_scalar_prefetch=6,
            grid=(n_super, nf),
            in_specs=[pl.BlockSpec(memory_space=pl.ANY), w_in_spec, w_in_spec, w_out_spec],
            out_specs=pl.BlockSpec(memory_space=pl.ANY),
            scratch_shapes=[pltpu.VMEM((n_fetch, half), jnp.uint32),
                            pltpu.VMEM((MOE_SUPER, d), BF16),
                            pltpu.VMEM((MOE_SUPER, d), F32),
                            pltpu.VMEM((MOE_SUPER, half), jnp.uint32),
                            pltpu.SemaphoreType.DMA((1,)), pltpu.SemaphoreType.DMA((SUBS_PER_TILE,)),
                            pltpu.SemaphoreType.DMA((n_unused_max,))]),
        out_shape=jax.ShapeDtypeStruct((n_sub * MOE_SUB, half), jnp.uint32),
        compiler_params=_params(("arbitrary", "arbitrary"), 58),
        name="moe_experts",
    )(tile_expert, tile_row0, tile_nsub, n_used, n_sub_used, row_token, hp, w_gate, w_up, w_down)


COMBINE_UNROLL = 8


def _combine_kernel(dest_ref, y_ref, x_ref, gates_ref, gate_f_ref, gn_ref, o_ref, rows_ref, sems,
                    *, tm, n_steps, final_norm):
    t = pl.program_id(0)
    slot = t % 2
    n = tm * TOP_K
    half = rows_ref.shape[-1]

    def row_copy(base, j, row, slot_):
        return pltpu.make_async_copy(y_ref.at[pl.ds(dest_ref[base + j], 1)],
                                     rows_ref.at[slot_, pl.ds(row, 1)], sems.at[slot_])

    def wait_rows(slot_):
        pltpu.make_async_copy(y_ref.at[pl.ds(0, n)], rows_ref.at[slot_], sems.at[slot_]).wait()

    @pl.when(t == 0)
    def _():
        def body(i, _):
            for u in range(COMBINE_UNROLL):
                row = (u % TOP_K) * tm + i * (COMBINE_UNROLL // TOP_K) + u // TOP_K
                row_copy(0, i * COMBINE_UNROLL + u, row, 0).start()
            return 0
        lax.fori_loop(0, n // COMBINE_UNROLL, body, 0)

    wait_rows(slot)
    next_base = jnp.minimum(t + 1, n_steps - 1) * n
    for j in range(n):
        row_copy(next_base, j, (j % TOP_K) * tm + j // TOP_K, 1 - slot).start()

    gates = gates_ref[...]
    g0, g1 = gates[:, 0:1], gates[:, 1:2]
    lo0, hi0 = _unpack_halves(rows_ref[slot, 0:tm, :])
    lo1, hi1 = _unpack_halves(rows_ref[slot, tm:2 * tm, :])
    out_lo = x_ref[:, :half] + gate_f_ref[:, :half] * (g0 * lo0 + g1 * lo1)
    out_hi = x_ref[:, half:] + gate_f_ref[:, half:] * (g0 * hi0 + g1 * hi1)
    if final_norm:
        ssq = jnp.sum(out_lo * out_lo, axis=-1, keepdims=True) + jnp.sum(out_hi * out_hi, axis=-1, keepdims=True)
        inv = lax.rsqrt(ssq / (2 * half) + EPS)
        out_lo = (out_lo * inv) * gn_ref[:, :half]
        out_hi = (out_hi * inv) * gn_ref[:, half:]
    o_ref[:, :half] = out_lo
    o_ref[:, half:] = out_hi
    pl.when(t == n_steps - 1)(functools.partial(wait_rows, 1 - slot))


def _combine(x, y_rows, dest, top_gate, gate_f, final_g, tm=256):
    bsz, seqlen, d = x.shape
    final_norm = final_g is not None
    gn = (final_g if final_norm else jnp.ones((d,), F32)).reshape(1, d)
    per_batch = seqlen // tm
    n_steps = bsz * per_batch
    row = lambda width: pl.BlockSpec((None, tm, width), lambda t, dst: (t // per_batch, t % per_batch, 0))
    return pl.pallas_call(
        functools.partial(_combine_kernel, tm=tm, n_steps=n_steps, final_norm=final_norm),
        grid_spec=pltpu.PrefetchScalarGridSpec(
            num_scalar_prefetch=1,
            grid=(n_steps,),
            in_specs=[pl.BlockSpec(memory_space=pl.ANY), row(d), row(LANES),
                      pl.BlockSpec((None, 1, d), lambda t, dst: (t // per_batch, 0, 0)),
                      pl.BlockSpec((1, d), lambda t, dst: (0, 0))],
            out_specs=row(d),
            scratch_shapes=[pltpu.VMEM((2, TOP_K * tm, d // 2), jnp.uint32), pltpu.SemaphoreType.DMA((2,))]),
        out_shape=jax.ShapeDtypeStruct(x.shape, F32),
        compiler_params=_params(("arbitrary",), 32),
        name="moe_combine",
    )(dest, y_rows, x, top_gate, gate_f[:, None, :], gn)


def _moe_ffn(x, g, shift, scale, gate_f, w_router, w_gate, w_up, w_down, final_g, tf=512):
    bsz, seqlen, d = x.shape
    n_tok = bsz * seqlen
    hp, top_idx, top_gate = _router(x, g, shift, scale, w_router)
    tables = _routing_tables(top_idx[..., :TOP_K], n_tok, w_gate.shape[2] // tf)
    y_rows = _moe_experts(hp.reshape(n_tok, d // 2), tables, w_gate, w_up, w_down, tf=tf)
    return _combine(x, y_rows, tables[0], top_gate, gate_f, final_g)


def _final_norm_kernel(x_ref, g_ref, o_ref):
    x = x_ref[...]
    ms = jnp.mean(x * x, axis=-1, keepdims=True)
    o_ref[...] = (x * lax.rsqrt(ms + EPS)) * g_ref[...]


def _final_norm(x, g, tm=512):
    bsz, seqlen, d = x.shape
    return pl.pallas_call(
        _final_norm_kernel,
        grid=(bsz, seqlen // tm),
        in_specs=[pl.BlockSpec((None, tm, d), lambda b, i: (b, i, 0)),
                  pl.BlockSpec((1, d), lambda b, i: (0, 0))],
        out_specs=pl.BlockSpec((None, tm, d), lambda b, i: (b, i, 0)),
        out_shape=jax.ShapeDtypeStruct(x.shape, F32),
        compiler_params=_params(("arbitrary", "arbitrary"), 32),
        name="final_norm",
    )(x, g.reshape(1, d))


def kernel(x, c, w_mod, b_mod, norm_mix_g, norm_ffn_g, w_in, ssm_a_re, ssm_a_im, ssm_log_dt, ssm_b_re, ssm_b_im, ssm_c_re, ssm_c_im, ssm_d, w_glu, b_glu, rel_bias, w_branch_ssm, w_branch_att, w_out, ffn_w_gate, ffn_w_up, ffn_w_down, moe_router, moe_w_gate, moe_w_up, moe_w_down, final_norm_g):
    depth = w_mod.shape[0]
    mod = _modulation(c, w_mod, b_mod)
    biases = [_bias_tile(rel_bias, g) for g in range(N_ATT_GROUPS)]
    tables = _s5_tables(ssm_a_re, ssm_a_im, ssm_log_dt, ssm_b_re, ssm_b_im, ssm_c_re, ssm_c_im)
    for i in range(depth):
        shift_m, scale_m, gate_m, shift_f, scale_f, gate_f = jnp.split(mod[i], N_MOD, axis=-1)
        main, qkv = _in_projection(x, norm_mix_g[i], shift_m, scale_m, w_in, i)
        y_ssm = _s5_branch(main, tables, i, ssm_d[i], w_glu, b_glu[i])
        att = [_attention_group(qkv[g], biases[g], g) for g in range(N_ATT_GROUPS)]
        x = _merge(x, gate_m, y_ssm, att, main, w_branch_ssm, w_branch_att, w_out, i)
        j = i // 2
        last = i == depth - 1
        if i % 2 == 0:
            x = _dense_ffn(x, norm_ffn_g[i], shift_f, scale_f, gate_f,
                           ffn_w_gate[j], ffn_w_up[j], ffn_w_down[j])
            if last:
                x = _final_norm(x, final_norm_g)
        else:
            x = _moe_ffn(x, norm_ffn_g[i], shift_f, scale_f, gate_f, moe_router[j],
                         moe_w_gate[j], moe_w_up[j], moe_w_down[j], final_norm_g if last else None)
    return x
```

```python
import functools
import math

import jax
import jax.numpy as jnp
from jax import lax
from jax.experimental import pallas as pl
from jax.experimental.pallas import tpu as pltpu

F32 = jnp.float32
BF16 = jnp.bfloat16

LANES = 128
SUBLANES = 8
VMEM_BYTES = 64 * 1024 * 1024

SSM_GROUP = 16
SSM_STATE = 64
SSM_WIDTH = 1024
HEAD_DIM = 128
DILATION_PATTERN = ((128, 1), (512, 4), (2048, 16))
HEADS_PER_GROUP = 4
N_ATT_GROUPS = len(DILATION_PATTERN)
ATT_OUT_WIDTH = HEADS_PER_GROUP * HEAD_DIM
Q_BLOCK = 128
NEG_INF = -1e30
N_BUCKETS = 32
MAX_DISTANCE = 2048
N_EXPERTS = 8
TOP_K = 2
N_MOD = 6
EPS = 1e-6

PROJ_TILE = 512
U_TILE0 = 0
Q_TILE0 = SSM_WIDTH // PROJ_TILE
K_TILE0 = Q_TILE0 + N_ATT_GROUPS
V_TILE0 = K_TILE0 + N_ATT_GROUPS
GS_TILE0 = V_TILE0 + N_ATT_GROUPS
MAIN_GATE0 = Q_TILE0


def _params(dims, vmem_mb):
    assert vmem_mb * 1024 * 1024 <= VMEM_BYTES
    return pltpu.CompilerParams(dimension_semantics=dims,
                                vmem_limit_bytes=vmem_mb * 1024 * 1024)


NORM_ROWS = 64


def _norm_modulate_into(h_ref, x_ref, g_ref, shift_ref, scale_ref, inv_ref):
    n_blocks = x_ref.shape[0] // NORM_ROWS
    reps = x_ref.shape[1] // LANES

    def block(i):
        return pl.ds(pl.multiple_of(i * NORM_ROWS, NORM_ROWS), NORM_ROWS)

    def rms(i, _):
        x = x_ref[block(i), :]
        inv = lax.rsqrt(jnp.mean(x * x, axis=-1, keepdims=True) + EPS)
        inv_ref[block(i), :] = jnp.broadcast_to(inv, (NORM_ROWS, LANES))
        return 0

    gain = g_ref[...] * (1.0 + scale_ref[...])
    shift = shift_ref[...]

    def scale_rows(i, _):
        y = x_ref[block(i), :] * jnp.concatenate([inv_ref[block(i), :]] * reps, axis=-1)
        h_ref[block(i), :] = (y * gain + shift).astype(h_ref.dtype)
        return 0

    lax.fori_loop(0, n_blocks, rms, 0, unroll=4)
    lax.fori_loop(0, n_blocks, scale_rows, 0)


def _mod_kernel(c_ref, w_ref, b_ref, o_ref):
    c = c_ref[...]
    cond = (c * jax.nn.sigmoid(c)).astype(BF16)
    o_ref[...] = jnp.dot(cond, w_ref[...].astype(BF16),
                         preferred_element_type=F32) + b_ref[...]


def _modulation(c, w_mod, b_mod):
    depth, d, n = w_mod.shape
    bsz = c.shape[0]
    rows = SUBLANES
    c_pad = jnp.zeros((rows, d), F32).at[:bsz].set(c)
    tn = 1536
    out = pl.pallas_call(
        _mod_kernel,
        grid=(depth, n // tn),
        in_specs=[pl.BlockSpec((rows, d), lambda l, j: (0, 0)),
                  pl.BlockSpec((None, d, tn), lambda l, j: (l, 0, j)),
                  pl.BlockSpec((None, 1, tn), lambda l, j: (l, 0, j))],
        out_specs=pl.BlockSpec((None, rows, tn), lambda l, j: (l, 0, j)),
        out_shape=jax.ShapeDtypeStruct((depth, rows, n), F32),
        compiler_params=_params(("arbitrary", "arbitrary"), 40),
        name="modulation",
    )(c_pad, w_mod, b_mod.reshape(depth, 1, n))
    return out[:, :bsz]


DEINTERLEAVE = 4


def _inproj_kernel(x_ref, g_ref, shift_ref, scale_ref, w_ref, main_ref, *rest, tm):
    qkv_refs, (h_ref, res_ref, res2_ref, inv_ref) = rest[:N_ATT_GROUPS], rest[N_ATT_GROUPS:]
    j = pl.program_id(2)

    @pl.when(j == 0)
    def _():
        _norm_modulate_into(h_ref, x_ref, g_ref, shift_ref, scale_ref, inv_ref)

    def project():
        return jnp.dot(h_ref[...], w_ref[...].astype(BF16), preferred_element_type=F32)

    @pl.when(jnp.logical_or(j < Q_TILE0, j >= GS_TILE0))
    def _():
        main_ref[...] = project().astype(BF16)

    for g, (_, dil) in enumerate(DILATION_PATTERN):
        is_g = functools.reduce(jnp.logical_or, [j == t0 + g for t0 in (Q_TILE0, K_TILE0, V_TILE0)])

        @pl.when(is_g)
        def _(g=g, dil=dil):
            res = project()
            if dil == 1:
                qkv_refs[g][0] = res.astype(BF16)
            else:
                n_chunks = PROJ_TILE // LANES
                for ch in range(n_chunks):
                    res_ref[ch] = res[:, ch * LANES:(ch + 1) * LANES]
                if dil <= DEINTERLEAVE:
                    for r in range(dil):
                        rows = [res_ref[ch, pl.ds(r, tm // dil, stride=dil), :] for ch in range(n_chunks)]
                        qkv_refs[g][r] = jnp.concatenate(rows, axis=-1).astype(BF16)
                else:
                    assert dil == DEINTERLEAVE * DEINTERLEAVE
                    sub = tm // DEINTERLEAVE
                    for a in range(DEINTERLEAVE):
                        for ch in range(n_chunks):
                            res2_ref[ch, a * sub:(a + 1) * sub, :] = (
                                res_ref[ch, pl.ds(a, sub, stride=DEINTERLEAVE), :])
                    for a in range(DEINTERLEAVE):
                        for m in range(DEINTERLEAVE):
                            rows = [res2_ref[ch, pl.ds(a * sub + m, tm // dil, stride=DEINTERLEAVE), :]
                                    for ch in range(n_chunks)]
                            qkv_refs[g][a + DEINTERLEAVE * m] = jnp.concatenate(rows, axis=-1).astype(BF16)


def _in_projection(x, g, shift, scale, w_in, layer, tm=1024):
    bsz, seqlen, d = x.shape
    n_tiles = w_in.shape[2] // PROJ_TILE
    n_main = n_tiles - 3 * N_ATT_GROUPS

    def main_map(b, i, j):
        return (jnp.where(j < Q_TILE0, j, jnp.where(j < GS_TILE0, Q_TILE0 - 1, j - 3 * N_ATT_GROUPS)), b, i, 0)

    def qkv_map(g):
        return lambda b, i, j: ((j > Q_TILE0 + g).astype(jnp.int32) + (j > K_TILE0 + g).astype(jnp.int32),
                                b, 0, i, 0)

    qkv_specs = [pl.BlockSpec((None, None, dil, tm // dil, PROJ_TILE), qkv_map(g))
                 for g, (_, dil) in enumerate(DILATION_PATTERN)]
    qkv_shapes = [jax.ShapeDtypeStruct((3, bsz, dil, seqlen // dil, PROJ_TILE), BF16)
                  for _, dil in DILATION_PATTERN]
    outs = pl.pallas_call(
        functools.partial(_inproj_kernel, tm=tm),
        grid=(bsz, seqlen // tm, n_tiles),
        in_specs=[pl.BlockSpec((None, tm, d), lambda b, i, j: (b, i, 0)),
                  pl.BlockSpec((1, d), lambda b, i, j: (0, 0)),
                  pl.BlockSpec((None, 1, d), lambda b, i, j: (b, 0, 0)),
                  pl.BlockSpec((None, 1, d), lambda b, i, j: (b, 0, 0)),
                  pl.BlockSpec((None, d, PROJ_TILE), lambda b, i, j: (layer, 0, j))],
        out_specs=[pl.BlockSpec((None, None, tm, PROJ_TILE), main_map)] + qkv_specs,
        out_shape=[jax.ShapeDtypeStruct((n_main, bsz, seqlen, PROJ_TILE), BF16)] + qkv_shapes,
        scratch_shapes=[pltpu.VMEM((tm, d), BF16), pltpu.VMEM((PROJ_TILE // LANES, tm, LANES), F32),
                        pltpu.VMEM((PROJ_TILE // LANES, tm, LANES), F32), pltpu.VMEM((tm, LANES), F32)],
        compiler_params=_params(("arbitrary", "arbitrary", "arbitrary"), 52),
        name="in_projection",
    )(x, g.reshape(1, d), shift[:, None, :], scale[:, None, :], w_in)
    return outs[0], outs[1:]


SSM_GB = 16
SSM_NGB = (SSM_WIDTH // SSM_GROUP) // SSM_GB
SSM_GB_IN = SSM_GB * SSM_GROUP
SSM_GB_RE = SSM_GB * SSM_STATE
SSM_CB = 2 * SSM_GB_RE // LANES
SSM_SLOTS = SSM_NGB * SSM_CB + SUBLANES


def _s5_kernel(ua0_ref, ua1_ref, un0_ref, un1_ref, bm_ref, cm_ref, ar_ref, ai_ref, d_ref, wglu_ref, bglu_ref,
               o_ref, xs0_ref, xs1_ref, ug_ref, st_ref, *, bsz, tt):
    xs_ref = (xs0_ref, xs1_ref)
    rows = bsz * tt
    n_nt = bm_ref.shape[1]
    n_pieces = SSM_NGB * n_nt
    unroll = 4
    assert tt == unroll * n_pieces
    half_cb = SSM_CB // 2

    def pair_halves(c):
        return tuple(r[:, c * tt:(c + 1) * tt, :].reshape(rows, PROJ_TILE) for r in (ua0_ref, ua1_ref))

    def stage_u(halves):
        for gb in range(SSM_NGB):
            half, off = divmod(gb * SSM_GB_IN, PROJ_TILE)
            ug_ref[gb] = halves[half][:, off:off + SSM_GB_IN]

    def b_piece(i, dst):
        gb, nt = i // n_nt, i % n_nt
        res = jnp.dot(ug_ref[gb], bm_ref[gb, nt], preferred_element_type=F32)
        for j in range(2):
            xs_ref[dst][pl.ds(gb * SSM_CB + 2 * nt + j, rows, stride=SSM_SLOTS), :] = (
                res[:, j * LANES:(j + 1) * LANES])

    a_re = [ar_ref[gb] for gb in range(SSM_NGB)]
    a_im = [ai_ref[gb] for gb in range(SSM_NGB)]

    def time_step(t, carry, src):
        new = []
        for b in range(bsz):
            for gb in range(SSM_NGB):
                k = (b * SSM_NGB + gb) * 2
                s_re, s_im = carry[k], carry[k + 1]
                row = (b * tt + t) * SSM_SLOTS + gb * SSM_CB
                n_re = a_re[gb] * s_re - a_im[gb] * s_im + xs_ref[src][pl.ds(row, half_cb), :]
                n_im = a_re[gb] * s_im + a_im[gb] * s_re + xs_ref[src][pl.ds(row + half_cb, half_cb), :]
                xs_ref[src][pl.ds(row, half_cb), :] = n_re
                xs_ref[src][pl.ds(row + half_cb, half_cb), :] = n_im
                new += [n_re, n_im]
        return tuple(new)

    n_carry = bsz * SSM_NGB * 2

    def scan_chunk(src):
        carry = tuple(st_ref[k] for k in range(n_carry))
        for it in range(n_pieces):
            b_piece(it, 1 - src)
            for k in range(unroll):
                carry = time_step(it * unroll + k, carry, src)
        for k in range(n_carry):
            st_ref[k] = carry[k]

    def finish_chunk(c):
        ys = []
        for gb in range(SSM_NGB):
            cols = [xs_ref[c][pl.ds(gb * SSM_CB + cb, rows, stride=SSM_SLOTS), :] for cb in range(SSM_CB)]
            xg = jnp.concatenate(cols, axis=-1).astype(BF16)
            ys.append(jnp.dot(xg, cm_ref[gb], preferred_element_type=F32))
        y = jnp.concatenate(ys, axis=-1)
        u32 = jnp.concatenate(pair_halves(c), axis=-1).astype(F32)
        y = jax.nn.gelu(y + d_ref[...] * u32)
        z = jnp.dot(y.astype(BF16), wglu_ref[...], preferred_element_type=F32) + bglu_ref[...]
        o_ref[:, c * tt:(c + 1) * tt, :] = (y * jax.nn.sigmoid(z)).astype(o_ref.dtype).reshape(bsz, tt, SSM_WIDTH)

    @pl.when(pl.program_id(0) == 0)
    def _():
        st_ref[...] = jnp.zeros_like(st_ref)
        stage_u(pair_halves(0))
        for i in range(n_pieces):
            b_piece(i, 0)

    stage_u(pair_halves(1))
    scan_chunk(0)
    finish_chunk(0)
    stage_u(tuple(r[...].reshape(rows, PROJ_TILE) for r in (un0_ref, un1_ref)))
    scan_chunk(1)
    finish_chunk(1)


def _block_diag(m):
    ngb, rows, c = m.shape
    r = rows // SSM_GB
    row_group = lax.broadcasted_iota(jnp.int32, (rows, SSM_GB * c), 0) // r
    col_group = lax.broadcasted_iota(jnp.int32, (rows, SSM_GB * c), 1) // c
    return jnp.where(row_group == col_group, jnp.tile(m, (1, 1, SSM_GB)), 0.0)


def _s5_tables(a_re, a_im, log_dt, b_re, b_im, c_re, c_im):
    depth = a_re.shape[0]
    lam = lax.complex(a_re.astype(F32), a_im.astype(F32))
    dt = jnp.exp(log_dt.astype(F32))[..., None]
    a_bar = jnp.exp(lam * dt)
    b_bar = ((a_bar - 1.0) / lam)[..., None] * lax.complex(b_re.astype(F32), b_im.astype(F32))
    _, g, p, h = b_bar.shape
    n = depth * SSM_NGB
    bt = jnp.transpose(b_bar, (0, 1, 3, 2)).reshape(n, SSM_GB * h, p)
    bm = jnp.concatenate([_block_diag(bt.real), _block_diag(bt.imag)], axis=-1)
    ct_re = jnp.transpose(c_re.astype(F32), (0, 1, 3, 2)).reshape(n, SSM_GB * p, h)
    ct_im = jnp.transpose(c_im.astype(F32), (0, 1, 3, 2)).reshape(n, SSM_GB * p, h)
    cm = jnp.concatenate([_block_diag(ct_re), -_block_diag(ct_im)], axis=1)
    lead = (depth, SSM_NGB)
    return (bm.astype(BF16).reshape(lead + bm.shape[1:]), cm.astype(BF16).reshape(lead + cm.shape[1:]),
            a_bar.real.reshape(lead + (SSM_CB // 2, LANES)), a_bar.imag.reshape(lead + (SSM_CB // 2, LANES)))


def _s5_branch(proj, tables, layer, d_skip, w_glu, b_glu, tt=128):
    _, bsz, seqlen, _ = proj.shape
    bm, cm, ar, ai = tables
    n_nt = bm.shape[-1] // SSM_GB_IN
    bm = jnp.swapaxes(bm.reshape(bm.shape[:3] + (n_nt, SSM_GB_IN)), 2, 3)
    d_skip = d_skip.reshape(1, SSM_WIDTH).astype(F32)
    w_glu = w_glu.astype(BF16)
    b_glu = b_glu.reshape(1, SSM_WIDTH).astype(F32)
    n_chunks = seqlen // tt
    pair = lambda tile: pl.BlockSpec((None, bsz, 2 * tt, PROJ_TILE), lambda s: (tile, 0, s, 0))
    nxt = lambda tile: pl.BlockSpec((None, bsz, tt, PROJ_TILE),
                                    lambda s: (tile, 0, jnp.minimum(2 * s + 2, n_chunks - 1), 0))
    return pl.pallas_call(
        functools.partial(_s5_kernel, bsz=bsz, tt=tt),
        grid=(n_chunks // 2,),
        in_specs=[pair(U_TILE0), pair(U_TILE0 + 1), nxt(U_TILE0), nxt(U_TILE0 + 1),
                  _resident(bm, layer), _resident(cm, layer), _resident(ar, layer), _resident(ai, layer),
                  _resident(d_skip), _resident(w_glu, layer), _resident(b_glu)],
        out_specs=pl.BlockSpec((bsz, 2 * tt, SSM_WIDTH), lambda s: (0, s, 0)),
        out_shape=jax.ShapeDtypeStruct((bsz, seqlen, SSM_WIDTH), BF16),
        scratch_shapes=[pltpu.VMEM((bsz * tt * SSM_SLOTS, LANES), F32),
                        pltpu.VMEM((bsz * tt * SSM_SLOTS, LANES), F32),
                        pltpu.VMEM((SSM_NGB, bsz * tt, SSM_GB_IN), BF16),
                        pltpu.VMEM((bsz * SSM_NGB * 2, SSM_CB // 2, LANES), F32)],
        compiler_params=_params(("arbitrary",), 48),
        name="s5_branch",
    )(proj, proj, proj, proj, bm, cm, ar, ai, d_skip, w_glu, b_glu)


def _t5_causal_bucket(dist):
    max_exact = N_BUCKETS // 2
    d32 = jnp.maximum(dist, 1).astype(F32)
    large = max_exact + (jnp.log(d32 / max_exact) / math.log(MAX_DISTANCE / max_exact)
                         * (N_BUCKETS - max_exact)).astype(jnp.int32)
    return jnp.where(dist < max_exact, dist, jnp.minimum(large, N_BUCKETS - 1))


def _bias_tile(rel_bias, group):
    window, dilation = DILATION_PATTERN[group]
    steps = window // dilation
    assert steps == Q_BLOCK
    heads = slice(group * HEADS_PER_GROUP, (group + 1) * HEADS_PER_GROUP)
    back = jnp.arange(steps, -1, -1, dtype=jnp.int32)
    vals = rel_bias[_t5_causal_bucket(back * dilation)][:, heads].astype(F32).T
    period = 3 * Q_BLOCK
    v = jnp.concatenate([vals, jnp.full((HEADS_PER_GROUP, period - steps - 1), NEG_INF, F32)], axis=1)
    flat = jnp.tile(v, (1, Q_BLOCK))[:, :Q_BLOCK * (period - 1)]
    tile = flat.reshape(HEADS_PER_GROUP, Q_BLOCK, period - 1)[:, :, :2 * Q_BLOCK]
    col = lax.broadcasted_iota(jnp.int32, tile.shape, 2)
    return jnp.concatenate([tile, jnp.where(col < Q_BLOCK, NEG_INF, tile)], axis=0)


LSE_LANES = LANES // HEADS_PER_GROUP


def _attn_kernel(q_ref, kc_ref, kp_ref, vc_ref, vp_ref, bias_ref, o_ref, lse_ref, kf_ref, vf_ref, *, tq, n_res):
    scale = HEAD_DIM ** -0.5
    first_tile = pl.program_id(2) == 0
    lane_head = lax.broadcasted_iota(jnp.int32, (Q_BLOCK, LANES), 1) // LSE_LANES

    for rr in range(n_res):
        kf_ref[rr, 0:Q_BLOCK, :] = kp_ref[rr]
        kf_ref[rr, Q_BLOCK:, :] = kc_ref[rr]
        vf_ref[rr, 0:Q_BLOCK, :] = vp_ref[rr]
        vf_ref[rr, Q_BLOCK:, :] = vc_ref[rr]

        def block(jb, _, rr=rr):
            r0 = pl.multiple_of(jb * Q_BLOCK, Q_BLOCK)
            bias_set = jnp.logical_and(first_tile, jb == 0).astype(jnp.int32) * HEADS_PER_GROUP
            lse = jnp.zeros((Q_BLOCK, LANES), F32)
            for h in range(HEADS_PER_GROUP):
                hs = slice(h * HEAD_DIM, (h + 1) * HEAD_DIM)
                q = q_ref[rr, pl.ds(r0, Q_BLOCK), hs]
                k2 = kf_ref[rr, pl.ds(r0, 2 * Q_BLOCK), hs]
                v2 = vf_ref[rr, pl.ds(r0, 2 * Q_BLOCK), hs]
                s = lax.dot_general(q, k2, (((1,), (1,)), ((), ())), preferred_element_type=F32)
                s = s * scale + bias_ref[bias_set + h]
                m = jnp.max(s, axis=-1, keepdims=True)
                p = jnp.exp(s - m)
                l = jnp.sum(p, axis=-1, keepdims=True)
                o = jnp.dot(p.astype(BF16), v2, preferred_element_type=F32) / l
                o_ref[rr, pl.ds(r0, Q_BLOCK), hs] = o.astype(o_ref.dtype)
                lse = jnp.where(lane_head == h, m + jnp.log(l), lse)
            lse_ref[rr, pl.ds(r0, Q_BLOCK), :] = lse
            return 0

        lax.fori_loop(0, tq // Q_BLOCK, block, 0, unroll=min(4, tq // Q_BLOCK))


ATTN_ROWS = 1024


def _attention_group(qkv, bias, group, tile0=0):
    _, bsz, d, lc, _ = qkv.shape
    tq = min(lc, ATTN_ROWS)
    n_res = min(d, ATTN_ROWS // tq)
    per_tq = tq // Q_BLOCK
    cur = lambda which: pl.BlockSpec((None, None, n_res, tq, PROJ_TILE),
                                     lambda b, r, i: (tile0 + which, b, r, i, 0))
    prev = lambda which: pl.BlockSpec((None, None, n_res, Q_BLOCK, PROJ_TILE),
                                      lambda b, r, i: (tile0 + which, b, r, jnp.maximum(i * per_tq - 1, 0), 0))
    out_spec = lambda width: pl.BlockSpec((None, n_res, tq, width), lambda b, r, i: (b, r, i, 0))
    return pl.pallas_call(
        functools.partial(_attn_kernel, tq=tq, n_res=n_res),
        grid=(bsz, d // n_res, lc // tq),
        in_specs=[cur(0), cur(1), prev(1), cur(2), prev(2),
                  pl.BlockSpec(bias.shape, lambda b, r, i: (0, 0, 0))],
        out_specs=[out_spec(ATT_OUT_WIDTH), out_spec(LANES)],
        out_shape=[jax.ShapeDtypeStruct((bsz, d, lc, ATT_OUT_WIDTH), BF16),
                   jax.ShapeDtypeStruct((bsz, d, lc, LANES), F32)],
        scratch_shapes=[pltpu.VMEM((n_res, Q_BLOCK + tq, PROJ_TILE), BF16),
                        pltpu.VMEM((n_res, Q_BLOCK + tq, PROJ_TILE), BF16)],
        compiler_params=_params(("arbitrary", "arbitrary", "arbitrary"), 40),
        name=f"attention_group{group}",
    )(qkv, qkv, qkv, qkv, qkv, bias)


def _merge_kernel(*refs, tm):
    ys_ref = refs[0]
    o_refs = refs[1:1 + N_ATT_GROUPS]
    l_refs = refs[1 + N_ATT_GROUPS:1 + 2 * N_ATT_GROUPS]
    k = 1 + 2 * N_ATT_GROUPS
    n_gate = (len(refs) - k - 7) // 2
    gs_refs = refs[k:k + n_gate]
    ga_refs = refs[k + n_gate:k + 2 * n_gate]
    x_ref, gate_ref, wbs_ref, wba_ref, wout_ref, out_ref, tok_ref = refs[k + 2 * n_gate:]

    def token_order(ref, g):
        dil = DILATION_PATTERN[g][1]
        if dil == 1:
            return ref[0].astype(F32)
        n_chunks = ref.shape[-1] // LANES
        for r in range(dil):
            for ch in range(n_chunks):
                tok_ref[ch, pl.ds(r, tm // dil, stride=dil), :] = (
                    ref[r, :, ch * LANES:(ch + 1) * LANES].astype(F32))
        return jnp.concatenate([tok_ref[ch] for ch in range(n_chunks)], axis=-1)

    lses = [token_order(r, g) for g, r in enumerate(l_refs)]
    m = functools.reduce(jnp.maximum, lses)
    es = [jnp.exp(l - m) for l in lses]
    den = functools.reduce(lambda a, b: a + b, es)

    def per_head(w):
        return jnp.concatenate([jnp.broadcast_to(w[:, h * LSE_LANES:h * LSE_LANES + 1], (tm, HEAD_DIM))
                                for h in range(HEADS_PER_GROUP)], axis=-1)

    y_att = functools.reduce(lambda a, b: a + b,
                             [per_head(e / den) * token_order(r, g) for g, (e, r) in enumerate(zip(es, o_refs))])

    m_ssm = jnp.dot(ys_ref[...], wbs_ref[...], preferred_element_type=F32)
    m_att = jnp.dot(y_att.astype(BF16), wba_ref[...], preferred_element_type=F32)
    g_ssm = jnp.concatenate([r[...] for r in gs_refs], axis=-1).astype(F32)
    g_att = jnp.concatenate([r[...] for r in ga_refs], axis=-1).astype(F32)
    merged = jax.nn.sigmoid(g_ssm) * m_ssm + jax.nn.sigmoid(g_att) * m_att
    mixed = jnp.dot(merged.astype(BF16), wout_ref[...], preferred_element_type=F32)
    out_ref[...] = x_ref[...] + gate_ref[...] * mixed


def _resident(a, layer=None):
    nd = a.ndim
    if layer is None:
        return pl.BlockSpec(a.shape, lambda *_: (0,) * nd, pipeline_mode=pl.Buffered(1))
    return pl.BlockSpec((None,) + a.shape[1:], lambda *_: (layer,) + (0,) * (nd - 1), pipeline_mode=pl.Buffered(1))


def _merge(x, gate, y_ssm, att, main, w_branch_ssm, w_branch_att, w_out, layer, tm=512):
    bsz, seqlen, d = x.shape
    n_gate = d // PROJ_TILE
    row = lambda width: pl.BlockSpec((None, tm, width), lambda b, i: (b, i, 0))
    tile = lambda t: pl.BlockSpec((None, None, tm, PROJ_TILE), lambda b, i: (t, b, i, 0))
    res_major = lambda dil, width: pl.BlockSpec((None, dil, tm // dil, width), lambda b, i: (b, 0, i, 0))
    wbs, wba, wout = (w.astype(BF16) for w in (w_branch_ssm, w_branch_att, w_out))
    os_, ls_ = zip(*att)
    in_specs = ([row(SSM_WIDTH)]
                + [res_major(dil, ATT_OUT_WIDTH) for _, dil in DILATION_PATTERN]
                + [res_major(dil, LANES) for _, dil in DILATION_PATTERN]
                + [tile(MAIN_GATE0 + t) for t in range(2 * n_gate)]
                + [row(d), pl.BlockSpec((None, 1, d), lambda b, i: (b, 0, 0)),
                   _resident(wbs, layer), _resident(wba, layer), _resident(wout, layer)])
    return pl.pallas_call(
        functools.partial(_merge_kernel, tm=tm),
        grid=(bsz, seqlen // tm),
        in_specs=in_specs,
        out_specs=row(d),
        out_shape=jax.ShapeDtypeStruct(x.shape, F32),
        scratch_shapes=[pltpu.VMEM((ATT_OUT_WIDTH // LANES, tm, LANES), F32)],
        compiler_params=_params(("arbitrary", "arbitrary"), 56),
        name="merge",
    )(y_ssm, *os_, *ls_, *([main] * (2 * n_gate)), x, gate[:, None, :], wbs, wba, wout)


def _ffn_kernel(x_ref, g_ref, shift_ref, scale_ref, gate_ref, wg_ref, wu_ref, wd_ref,
                o_ref, h_ref, inv_ref):
    f = pl.program_id(2)

    @pl.when(f == 0)
    def _():
        _norm_modulate_into(h_ref, x_ref, g_ref, shift_ref, scale_ref, inv_ref)
        o_ref[...] = jnp.zeros_like(o_ref)

    h = h_ref[...]
    a = jnp.dot(h, wg_ref[...].astype(BF16), preferred_element_type=F32)
    b = jnp.dot(h, wu_ref[...].astype(BF16), preferred_element_type=F32)
    act = (a * jax.nn.sigmoid(a)) * b
    o_ref[...] += jnp.dot(act.astype(BF16), wd_ref[...].astype(BF16), preferred_element_type=F32)

    @pl.when(f == pl.num_programs(2) - 1)
    def _():
        o_ref[...] = x_ref[...] + gate_ref[...] * o_ref[...]


def _dense_ffn(x, g, shift, scale, gate, w_gate, w_up, w_down, tm=1024, tf=512):
    bsz, seqlen, d = x.shape
    dff = w_gate.shape[1]
    vec = pl.BlockSpec((None, 1, d), lambda b, i, f: (b, 0, 0))
    row = pl.BlockSpec((None, tm, d), lambda b, i, f: (b, i, 0), pipeline_mode=pl.Buffered(1))
    return pl.pallas_call(
        _ffn_kernel,
        grid=(bsz, seqlen // tm, dff // tf),
        in_specs=[pl.BlockSpec((None, tm, d), lambda b, i, f: (b, i, 0)),
                  pl.BlockSpec((1, d), lambda b, i, f: (0, 0)),
                  vec, vec, vec,
                  pl.BlockSpec((d, tf), lambda b, i, f: (0, f)),
                  pl.BlockSpec((d, tf), lambda b, i, f: (0, f)),
                  pl.BlockSpec((tf, d), lambda b, i, f: (f, 0))],
        out_specs=row,
        out_shape=jax.ShapeDtypeStruct(x.shape, F32),
        scratch_shapes=[pltpu.VMEM((tm, d), BF16), pltpu.VMEM((tm, LANES), F32)],
        compiler_params=_params(("arbitrary", "arbitrary", "arbitrary"), 56),
        name="dense_ffn",
    )(x, g.reshape(1, d), shift[:, None, :], scale[:, None, :], gate[:, None, :],
      w_gate.astype(BF16), w_up.astype(BF16), w_down.astype(BF16))


MOE_SUB = 256
SUBS_PER_TILE = 5
MOE_SUPER = SUBS_PER_TILE * MOE_SUB


def _pack_halves(x):
    half = x.shape[-1] // 2
    bits = lambda v: lax.bitcast_convert_type(v.astype(BF16).astype(F32), jnp.uint32)
    return bits(x[:, half:]) | (bits(x[:, :half]) >> 16)


def _unpack_halves(w):
    lo = lax.bitcast_convert_type(w << 16, F32)
    hi = lax.bitcast_convert_type(w & jnp.uint32(0xFFFF0000), F32)
    return lo, hi


def _router_kernel(x_ref, g_ref, shift_ref, scale_ref, w2_ref, h_ref, idx_ref, gate_ref, hf_ref, inv_ref):
    _norm_modulate_into(hf_ref, x_ref, g_ref, shift_ref, scale_ref, inv_ref)
    h = hf_ref[...]
    h_ref[...] = _pack_halves(h)
    h_hi = h.astype(BF16)
    h_lo = (h - h_hi.astype(F32)).astype(BF16)
    prod = (jnp.dot(h_hi, w2_ref[...], preferred_element_type=F32)
            + jnp.dot(h_lo, w2_ref[...], preferred_element_type=F32))
    logits = prod[:, :LANES] + prod[:, LANES:]
    lane_i = lax.broadcasted_iota(jnp.int32, logits.shape, 1)
    lane = lane_i.astype(F32)
    logits = jnp.where(lane_i < N_EXPERTS, logits, -jnp.inf)
    m1 = jnp.max(logits, axis=-1, keepdims=True)
    i1 = jnp.min(jnp.where(logits == m1, lane, float(LANES)), axis=-1, keepdims=True)
    rest = jnp.where(lane == i1, -jnp.inf, logits)
    m2 = jnp.max(rest, axis=-1, keepdims=True)
    i2 = jnp.min(jnp.where(rest == m2, lane, float(LANES)), axis=-1, keepdims=True)
    e2 = jnp.exp(m2 - m1)
    den = 1.0 + e2
    idx_ref[...] = jnp.where(lane_i == 0, i1, jnp.where(lane_i == 1, i2, 0.0)).astype(jnp.int32)
    gate_ref[...] = jnp.where(lane_i == 0, 1.0 / den, jnp.where(lane_i == 1, e2 / den, 0.0))


def _router(x, g, shift, scale, w_router, tm=512):
    bsz, seqlen, d = x.shape
    wr = jnp.zeros((d, LANES), F32).at[:, :N_EXPERTS].set(w_router)
    w_hi = wr.astype(BF16)
    w_lo = (wr - w_hi.astype(F32)).astype(BF16)
    w2 = jnp.concatenate([w_hi, w_lo], axis=1)
    row = lambda width: pl.BlockSpec((None, tm, width), lambda b, i: (b, i, 0))
    vec = pl.BlockSpec((None, 1, d), lambda b, i: (b, 0, 0))
    return pl.pallas_call(
        _router_kernel,
        grid=(bsz, seqlen // tm),
        in_specs=[row(d), pl.BlockSpec((1, d), lambda b, i: (0, 0)), vec, vec,
                  pl.BlockSpec((d, 2 * LANES), lambda b, i: (0, 0))],
        out_specs=[row(d // 2), row(LANES), row(LANES)],
        out_shape=[jax.ShapeDtypeStruct((bsz, seqlen, d // 2), jnp.uint32),
                   jax.ShapeDtypeStruct((bsz, seqlen, LANES), jnp.int32),
                   jax.ShapeDtypeStruct((bsz, seqlen, LANES), F32)],
        scratch_shapes=[pltpu.VMEM((tm, d), F32), pltpu.VMEM((tm, LANES), F32)],
        compiler_params=_params(("arbitrary", "arbitrary"), 40),
        name="moe_router",
    )(x, g.reshape(1, d), shift[:, None, :], scale[:, None, :], w2)


def _moe_dims(n_tok, nf):
    n_sub = n_tok * TOP_K // MOE_SUB + N_EXPERTS
    n_super = (n_sub + (SUBS_PER_TILE - 1) * N_EXPERTS) // SUBS_PER_TILE + 1
    rows_per_step = -(-MOE_SUPER // nf)
    while (rows_per_step * nf) % SUBLANES:
        rows_per_step += 1
    n_fetch = rows_per_step * nf
    return n_sub, n_super, rows_per_step, n_fetch


def _routing_tables(top_expert, n_tok, nf):
    n_sub, n_super, _, n_fetch = _moe_dims(n_tok, nf)
    i32 = jnp.int32
    flat_e = top_expert.reshape(-1)
    onehot = (flat_e[:, None] == jnp.arange(N_EXPERTS, dtype=i32)[None, :]).astype(i32)
    csum = jnp.cumsum(onehot, axis=0)
    rank = jnp.sum(csum * onehot, axis=1) - 1
    counts = csum[-1]
    subs = (counts + MOE_SUB - 1) // MOE_SUB
    pend = jnp.cumsum(subs) * MOE_SUB
    pstart = pend - subs * MOE_SUB
    dest = (pstart[flat_e] + rank).astype(i32)
    flat_token = jnp.arange(n_tok * TOP_K, dtype=i32) // TOP_K
    row_token = jnp.zeros((n_sub * MOE_SUB + n_fetch,), i32).at[dest].set(flat_token)
    supers = (subs + SUBS_PER_TILE - 1) // SUBS_PER_TILE
    send = jnp.cumsum(supers)
    sstart = send - supers
    s = jnp.arange(n_super + 1, dtype=i32)
    e = jnp.minimum(jnp.searchsorted(send, s, side='right'), N_EXPERTS - 1).astype(i32)
    local = s - sstart[e]
    used = s < send[-1]
    tile_expert = jnp.where(used, e, e[jnp.maximum(send[-1] - 1, 0)]).astype(i32)
    n_tiles_e = jnp.maximum(supers[e], 1)
    base, rem = subs[e] // n_tiles_e, subs[e] % n_tiles_e
    first_sub = local * base + jnp.minimum(local, rem)
    tile_row0 = jnp.where(used, pstart[e] + first_sub * MOE_SUB, 0).astype(i32)
    tile_nsub = jnp.where(used, base + (local < rem).astype(i32), 0).astype(i32)
    n_used = send[-1].astype(i32).reshape(1)
    n_sub_used = (pend[-1] // MOE_SUB).astype(i32).reshape(1)
    return dest, row_token, tile_expert, tile_row0, tile_nsub, n_used, n_sub_used


def _moe_kernel(te_ref, row0_ref, nsub_ref, nu_ref, nsu_ref, tok_ref, hp_ref, wg_ref, wu_ref, wd_ref, y_ref,
                xg_ref, xb_ref, acc_ref, yb_ref, gsem, osem, fsem, *, nf, rows_per_step, n_sub_alloc, n_fill):
    s, f = pl.program_id(0), pl.program_id(1)
    n_used = nu_ref[0]
    used = s < n_used
    n_fetch = rows_per_step * nf
    half = xg_ref.shape[-1]

    def row_copy(tile, r):
        tok = tok_ref[row0_ref[tile] + r]
        return pltpu.make_async_copy(hp_ref.at[pl.ds(tok, 1)], xg_ref.at[pl.ds(r, 1)], gsem.at[0])

    def wait_rows():
        pltpu.make_async_copy(hp_ref.at[pl.ds(0, n_fetch)], xg_ref, gsem.at[0]).wait()

    def out_copies(tile):
        r0 = pl.multiple_of(row0_ref[tile], MOE_SUB)
        return [pltpu.make_async_copy(yb_ref.at[pl.ds(k * MOE_SUB, MOE_SUB)],
                                      y_ref.at[pl.ds(r0 + k * MOE_SUB, MOE_SUB)], osem.at[k])
                for k in range(SUBS_PER_TILE)]

    def start_out(tile):
        for k, cp in enumerate(out_copies(tile)):
            pl.when(k < nsub_ref[tile])(cp.start)

    def wait_out(tile):
        for k, cp in enumerate(out_copies(tile)):
            pl.when(k < nsub_ref[tile])(cp.wait)

    @pl.when(jnp.logical_and(s == 0, f == 0))
    def _():
        def body(r, _):
            row_copy(0, r).start()
            return 0
        lax.fori_loop(0, n_fetch, body, 0)

    @pl.when(jnp.logical_and(f == 0, s <= n_used))
    def _():
        wait_rows()

    @pl.when(jnp.logical_and(used, f == 0))
    def _():
        lo, hi = _unpack_halves(xg_ref[0:MOE_SUPER, :])
        xb_ref[:, :half] = lo.astype(BF16)
        xb_ref[:, half:] = hi.astype(BF16)
        acc_ref[...] = jnp.zeros_like(acc_ref)

    def step(n_rows):
        for k in range(rows_per_step):
            row_copy(s + 1, f * rows_per_step + k).start()
        h = xb_ref[0:n_rows, :]
        a = jnp.dot(h, wg_ref[...].astype(BF16), preferred_element_type=F32)
        b = jnp.dot(h, wu_ref[...].astype(BF16), preferred_element_type=F32)
        act = (a * jax.nn.sigmoid(a)) * b
        acc_ref[0:n_rows, :] += jnp.dot(act.astype(BF16), wd_ref[...].astype(BF16), preferred_element_type=F32)

    for n in range(1, SUBS_PER_TILE + 1):
        pl.when(jnp.logical_and(used, nsub_ref[s] == n))(functools.partial(step, n * MOE_SUB))

    @pl.when(jnp.logical_and(used, f == nf - 1))
    def _():
        pl.when(s > 0)(functools.partial(wait_out, s - 1))
        yb_ref[...] = _pack_halves(acc_ref[...])
        start_out(s)

    @pl.when(jnp.logical_and(s == n_used, f == 0))
    def _():
        wait_out(s - 1)
        yb_ref[0:MOE_SUB, :] = jnp.zeros((MOE_SUB, half), yb_ref.dtype)
        fills = []
        for k in range(n_fill):
            sub = nsu_ref[0] + k
            cp = pltpu.make_async_copy(yb_ref.at[pl.ds(0, MOE_SUB)],
                                       y_ref.at[pl.ds(pl.multiple_of(sub * MOE_SUB, MOE_SUB), MOE_SUB)], fsem.at[k])
            fills.append((sub < n_sub_alloc, cp))
        for cond, cp in fills:
            pl.when(cond)(cp.start)
        for cond, cp in fills:
            pl.when(cond)(cp.wait)


def _moe_experts(hp, tables, w_gate, w_up, w_down, tf=512):
    n_tok, half = hp.shape
    d = 2 * half
    dff = w_gate.shape[2]
    nf = dff // tf
    n_sub, n_super, rows_per_step, n_fetch = _moe_dims(n_tok, nf)
    _, row_token, tile_expert, tile_row0, tile_nsub, n_used, n_sub_used = tables
    n_unused_max = n_sub - n_tok * TOP_K // MOE_SUB
    tile_f = lambda s, f, nu: jnp.where(s < nu[0], f, nf - 1)
    w_in_spec = pl.BlockSpec((None, d, tf), lambda s, f, te, r0, ns, nu, nsu, tok: (te[s], 0, tile_f(s, f, nu)))
    w_out_spec = pl.BlockSpec((None, tf, d), lambda s, f, te, r0, ns, nu, nsu, tok: (te[s], tile_f(s, f, nu), 0))
    return pl.pallas_call(
        functools.partial(_moe_kernel, nf=nf, rows_per_step=rows_per_step, n_sub_alloc=n_sub,
                          n_fill=n_unused_max),
        grid_spec=pltpu.PrefetchScalarGridSpec(
            num_scalar_prefetch=6,
            grid=(n_super, nf),
            in_specs=[pl.BlockSpec(memory_space=pl.ANY), w_in_spec, w_in_spec, w_out_spec],
            out_specs=pl.BlockSpec(memory_space=pl.ANY),
            scratch_shapes=[pltpu.VMEM((n_fetch, half), jnp.uint32),
                            pltpu.VMEM((MOE_SUPER, d), BF16),
                            pltpu.VMEM((MOE_SUPER, d), F32),
                            pltpu.VMEM((MOE_SUPER, half), jnp.uint32),
                            pltpu.SemaphoreType.DMA((1,)), pltpu.SemaphoreType.DMA((SUBS_PER_TILE,)),
                            pltpu.SemaphoreType.DMA((n_unused_max,))]),
        out_shape=jax.ShapeDtypeStruct((n_sub * MOE_SUB, half), jnp.uint32),
        compiler_params=_params(("arbitrary", "arbitrary"), 58),
        name="moe_experts",
    )(tile_expert, tile_row0, tile_nsub, n_used, n_sub_used, row_token, hp, w_gate, w_up, w_down)


COMBINE_UNROLL = 8


def _combine_kernel(dest_ref, y_ref, x_ref, gates_ref, gate_f_ref, gn_ref, o_ref, rows_ref, sems,
                    *, tm, n_steps, final_norm):
    t = pl.program_id(0)
    slot = t % 2
    n = tm * TOP_K
    half = rows_ref.shape[-1]

    def row_copy(base, j, row, slot_):
        return pltpu.make_async_copy(y_ref.at[pl.ds(dest_ref[base + j], 1)],
                                     rows_ref.at[slot_, pl.ds(row, 1)], sems.at[slot_])

    def wait_rows(slot_):
        pltpu.make_async_copy(y_ref.at[pl.ds(0, n)], rows_ref.at[slot_], sems.at[slot_]).wait()

    @pl.when(t == 0)
    def _():
        def body(i, _):
            for u in range(COMBINE_UNROLL):
                row = (u % TOP_K) * tm + i * (COMBINE_UNROLL // TOP_K) + u // TOP_K
                row_copy(0, i * COMBINE_UNROLL + u, row, 0).start()
            return 0
        lax.fori_loop(0, n // COMBINE_UNROLL, body, 0)

    wait_rows(slot)
    next_base = jnp.minimum(t + 1, n_steps - 1) * n
    for j in range(n):
        row_copy(next_base, j, (j % TOP_K) * tm + j // TOP_K, 1 - slot).start(priority=j % 2)

    gates = gates_ref[...]
    g0, g1 = gates[:, 0:1], gates[:, 1:2]
    lo0, hi0 = _unpack_halves(rows_ref[slot, 0:tm, :])
    lo1, hi1 = _unpack_halves(rows_ref[slot, tm:2 * tm, :])
    out_lo = x_ref[:, :half] + gate_f_ref[:, :half] * (g0 * lo0 + g1 * lo1)
    out_hi = x_ref[:, half:] + gate_f_ref[:, half:] * (g0 * hi0 + g1 * hi1)
    if final_norm:
        ssq = jnp.sum(out_lo * out_lo, axis=-1, keepdims=True) + jnp.sum(out_hi * out_hi, axis=-1, keepdims=True)
        inv = lax.rsqrt(ssq / (2 * half) + EPS)
        out_lo = (out_lo * inv) * gn_ref[:, :half]
        out_hi = (out_hi * inv) * gn_ref[:, half:]
    o_ref[:, :half] = out_lo
    o_ref[:, half:] = out_hi
    pl.when(t == n_steps - 1)(functools.partial(wait_rows, 1 - slot))


def _combine(x, y_rows, dest, top_gate, gate_f, final_g, tm=256):
    bsz, seqlen, d = x.shape
    final_norm = final_g is not None
    gn = (final_g if final_norm else jnp.ones((d,), F32)).reshape(1, d)
    per_batch = seqlen // tm
    n_steps = bsz * per_batch
    row = lambda width: pl.BlockSpec((None, tm, width), lambda t, dst: (t // per_batch, t % per_batch, 0))
    return pl.pallas_call(
        functools.partial(_combine_kernel, tm=tm, n_steps=n_steps, final_norm=final_norm),
        grid_spec=pltpu.PrefetchScalarGridSpec(
            num_scalar_prefetch=1,
            grid=(n_steps,),
            in_specs=[pl.BlockSpec(memory_space=pl.ANY), row(d), row(LANES),
                      pl.BlockSpec((None, 1, d), lambda t, dst: (t // per_batch, 0, 0)),
                      pl.BlockSpec((1, d), lambda t, dst: (0, 0))],
            out_specs=row(d),
            scratch_shapes=[pltpu.VMEM((2, TOP_K * tm, d // 2), jnp.uint32), pltpu.SemaphoreType.DMA((2,))]),
        out_shape=jax.ShapeDtypeStruct(x.shape, F32),
        compiler_params=_params(("arbitrary",), 32),
        name="moe_combine",
    )(dest, y_rows, x, top_gate, gate_f[:, None, :], gn)


def _moe_ffn(x, g, shift, scale, gate_f, w_router, w_gate, w_up, w_down, final_g, tf=512):
    bsz, seqlen, d = x.shape
    n_tok = bsz * seqlen
    hp, top_idx, top_gate = _router(x, g, shift, scale, w_router)
    tables = _routing_tables(top_idx[..., :TOP_K], n_tok, w_gate.shape[2] // tf)
    y_rows = _moe_experts(hp.reshape(n_tok, d // 2), tables, w_gate, w_up, w_down, tf=tf)
    return _combine(x, y_rows, tables[0], top_gate, gate_f, final_g)


def _final_norm_kernel(x_ref, g_ref, o_ref):
    x = x_ref[...]
    ms = jnp.mean(x * x, axis=-1, keepdims=True)
    o_ref[...] = (x * lax.rsqrt(ms + EPS)) * g_ref[...]


def _final_norm(x, g, tm=512):
    bsz, seqlen, d = x.shape
    return pl.pallas_call(
        _final_norm_kernel,
        grid=(bsz, seqlen // tm),
        in_specs=[pl.BlockSpec((None, tm, d), lambda b, i: (b, i, 0)),
                  pl.BlockSpec((1, d), lambda b, i: (0, 0))],
        out_specs=pl.BlockSpec((None, tm, d), lambda b, i: (b, i, 0)),
        out_shape=jax.ShapeDtypeStruct(x.shape, F32),
        compiler_params=_params(("arbitrary", "arbitrary"), 32),
        name="final_norm",
    )(x, g.reshape(1, d))


def kernel(x, c, w_mod, b_mod, norm_mix_g, norm_ffn_g, w_in, ssm_a_re, ssm_a_im, ssm_log_dt, ssm_b_re, ssm_b_im, ssm_c_re, ssm_c_im, ssm_d, w_glu, b_glu, rel_bias, w_branch_ssm, w_branch_att, w_out, ffn_w_gate, ffn_w_up, ffn_w_down, moe_router, moe_w_gate, moe_w_up, moe_w_down, final_norm_g):
    depth = w_mod.shape[0]
    mod = _modulation(c, w_mod, b_mod)
    biases = [_bias_tile(rel_bias, g) for g in range(N_ATT_GROUPS)]
    tables = _s5_tables(ssm_a_re, ssm_a_im, ssm_log_dt, ssm_b_re, ssm_b_im, ssm_c_re, ssm_c_im)
    for i in range(depth):
        shift_m, scale_m, gate_m, shift_f, scale_f, gate_f = jnp.split(mod[i], N_MOD, axis=-1)
        main, qkv = _in_projection(x, norm_mix_g[i], shift_m, scale_m, w_in, i)
        y_ssm = _s5_branch(main, tables, i, ssm_d[i], w_glu, b_glu[i])
        att = [_attention_group(qkv[g], biases[g], g) for g in range(N_ATT_GROUPS)]
        x = _merge(x, gate_m, y_ssm, att, main, w_branch_ssm, w_branch_att, w_out, i)
        j = i // 2
        last = i == depth - 1
        if i % 2 == 0:
            x = _dense_ffn(x, norm_ffn_g[i], shift_f, scale_f, gate_f,
                           ffn_w_gate[j], ffn_w_up[j], ffn_w_down[j])
            if last:
                x = _final_norm(x, final_norm_g)
        else:
            x = _moe_ffn(x, norm_ffn_g[i], shift_f, scale_f, gate_f, moe_router[j],
                         moe_w_gate[j], moe_w_up[j], moe_w_down[j], final_norm_g if last else None)
    return x
```
